```python
import jax
import jax.numpy as jnp
from jax import lax
import numpy as np

D_MODEL = 1024
BATCH = 8
SEQ = 4096
DEPTH = 4

GRID_W = 64
CTX_LEN = 256
ATT_HEADS = 8
ATT_KV_HEADS = 2
ATT_HEAD_DIM = 64
ATT_GROUP = ATT_HEADS // ATT_KV_HEADS
WINDOW = 128
ATT_BLOCK = 128
ATT_SPAN = ATT_BLOCK + 2 * WINDOW
ROPE_THETA = 10000.0
ML_HEADS = 4
ML_QK_DIM = 64
ML_V_DIM = 128
ML_CHUNK = 64
ML_CONV = 5
GATE_CAP = 15.0
ATT_WIDTH = ATT_HEADS * ATT_HEAD_DIM
ATT_KV_WIDTH = ATT_KV_HEADS * ATT_HEAD_DIM
ML_QK_WIDTH = ML_HEADS * ML_QK_DIM
ML_WIDTH = ML_HEADS * ML_V_DIM
ML_GATES = 4 * ML_HEADS
MIX_WIDTH = ATT_WIDTH + ML_WIDTH
SPLIT_SIZES = (ATT_WIDTH, ATT_KV_WIDTH, ATT_KV_WIDTH, ML_QK_WIDTH, ML_QK_WIDTH, ML_WIDTH, ML_WIDTH, ML_GATES)
IN_WIDTH = ATT_WIDTH + 2 * ATT_KV_WIDTH + 2 * ML_QK_WIDTH + 2 * ML_WIDTH + ML_GATES
FFN_DIM = 2816
N_EXPERTS = 8
TOP_K = 2
EPS = 1e-6

kernel_name = 'hybrid_swa_mlstm_moe_diffusion'


def rmsnorm(x, g):
    xf = x.astype(jnp.float32)
    y = xf * lax.rsqrt(jnp.mean(xf * xf, axis=-1, keepdims=True) + EPS)
    return (y * g.astype(jnp.float32)).astype(x.dtype)


def adaln(x, g, shift, scale):
    return rmsnorm(x, g) * (1 + scale) + shift


def split_columns(u):
    points = [int(p) for p in np.cumsum(SPLIT_SIZES)[:-1]]
    return jnp.split(u, points, axis=-1)


def axial_rope_tables(n_tokens):
    rows = n_tokens // GRID_W
    t = jnp.arange(rows * GRID_W)
    row = jnp.repeat(jnp.arange(rows), GRID_W).astype(jnp.float32)
    col = (t % GRID_W).astype(jnp.float32)
    quarter = ATT_HEAD_DIM // 4
    inv = ROPE_THETA ** (-jnp.arange(quarter, dtype=jnp.float32) / quarter)
    ang_r = row[:, None] * inv[None, :]
    ang_c = col[:, None] * inv[None, :]
    return (jnp.cos(ang_r), jnp.sin(ang_r), jnp.cos(ang_c), jnp.sin(ang_c))


def _rotate_half(x, cos, sin):
    x1, x2 = jnp.split(x, 2, axis=-1)
    return jnp.concatenate([x1 * cos - x2 * sin, x2 * cos + x1 * sin], axis=-1)


def apply_axial_rope(x, tabs):
    cos_r, sin_r, cos_c, sin_c = [t[:, None, :] for t in tabs]
    xf = x.astype(jnp.float32)
    half = x.shape[-1] // 2
    out = jnp.concatenate([_rotate_half(xf[..., :half], cos_r, sin_r),
                           _rotate_half(xf[..., half:], cos_c, sin_c)], axis=-1)
    return out.astype(x.dtype)


def sink_attention(qg, kv_sets, sink_g):
    sink = sink_g[None, :, :, None]
    m = sink
    scores = []
    for k, v, mask in kv_sets:
        s = jnp.einsum('bqhgd,bkhd->bhgqk', qg, k).astype(jnp.float32)
        if mask is not None:
            s = jnp.where(mask, s, -jnp.inf)
        scores.append(s)
        m = jnp.maximum(m, s.max(-1))
    denom = jnp.exp(sink - m)
    num = None
    for s, (k, v, mask) in zip(scores, kv_sets):
        p = jnp.exp(s - m[..., None])
        denom = denom + p.sum(-1)
        o = jnp.einsum('bhgqk,bkhd->bhgqd', p, v.astype(jnp.float32))
        num = o if num is None else num + o
    out = num / denom[..., None]
    return jnp.transpose(out, (0, 3, 1, 2, 4))


def window_attention(q, k, v, k_ctx, v_ctx, sink_g):
    B, S = q.shape[:2]
    nb = S // ATT_BLOCK
    qg = (q * (ATT_HEAD_DIM ** -0.5)).reshape(B, nb, ATT_BLOCK, ATT_KV_HEADS, ATT_GROUP, ATT_HEAD_DIM)
    pad = ((0, 0), (WINDOW, WINDOW), (0, 0), (0, 0))
    kp = jnp.pad(k, pad)
    vp = jnp.pad(v, pad)
    valid = jnp.pad(jnp.ones((S,), dtype=bool), (WINDOW, WINDOW))
    rel = jnp.abs(jnp.arange(ATT_BLOCK)[:, None] + WINDOW - jnp.arange(ATT_SPAN)[None, :]) <= WINDOW

    def block(j):
        start = j * ATT_BLOCK
        qb = lax.dynamic_index_in_dim(qg, j, axis=1, keepdims=False)
        kb = lax.dynamic_slice_in_dim(kp, start, ATT_SPAN, axis=1)
        vb = lax.dynamic_slice_in_dim(vp, start, ATT_SPAN, axis=1)
        mask = rel & lax.dynamic_slice_in_dim(valid, start, ATT_SPAN)[None, :]
        out = sink_attention(qb, ((kb, vb, mask), (k_ctx, v_ctx, None)), sink_g)
        return out.reshape(B, ATT_BLOCK, ATT_WIDTH).astype(q.dtype)

    outs = lax.map(block, jnp.arange(nb))
    return jnp.moveaxis(outs, 0, 1).reshape(B, S, ATT_WIDTH)


def mlstm_chunkwise(q, k, v, i_pre, f_pre, state):
    B, T, H, DK = q.shape
    DV = v.shape[-1]
    nc = T // ML_CHUNK
    f32 = jnp.float32

    def chunks(a):
        return jnp.moveaxis(a.reshape((B, nc, ML_CHUNK) + a.shape[2:]), 1, 0)

    qs = chunks(q.astype(f32) * (DK ** -0.5))
    ks = chunks(k.astype(f32))
    vs = chunks(v.astype(f32))
    lis = chunks(i_pre.astype(f32))
    lfs = chunks(jax.nn.log_sigmoid(f_pre.astype(f32)))
    causal = jnp.tril(jnp.ones((ML_CHUNK, ML_CHUNK), dtype=bool))

    def step(carry, inp):
        C, n, m = carry
        qc, kc, vc, li, lf = inp
        b = jnp.cumsum(lf, axis=1).transpose(0, 2, 1)
        ig = li.transpose(0, 2, 1)
        d = jnp.where(causal, b[..., :, None] - b[..., None, :] + ig[..., None, :], -jnp.inf)
        inter = b + m[..., None]
        m_t = jnp.maximum(inter, d.max(-1))
        w_inter = jnp.exp(inter - m_t)
        s = jnp.einsum('bthd,bshd->bhts', qc, kc) * jnp.exp(d - m_t[..., None])
        num = jnp.einsum('bhts,bshe->bhte', s, vc) + w_inter[..., None] * jnp.einsum('bthd,bhde->bhte', qc, C)
        den = s.sum(-1) + w_inter * jnp.einsum('bthd,bhd->bht', qc, n)
        h = num / jnp.maximum(jnp.abs(den), jnp.exp(-m_t))[..., None]
        b_end = b[..., -1]
        g = b_end[..., None] - b + ig
        m_new = jnp.maximum(b_end + m, g.max(-1))
        decay = jnp.exp(b_end + m - m_new)
        wk = jnp.exp(g - m_new[..., None])
        C_new = decay[..., None, None] * C + jnp.einsum('bhs,bshd,bshe->bhde', wk, kc, vc)
        n_new = decay[..., None] * n + jnp.einsum('bhs,bshd->bhd', wk, kc)
        return (C_new, n_new, m_new), h.transpose(0, 2, 1, 3)

    state, hs = lax.scan(step, state, (qs, ks, vs, lis, lfs))
    h = jnp.moveaxis(hs, 0, 1).reshape(B, T, H, DV)
    return h, state


def depthwise_conv(x, w, b):
    C = x.shape[-1]
    y = lax.conv_general_dilated(x, w.reshape(ML_CONV, 1, C).astype(x.dtype), (1,),
                                 [(ML_CONV // 2, ML_CONV // 2)],
                                 dimension_numbers=('NWC', 'WIO', 'NWC'), feature_group_count=C)
    return y + b


def mlstm_inputs(qm, km, vm, gt, conv_w, conv_b, b_gates):
    B, T, _ = qm.shape
    qk = jax.nn.silu(depthwise_conv(jnp.concatenate([qm, km], axis=-1), conv_w, conv_b))
    q, k = jnp.split(qk, 2, axis=-1)
    q = q.reshape(B, T, ML_HEADS, ML_QK_DIM)
    k = k.reshape(B, T, ML_HEADS, ML_QK_DIM)
    v = vm.reshape(B, T, ML_HEADS, ML_V_DIM)
    g = gt.astype(jnp.float32) + b_gates.astype(jnp.float32)
    g = (GATE_CAP * jnp.tanh(g / GATE_CAP)).reshape(B, T, 4, ML_HEADS)
    return q, k, v, g[:, :, 0], g[:, :, 1], g[:, :, 2], g[:, :, 3]


def init_state(B):
    f32 = jnp.float32
    return (jnp.zeros((B, ML_HEADS, ML_QK_DIM, ML_V_DIM), f32),
            jnp.zeros((B, ML_HEADS, ML_QK_DIM), f32),
            jnp.zeros((B, ML_HEADS), f32))


def merge_groups(att, h_ml, o_pre, g_att, g_ml, w_out):
    B, T = att.shape[:2]
    att_n = rmsnorm(att, g_att)
    hf = h_ml.astype(jnp.float32)
    hn = hf * lax.rsqrt(jnp.mean(hf * hf, axis=-1, keepdims=True) + EPS)
    hn = hn * g_ml.astype(jnp.float32).reshape(ML_HEADS, ML_V_DIM)
    ml = hn.reshape(B, T, ML_WIDTH).astype(att.dtype) * jax.nn.sigmoid(o_pre)
    return jnp.concatenate([att_n, ml], axis=-1) @ w_out


def token_mixing(hx, hc, w_in, conv_w, conv_b, b_gates, attn_sink, g_att, g_ml, w_out, rope, with_ctx_out):
    B, S, _ = hx.shape
    L = hc.shape[1]
    qa_x, ka_x, va_x, qm_x, km_x, vm_x, om_x, gt_x = split_columns(hx @ w_in)
    qa_c, ka_c, va_c, qm_c, km_c, vm_c, om_c, gt_c = split_columns(hc @ w_in)
    sink_g = attn_sink.astype(jnp.float32).reshape(ATT_KV_HEADS, ATT_GROUP)
    k_ctx = ka_c.reshape(B, L, ATT_KV_HEADS, ATT_HEAD_DIM)
    v_ctx = va_c.reshape(B, L, ATT_KV_HEADS, ATT_HEAD_DIM)
    q = apply_axial_rope(qa_x.reshape(B, S, ATT_HEADS, ATT_HEAD_DIM), rope)
    k = apply_axial_rope(ka_x.reshape(B, S, ATT_KV_HEADS, ATT_HEAD_DIM), rope)
    v = va_x.reshape(B, S, ATT_KV_HEADS, ATT_HEAD_DIM)
    att_x = window_attention(q, k, v, k_ctx, v_ctx, sink_g)
    qc_, kc_, vc_, icf, fcf, icb, fcb = mlstm_inputs(qm_c, km_c, vm_c, gt_c, conv_w, conv_b, b_gates)
    qx_, kx_, vx_, ixf, fxf, ixb, fxb = mlstm_inputs(qm_x, km_x, vm_x, gt_x, conv_w, conv_b, b_gates)
    flip = lambda a: jnp.flip(a, axis=1)
    h_cf, st_f = mlstm_chunkwise(qc_, kc_, vc_, icf, fcf, init_state(B))
    h_cb, st_b = mlstm_chunkwise(flip(qc_), flip(kc_), flip(vc_), flip(icb), flip(fcb), init_state(B))
    h_xf, _ = mlstm_chunkwise(qx_, kx_, vx_, ixf, fxf, st_f)
    h_xb, _ = mlstm_chunkwise(flip(qx_), flip(kx_), flip(vx_), flip(ixb), flip(fxb), st_b)
    out_x = merge_groups(att_x, h_xf + flip(h_xb), om_x, g_att, g_ml, w_out)
    if not with_ctx_out:
        return out_x, None
    qg_c = (qa_c * (ATT_HEAD_DIM ** -0.5)).reshape(B, L, ATT_KV_HEADS, ATT_GROUP, ATT_HEAD_DIM)
    att_c = sink_attention(qg_c, ((k_ctx, v_ctx, None),), sink_g).reshape(B, L, ATT_WIDTH).astype(hc.dtype)
    out_c = merge_groups(att_c, h_cf + flip(h_cb), om_c, g_att, g_ml, w_out)
    return out_x, out_c


def swiglu(t, w1, w3, w2):
    return (jax.nn.silu(t @ w1) * (t @ w3)) @ w2


def moe_swiglu(h, w_router, b_router, w1, w3, w2):
    shp = h.shape
    t = h.reshape(-1, shp[-1])
    logits = (t @ w_router).astype(jnp.float32) + b_router.astype(jnp.float32)
    top_v, top_i = lax.top_k(logits, TOP_K)
    top_w = jax.nn.softmax(top_v, axis=-1)
    combine = jnp.sum(jax.nn.one_hot(top_i, N_EXPERTS, dtype=jnp.float32) * top_w[..., None], axis=1)
    y = None
    for e in range(N_EXPERTS):
        ye = combine[:, e:e + 1] * swiglu(t, w1[e], w3[e], w2[e]).astype(jnp.float32)
        y = ye if y is None else y + ye
    return y.astype(h.dtype).reshape(shp)


def channel_mix(h, layer, ffn_w1, ffn_w3, ffn_w2, w_router, b_router, exp_w1, exp_w3, exp_w2):
    i = layer // 2
    if layer % 2 == 0:
        return swiglu(h, ffn_w1[i], ffn_w3[i], ffn_w2[i])
    return moe_swiglu(h, w_router[i], b_router[i], exp_w1[i], exp_w3[i], exp_w2[i])


def setup_inputs(seed: int = 0) -> dict:
    key = jax.random.key(seed)
    ks = jax.random.split(key, 26)
    D = D_MODEL
    n_dense = (DEPTH + 1) // 2
    n_moe = DEPTH // 2

    def nrm(i, shape, scale):
        return jax.random.normal(ks[i], shape, jnp.float32) * scale

    i_bias = -2.0 + nrm(11, (DEPTH, 2, ML_HEADS), 0.1)
    f_bias = 4.0 + nrm(12, (DEPTH, 2, ML_HEADS), 0.5)
    b_gates = jnp.stack([i_bias[:, 0], f_bias[:, 0], i_bias[:, 1], f_bias[:, 1]], axis=1).reshape(DEPTH, ML_GATES)
    return {
        'x': nrm(0, (BATCH, SEQ, D), 1.0),
        'c': nrm(1, (BATCH, D), 1.0),
        'ctx': nrm(2, (BATCH, CTX_LEN, D), 1.0),
        'c_ctx': nrm(3, (D,), 1.0),
        'norm1_g': 1.0 + nrm(4, (DEPTH, D), 0.02),
        'norm2_g': 1.0 + nrm(5, (DEPTH, D), 0.02),
        'w_mod': nrm(6, (DEPTH, D, 6 * D), 0.5 * D ** -0.5),
        'b_mod': nrm(7, (DEPTH, 6 * D), 0.02),
        'w_in': nrm(8, (DEPTH, D, IN_WIDTH), D ** -0.5),
        'conv_w': nrm(9, (DEPTH, ML_CONV, 2 * ML_QK_WIDTH), ML_CONV ** -0.5),
        'conv_b': nrm(10, (DEPTH, 2 * ML_QK_WIDTH), 0.02),
        'b_gates': b_gates,
        'attn_sink': nrm(13, (DEPTH, ATT_HEADS), 0.5),
        'g_att': 1.0 + nrm(14, (DEPTH, ATT_WIDTH), 0.02),
        'g_ml': 1.0 + nrm(15, (DEPTH, ML_WIDTH), 0.02),
        'w_out': nrm(16, (DEPTH, MIX_WIDTH, D), MIX_WIDTH ** -0.5),
        'ffn_w1': nrm(17, (n_dense, D, FFN_DIM), D ** -0.5),
        'ffn_w3': nrm(18, (n_dense, D, FFN_DIM), D ** -0.5),
        'ffn_w2': nrm(19, (n_dense, FFN_DIM, D), FFN_DIM ** -0.5),
        'w_router': nrm(20, (n_moe, D, N_EXPERTS), D ** -0.5),
        'b_router': nrm(21, (n_moe, N_EXPERTS), 0.01),
        'exp_w1': nrm(22, (n_moe, N_EXPERTS, D, FFN_DIM), D ** -0.5),
        'exp_w3': nrm(23, (n_moe, N_EXPERTS, D, FFN_DIM), D ** -0.5),
        'exp_w2': nrm(24, (n_moe, N_EXPERTS, FFN_DIM, D), FFN_DIM ** -0.5),
        'final_g': 1.0 + nrm(25, (D,), 0.02),
    }


def reference(x, c, ctx, c_ctx, norm1_g, norm2_g, w_mod, b_mod, w_in, conv_w, conv_b, b_gates,
              attn_sink, g_att, g_ml, w_out, ffn_w1, ffn_w3, ffn_w2, w_router, b_router,
              exp_w1, exp_w3, exp_w2, final_g):
    rope = axial_rope_tables(x.shape[1])
    cond_x = jax.nn.silu(c)
    cond_c = jax.nn.silu(c_ctx)[None]
    for layer in range(DEPTH):
        last = layer == DEPTH - 1
        mod_x = jnp.split((cond_x @ w_mod[layer] + b_mod[layer])[:, None, :], 6, axis=-1)
        mod_c = jnp.split((cond_c @ w_mod[layer] + b_mod[layer])[:, None, :], 6, axis=-1)
        hx = adaln(x, norm1_g[layer], mod_x[0], mod_x[1])
        hc = adaln(ctx, norm1_g[layer], mod_c[0], mod_c[1])
        mix_x, mix_c = token_mixing(hx, hc, w_in[layer], conv_w[layer], conv_b[layer], b_gates[layer],
                                    attn_sink[layer], g_att[layer], g_ml[layer], w_out[layer], rope,
                                    not last)
        x = x + mod_x[2] * mix_x
        hx = adaln(x, norm2_g[layer], mod_x[3], mod_x[4])
        x = x + mod_x[5] * channel_mix(hx, layer, ffn_w1, ffn_w3, ffn_w2, w_router, b_router,
                                       exp_w1, exp_w3, exp_w2)
        if not last:
            ctx = ctx + mod_c[2] * mix_c
            hc = adaln(ctx, norm2_g[layer], mod_c[3], mod_c[4])
            ctx = ctx + mod_c[5] * channel_mix(hc, layer, ffn_w1, ffn_w3, ffn_w2, w_router, b_router,
                                               exp_w1, exp_w3, exp_w2)
    return rmsnorm(x, final_g)
```

```python
import functools

import jax
import jax.numpy as jnp
import numpy as np
from jax import lax
from jax.experimental import pallas as pl
from jax.experimental.pallas import tpu as pltpu

F32 = jnp.float32
BF16 = jnp.bfloat16

GRID_W = 64
ATT_HEADS = 8
ATT_KV_HEADS = 2
ATT_HEAD_DIM = 64
ATT_GROUP = ATT_HEADS // ATT_KV_HEADS
WINDOW = 128
ATT_BLOCK = 128
ROPE_THETA = 10000.0
ML_HEADS = 4
ML_QK_DIM = 64
ML_V_DIM = 128
ML_CONV = 5
GATE_CAP = 15.0
ATT_WIDTH = ATT_HEADS * ATT_HEAD_DIM
ATT_KV_WIDTH = ATT_KV_HEADS * ATT_HEAD_DIM
ML_QK_WIDTH = ML_HEADS * ML_QK_DIM
ML_WIDTH = ML_HEADS * ML_V_DIM
ML_GATES = 4 * ML_HEADS
MAIN_WIDTH = ATT_WIDTH + 2 * ATT_KV_WIDTH + 2 * ML_QK_WIDTH + 2 * ML_WIDTH
N_EXPERTS = 8
EPS = 1e-6

LANES = 128
SUBLANES = 8
VMEM_LIMIT = 56 * 1024 * 1024
NEG = -1e30

ROW_TILE = 512
MOE_ROW_TILE = 256
ML_CHUNK = 128
CONV_HALO = SUBLANES
MOD_ROWS = 16
MOD_COL_TILE = 1536


def _dot(a, b):
    return jnp.dot(a, b, preferred_element_type=F32)


def _dot_nt(a, b):
    return lax.dot_general(a, b, (((1,), (1,)), ((), ())), preferred_element_type=F32)


def _dot_tn(a, b):
    return lax.dot_general(a, b, (((0,), (0,)), ((), ())), preferred_element_type=F32)


def _dot_f32(a, b):
    return jnp.dot(a, b, preferred_element_type=F32, precision=lax.Precision.HIGHEST)


def _sigmoid(x):
    return 1.0 / (1.0 + jnp.exp(-x))


def _rms(x, g):
    return x * lax.rsqrt(jnp.mean(x * x, axis=-1, keepdims=True) + EPS) * g


def _adaln(x, g, shift, scale):
    return _rms(x, g) * (1.0 + scale) + shift


def _params(n_axes):
    return pltpu.CompilerParams(dimension_semantics=("arbitrary",) * n_axes,
                                vmem_limit_bytes=VMEM_LIMIT)


def _resident(shape):
    zeros = (0,) * len(shape)
    return pl.BlockSpec(shape, lambda *_: zeros, pipeline_mode=pl.Buffered(1))


def _mod_kernel(c_ref, w_ref, b_ref, o_ref):
    c = c_ref[...]
    a = (c * _sigmoid(c)).astype(BF16)
    o_ref[0] = _dot(a, w_ref[0].astype(BF16)) + b_ref[0]


def _modulation(cond, w_mod, b_mod):
    depth, d, width = w_mod.shape
    tn = MOD_COL_TILE if width % MOD_COL_TILE == 0 else width
    return pl.pallas_call(
        _mod_kernel,
        grid=(depth, width // tn),
        in_specs=[pl.BlockSpec((MOD_ROWS, d), lambda l, j: (0, 0)),
                  pl.BlockSpec((1, d, tn), lambda l, j: (l, 0, j)),
                  pl.BlockSpec((1, 1, tn), lambda l, j: (l, 0, j))],
        out_specs=pl.BlockSpec((1, MOD_ROWS, tn), lambda l, j: (l, 0, j)),
        out_shape=jax.ShapeDtypeStruct((depth, MOD_ROWS, width), F32),
        compiler_params=_params(2),
        name="modulation",
    )(cond, w_mod, b_mod.reshape(depth, 1, width))


def _gate_act(u, is_forget):
    g = GATE_CAP * jnp.tanh(u / GATE_CAP)
    log_sig = jnp.minimum(g, 0.0) - jnp.log1p(jnp.exp(-jnp.abs(g)))
    return jnp.where(is_forget, log_sig, g)


def _inproj_kernel(x_ref, mod_ref, g_ref, w_ref, wgc_ref, wgr_ref, bgc_ref, bgr_ref,
                   cos_ref, sa_ref, sb_ref,
                   qa_ref, ka_ref, va_ref, qk_ref, vm_ref, om_ref, gc_ref, gr_ref):
    mod = mod_ref[0]
    hx = _adaln(x_ref[...], g_ref[...], mod[0:1], mod[1:2]).astype(BF16)
    cos, sa, sb = cos_ref[...], sa_ref[...], sb_ref[...]
    quarter = ATT_HEAD_DIM // 4

    def rope(u):
        return (u * cos + pltpu.roll(u, LANES - quarter, 1) * sa + pltpu.roll(u, quarter, 1) * sb)

    c0 = 0
    q = _dot(hx, w_ref[:, c0:c0 + ATT_WIDTH])
    for j in range(ATT_WIDTH // LANES):
        sl = slice(j * LANES, (j + 1) * LANES)
        qa_ref[:, sl] = (rope(q[:, sl]) * (ATT_HEAD_DIM ** -0.5)).astype(BF16)
    c0 += ATT_WIDTH
    kv = _dot(hx, w_ref[:, c0:c0 + 2 * ATT_KV_WIDTH])
    ka_ref[...] = rope(kv[:, :ATT_KV_WIDTH]).astype(BF16)
    va_ref[...] = kv[:, ATT_KV_WIDTH:].astype(BF16)
    c0 += 2 * ATT_KV_WIDTH
    qk_ref[...] = _dot(hx, w_ref[:, c0:c0 + 2 * ML_QK_WIDTH])
    c0 += 2 * ML_QK_WIDTH
    vm_ref[...] = _dot(hx, w_ref[:, c0:c0 + ML_WIDTH]).astype(BF16)
    c0 += ML_WIDTH
    om_ref[...] = _dot(hx, w_ref[:, c0:c0 + ML_WIDTH]).astype(BF16)
    gc = _dot(hx, wgc_ref[...]) + bgc_ref[...]
    lane = lax.broadcasted_iota(jnp.int32, gc.shape, 1)
    gc_ref[...] = _gate_act(gc, (lane // ML_HEADS) % 2 == 1)
    gr = _dot_nt(wgr_ref[...], hx) + bgr_ref[...]
    sub = lax.broadcasted_iota(jnp.int32, gr.shape, 0)
    gr_ref[...] = _gate_act(gr, (sub // ML_HEADS) % 2 == 1)


def _input_projection(xs, mod, g1, w_main, w_gc, w_gr, b_gc, b_gr, rope, seg):
    n, d = xs.shape
    tm = seg["tm"]
    nlat = seg["n_lat"] // tm
    s_tiles = seg["S"] // tm
    row = lambda i: (i, 0)
    mod_map = lambda i: (jnp.where(i < nlat, i // s_tiles, seg["B"]), 0, 0)
    rope_map = lambda i: (jnp.where(i < nlat, i % s_tiles, s_tiles), 0)
    widths = [(ATT_WIDTH, BF16), (ATT_KV_WIDTH, BF16), (ATT_KV_WIDTH, BF16), (2 * ML_QK_WIDTH, F32),
              (ML_WIDTH, BF16), (ML_WIDTH, BF16), (LANES, F32)]
    out_shape = [jax.ShapeDtypeStruct((n, w), t) for w, t in widths]
    out_specs = [pl.BlockSpec((tm, w), row) for w, _ in widths]
    out_shape.append(jax.ShapeDtypeStruct((ML_GATES, n), F32))
    out_specs.append(pl.BlockSpec((ML_GATES, tm), lambda i: (0, i)))
    return pl.pallas_call(
        _inproj_kernel,
        grid=(n // tm,),
        in_specs=[pl.BlockSpec((tm, d), row),
                  pl.BlockSpec((1, 6, d), mod_map),
                  _resident((1, d)),
                  _resident(w_main.shape), _resident(w_gc.shape), _resident(w_gr.shape),
                  _resident(b_gc.shape), _resident(b_gr.shape),
                  pl.BlockSpec((tm, LANES), rope_map),
                  pl.BlockSpec((tm, LANES), rope_map),
                  pl.BlockSpec((tm, LANES), rope_map)],
        out_specs=out_specs,
        out_shape=out_shape,
        compiler_params=_params(1),
        name="input_projection",
    )(xs, mod, g1, w_main, w_gc, w_gr, b_gc, b_gr, *rope)


def _rope_tables(s, tm):
    quarter = ATT_HEAD_DIM // 4
    t = jnp.arange(s)
    row = (t // GRID_W).astype(F32)
    col = (t % GRID_W).astype(F32)
    inv = ROPE_THETA ** (-jnp.arange(quarter, dtype=F32) / quarter)
    ang_r = row[:, None] * inv[None, :]
    ang_c = col[:, None] * inv[None, :]
    zero = jnp.zeros_like(ang_r)
    cos = jnp.concatenate([jnp.cos(ang_r)] * 2 + [jnp.cos(ang_c)] * 2, axis=1)
    sin_up = jnp.concatenate([-jnp.sin(ang_r), zero, -jnp.sin(ang_c), zero], axis=1)
    sin_dn = jnp.concatenate([zero, jnp.sin(ang_r), zero, jnp.sin(ang_c)], axis=1)
    reps = LANES // ATT_HEAD_DIM
    ident = [jnp.ones((tm, LANES), F32), jnp.zeros((tm, LANES), F32), jnp.zeros((tm, LANES), F32)]
    return tuple(jnp.concatenate([jnp.tile(a, (1, reps)), i], axis=0)
                 for a, i in zip((cos, sin_up, sin_dn), ident))


def _attn_kernel(sink_ref, q_ref, kp_ref, kc_ref, kn_ref, vp_ref, vc_ref, vn_ref, kx_ref, vx_ref,
                 g_ref, o_ref, *, n_lat_blocks):
    j = pl.program_id(1)
    is_lat = j < n_lat_blocks
    blk = ATT_BLOCK
    rows = lax.broadcasted_iota(jnp.int32, (blk, blk), 0)
    cols = lax.broadcasted_iota(jnp.int32, (blk, blk), 1)
    ok_p = jnp.logical_and(cols >= rows, jnp.logical_and(is_lat, j > 0))
    ok_c = jnp.logical_and(cols >= 0, is_lat)
    ok_n = jnp.logical_and(cols <= rows, jnp.logical_and(is_lat, j < n_lat_blocks - 1))
    n_ctx = kx_ref.shape[0]
    bias = jnp.concatenate([jnp.where(ok_p, 0.0, NEG), jnp.where(ok_c, 0.0, NEG),
                            jnp.where(ok_n, 0.0, NEG), jnp.zeros((blk, n_ctx), F32)], axis=1)
    q = q_ref[...]
    k_all = jnp.concatenate([kp_ref[...], kc_ref[...], kn_ref[...], kx_ref[...]], axis=0)
    v_all = jnp.concatenate([vp_ref[...], vc_ref[...], vn_ref[...], vx_ref[...]], axis=0)
    dh = ATT_HEAD_DIM
    outs = []
    for h in range(ATT_KV_HEADS):
        k_h = k_all[:, h * dh:(h + 1) * dh]
        v_h = v_all[:, h * dh:(h + 1) * dh]
        q_h = jnp.concatenate([q[:, (h * ATT_GROUP + g) * dh:(h * ATT_GROUP + g + 1) * dh]
                               for g in range(ATT_GROUP)], axis=0)
        s_all = _dot_nt(q_h, k_h)
        p_parts, inv_parts = [], []
        for g in range(ATT_GROUP):
            sink = sink_ref[h * ATT_GROUP + g]
            s = s_all[g * blk:(g + 1) * blk] + bias
            m = jnp.maximum(jnp.max(s, axis=-1, keepdims=True), sink)
            p = jnp.exp(s - m)
            denom = jnp.sum(p, axis=-1, keepdims=True) + jnp.exp(sink - m)
            p_parts.append(p.astype(BF16))
            inv_parts.append(1.0 / denom)
        o = _dot(jnp.concatenate(p_parts, axis=0), v_h)
        for g in range(ATT_GROUP):
            outs.append(o[g * blk:(g + 1) * blk] * inv_parts[g])
    att = jnp.concatenate(outs, axis=1)
    o_ref[...] = _rms(att, g_ref[...]).astype(o_ref.dtype)


def _attention(qa, ka, va, sink, g_att, seg, with_ctx):
    n = qa.shape[0]
    b_, s_, lc = seg["B"], seg["S"], seg["Lc"]
    blk = ATT_BLOCK
    nqb = s_ // blk
    ncb = lc // blk if with_ctx else 0
    lat_blocks = b_ * nqb

    def q_map(b, j, *_):
        return (jnp.where(j < nqb, b * nqb + j, lat_blocks + b * (lc // blk) + (j - nqb)), 0)

    def win_map(off):
        def f(b, j, *_):
            jj = jnp.clip(jnp.where(j < nqb, j, 0) + off, 0, nqb - 1)
            return (b * nqb + jj, 0)
        return f

    ctx_map = lambda b, j, *_: (b_ * s_ // lc + b, 0)
    kv_specs = [pl.BlockSpec((blk, ATT_KV_WIDTH), win_map(o)) for o in (-1, 0, 1)]
    grid_spec = pltpu.PrefetchScalarGridSpec(
        num_scalar_prefetch=1,
        grid=(b_, nqb + ncb),
        in_specs=[pl.BlockSpec((blk, ATT_WIDTH), q_map)] + kv_specs + kv_specs
                 + [pl.BlockSpec((lc, ATT_KV_WIDTH), ctx_map), pl.BlockSpec((lc, ATT_KV_WIDTH), ctx_map),
                    pl.BlockSpec((1, ATT_WIDTH), lambda b, j, *_: (0, 0))],
        out_specs=pl.BlockSpec((blk, ATT_WIDTH), q_map),
    )
    return pl.pallas_call(
        functools.partial(_attn_kernel, n_lat_blocks=nqb),
        grid_spec=grid_spec,
        out_shape=jax.ShapeDtypeStruct((n, ATT_WIDTH), BF16),
        compiler_params=_params(2),
        name="window_attention",
    )(sink, qa, ka, ka, ka, va, va, va, ka, va, g_att)


def _mlstm_kernel(*refs, reverse, n_ctx_chunks, n_lat_chunks):
    if reverse:
        (qkp_ref, qkc_ref, qkn_ref, v_ref, gc_ref, gr_ref, cw_ref, cb_ref,
         out_ref, c_ref, n_ref) = refs
    else:
        (qkp_ref, qkc_ref, qkn_ref, v_ref, gc_ref, gr_ref, cw_ref, cb_ref,
         om_ref, hb_ref, gml_ref, out_ref, c_ref, n_ref) = refs
    step = pl.program_id(1)
    chunk = qkc_ref.shape[0]

    @pl.when(step == 0)
    def _():
        c_ref[...] = jnp.zeros_like(c_ref)
        n_ref[...] = jnp.zeros_like(n_ref)

    in_ctx = step < n_ctx_chunks
    pos = jnp.where(in_ctx, step, step - n_ctx_chunks)
    n_seq = jnp.where(in_ctx, n_ctx_chunks, n_lat_chunks)
    orig = n_seq - 1 - pos if reverse else pos
    has_prev = (orig > 0).astype(F32)
    has_next = (orig < n_seq - 1).astype(F32)

    xe = jnp.concatenate([qkp_ref[...] * has_prev, qkc_ref[...], qkn_ref[...] * has_next], axis=0)
    n_ext = xe.shape[0]
    cw = cw_ref[...]
    y = cb_ref[...] + cw[ML_CONV // 2:ML_CONV // 2 + 1] * xe[CONV_HALO:CONV_HALO + chunk]
    for tap in range(ML_CONV):
        shift = ML_CONV // 2 - tap
        if shift == 0:
            continue
        y = y + cw[tap:tap + 1] * pltpu.roll(xe, shift % n_ext, 0)[CONV_HALO:CONV_HALO + chunk]
    y = y * _sigmoid(y)
    q_all = (y[:, :ML_QK_WIDTH] * (ML_QK_DIM ** -0.5)).astype(BF16)
    k_all = y[:, ML_QK_WIDTH:]

    rows = lax.broadcasted_iota(jnp.int32, (chunk, chunk), 0)
    cols = lax.broadcasted_iota(jnp.int32, (chunk, chunk), 1)
    lower = rows >= cols
    upper = rows <= cols
    seen = upper if reverse else lower
    gc = gc_ref[...]
    gr = gr_ref[...]
    b_col = _dot_f32(seen.astype(F32), gc)
    b_row = _dot_f32(gr, (lower if reverse else upper).astype(F32))
    b_end = jnp.sum(gc, axis=0, keepdims=True)
    base = 2 * ML_HEADS if reverse else 0

    for h in range(ML_HEADS):
        i_idx = base + h
        f_idx = base + ML_HEADS + h
        q_h = q_all[:, h * ML_QK_DIM:(h + 1) * ML_QK_DIM]
        k_h = k_all[:, h * ML_QK_DIM:(h + 1) * ML_QK_DIM]
        v_h = v_ref[:, h * ML_V_DIM:(h + 1) * ML_V_DIM]
        bc = b_col[:, f_idx:f_idx + 1]
        d = bc - b_row[f_idx:f_idx + 1, :] + gr[i_idx:i_idx + 1, :]
        w = jnp.exp(jnp.where(seen, d, NEG))
        s = _dot_nt(q_h, k_h.astype(BF16)) * w
        state_c = c_ref[h]
        state_n = n_ref[h:h + 1, :]
        carry = jnp.exp(bc)
        num = _dot(s.astype(BF16), v_h) + carry * _dot(q_h, state_c.astype(BF16))
        den = (jnp.sum(s, axis=-1, keepdims=True)
               + carry * jnp.sum(q_h.astype(F32) * state_n, axis=-1, keepdims=True))
        h_out = num / jnp.maximum(jnp.abs(den), 1.0)

        be = b_end[:, f_idx:f_idx + 1]
        kw = k_h * jnp.exp(be - bc + gc[:, i_idx:i_idx + 1])
        decay = jnp.exp(be)
        c_ref[h] = decay * state_c + _dot_tn(kw.astype(BF16), v_h)
        n_ref[h:h + 1, :] = decay * state_n + jnp.sum(kw, axis=0, keepdims=True)

        sl = slice(h * ML_V_DIM, (h + 1) * ML_V_DIM)
        if reverse:
            out_ref[:, sl] = h_out
        else:
            tot = h_out + hb_ref[:, sl]
            hn = _rms(tot, gml_ref[:, sl])
            out_ref[:, sl] = (hn * _sigmoid(om_ref[:, sl].astype(F32))).astype(out_ref.dtype)


def _mlstm(qk, vm, gc, gr, conv_w, conv_b, seg, *, reverse, om=None, hb=None, g_ml=None):
    n = qk.shape[0]
    b_, s_, lc = seg["B"], seg["S"], seg["Lc"]
    chunk = ML_CHUNK
    ncc, ncl = lc // chunk, s_ // chunk
    lat_chunks = b_ * ncl
    per_chunk = chunk // CONV_HALO
    n_halo = n // CONV_HALO

    def blk(b, c):
        pos = jnp.where(c < ncc, c, c - ncc)
        if reverse:
            pos = jnp.where(c < ncc, ncc, ncl) - 1 - pos
        return jnp.where(c < ncc, lat_chunks + b * ncc + pos, b * ncl + pos)

    cur = lambda b, c: (blk(b, c), 0)
    prev = lambda b, c: (jnp.maximum(blk(b, c) * per_chunk - 1, 0), 0)
    nxt = lambda b, c: (jnp.minimum((blk(b, c) + 1) * per_chunk, n_halo - 1), 0)
    const = lambda b, c: (0, 0)
    in_specs = [pl.BlockSpec((CONV_HALO, 2 * ML_QK_WIDTH), prev),
                pl.BlockSpec((chunk, 2 * ML_QK_WIDTH), cur),
                pl.BlockSpec((CONV_HALO, 2 * ML_QK_WIDTH), nxt),
                pl.BlockSpec((chunk, ML_WIDTH), cur),
                pl.BlockSpec((chunk, LANES), cur),
                pl.BlockSpec((ML_GATES, chunk), lambda b, c: (0, blk(b, c))),
                pl.BlockSpec(conv_w.shape, const),
                pl.BlockSpec(conv_b.shape, const)]
    args = [qk, qk, qk, vm, gc, gr, conv_w, conv_b]
    if reverse:
        out_dtype = F32
    else:
        in_specs += [pl.BlockSpec((chunk, ML_WIDTH), cur), pl.BlockSpec((chunk, ML_WIDTH), cur),
                     pl.BlockSpec((1, ML_WIDTH), const)]
        args += [om, hb, g_ml]
        out_dtype = BF16
    return pl.pallas_call(
        functools.partial(_mlstm_kernel, reverse=reverse, n_ctx_chunks=ncc, n_lat_chunks=ncl),
        grid=(b_, ncc + ncl),
        in_specs=in_specs,
        out_specs=pl.BlockSpec((chunk, ML_WIDTH), cur),
        out_shape=jax.ShapeDtypeStruct((n, ML_WIDTH), out_dtype),
        scratch_shapes=[pltpu.VMEM((ML_HEADS, ML_QK_DIM, ML_V_DIM), F32),
                        pltpu.VMEM((ML_HEADS, ML_QK_DIM), F32)],
        compiler_params=_params(2),
        name="mlstm_bwd" if reverse else "mlstm_fwd",
    )(*args)


def _swiglu(h, w1, w3, w2):
    a = _dot(h, w1)
    b = _dot(h, w3)
    return _dot((a * _sigmoid(a) * b).astype(BF16), w2)


def _mix_residual(x_ref, att_ref, ml_ref, mod, wo_ref):
    mix = _dot(att_ref[...], wo_ref[:ATT_WIDTH, :]) + _dot(ml_ref[...], wo_ref[ATT_WIDTH:, :])
    return x_ref[...] + mod[2:3] * mix


def _dense_layer_kernel(x_ref, att_ref, ml_ref, mod_ref, g_ref, wo_ref, w1_ref, w3_ref, w2_ref,
                        fg_ref, o_ref, *, final):
    mod = mod_ref[0]
    x1 = _mix_residual(x_ref, att_ref, ml_ref, mod, wo_ref)
    hx = _adaln(x1, g_ref[...], mod[3:4], mod[4:5]).astype(BF16)
    out = x1 + mod[5:6] * _swiglu(hx, w1_ref[...], w3_ref[...], w2_ref[...])
    if final:
        out = _rms(out, fg_ref[...])
    o_ref[...] = out


def _row_maps(seg, tm):
    nlat = seg["n_lat"] // tm
    s_tiles = seg["S"] // tm
    return (lambda i, *_: (i, 0)), (lambda i, *_: (jnp.where(i < nlat, i // s_tiles, seg["B"]), 0, 0))


def _dense_layer(xs, att, ml, mod, g2, wo, w1, w3, w2, final_g, seg, n_rows, final):
    d = xs.shape[1]
    tm = seg["tm"]
    row, mod_map = _row_maps(seg, tm)
    return pl.pallas_call(
        functools.partial(_dense_layer_kernel, final=final),
        grid=(n_rows // tm,),
        in_specs=[pl.BlockSpec((tm, d), row), pl.BlockSpec((tm, ATT_WIDTH), row),
                  pl.BlockSpec((tm, ML_WIDTH), row), pl.BlockSpec((1, 6, d), mod_map),
                  _resident((1, d)), _resident(wo.shape), _resident(w1.shape), _resident(w3.shape),
                  _resident(w2.shape), _resident((1, d))],
        out_specs=pl.BlockSpec((tm, d), row),
        out_shape=jax.ShapeDtypeStruct((n_rows, d), F32),
        compiler_params=_params(1),
        name="dense_layer",
    )(xs, att, ml, mod, g2, wo, w1, w3, w2, final_g)


def _router_kernel(x_ref, att_ref, ml_ref, mod_ref, g_ref, wo_ref, wr_ref, br_ref,
                   x1_ref, h_ref, comb_ref):
    mod = mod_ref[0]
    x1 = _mix_residual(x_ref, att_ref, ml_ref, mod, wo_ref)
    x1_ref[...] = x1
    hx = _adaln(x1, g_ref[...], mod[3:4], mod[4:5])
    h_ref[...] = hx.astype(BF16)
    logits = _dot_f32(hx, wr_ref[...]) + br_ref[...]
    lane = lax.broadcasted_iota(jnp.int32, logits.shape, 1)
    logits = jnp.where(lane < N_EXPERTS, logits, -jnp.inf)
    top1 = jnp.max(logits, axis=-1, keepdims=True)
    idx1 = jnp.min(jnp.where(logits == top1, lane, LANES), axis=-1, keepdims=True)
    rest = jnp.where(lane == idx1, -jnp.inf, logits)
    top2 = jnp.max(rest, axis=-1, keepdims=True)
    idx2 = jnp.min(jnp.where(rest == top2, lane, LANES), axis=-1, keepdims=True)
    e2 = jnp.exp(top2 - top1)
    w_first = 1.0 / (1.0 + e2)
    comb_ref[...] = jnp.where(lane == idx1, w_first, 0.0) + jnp.where(lane == idx2, e2 * w_first, 0.0)


def _router(xs, att, ml, mod, g2, wo, wr, br, seg, n_rows):
    d = xs.shape[1]
    tm = seg["tm"]
    row, mod_map = _row_maps(seg, tm)
    return pl.pallas_call(
        _router_kernel,
        grid=(n_rows // tm,),
        in_specs=[pl.BlockSpec((tm, d), row), pl.BlockSpec((tm, ATT_WIDTH), row),
                  pl.BlockSpec((tm, ML_WIDTH), row), pl.BlockSpec((1, 6, d), mod_map),
                  _resident((1, d)), _resident(wo.shape), _resident(wr.shape), _resident(br.shape)],
        out_specs=[pl.BlockSpec((tm, d), row), pl.BlockSpec((tm, d), row), pl.BlockSpec((tm, LANES), row)],
        out_shape=[jax.ShapeDtypeStruct((n_rows, d), F32), jax.ShapeDtypeStruct((n_rows, d), BF16),
                   jax.ShapeDtypeStruct((n_rows, LANES), F32)],
        compiler_params=_params(1),
        name="mix_router",
    )(xs, att, ml, mod, g2, wo, wr, br)


def _experts_kernel(x1_ref, h_ref, comb_ref, mod_ref, w1_ref, w3_ref, w2_ref, fg_ref, o_ref, acc_ref,
                    *, final):
    e = pl.program_id(1)

    @pl.when(e == 0)
    def _():
        acc_ref[...] = jnp.zeros_like(acc_ref)

    comb = comb_ref[...]
    lane = lax.broadcasted_iota(jnp.int32, comb.shape, 1)
    weight = jnp.sum(jnp.where(lane == e, comb, 0.0), axis=-1, keepdims=True)
    acc_ref[...] += weight * _swiglu(h_ref[...], w1_ref[0], w3_ref[0], w2_ref[0])

    @pl.when(e == pl.num_programs(1) - 1)
    def _():
        out = x1_ref[...] + mod_ref[0][5:6] * acc_ref[...]
        if final:
            out = _rms(out, fg_ref[...])
        o_ref[...] = out


def _experts(x1, h2, comb, mod, w1, w3, w2, final_g, seg, final):
    n_rows, d = x1.shape
    n_exp, _, f = w1.shape
    tm = MOE_ROW_TILE
    row, mod_map = _row_maps(seg, tm)
    return pl.pallas_call(
        functools.partial(_experts_kernel, final=final),
        grid=(n_rows // tm, n_exp),
        in_specs=[pl.BlockSpec((tm, d), row), pl.BlockSpec((tm, d), row),
                  pl.BlockSpec((tm, LANES), row), pl.BlockSpec((1, 6, d), mod_map),
                  pl.BlockSpec((1, d, f), lambda i, e: (e, 0, 0)),
                  pl.BlockSpec((1, d, f), lambda i, e: (e, 0, 0)),
                  pl.BlockSpec((1, f, d), lambda i, e: (e, 0, 0)),
                  pl.BlockSpec((1, d), lambda i, e: (0, 0))],
        out_specs=pl.BlockSpec((tm, d), row),
        out_shape=jax.ShapeDtypeStruct((n_rows, d), F32),
        scratch_shapes=[pltpu.VMEM((tm, d), F32)],
        compiler_params=_params(2),
        name="experts",
    )(x1, h2, comb, mod, w1, w3, w2, final_g)


def kernel(x, c, ctx, c_ctx, norm1_g, norm2_g, w_mod, b_mod, w_in, conv_w, conv_b, b_gates, attn_sink,
           g_att, g_ml, w_out, ffn_w1, ffn_w3, ffn_w2, w_router, b_router, exp_w1, exp_w3, exp_w2,
           final_g):
    b_, s_, d = x.shape
    lc = ctx.shape[1]
    depth = w_in.shape[0]
    n_lat, n_ctx = b_ * s_, b_ * lc
    tm = min(ROW_TILE, s_)
    assert s_ % tm == 0 and n_ctx % tm == 0 and tm % MOE_ROW_TILE == 0
    assert s_ % ML_CHUNK == 0 and lc % ML_CHUNK == 0 and n_lat % lc == 0 and b_ < MOD_ROWS
    seg = dict(B=b_, S=s_, Lc=lc, n_lat=n_lat, tm=tm)

    cond = jnp.zeros((MOD_ROWS, d), F32).at[:b_].set(c).at[b_].set(c_ctx)
    mods = _modulation(cond, w_mod, b_mod).reshape(depth, MOD_ROWS, 6, d)
    rope = _rope_tables(s_, tm)
    xs = jnp.concatenate([x.reshape(n_lat, d), ctx.reshape(n_ctx, d)], axis=0)
    final_row = final_g.reshape(1, d)

    for layer in range(depth):
        last = layer == depth - 1
        w_l = w_in[layer]
        w_gates = w_l[:, MAIN_WIDTH:]
        w_gc = jnp.pad(w_gates, ((0, 0), (0, LANES - ML_GATES))).astype(BF16)
        b_gc = jnp.pad(b_gates[layer], (0, LANES - ML_GATES)).reshape(1, LANES)
        qa, ka, va, qk, vm, om, gc, gr = _input_projection(
            xs, mods[layer], norm1_g[layer].reshape(1, d), w_l[:, :MAIN_WIDTH].astype(BF16), w_gc,
            w_gates.T.astype(BF16), b_gc, b_gates[layer].reshape(ML_GATES, 1), rope, seg)
        att = _attention(qa, ka, va, attn_sink[layer], g_att[layer].reshape(1, ATT_WIDTH), seg, not last)
        cw, cb = conv_w[layer], conv_b[layer].reshape(1, -1)
        hb = _mlstm(qk, vm, gc, gr, cw, cb, seg, reverse=True)
        ml = _mlstm(qk, vm, gc, gr, cw, cb, seg, reverse=False, om=om, hb=hb,
                    g_ml=g_ml[layer].reshape(1, ML_WIDTH))
        n_rows = n_lat if last else n_lat + n_ctx
        g2 = norm2_g[layer].reshape(1, d)
        wo = w_out[layer].astype(BF16)
        i = layer // 2
        if layer % 2 == 0:
            xs = _dense_layer(xs, att, ml, mods[layer], g2, wo, ffn_w1[i].astype(BF16),
                              ffn_w3[i].astype(BF16), ffn_w2[i].astype(BF16), final_row, seg, n_rows, last)
        else:
            wr = jnp.pad(w_router[i], ((0, 0), (0, LANES - N_EXPERTS)))
            br = jnp.pad(b_router[i], (0, LANES - N_EXPERTS)).reshape(1, LANES)
            x1, h2, comb = _router(xs, att, ml, mods[layer], g2, wo, wr, br, seg, n_rows)
            xs = _experts(x1, h2, comb, mods[layer], exp_w1[i].astype(BF16), exp_w3[i].astype(BF16),
                          exp_w2[i].astype(BF16), final_row, seg, last)
    return xs[:n_lat].reshape(b_, s_, d)
```

```python
import functools

import jax
import jax.numpy as jnp
import numpy as np
from jax import lax
from jax.experimental import pallas as pl
from jax.experimental.pallas import tpu as pltpu
from jax.experimental.pallas import tpu_sc as plsc

F32 = jnp.float32
BF16 = jnp.bfloat16

GRID_W = 64
ATT_HEADS = 8
ATT_KV_HEADS = 2
ATT_HEAD_DIM = 64
ATT_GROUP = ATT_HEADS // ATT_KV_HEADS
WINDOW = 128
ATT_BLOCK = 128
ROPE_THETA = 10000.0
ML_HEADS = 4
ML_QK_DIM = 64
ML_V_DIM = 128
ML_CONV = 5
GATE_CAP = 15.0
ATT_WIDTH = ATT_HEADS * ATT_HEAD_DIM
ATT_KV_WIDTH = ATT_KV_HEADS * ATT_HEAD_DIM
ML_QK_WIDTH = ML_HEADS * ML_QK_DIM
ML_WIDTH = ML_HEADS * ML_V_DIM
ML_GATES = 4 * ML_HEADS
MAIN_WIDTH = ATT_WIDTH + 2 * ATT_KV_WIDTH + 2 * ML_QK_WIDTH + 2 * ML_WIDTH
N_EXPERTS = 8
TOP_K = 2
EPS = 1e-6

LANES = 128
SUBLANES = 8
VMEM_LIMIT = 56 * 1024 * 1024
NEG = -1e30
SC_CORES = 2
SC_SUBCORES = 16
SC_GATHER_ROWS = 64

ROW_TILE = 512
MOE_ROW_TILE = 512
FFN_CHUNKS = 2
ML_CHUNK = 128
CONV_HALO = SUBLANES
MOD_ROWS = 16
MOD_COL_TILE = 1536


def _dot(a, b):
    return jnp.dot(a, b, preferred_element_type=F32)


def _dot_nt(a, b):
    return lax.dot_general(a, b, (((1,), (1,)), ((), ())), preferred_element_type=F32)


def _dot_tn(a, b):
    return lax.dot_general(a, b, (((0,), (0,)), ((), ())), preferred_element_type=F32)


def _dot_f32(a, b):
    return jnp.dot(a, b, preferred_element_type=F32, precision=lax.Precision.HIGHEST)


def _sigmoid(x):
    return 1.0 / (1.0 + jnp.exp(-x))


def _rms(x, g):
    return x * lax.rsqrt(jnp.mean(x * x, axis=-1, keepdims=True) + EPS) * g


def _adaln(x, g, shift, scale):
    return _rms(x, g) * (1.0 + scale) + shift


def _params(n_axes):
    return pltpu.CompilerParams(dimension_semantics=("arbitrary",) * n_axes,
                                vmem_limit_bytes=VMEM_LIMIT)


def _resident(shape):
    zeros = (0,) * len(shape)
    return pl.BlockSpec(shape, lambda *_: zeros, pipeline_mode=pl.Buffered(1))


def _mod_kernel(c_ref, w_ref, b_ref, o_ref):
    c = c_ref[...]
    a = (c * _sigmoid(c)).astype(BF16)
    o_ref[0] = _dot(a, w_ref[0].astype(BF16)) + b_ref[0]


def _modulation(cond, w_mod, b_mod):
    depth, d, width = w_mod.shape
    tn = MOD_COL_TILE if width % MOD_COL_TILE == 0 else width
    return pl.pallas_call(
        _mod_kernel,
        grid=(depth, width // tn),
        in_specs=[pl.BlockSpec((MOD_ROWS, d), lambda l, j: (0, 0)),
                  pl.BlockSpec((1, d, tn), lambda l, j: (l, 0, j)),
                  pl.BlockSpec((1, 1, tn), lambda l, j: (l, 0, j))],
        out_specs=pl.BlockSpec((1, MOD_ROWS, tn), lambda l, j: (l, 0, j)),
        out_shape=jax.ShapeDtypeStruct((depth, MOD_ROWS, width), F32),
        compiler_params=_params(2),
        name="modulation",
    )(cond, w_mod, b_mod.reshape(depth, 1, width))


def _gate_act(u, is_forget):
    g = GATE_CAP * jnp.tanh(u / GATE_CAP)
    log_sig = jnp.minimum(g, 0.0) - jnp.log1p(jnp.exp(-jnp.abs(g)))
    return jnp.where(is_forget, log_sig, g)


def _inproj_kernel(x_ref, mod_ref, g_ref, w_ref, wgc_ref, wgr_ref, bgc_ref, bgr_ref,
                   cos_ref, sa_ref, sb_ref,
                   qa_ref, ka_ref, va_ref, qk_ref, vm_ref, om_ref, gc_ref, gr_ref):
    mod = mod_ref[0]
    hx = _adaln(x_ref[...], g_ref[...], mod[0:1], mod[1:2]).astype(BF16)
    cos, sa, sb = cos_ref[...], sa_ref[...], sb_ref[...]
    quarter = ATT_HEAD_DIM // 4

    def rope(u):
        return (u * cos + pltpu.roll(u, LANES - quarter, 1) * sa + pltpu.roll(u, quarter, 1) * sb)

    c0 = 0
    q = _dot(hx, w_ref[:, c0:c0 + ATT_WIDTH])
    for j in range(ATT_WIDTH // LANES):
        sl = slice(j * LANES, (j + 1) * LANES)
        qa_ref[:, sl] = (rope(q[:, sl]) * (ATT_HEAD_DIM ** -0.5)).astype(BF16)
    c0 += ATT_WIDTH
    kv = _dot(hx, w_ref[:, c0:c0 + 2 * ATT_KV_WIDTH])
    ka_ref[...] = rope(kv[:, :ATT_KV_WIDTH]).astype(BF16)
    va_ref[...] = kv[:, ATT_KV_WIDTH:].astype(BF16)
    c0 += 2 * ATT_KV_WIDTH
    qk_ref[...] = _dot(hx, w_ref[:, c0:c0 + 2 * ML_QK_WIDTH])
    c0 += 2 * ML_QK_WIDTH
    vm_ref[...] = _dot(hx, w_ref[:, c0:c0 + ML_WIDTH]).astype(BF16)
    c0 += ML_WIDTH
    om_ref[...] = _dot(hx, w_ref[:, c0:c0 + ML_WIDTH]).astype(BF16)
    gc = _dot(hx, wgc_ref[...]) + bgc_ref[...]
    lane = lax.broadcasted_iota(jnp.int32, gc.shape, 1)
    gc_ref[...] = _gate_act(gc, (lane // ML_HEADS) % 2 == 1)
    gr = _dot_nt(wgr_ref[...], hx) + bgr_ref[...]
    sub = lax.broadcasted_iota(jnp.int32, gr.shape, 0)
    gr_ref[...] = _gate_act(gr, (sub // ML_HEADS) % 2 == 1)


def _input_projection(xs, mod, g1, w_main, w_gc, w_gr, b_gc, b_gr, rope, seg):
    n, d = xs.shape
    tm = seg["tm"]
    nlat = seg["n_lat"] // tm
    s_tiles = seg["S"] // tm
    row = lambda i: (i, 0)
    mod_map = lambda i: (jnp.where(i < nlat, i // s_tiles, seg["B"]), 0, 0)
    rope_map = lambda i: (jnp.where(i < nlat, i % s_tiles, s_tiles), 0)
    widths = [(ATT_WIDTH, BF16), (ATT_KV_WIDTH, BF16), (ATT_KV_WIDTH, BF16), (2 * ML_QK_WIDTH, F32),
              (ML_WIDTH, BF16), (ML_WIDTH, BF16), (LANES, F32)]
    out_shape = [jax.ShapeDtypeStruct((n, w), t) for w, t in widths]
    out_specs = [pl.BlockSpec((tm, w), row) for w, _ in widths]
    out_shape.append(jax.ShapeDtypeStruct((ML_GATES, n), F32))
    out_specs.append(pl.BlockSpec((ML_GATES, tm), lambda i: (0, i)))
    return pl.pallas_call(
        _inproj_kernel,
        grid=(n // tm,),
        in_specs=[pl.BlockSpec((tm, d), row),
                  pl.BlockSpec((1, 6, d), mod_map),
                  _resident((1, d)),
                  _resident(w_main.shape), _resident(w_gc.shape), _resident(w_gr.shape),
                  _resident(b_gc.shape), _resident(b_gr.shape),
                  pl.BlockSpec((tm, LANES), rope_map),
                  pl.BlockSpec((tm, LANES), rope_map),
                  pl.BlockSpec((tm, LANES), rope_map)],
        out_specs=out_specs,
        out_shape=out_shape,
        compiler_params=_params(1),
        name="input_projection",
    )(xs, mod, g1, w_main, w_gc, w_gr, b_gc, b_gr, *rope)


def _rope_tables(s, tm):
    quarter = ATT_HEAD_DIM // 4
    t = jnp.arange(s)
    row = (t // GRID_W).astype(F32)
    col = (t % GRID_W).astype(F32)
    inv = ROPE_THETA ** (-jnp.arange(quarter, dtype=F32) / quarter)
    ang_r = row[:, None] * inv[None, :]
    ang_c = col[:, None] * inv[None, :]
    zero = jnp.zeros_like(ang_r)
    cos = jnp.concatenate([jnp.cos(ang_r)] * 2 + [jnp.cos(ang_c)] * 2, axis=1)
    sin_up = jnp.concatenate([-jnp.sin(ang_r), zero, -jnp.sin(ang_c), zero], axis=1)
    sin_dn = jnp.concatenate([zero, jnp.sin(ang_r), zero, jnp.sin(ang_c)], axis=1)
    reps = LANES // ATT_HEAD_DIM
    ident = [jnp.ones((tm, LANES), F32), jnp.zeros((tm, LANES), F32), jnp.zeros((tm, LANES), F32)]
    return tuple(jnp.concatenate([jnp.tile(a, (1, reps)), i], axis=0)
                 for a, i in zip((cos, sin_up, sin_dn), ident))


def _attn_kernel(sink_ref, q_ref, kp_ref, kc_ref, kn_ref, vp_ref, vc_ref, vn_ref, kx_ref, vx_ref,
                 g_ref, o_ref, *, n_lat_blocks):
    j = pl.program_id(1)
    is_lat = j < n_lat_blocks
    blk = ATT_BLOCK
    rows = lax.broadcasted_iota(jnp.int32, (blk, blk), 0)
    cols = lax.broadcasted_iota(jnp.int32, (blk, blk), 1)
    ok_p = jnp.logical_and(cols >= rows, jnp.logical_and(is_lat, j > 0))
    ok_c = jnp.logical_and(cols >= 0, is_lat)
    ok_n = jnp.logical_and(cols <= rows, jnp.logical_and(is_lat, j < n_lat_blocks - 1))
    n_ctx = kx_ref.shape[0]
    bias = jnp.concatenate([jnp.where(ok_p, 0.0, NEG), jnp.where(ok_c, 0.0, NEG),
                            jnp.where(ok_n, 0.0, NEG), jnp.zeros((blk, n_ctx), F32)], axis=1)
    q = q_ref[...]
    k_all = jnp.concatenate([kp_ref[...], kc_ref[...], kn_ref[...], kx_ref[...]], axis=0)
    v_all = jnp.concatenate([vp_ref[...], vc_ref[...], vn_ref[...], vx_ref[...]], axis=0)
    dh = ATT_HEAD_DIM
    outs = []
    for h in range(ATT_KV_HEADS):
        k_h = k_all[:, h * dh:(h + 1) * dh]
        v_h = v_all[:, h * dh:(h + 1) * dh]
        q_h = jnp.concatenate([q[:, (h * ATT_GROUP + g) * dh:(h * ATT_GROUP + g + 1) * dh]
                               for g in range(ATT_GROUP)], axis=0)
        s_all = _dot_nt(q_h, k_h)
        p_parts, inv_parts = [], []
        for g in range(ATT_GROUP):
            sink = sink_ref[h * ATT_GROUP + g]
            s = s_all[g * blk:(g + 1) * blk] + bias
            m = jnp.maximum(jnp.max(s, axis=-1, keepdims=True), sink)
            p = jnp.exp(s - m)
            denom = jnp.sum(p, axis=-1, keepdims=True) + jnp.exp(sink - m)
            p_parts.append(p.astype(BF16))
            inv_parts.append(1.0 / denom)
        o = _dot(jnp.concatenate(p_parts, axis=0), v_h)
        for g in range(ATT_GROUP):
            outs.append(o[g * blk:(g + 1) * blk] * inv_parts[g])
    att = jnp.concatenate(outs, axis=1)
    o_ref[...] = _rms(att, g_ref[...]).astype(o_ref.dtype)


def _attention(qa, ka, va, sink, g_att, seg, with_ctx):
    n = qa.shape[0]
    b_, s_, lc = seg["B"], seg["S"], seg["Lc"]
    blk = ATT_BLOCK
    nqb = s_ // blk
    ncb = lc // blk if with_ctx else 0
    lat_blocks = b_ * nqb

    def q_map(b, j, *_):
        return (jnp.where(j < nqb, b * nqb + j, lat_blocks + b * (lc // blk) + (j - nqb)), 0)

    def win_map(off):
        def f(b, j, *_):
            jj = jnp.clip(jnp.where(j < nqb, j, 0) + off, 0, nqb - 1)
            return (b * nqb + jj, 0)
        return f

    ctx_map = lambda b, j, *_: (b_ * s_ // lc + b, 0)
    kv_specs = [pl.BlockSpec((blk, ATT_KV_WIDTH), win_map(o)) for o in (-1, 0, 1)]
    grid_spec = pltpu.PrefetchScalarGridSpec(
        num_scalar_prefetch=1,
        grid=(b_, nqb + ncb),
        in_specs=[pl.BlockSpec((blk, ATT_WIDTH), q_map)] + kv_specs + kv_specs
                 + [pl.BlockSpec((lc, ATT_KV_WIDTH), ctx_map), pl.BlockSpec((lc, ATT_KV_WIDTH), ctx_map),
                    pl.BlockSpec((1, ATT_WIDTH), lambda b, j, *_: (0, 0))],
        out_specs=pl.BlockSpec((blk, ATT_WIDTH), q_map),
    )
    return pl.pallas_call(
        functools.partial(_attn_kernel, n_lat_blocks=nqb),
        grid_spec=grid_spec,
        out_shape=jax.ShapeDtypeStruct((n, ATT_WIDTH), BF16),
        compiler_params=_params(2),
        name="window_attention",
    )(sink, qa, ka, ka, ka, va, va, va, ka, va, g_att)


def _mlstm_kernel(*refs, reverse, n_ctx_chunks, n_lat_chunks):
    if reverse:
        (qkp_ref, qkc_ref, qkn_ref, v_ref, gc_ref, gr_ref, cw_ref, cb_ref,
         out_ref, c_ref, n_ref) = refs
    else:
        (qkp_ref, qkc_ref, qkn_ref, v_ref, gc_ref, gr_ref, cw_ref, cb_ref,
         om_ref, hb_ref, gml_ref, out_ref, c_ref, n_ref) = refs
    step = pl.program_id(1)
    chunk = qkc_ref.shape[0]

    @pl.when(step == 0)
    def _():
        c_ref[...] = jnp.zeros_like(c_ref)
        n_ref[...] = jnp.zeros_like(n_ref)

    in_ctx = step < n_ctx_chunks
    pos = jnp.where(in_ctx, step, step - n_ctx_chunks)
    n_seq = jnp.where(in_ctx, n_ctx_chunks, n_lat_chunks)
    orig = n_seq - 1 - pos if reverse else pos
    has_prev = (orig > 0).astype(F32)
    has_next = (orig < n_seq - 1).astype(F32)

    xe = jnp.concatenate([qkp_ref[...] * has_prev, qkc_ref[...], qkn_ref[...] * has_next], axis=0)
    n_ext = xe.shape[0]
    cw = cw_ref[...]
    y = cb_ref[...] + cw[ML_CONV // 2:ML_CONV // 2 + 1] * xe[CONV_HALO:CONV_HALO + chunk]
    for tap in range(ML_CONV):
        shift = ML_CONV // 2 - tap
        if shift == 0:
            continue
        y = y + cw[tap:tap + 1] * pltpu.roll(xe, shift % n_ext, 0)[CONV_HALO:CONV_HALO + chunk]
    y = y * _sigmoid(y)
    q_all = (y[:, :ML_QK_WIDTH] * (ML_QK_DIM ** -0.5)).astype(BF16)
    k_all = y[:, ML_QK_WIDTH:]

    rows = lax.broadcasted_iota(jnp.int32, (chunk, chunk), 0)
    cols = lax.broadcasted_iota(jnp.int32, (chunk, chunk), 1)
    lower = rows >= cols
    upper = rows <= cols
    seen = upper if reverse else lower
    gc = gc_ref[...]
    gr = gr_ref[...]
    b_col = _dot_f32(seen.astype(F32), gc)
    b_row = _dot_f32(gr, (lower if reverse else upper).astype(F32))
    b_end = jnp.sum(gc, axis=0, keepdims=True)
    base = 2 * ML_HEADS if reverse else 0

    for h in range(ML_HEADS):
        i_idx = base + h
        f_idx = base + ML_HEADS + h
        q_h = q_all[:, h * ML_QK_DIM:(h + 1) * ML_QK_DIM]
        k_h = k_all[:, h * ML_QK_DIM:(h + 1) * ML_QK_DIM]
        v_h = v_ref[:, h * ML_V_DIM:(h + 1) * ML_V_DIM]
        bc = b_col[:, f_idx:f_idx + 1]
        d = bc - b_row[f_idx:f_idx + 1, :] + gr[i_idx:i_idx + 1, :]
        w = jnp.exp(jnp.where(seen, d, NEG))
        s = _dot_nt(q_h, k_h.astype(BF16)) * w
        state_c = c_ref[h]
        state_n = n_ref[h:h + 1, :]
        carry = jnp.exp(bc)
        num = _dot(s.astype(BF16), v_h) + carry * _dot(q_h, state_c.astype(BF16))
        den = (jnp.sum(s, axis=-1, keepdims=True)
               + carry * jnp.sum(q_h.astype(F32) * state_n, axis=-1, keepdims=True))
        h_out = num / jnp.maximum(jnp.abs(den), 1.0)

        be = b_end[:, f_idx:f_idx + 1]
        kw = k_h * jnp.exp(be - bc + gc[:, i_idx:i_idx + 1])
        decay = jnp.exp(be)
        c_ref[h] = decay * state_c + _dot_tn(kw.astype(BF16), v_h)
        n_ref[h:h + 1, :] = decay * state_n + jnp.sum(kw, axis=0, keepdims=True)

        sl = slice(h * ML_V_DIM, (h + 1) * ML_V_DIM)
        if reverse:
            out_ref[:, sl] = h_out
        else:
            tot = h_out + hb_ref[:, sl]
            hn = _rms(tot, gml_ref[:, sl])
            out_ref[:, sl] = (hn * _sigmoid(om_ref[:, sl].astype(F32))).astype(out_ref.dtype)


def _mlstm(qk, vm, gc, gr, conv_w, conv_b, seg, *, reverse, om=None, hb=None, g_ml=None):
    n = qk.shape[0]
    b_, s_, lc = seg["B"], seg["S"], seg["Lc"]
    chunk = ML_CHUNK
    ncc, ncl = lc // chunk, s_ // chunk
    lat_chunks = b_ * ncl
    per_chunk = chunk // CONV_HALO
    n_halo = n // CONV_HALO

    def blk(b, c):
        pos = jnp.where(c < ncc, c, c - ncc)
        if reverse:
            pos = jnp.where(c < ncc, ncc, ncl) - 1 - pos
        return jnp.where(c < ncc, lat_chunks + b * ncc + pos, b * ncl + pos)

    cur = lambda b, c: (blk(b, c), 0)
    prev = lambda b, c: (jnp.maximum(blk(b, c) * per_chunk - 1, 0), 0)
    nxt = lambda b, c: (jnp.minimum((blk(b, c) + 1) * per_chunk, n_halo - 1), 0)
    const = lambda b, c: (0, 0)
    in_specs = [pl.BlockSpec((CONV_HALO, 2 * ML_QK_WIDTH), prev),
                pl.BlockSpec((chunk, 2 * ML_QK_WIDTH), cur),
                pl.BlockSpec((CONV_HALO, 2 * ML_QK_WIDTH), nxt),
                pl.BlockSpec((chunk, ML_WIDTH), cur),
                pl.BlockSpec((chunk, LANES), cur),
                pl.BlockSpec((ML_GATES, chunk), lambda b, c: (0, blk(b, c))),
                pl.BlockSpec(conv_w.shape, const),
                pl.BlockSpec(conv_b.shape, const)]
    args = [qk, qk, qk, vm, gc, gr, conv_w, conv_b]
    if reverse:
        out_dtype = F32
    else:
        in_specs += [pl.BlockSpec((chunk, ML_WIDTH), cur), pl.BlockSpec((chunk, ML_WIDTH), cur),
                     pl.BlockSpec((1, ML_WIDTH), const)]
        args += [om, hb, g_ml]
        out_dtype = BF16
    return pl.pallas_call(
        functools.partial(_mlstm_kernel, reverse=reverse, n_ctx_chunks=ncc, n_lat_chunks=ncl),
        grid=(b_, ncc + ncl),
        in_specs=in_specs,
        out_specs=pl.BlockSpec((chunk, ML_WIDTH), cur),
        out_shape=jax.ShapeDtypeStruct((n, ML_WIDTH), out_dtype),
        scratch_shapes=[pltpu.VMEM((ML_HEADS, ML_QK_DIM, ML_V_DIM), F32),
                        pltpu.VMEM((ML_HEADS, ML_QK_DIM), F32)],
        compiler_params=_params(2),
        name="mlstm_bwd" if reverse else "mlstm_fwd",
    )(*args)


def _swiglu(h, w1, w3, w2):
    a = _dot(h, w1)
    b = _dot(h, w3)
    return _dot((a * _sigmoid(a) * b).astype(BF16), w2)


def _mix_residual(x_ref, att_ref, ml_ref, mod, wo_ref):
    mix = _dot(att_ref[...], wo_ref[:ATT_WIDTH, :]) + _dot(ml_ref[...], wo_ref[ATT_WIDTH:, :])
    return x_ref[...] + mod[2:3] * mix


def _dense_layer_kernel(x_ref, att_ref, ml_ref, mod_ref, g_ref, wo_ref, w1_ref, w3_ref, w2_ref,
                        fg_ref, o_ref, *, final):
    mod = mod_ref[0]
    x1 = _mix_residual(x_ref, att_ref, ml_ref, mod, wo_ref)
    hx = _adaln(x1, g_ref[...], mod[3:4], mod[4:5]).astype(BF16)
    out = x1 + mod[5:6] * _swiglu(hx, w1_ref[...], w3_ref[...], w2_ref[...])
    if final:
        out = _rms(out, fg_ref[...])
    o_ref[...] = out


def _row_maps(seg, tm):
    nlat = seg["n_lat"] // tm
    s_tiles = seg["S"] // tm
    return (lambda i, *_: (i, 0)), (lambda i, *_: (jnp.where(i < nlat, i // s_tiles, seg["B"]), 0, 0))


def _dense_layer(xs, att, ml, mod, g2, wo, w1, w3, w2, final_g, seg, n_rows, final):
    d = xs.shape[1]
    tm = seg["tm"]
    row, mod_map = _row_maps(seg, tm)
    return pl.pallas_call(
        functools.partial(_dense_layer_kernel, final=final),
        grid=(n_rows // tm,),
        in_specs=[pl.BlockSpec((tm, d), row), pl.BlockSpec((tm, ATT_WIDTH), row),
                  pl.BlockSpec((tm, ML_WIDTH), row), pl.BlockSpec((1, 6, d), mod_map),
                  _resident((1, d)), _resident(wo.shape), _resident(w1.shape), _resident(w3.shape),
                  _resident(w2.shape), _resident((1, d))],
        out_specs=pl.BlockSpec((tm, d), row),
        out_shape=jax.ShapeDtypeStruct((n_rows, d), F32),
        compiler_params=_params(1),
        name="dense_layer",
    )(xs, att, ml, mod, g2, wo, w1, w3, w2, final_g)


def _router_kernel(x_ref, att_ref, ml_ref, mod_ref, g_ref, wo_ref, wr_ref, br_ref,
                   x1_ref, h_ref, route_ref):
    mod = mod_ref[0]
    x1 = _mix_residual(x_ref, att_ref, ml_ref, mod, wo_ref)
    x1_ref[...] = x1
    hx = _adaln(x1, g_ref[...], mod[3:4], mod[4:5])
    h_ref[...] = _pack_bf16_pairs(hx)
    logits = _dot_f32(hx, wr_ref[...]) + br_ref[...]
    lane = lax.broadcasted_iota(jnp.int32, logits.shape, 1)
    logits = jnp.where(lane < N_EXPERTS, logits, -jnp.inf)
    top1 = jnp.max(logits, axis=-1, keepdims=True)
    idx1 = jnp.min(jnp.where(logits == top1, lane, LANES), axis=-1, keepdims=True)
    rest = jnp.where(lane == idx1, -jnp.inf, logits)
    top2 = jnp.max(rest, axis=-1, keepdims=True)
    idx2 = jnp.min(jnp.where(rest == top2, lane, LANES), axis=-1, keepdims=True)
    e2 = jnp.exp(top2 - top1)
    w_first = 1.0 / (1.0 + e2)
    route_ref[...] = jnp.where(lane == 0, idx1.astype(F32),
                               jnp.where(lane == 1, idx2.astype(F32),
                                         jnp.where(lane == 2, w_first,
                                                   jnp.where(lane == 3, e2 * w_first, 0.0))))


def _router(xs, att, ml, mod, g2, wo, wr, br, seg, n_rows):
    d = xs.shape[1]
    tm = seg["tm"]
    row, mod_map = _row_maps(seg, tm)
    return pl.pallas_call(
        _router_kernel,
        grid=(n_rows // tm,),
        in_specs=[pl.BlockSpec((tm, d), row), pl.BlockSpec((tm, ATT_WIDTH), row),
                  pl.BlockSpec((tm, ML_WIDTH), row), pl.BlockSpec((1, 6, d), mod_map),
                  _resident((1, d)), _resident(wo.shape), _resident(wr.shape), _resident(br.shape)],
        out_specs=[pl.BlockSpec((tm, d), row), pl.BlockSpec((tm, d // 2), row),
                   pl.BlockSpec((tm, LANES), row)],
        out_shape=[jax.ShapeDtypeStruct((n_rows, d), F32), jax.ShapeDtypeStruct((n_rows, d // 2), jnp.int32),
                   jax.ShapeDtypeStruct((n_rows, LANES), F32)],
        compiler_params=_params(1),
        name="mix_router",
    )(xs, att, ml, mod, g2, wo, wr, br)


def _pack_bf16_pairs(h):
    half = h.shape[1] // 2
    hi = lax.bitcast_convert_type(h[:, :half].astype(BF16).astype(F32), jnp.int32)
    lo = lax.bitcast_convert_type(h[:, half:].astype(BF16).astype(F32), jnp.int32)
    return (hi & jnp.int32(-65536)) | lax.shift_right_logical(lo, 16)


def _unpack_bf16_pairs(p):
    hi = lax.bitcast_convert_type(p & jnp.int32(-65536), F32)
    lo = lax.bitcast_convert_type(lax.shift_left(p, 16), F32)
    return hi, lo


def _route_plan(route, tm):
    n_rows = route.shape[0]
    n_slots = TOP_K * n_rows
    idx1 = route[:, 0].astype(jnp.int32)
    idx2 = route[:, 1].astype(jnp.int32)
    hot = jax.nn.one_hot(idx1, N_EXPERTS, dtype=jnp.int32) + jax.nn.one_hot(idx2, N_EXPERTS, dtype=jnp.int32)
    incl = jnp.cumsum(hot, axis=0)
    offs = jnp.concatenate([jnp.zeros((1,), jnp.int32), jnp.cumsum(incl[-1])])
    rank = incl - hot
    slot_a = offs[idx1] + jnp.take_along_axis(rank, idx1[:, None], axis=1)[:, 0]
    slot_b = offs[idx2] + jnp.take_along_axis(rank, idx2[:, None], axis=1)[:, 0]
    tok = jnp.arange(n_rows, dtype=jnp.int32)
    token_of_slot = (jnp.zeros((n_slots,), jnp.int32).at[slot_a].set(tok, unique_indices=True)
                     .at[slot_b].set(tok, unique_indices=True))
    n_tiles = n_slots // tm
    t_start = jnp.arange(n_tiles, dtype=jnp.int32) * tm
    e_first = jnp.searchsorted(offs[1:], t_start, side="right").astype(jnp.int32)
    base_hi = jnp.minimum(t_start + tm, offs[e_first + 1])
    e_next = jnp.arange(1, N_EXPERTS, dtype=jnp.int32)
    start = offs[1:N_EXPERTS]
    x_tile = jnp.minimum(start // tm, n_tiles - 1)
    x_hi = jnp.where(start % tm != 0, jnp.minimum(offs[2:], (x_tile + 1) * tm), start)
    tiles = jnp.concatenate([t_start // tm, x_tile])
    experts = jnp.concatenate([e_first, e_next])
    lo = jnp.concatenate([t_start, start])
    hi = jnp.concatenate([base_hi, x_hi])
    order = jnp.argsort(tiles * (2 * N_EXPERTS) + experts)
    tiles, experts, lo, hi = tiles[order], experts[order], lo[order], hi[order]
    change = tiles[1:] != tiles[:-1]
    one = jnp.ones((1,), bool)
    first = jnp.concatenate([one, change]).astype(jnp.int32)
    last = jnp.concatenate([change, one]).astype(jnp.int32)
    return slot_a, slot_b, token_of_slot, (tiles, experts, lo, hi, first, last)


def _gather_rows(table, idx):
    n_idx = idx.shape[0]
    width = table.shape[1]
    workers = SC_CORES * SC_SUBCORES
    per_worker = n_idx // workers
    assert n_idx % (workers * SC_GATHER_ROWS) == 0
    mesh = plsc.VectorSubcoreMesh(core_axis_name="c", subcore_axis_name="s")

    @functools.partial(
        pl.kernel, mesh=mesh,
        out_type=jax.ShapeDtypeStruct((n_idx, width), table.dtype),
        scratch_types=[pltpu.VMEM((SC_GATHER_ROWS,), jnp.int32),
                       pltpu.VMEM((SC_GATHER_ROWS, width), table.dtype),
                       pltpu.SemaphoreType.DMA],
        name="gather_rows")
    def gather(table_hbm, idx_hbm, out_hbm, idx_v, rows_v, sem):
        base = (lax.axis_index("s") * SC_CORES + lax.axis_index("c")) * per_worker

        @pl.loop(0, per_worker // SC_GATHER_ROWS)
        def _(i):
            off = pl.multiple_of(base + i * SC_GATHER_ROWS, SC_GATHER_ROWS)
            pltpu.sync_copy(idx_hbm.at[pl.ds(off, SC_GATHER_ROWS)], idx_v)
            pltpu.async_copy(table_hbm.at[idx_v], rows_v, sem).wait()
            pltpu.sync_copy(rows_v, out_hbm.at[pl.ds(off, SC_GATHER_ROWS)])

    return gather(table, idx)


def _experts_kernel(tile_ref, exp_ref, lo_ref, hi_ref, first_ref, last_ref,
                    x_ref, w1_ref, w3_ref, w2_ref, o_ref, acc_ref):
    i = pl.program_id(0)
    tm, half = x_ref.shape
    f = w1_ref.shape[2]
    fc = f // FFN_CHUNKS

    @pl.when(first_ref[i] == 1)
    def _():
        acc_ref[...] = jnp.zeros_like(acc_ref)

    lo, hi = lo_ref[i], hi_ref[i]

    @pl.when(hi > lo)
    def _():
        x_hi, x_lo = _unpack_bf16_pairs(x_ref[...])
        x_hi, x_lo = x_hi.astype(BF16), x_lo.astype(BF16)
        y = None
        for c in range(FFN_CHUNKS):
            cols = slice(c * fc, (c + 1) * fc)
            a = _dot(x_hi, w1_ref[0, :half, cols]) + _dot(x_lo, w1_ref[0, half:, cols])
            b = _dot(x_hi, w3_ref[0, :half, cols]) + _dot(x_lo, w3_ref[0, half:, cols])
            part = _dot((a * _sigmoid(a) * b).astype(BF16), w2_ref[0, cols, :])
            y = part if y is None else y + part
        rows = tile_ref[i] * tm + lax.broadcasted_iota(jnp.int32, (tm, 1), 0)
        keep = jnp.logical_and(rows >= lo, rows < hi)
        acc_ref[...] += jnp.where(keep, y, 0.0)

    @pl.when(last_ref[i] == 1)
    def _():
        o_ref[...] = _pack_bf16_pairs(acc_ref[...])


def _experts(xs_sorted, items, w1, w3, w2):
    n_slots, half = xs_sorted.shape
    n_exp, d, f = w1.shape
    tm = MOE_ROW_TILE
    assert f % (FFN_CHUNKS * LANES) == 0
    tile_map = lambda i, tiles, *_: (tiles[i], 0)
    exp_map = lambda i, tiles, experts, *_: (experts[i], 0, 0)
    grid_spec = pltpu.PrefetchScalarGridSpec(
        num_scalar_prefetch=len(items),
        grid=(items[0].shape[0],),
        in_specs=[pl.BlockSpec((tm, half), tile_map),
                  pl.BlockSpec((1, d, f), exp_map), pl.BlockSpec((1, d, f), exp_map),
                  pl.BlockSpec((1, f, d), exp_map)],
        out_specs=pl.BlockSpec((tm, half), tile_map),
        scratch_shapes=[pltpu.VMEM((tm, d), F32)],
    )
    return pl.pallas_call(
        _experts_kernel,
        grid_spec=grid_spec,
        out_shape=jax.ShapeDtypeStruct((n_slots, half), jnp.int32),
        compiler_params=_params(1),
        name="experts",
    )(*items, xs_sorted, w1, w3, w2)


def _combine_kernel(x1_ref, ya_ref, yb_ref, route_ref, mod_ref, fg_ref, o_ref, *, final):
    half = ya_ref.shape[1]
    route = route_ref[...]
    wa, wb = route[:, 2:3], route[:, 3:4]
    a_hi, a_lo = _unpack_bf16_pairs(ya_ref[...])
    b_hi, b_lo = _unpack_bf16_pairs(yb_ref[...])
    gate = mod_ref[0][5:6]
    out_hi = x1_ref[:, :half] + gate[:, :half] * (wa * a_hi + wb * b_hi)
    out_lo = x1_ref[:, half:] + gate[:, half:] * (wa * a_lo + wb * b_lo)
    if final:
        total = jnp.sum(out_hi * out_hi, axis=-1, keepdims=True) + jnp.sum(out_lo * out_lo, axis=-1, keepdims=True)
        scale = lax.rsqrt(total / (2 * half) + EPS)
        out_hi = out_hi * scale * fg_ref[:, :half]
        out_lo = out_lo * scale * fg_ref[:, half:]
    o_ref[:, :half] = out_hi
    o_ref[:, half:] = out_lo


def _combine(x1, y_pairs, route, mod, final_g, seg, final):
    n_rows, d = x1.shape
    tm = seg["tm"]
    row, mod_map = _row_maps(seg, tm)
    second = n_rows // tm
    return pl.pallas_call(
        functools.partial(_combine_kernel, final=final),
        grid=(n_rows // tm,),
        in_specs=[pl.BlockSpec((tm, d), row), pl.BlockSpec((tm, d // 2), row),
                  pl.BlockSpec((tm, d // 2), lambda i: (second + i, 0)),
                  pl.BlockSpec((tm, LANES), row), pl.BlockSpec((1, 6, d), mod_map),
                  pl.BlockSpec((1, d), lambda i: (0, 0))],
        out_specs=pl.BlockSpec((tm, d), row),
        out_shape=jax.ShapeDtypeStruct((n_rows, d), F32),
        compiler_params=_params(1),
        name="moe_combine",
    )(x1, y_pairs, y_pairs, route, mod, final_g)


def _moe_layer(xs, att, ml, mod, g2, wo, wr, br, w1, w3, w2, final_g, seg, n_rows, final):
    x1, h_pairs, route = _router(xs, att, ml, mod, g2, wo, wr, br, seg, n_rows)
    slot_a, slot_b, token_of_slot, items = _route_plan(route, MOE_ROW_TILE)
    y_sorted = _experts(_gather_rows(h_pairs, token_of_slot), items, w1, w3, w2)
    y_pairs = _gather_rows(y_sorted, jnp.concatenate([slot_a, slot_b]))
    return _combine(x1, y_pairs, route, mod, final_g, seg, final)


def kernel(x, c, ctx, c_ctx, norm1_g, norm2_g, w_mod, b_mod, w_in, conv_w, conv_b, b_gates, attn_sink,
           g_att, g_ml, w_out, ffn_w1, ffn_w3, ffn_w2, w_router, b_router, exp_w1, exp_w3, exp_w2,
           final_g):
    b_, s_, d = x.shape
    lc = ctx.shape[1]
    depth = w_in.shape[0]
    n_lat, n_ctx = b_ * s_, b_ * lc
    tm = min(ROW_TILE, s_)
    assert s_ % tm == 0 and n_ctx % tm == 0
    assert (TOP_K * n_lat) % MOE_ROW_TILE == 0 and (TOP_K * n_ctx) % MOE_ROW_TILE == 0
    assert s_ % ML_CHUNK == 0 and lc % ML_CHUNK == 0 and n_lat % lc == 0 and b_ < MOD_ROWS
    seg = dict(B=b_, S=s_, Lc=lc, n_lat=n_lat, tm=tm)

    cond = jnp.zeros((MOD_ROWS, d), F32).at[:b_].set(c).at[b_].set(c_ctx)
    mods = _modulation(cond, w_mod, b_mod).reshape(depth, MOD_ROWS, 6, d)
    rope = _rope_tables(s_, tm)
    xs = jnp.concatenate([x.reshape(n_lat, d), ctx.reshape(n_ctx, d)], axis=0)
    final_row = final_g.reshape(1, d)

    for layer in range(depth):
        last = layer == depth - 1
        w_l = w_in[layer]
        w_gates = w_l[:, MAIN_WIDTH:]
        w_gc = jnp.pad(w_gates, ((0, 0), (0, LANES - ML_GATES))).astype(BF16)
        b_gc = jnp.pad(b_gates[layer], (0, LANES - ML_GATES)).reshape(1, LANES)
        qa, ka, va, qk, vm, om, gc, gr = _input_projection(
            xs, mods[layer], norm1_g[layer].reshape(1, d), w_l[:, :MAIN_WIDTH].astype(BF16), w_gc,
            w_gates.T.astype(BF16), b_gc, b_gates[layer].reshape(ML_GATES, 1), rope, seg)
        att = _attention(qa, ka, va, attn_sink[layer], g_att[layer].reshape(1, ATT_WIDTH), seg, not last)
        cw, cb = conv_w[layer], conv_b[layer].reshape(1, -1)
        hb = _mlstm(qk, vm, gc, gr, cw, cb, seg, reverse=True)
        ml = _mlstm(qk, vm, gc, gr, cw, cb, seg, reverse=False, om=om, hb=hb,
                    g_ml=g_ml[layer].reshape(1, ML_WIDTH))
        n_rows = n_lat if last else n_lat + n_ctx
        g2 = norm2_g[layer].reshape(1, d)
        wo = w_out[layer].astype(BF16)
        i = layer // 2
        if layer % 2 == 0:
            xs = _dense_layer(xs, att, ml, mods[layer], g2, wo, ffn_w1[i].astype(BF16),
                              ffn_w3[i].astype(BF16), ffn_w2[i].astype(BF16), final_row, seg, n_rows, last)
        else:
            wr = jnp.pad(w_router[i], ((0, 0), (0, LANES - N_EXPERTS)))
            br = jnp.pad(b_router[i], (0, LANES - N_EXPERTS)).reshape(1, LANES)
            xs = _moe_layer(xs, att, ml, mods[layer], g2, wo, wr, br, exp_w1[i].astype(BF16),
                            exp_w3[i].astype(BF16), exp_w2[i].astype(BF16), final_row, seg, n_rows, last)
    return xs[:n_lat].reshape(b_, s_, d)
```

```python
import functools

import jax
import jax.numpy as jnp
import numpy as np
from jax import lax
from jax.experimental import pallas as pl
from jax.experimental.pallas import tpu as pltpu
from jax.experimental.pallas import tpu_sc as plsc

F32 = jnp.float32
BF16 = jnp.bfloat16

GRID_W = 64
ATT_HEADS = 8
ATT_KV_HEADS = 2
ATT_HEAD_DIM = 64
ATT_GROUP = ATT_HEADS // ATT_KV_HEADS
WINDOW = 128
ATT_BLOCK = 128
ROPE_THETA = 10000.0
ML_HEADS = 4
ML_QK_DIM = 64
ML_V_DIM = 128
ML_CONV = 5
GATE_CAP = 15.0
ATT_WIDTH = ATT_HEADS * ATT_HEAD_DIM
ATT_KV_WIDTH = ATT_KV_HEADS * ATT_HEAD_DIM
ML_QK_WIDTH = ML_HEADS * ML_QK_DIM
ML_WIDTH = ML_HEADS * ML_V_DIM
ML_GATES = 4 * ML_HEADS
MAIN_WIDTH = ATT_WIDTH + 2 * ATT_KV_WIDTH + 2 * ML_QK_WIDTH + 2 * ML_WIDTH
N_EXPERTS = 8
TOP_K = 2
EPS = 1e-6

LANES = 128
SUBLANES = 8
VMEM_LIMIT = 56 * 1024 * 1024
NEG = -1e30
SC_CORES = 2
SC_SUBCORES = 16
SC_GATHER_ROWS = 64

ROW_TILE = 512
MOE_ROW_TILE = 512
FFN_CHUNKS = 2
ML_CHUNK = 128
CONV_HALO = SUBLANES
MOD_ROWS = 16
MOD_COL_TILE = 1536


def _dot(a, b):
    return jnp.dot(a, b, preferred_element_type=F32)


def _dot_nt(a, b):
    return lax.dot_general(a, b, (((1,), (1,)), ((), ())), preferred_element_type=F32)


def _dot_tn(a, b):
    return lax.dot_general(a, b, (((0,), (0,)), ((), ())), preferred_element_type=F32)


def _dot_f32(a, b):
    return jnp.dot(a, b, preferred_element_type=F32, precision=lax.Precision.HIGHEST)


def _sigmoid(x):
    return 1.0 / (1.0 + jnp.exp(-x))


def _rms(x, g):
    return x * lax.rsqrt(jnp.mean(x * x, axis=-1, keepdims=True) + EPS) * g


def _adaln(x, g, shift, scale):
    return _rms(x, g) * (1.0 + scale) + shift


def _params(n_axes):
    return pltpu.CompilerParams(dimension_semantics=("arbitrary",) * n_axes,
                                vmem_limit_bytes=VMEM_LIMIT)


def _resident(shape):
    zeros = (0,) * len(shape)
    return pl.BlockSpec(shape, lambda *_: zeros, pipeline_mode=pl.Buffered(1))


def _mod_kernel(c_ref, w_ref, b_ref, o_ref):
    c = c_ref[...]
    a = (c * _sigmoid(c)).astype(BF16)
    o_ref[0] = _dot(a, w_ref[0].astype(BF16)) + b_ref[0]


def _modulation(cond, w_mod, b_mod):
    depth, d, width = w_mod.shape
    tn = MOD_COL_TILE if width % MOD_COL_TILE == 0 else width
    return pl.pallas_call(
        _mod_kernel,
        grid=(depth, width // tn),
        in_specs=[pl.BlockSpec((MOD_ROWS, d), lambda l, j: (0, 0)),
                  pl.BlockSpec((1, d, tn), lambda l, j: (l, 0, j)),
                  pl.BlockSpec((1, 1, tn), lambda l, j: (l, 0, j))],
        out_specs=pl.BlockSpec((1, MOD_ROWS, tn), lambda l, j: (l, 0, j)),
        out_shape=jax.ShapeDtypeStruct((depth, MOD_ROWS, width), F32),
        compiler_params=_params(2),
        name="modulation",
    )(cond, w_mod, b_mod.reshape(depth, 1, width))


def _gate_act(u, is_forget):
    g = GATE_CAP * jnp.tanh(u / GATE_CAP)
    log_sig = jnp.minimum(g, 0.0) - jnp.log1p(jnp.exp(-jnp.abs(g)))
    return jnp.where(is_forget, log_sig, g)


def _conv_silu(xe, cw, cb, rows):
    n_ext = xe.shape[0]
    mid = ML_CONV // 2
    y = cb + cw[mid:mid + 1] * xe[CONV_HALO:CONV_HALO + rows]
    for tap in range(ML_CONV):
        if tap != mid:
            y = y + cw[tap:tap + 1] * pltpu.roll(xe, (mid - tap) % n_ext, 0)[CONV_HALO:CONV_HALO + rows]
    return y * _sigmoid(y)


def _inproj_kernel(x_ref, xp_ref, xn_ref, mod_ref, g_ref, w_ref, wgc_ref, wgr_ref, bgc_ref, bgr_ref,
                   cw_ref, cb_ref, cos_ref, sa_ref, sb_ref,
                   qa_ref, ka_ref, va_ref, qm_ref, km_ref, vm_ref, om_ref, gc_ref, gr_ref,
                   *, n_lat_tiles, seq_lat, seq_ctx):
    mod = mod_ref[0]
    tm = x_ref.shape[0]
    normed = lambda ref: _adaln(ref[...], g_ref[...], mod[0:1], mod[1:2]).astype(BF16)
    hx = normed(x_ref)
    cos, sa, sb = cos_ref[...], sa_ref[...], sb_ref[...]
    quarter = ATT_HEAD_DIM // 4

    def rope(u):
        return (u * cos + pltpu.roll(u, LANES - quarter, 1) * sa + pltpu.roll(u, quarter, 1) * sb)

    c0 = 0
    q = _dot(hx, w_ref[:, c0:c0 + ATT_WIDTH])
    for j in range(ATT_WIDTH // LANES):
        sl = slice(j * LANES, (j + 1) * LANES)
        qa_ref[:, sl] = (rope(q[:, sl]) * (ATT_HEAD_DIM ** -0.5)).astype(BF16)
    c0 += ATT_WIDTH
    kv = _dot(hx, w_ref[:, c0:c0 + 2 * ATT_KV_WIDTH])
    ka_ref[...] = rope(kv[:, :ATT_KV_WIDTH]).astype(BF16)
    va_ref[...] = kv[:, ATT_KV_WIDTH:].astype(BF16)
    c0 += 2 * ATT_KV_WIDTH
    w_qk = w_ref[:, c0:c0 + 2 * ML_QK_WIDTH]
    qk = _dot(hx, w_qk)
    qk_prev = _dot(normed(xp_ref), w_qk)
    qk_next = _dot(normed(xn_ref), w_qk)
    i = pl.program_id(0)
    seq_len = jnp.where(i < n_lat_tiles, seq_lat, seq_ctx)
    seg = min(tm, seq_ctx)
    cw, cb = cw_ref[...], cb_ref[...]
    for j in range(tm // seg):
        first_row = i * tm + j * seg
        has_prev = (lax.rem(first_row, seq_len) != 0).astype(F32)
        has_next = (lax.rem(first_row + seg, seq_len) != 0).astype(F32)
        prev = qk_prev if j == 0 else qk[j * seg - CONV_HALO:j * seg]
        nxt = qk_next if (j + 1) * seg == tm else qk[(j + 1) * seg:(j + 1) * seg + CONV_HALO]
        xe = jnp.concatenate([prev * has_prev, qk[j * seg:(j + 1) * seg], nxt * has_next], axis=0)
        y = _conv_silu(xe, cw, cb, seg)
        qm_ref[j * seg:(j + 1) * seg, :] = (y[:, :ML_QK_WIDTH] * (ML_QK_DIM ** -0.5)).astype(BF16)
        km_ref[j * seg:(j + 1) * seg, :] = y[:, ML_QK_WIDTH:].astype(BF16)
    c0 += 2 * ML_QK_WIDTH
    vm_ref[...] = _dot(hx, w_ref[:, c0:c0 + ML_WIDTH]).astype(BF16)
    c0 += ML_WIDTH
    om_ref[...] = _dot(hx, w_ref[:, c0:c0 + ML_WIDTH]).astype(BF16)
    gc = _dot(hx, wgc_ref[...]) + bgc_ref[...]
    lane = lax.broadcasted_iota(jnp.int32, gc.shape, 1)
    gc_ref[...] = _gate_act(gc, (lane // ML_HEADS) % 2 == 1)
    gr = _dot_nt(wgr_ref[...], hx) + bgr_ref[...]
    sub = lax.broadcasted_iota(jnp.int32, gr.shape, 0)
    gr_ref[...] = _gate_act(gr, (sub // ML_HEADS) % 2 == 1)


def _input_projection(xs, mod, g1, w_main, w_gc, w_gr, b_gc, b_gr, conv_w, conv_b, rope, seg):
    n, d = xs.shape
    tm = seg["tm"]
    nlat = seg["n_lat"] // tm
    s_tiles = seg["S"] // tm
    halos_per_tile = tm // CONV_HALO
    row = lambda i: (i, 0)
    prev = lambda i: (jnp.maximum(i * halos_per_tile - 1, 0), 0)
    nxt = lambda i: (jnp.minimum((i + 1) * halos_per_tile, n // CONV_HALO - 1), 0)
    mod_map = lambda i: (jnp.where(i < nlat, i // s_tiles, seg["B"]), 0, 0)
    rope_map = lambda i: (jnp.where(i < nlat, i % s_tiles, s_tiles), 0)
    widths = [(ATT_WIDTH, BF16), (ATT_KV_WIDTH, BF16), (ATT_KV_WIDTH, BF16), (ML_QK_WIDTH, BF16),
              (ML_QK_WIDTH, BF16), (ML_WIDTH, BF16), (ML_WIDTH, BF16), (LANES, F32)]
    out_shape = [jax.ShapeDtypeStruct((n, w), t) for w, t in widths]
    out_specs = [pl.BlockSpec((tm, w), row) for w, _ in widths]
    out_shape.append(jax.ShapeDtypeStruct((ML_GATES, n), F32))
    out_specs.append(pl.BlockSpec((ML_GATES, tm), lambda i: (0, i)))
    return pl.pallas_call(
        functools.partial(_inproj_kernel, n_lat_tiles=nlat, seq_lat=seg["S"], seq_ctx=seg["Lc"]),
        grid=(n // tm,),
        in_specs=[pl.BlockSpec((tm, d), row),
                  pl.BlockSpec((CONV_HALO, d), prev),
                  pl.BlockSpec((CONV_HALO, d), nxt),
                  pl.BlockSpec((1, 6, d), mod_map),
                  _resident((1, d)),
                  _resident(w_main.shape), _resident(w_gc.shape), _resident(w_gr.shape),
                  _resident(b_gc.shape), _resident(b_gr.shape),
                  _resident(conv_w.shape), _resident(conv_b.shape),
                  pl.BlockSpec((tm, LANES), rope_map),
                  pl.BlockSpec((tm, LANES), rope_map),
                  pl.BlockSpec((tm, LANES), rope_map)],
        out_specs=out_specs,
        out_shape=out_shape,
        compiler_params=_params(1),
        name="input_projection",
    )(xs, xs, xs, mod, g1, w_main, w_gc, w_gr, b_gc, b_gr, conv_w, conv_b, *rope)


def _rope_tables(s, tm):
    quarter = ATT_HEAD_DIM // 4
    t = jnp.arange(s)
    row = (t // GRID_W).astype(F32)
    col = (t % GRID_W).astype(F32)
    inv = ROPE_THETA ** (-jnp.arange(quarter, dtype=F32) / quarter)
    ang_r = row[:, None] * inv[None, :]
    ang_c = col[:, None] * inv[None, :]
    zero = jnp.zeros_like(ang_r)
    cos = jnp.concatenate([jnp.cos(ang_r)] * 2 + [jnp.cos(ang_c)] * 2, axis=1)
    sin_up = jnp.concatenate([-jnp.sin(ang_r), zero, -jnp.sin(ang_c), zero], axis=1)
    sin_dn = jnp.concatenate([zero, jnp.sin(ang_r), zero, jnp.sin(ang_c)], axis=1)
    reps = LANES // ATT_HEAD_DIM
    ident = [jnp.ones((tm, LANES), F32), jnp.zeros((tm, LANES), F32), jnp.zeros((tm, LANES), F32)]
    return tuple(jnp.concatenate([jnp.tile(a, (1, reps)), i], axis=0)
                 for a, i in zip((cos, sin_up, sin_dn), ident))


def _attn_kernel(sink_ref, q_ref, kp_ref, kc_ref, kn_ref, vp_ref, vc_ref, vn_ref, kx_ref, vx_ref,
                 g_ref, o_ref, *, n_lat_blocks):
    j = pl.program_id(1)
    is_lat = j < n_lat_blocks
    blk = ATT_BLOCK
    rows = lax.broadcasted_iota(jnp.int32, (blk, blk), 0)
    cols = lax.broadcasted_iota(jnp.int32, (blk, blk), 1)
    ok_p = jnp.logical_and(cols >= rows, jnp.logical_and(is_lat, j > 0))
    ok_c = jnp.logical_and(cols >= 0, is_lat)
    ok_n = jnp.logical_and(cols <= rows, jnp.logical_and(is_lat, j < n_lat_blocks - 1))
    n_ctx = kx_ref.shape[0]
    bias = jnp.concatenate([jnp.where(ok_p, 0.0, NEG), jnp.where(ok_c, 0.0, NEG),
                            jnp.where(ok_n, 0.0, NEG), jnp.zeros((blk, n_ctx), F32)], axis=1)
    q = q_ref[...]
    k_all = jnp.concatenate([kp_ref[...], kc_ref[...], kn_ref[...], kx_ref[...]], axis=0)
    v_all = jnp.concatenate([vp_ref[...], vc_ref[...], vn_ref[...], vx_ref[...]], axis=0)
    dh = ATT_HEAD_DIM
    outs = []
    for h in range(ATT_KV_HEADS):
        k_h = k_all[:, h * dh:(h + 1) * dh]
        v_h = v_all[:, h * dh:(h + 1) * dh]
        q_h = jnp.concatenate([q[:, (h * ATT_GROUP + g) * dh:(h * ATT_GROUP + g + 1) * dh]
                               for g in range(ATT_GROUP)], axis=0)
        s_all = _dot_nt(q_h, k_h)
        p_parts, inv_parts = [], []
        for g in range(ATT_GROUP):
            sink = sink_ref[h * ATT_GROUP + g]
            s = s_all[g * blk:(g + 1) * blk] + bias
            m = jnp.maximum(jnp.max(s, axis=-1, keepdims=True), sink)
            p = jnp.exp(s - m)
            denom = jnp.sum(p, axis=-1, keepdims=True) + jnp.exp(sink - m)
            p_parts.append(p.astype(BF16))
            inv_parts.append(1.0 / denom)
        o = _dot(jnp.concatenate(p_parts, axis=0), v_h)
        for g in range(ATT_GROUP):
            outs.append(o[g * blk:(g + 1) * blk] * inv_parts[g])
    att = jnp.concatenate(outs, axis=1)
    o_ref[...] = _rms(att, g_ref[...]).astype(o_ref.dtype)


def _attention(qa, ka, va, sink, g_att, seg, with_ctx):
    n = qa.shape[0]
    b_, s_, lc = seg["B"], seg["S"], seg["Lc"]
    blk = ATT_BLOCK
    nqb = s_ // blk
    ncb = lc // blk if with_ctx else 0
    lat_blocks = b_ * nqb

    def q_map(b, j, *_):
        return (jnp.where(j < nqb, b * nqb + j, lat_blocks + b * (lc // blk) + (j - nqb)), 0)

    def win_map(off):
        def f(b, j, *_):
            jj = jnp.clip(jnp.where(j < nqb, j, 0) + off, 0, nqb - 1)
            return (b * nqb + jj, 0)
        return f

    ctx_map = lambda b, j, *_: (b_ * s_ // lc + b, 0)
    kv_specs = [pl.BlockSpec((blk, ATT_KV_WIDTH), win_map(o)) for o in (-1, 0, 1)]
    grid_spec = pltpu.PrefetchScalarGridSpec(
        num_scalar_prefetch=1,
        grid=(b_, nqb + ncb),
        in_specs=[pl.BlockSpec((blk, ATT_WIDTH), q_map)] + kv_specs + kv_specs
                 + [pl.BlockSpec((lc, ATT_KV_WIDTH), ctx_map), pl.BlockSpec((lc, ATT_KV_WIDTH), ctx_map),
                    pl.BlockSpec((1, ATT_WIDTH), lambda b, j, *_: (0, 0))],
        out_specs=pl.BlockSpec((blk, ATT_WIDTH), q_map),
    )
    return pl.pallas_call(
        functools.partial(_attn_kernel, n_lat_blocks=nqb),
        grid_spec=grid_spec,
        out_shape=jax.ShapeDtypeStruct((n, ATT_WIDTH), BF16),
        compiler_params=_params(2),
        name="window_attention",
    )(sink, qa, ka, ka, ka, va, va, va, ka, va, g_att)


def _mlstm_direction(q_ref, k_ref, v_ref, gc_ref, gr_ref, out_ref, state_ref, *, reverse):
    chunk = q_ref.shape[0]
    rows = lax.broadcasted_iota(jnp.int32, (chunk, chunk), 0)
    cols = lax.broadcasted_iota(jnp.int32, (chunk, chunk), 1)
    lower = rows >= cols
    upper = rows <= cols
    seen = upper if reverse else lower
    gc = gc_ref[...]
    gr = gr_ref[...]
    b_col = _dot_f32(seen.astype(F32), gc)
    b_row = _dot_f32(gr, (lower if reverse else upper).astype(F32))
    b_end = jnp.sum(gc, axis=0, keepdims=True)
    base = 2 * ML_HEADS if reverse else 0
    pair_width = 2 * ML_QK_DIM
    lane = lax.broadcasted_iota(jnp.int32, (chunk, pair_width), 1)
    state_row = lax.broadcasted_iota(jnp.int32, (pair_width, 1), 0)
    ones = jnp.ones((chunk, ML_V_DIM), BF16)

    for pair in range(ML_HEADS // 2):
        q_pair = q_ref[:, pair * pair_width:(pair + 1) * pair_width]
        k_pair = k_ref[:, pair * pair_width:(pair + 1) * pair_width]
        state = state_ref[pair]
        state_bf = state.astype(BF16)
        update = None
        decays = []
        for sub in range(2):
            h = 2 * pair + sub
            i_idx = base + h
            f_idx = base + ML_HEADS + h
            own = (lane >= ML_QK_DIM) if sub else (lane < ML_QK_DIM)
            q_h = jnp.where(own, q_pair, jnp.zeros_like(q_pair))
            vx = jnp.concatenate([v_ref[:, h * ML_V_DIM:(h + 1) * ML_V_DIM], ones], axis=1)
            bc = b_col[:, f_idx:f_idx + 1]
            d = bc - b_row[f_idx:f_idx + 1, :] + gr[i_idx:i_idx + 1, :]
            w = jnp.exp(jnp.where(seen, d, NEG))
            s = _dot_nt(q_h, k_pair) * w
            tot = _dot(s.astype(BF16), vx) + jnp.exp(bc) * _dot(q_h, state_bf)
            h_out = tot[:, :ML_V_DIM] / jnp.maximum(jnp.abs(tot[:, ML_V_DIM:]), 1.0)
            out_ref[:, h * ML_V_DIM:(h + 1) * ML_V_DIM] = h_out.astype(out_ref.dtype)

            be = b_end[:, f_idx:f_idx + 1]
            kw = jnp.where(own, k_pair.astype(F32) * jnp.exp(be - bc + gc[:, i_idx:i_idx + 1]), 0.0)
            part = _dot_tn(kw.astype(BF16), vx)
            update = part if update is None else update + part
            decays.append(jnp.exp(be))
        decay = jnp.where(state_row < ML_QK_DIM, decays[0], decays[1])
        state_ref[pair] = decay * state + update


def _mlstm_kernel(qf_ref, kf_ref, vf_ref, gcf_ref, grf_ref, qb_ref, kb_ref, vb_ref, gcb_ref, grb_ref,
                  hf_ref, hb_ref, sf_ref, sb_ref):
    @pl.when(pl.program_id(1) == 0)
    def _():
        sf_ref[...] = jnp.zeros_like(sf_ref)
        sb_ref[...] = jnp.zeros_like(sb_ref)

    _mlstm_direction(qf_ref, kf_ref, vf_ref, gcf_ref, grf_ref, hf_ref, sf_ref, reverse=False)
    _mlstm_direction(qb_ref, kb_ref, vb_ref, gcb_ref, grb_ref, hb_ref, sb_ref, reverse=True)


def _mlstm(qm, km, vm, gc, gr, seg):
    n = qm.shape[0]
    b_, s_, lc = seg["B"], seg["S"], seg["Lc"]
    chunk = ML_CHUNK
    ncc, ncl = lc // chunk, s_ // chunk
    lat_chunks = b_ * ncl

    def blk(b, c, reverse):
        pos = jnp.where(c < ncc, c, c - ncc)
        if reverse:
            pos = jnp.where(c < ncc, ncc, ncl) - 1 - pos
        return jnp.where(c < ncc, lat_chunks + b * ncc + pos, b * ncl + pos)

    def specs(reverse):
        cur = lambda b, c: (blk(b, c, reverse), 0)
        return [pl.BlockSpec((chunk, ML_QK_WIDTH), cur), pl.BlockSpec((chunk, ML_QK_WIDTH), cur),
                pl.BlockSpec((chunk, ML_WIDTH), cur), pl.BlockSpec((chunk, LANES), cur),
                pl.BlockSpec((ML_GATES, chunk), lambda b, c: (0, blk(b, c, reverse)))]

    out = lambda reverse: pl.BlockSpec((chunk, ML_WIDTH), lambda b, c: (blk(b, c, reverse), 0))
    state = pltpu.VMEM((ML_HEADS // 2, 2 * ML_QK_DIM, 2 * ML_V_DIM), F32)
    return pl.pallas_call(
        _mlstm_kernel,
        grid=(b_, ncc + ncl),
        in_specs=specs(False) + specs(True),
        out_specs=[out(False), out(True)],
        out_shape=[jax.ShapeDtypeStruct((n, ML_WIDTH), BF16)] * 2,
        scratch_shapes=[state, state],
        compiler_params=_params(2),
        name="mlstm_scan",
    )(qm, km, vm, gc, gr, qm, km, vm, gc, gr)


def _swiglu(h, w1, w3, w2):
    a = _dot(h, w1)
    b = _dot(h, w3)
    return _dot((a * _sigmoid(a) * b).astype(BF16), w2)


def _mix_residual(x_ref, mix_refs, mod, wo_ref):
    att_ref, hf_ref, hb_ref, om_ref, gml_ref = mix_refs
    mix = _dot(att_ref[...], wo_ref[:ATT_WIDTH, :])
    for h in range(ML_HEADS):
        sl = slice(h * ML_V_DIM, (h + 1) * ML_V_DIM)
        tot = hf_ref[:, sl].astype(F32) + hb_ref[:, sl].astype(F32)
        ml = _rms(tot, gml_ref[:, sl]) * _sigmoid(om_ref[:, sl].astype(F32))
        mix = mix + _dot(ml.astype(BF16), wo_ref[ATT_WIDTH + h * ML_V_DIM:ATT_WIDTH + (h + 1) * ML_V_DIM, :])
    return x_ref[...] + mod[2:3] * mix


def _mix_specs(tm, row):
    return [pl.BlockSpec((tm, ATT_WIDTH), row)] + [pl.BlockSpec((tm, ML_WIDTH), row)] * 3 \
        + [_resident((1, ML_WIDTH))]


def _dense_layer_kernel(x_ref, att_ref, hf_ref, hb_ref, om_ref, gml_ref, mod_ref, g_ref, wo_ref,
                        w1_ref, w3_ref, w2_ref, fg_ref, o_ref, *, final):
    mod = mod_ref[0]
    x1 = _mix_residual(x_ref, (att_ref, hf_ref, hb_ref, om_ref, gml_ref), mod, wo_ref)
    hx = _adaln(x1, g_ref[...], mod[3:4], mod[4:5]).astype(BF16)
    out = x1 + mod[5:6] * _swiglu(hx, w1_ref[...], w3_ref[...], w2_ref[...])
    if final:
        out = _rms(out, fg_ref[...])
    o_ref[...] = out


def _row_maps(seg, tm):
    nlat = seg["n_lat"] // tm
    s_tiles = seg["S"] // tm
    return (lambda i, *_: (i, 0)), (lambda i, *_: (jnp.where(i < nlat, i // s_tiles, seg["B"]), 0, 0))


def _dense_layer(xs, mix, mod, g2, wo, w1, w3, w2, final_g, seg, n_rows, final):
    d = xs.shape[1]
    tm = seg["tm"]
    row, mod_map = _row_maps(seg, tm)
    return pl.pallas_call(
        functools.partial(_dense_layer_kernel, final=final),
        grid=(n_rows // tm,),
        in_specs=[pl.BlockSpec((tm, d), row)] + _mix_specs(tm, row)
                 + [pl.BlockSpec((1, 6, d), mod_map),
                    _resident((1, d)), _resident(wo.shape), _resident(w1.shape), _resident(w3.shape),
                    _resident(w2.shape), _resident((1, d))],
        out_specs=pl.BlockSpec((tm, d), row),
        out_shape=jax.ShapeDtypeStruct((n_rows, d), F32),
        compiler_params=_params(1),
        name="dense_layer",
    )(xs, *mix, mod, g2, wo, w1, w3, w2, final_g)


def _router_kernel(x_ref, att_ref, hf_ref, hb_ref, om_ref, gml_ref, mod_ref, g_ref, wo_ref, wr_ref, br_ref,
                   x1_ref, h_ref, route_ref):
    mod = mod_ref[0]
    x1 = _mix_residual(x_ref, (att_ref, hf_ref, hb_ref, om_ref, gml_ref), mod, wo_ref)
    x1_ref[...] = x1
    hx = _adaln(x1, g_ref[...], mod[3:4], mod[4:5])
    h_ref[...] = _pack_bf16_pairs(hx)
    logits = _dot_f32(hx, wr_ref[...]) + br_ref[...]
    lane = lax.broadcasted_iota(jnp.int32, logits.shape, 1)
    logits = jnp.where(lane < N_EXPERTS, logits, -jnp.inf)
    top1 = jnp.max(logits, axis=-1, keepdims=True)
    idx1 = jnp.min(jnp.where(logits == top1, lane, LANES), axis=-1, keepdims=True)
    rest = jnp.where(lane == idx1, -jnp.inf, logits)
    top2 = jnp.max(rest, axis=-1, keepdims=True)
    idx2 = jnp.min(jnp.where(rest == top2, lane, LANES), axis=-1, keepdims=True)
    e2 = jnp.exp(top2 - top1)
    w_first = 1.0 / (1.0 + e2)
    route_ref[...] = jnp.where(lane == 0, idx1.astype(F32),
                               jnp.where(lane == 1, idx2.astype(F32),
                                         jnp.where(lane == 2, w_first,
                                                   jnp.where(lane == 3, e2 * w_first, 0.0))))


def _router(xs, mix, mod, g2, wo, wr, br, seg, n_rows):
    d = xs.shape[1]
    tm = seg["tm"]
    row, mod_map = _row_maps(seg, tm)
    return pl.pallas_call(
        _router_kernel,
        grid=(n_rows // tm,),
        in_specs=[pl.BlockSpec((tm, d), row)] + _mix_specs(tm, row)
                 + [pl.BlockSpec((1, 6, d), mod_map),
                    _resident((1, d)), _resident(wo.shape), _resident(wr.shape), _resident(br.shape)],
        out_specs=[pl.BlockSpec((tm, d), row), pl.BlockSpec((tm, d // 2), row),
                   pl.BlockSpec((tm, LANES), row)],
        out_shape=[jax.ShapeDtypeStruct((n_rows, d), F32), jax.ShapeDtypeStruct((n_rows, d // 2), jnp.int32),
                   jax.ShapeDtypeStruct((n_rows, LANES), F32)],
        compiler_params=_params(1),
        name="mix_router",
    )(xs, *mix, mod, g2, wo, wr, br)


def _pack_bf16_pairs(h):
    half = h.shape[1] // 2
    hi = lax.bitcast_convert_type(h[:, :half].astype(BF16).astype(F32), jnp.int32)
    lo = lax.bitcast_convert_type(h[:, half:].astype(BF16).astype(F32), jnp.int32)
    return (hi & jnp.int32(-65536)) | lax.shift_right_logical(lo, 16)


def _unpack_bf16_pairs(p):
    hi = lax.bitcast_convert_type(p & jnp.int32(-65536), F32)
    lo = lax.bitcast_convert_type(lax.shift_left(p, 16), F32)
    return hi, lo


def _route_plan(route, tm):
    n_rows = route.shape[0]
    n_slots = TOP_K * n_rows
    idx1 = route[:, 0].astype(jnp.int32)
    idx2 = route[:, 1].astype(jnp.int32)
    hot = jax.nn.one_hot(idx1, N_EXPERTS, dtype=jnp.int32) + jax.nn.one_hot(idx2, N_EXPERTS, dtype=jnp.int32)
    incl = jnp.cumsum(hot, axis=0)
    offs = jnp.concatenate([jnp.zeros((1,), jnp.int32), jnp.cumsum(incl[-1])])
    rank = incl - hot
    slot_a = offs[idx1] + jnp.take_along_axis(rank, idx1[:, None], axis=1)[:, 0]
    slot_b = offs[idx2] + jnp.take_along_axis(rank, idx2[:, None], axis=1)[:, 0]
    tok = jnp.arange(n_rows, dtype=jnp.int32)
    token_of_slot = (jnp.zeros((n_slots,), jnp.int32).at[slot_a].set(tok, unique_indices=True)
                     .at[slot_b].set(tok, unique_indices=True))
    n_tiles = n_slots // tm
    t_start = jnp.arange(n_tiles, dtype=jnp.int32) * tm
    e_first = jnp.searchsorted(offs[1:], t_start, side="right").astype(jnp.int32)
    base_hi = jnp.minimum(t_start + tm, offs[e_first + 1])
    e_next = jnp.arange(1, N_EXPERTS, dtype=jnp.int32)
    start = offs[1:N_EXPERTS]
    x_tile = jnp.minimum(start // tm, n_tiles - 1)
    x_hi = jnp.where(start % tm != 0, jnp.minimum(offs[2:], (x_tile + 1) * tm), start)
    tiles = jnp.concatenate([t_start // tm, x_tile])
    experts = jnp.concatenate([e_first, e_next])
    lo = jnp.concatenate([t_start, start])
    hi = jnp.concatenate([base_hi, x_hi])
    order = jnp.argsort(tiles * (2 * N_EXPERTS) + experts)
    tiles, experts, lo, hi = tiles[order], experts[order], lo[order], hi[order]
    change = tiles[1:] != tiles[:-1]
    one = jnp.ones((1,), bool)
    first = jnp.concatenate([one, change]).astype(jnp.int32)
    last = jnp.concatenate([change, one]).astype(jnp.int32)
    return slot_a, slot_b, token_of_slot, (tiles, experts, lo, hi, first, last)


def _gather_rows(table, idx):
    n_idx = idx.shape[0]
    width = table.shape[1]
    workers = SC_CORES * SC_SUBCORES
    per_worker = n_idx // workers
    assert n_idx % (workers * SC_GATHER_ROWS) == 0
    mesh = plsc.VectorSubcoreMesh(core_axis_name="c", subcore_axis_name="s")

    @functools.partial(
        pl.kernel, mesh=mesh,
        out_type=jax.ShapeDtypeStruct((n_idx, width), table.dtype),
        scratch_types=[pltpu.VMEM((SC_GATHER_ROWS,), jnp.int32),
                       pltpu.VMEM((SC_GATHER_ROWS, width), table.dtype),
                       pltpu.SemaphoreType.DMA],
        name="gather_rows")
    def gather(table_hbm, idx_hbm, out_hbm, idx_v, rows_v, sem):
        base = (lax.axis_index("s") * SC_CORES + lax.axis_index("c")) * per_worker

        @pl.loop(0, per_worker // SC_GATHER_ROWS)
        def _(i):
            off = pl.multiple_of(base + i * SC_GATHER_ROWS, SC_GATHER_ROWS)
            pltpu.sync_copy(idx_hbm.at[pl.ds(off, SC_GATHER_ROWS)], idx_v)
            pltpu.async_copy(table_hbm.at[idx_v], rows_v, sem).wait()
            pltpu.sync_copy(rows_v, out_hbm.at[pl.ds(off, SC_GATHER_ROWS)])

    return gather(table, idx)


def _experts_kernel(tile_ref, exp_ref, lo_ref, hi_ref, first_ref, last_ref,
                    x_ref, w1_ref, w3_ref, w2_ref, o_ref, acc_ref):
    i = pl.program_id(0)
    tm, half = x_ref.shape
    f = w1_ref.shape[2]
    fc = f // FFN_CHUNKS

    @pl.when(first_ref[i] == 1)
    def _():
        acc_ref[...] = jnp.zeros_like(acc_ref)

    lo, hi = lo_ref[i], hi_ref[i]

    @pl.when(hi > lo)
    def _():
        x_hi, x_lo = _unpack_bf16_pairs(x_ref[...])
        x_hi, x_lo = x_hi.astype(BF16), x_lo.astype(BF16)
        y = None
        for c in range(FFN_CHUNKS):
            cols = slice(c * fc, (c + 1) * fc)
            a = _dot(x_hi, w1_ref[0, :half, cols]) + _dot(x_lo, w1_ref[0, half:, cols])
            b = _dot(x_hi, w3_ref[0, :half, cols]) + _dot(x_lo, w3_ref[0, half:, cols])
            part = _dot((a * _sigmoid(a) * b).astype(BF16), w2_ref[0, cols, :])
            y = part if y is None else y + part
        rows = tile_ref[i] * tm + lax.broadcasted_iota(jnp.int32, (tm, 1), 0)
        keep = jnp.logical_and(rows >= lo, rows < hi)
        acc_ref[...] += jnp.where(keep, y, 0.0)

    @pl.when(last_ref[i] == 1)
    def _():
        o_ref[...] = _pack_bf16_pairs(acc_ref[...])


def _experts(xs_sorted, items, w1, w3, w2):
    n_slots, half = xs_sorted.shape
    n_exp, d, f = w1.shape
    tm = MOE_ROW_TILE
    assert f % (FFN_CHUNKS * LANES) == 0
    tile_map = lambda i, tiles, *_: (tiles[i], 0)
    exp_map = lambda i, tiles, experts, *_: (experts[i], 0, 0)
    grid_spec = pltpu.PrefetchScalarGridSpec(
        num_scalar_prefetch=len(items),
        grid=(items[0].shape[0],),
        in_specs=[pl.BlockSpec((tm, half), tile_map),
                  pl.BlockSpec((1, d, f), exp_map), pl.BlockSpec((1, d, f), exp_map),
                  pl.BlockSpec((1, f, d), exp_map)],
        out_specs=pl.BlockSpec((tm, half), tile_map),
        scratch_shapes=[pltpu.VMEM((tm, d), F32)],
    )
    return pl.pallas_call(
        _experts_kernel,
        grid_spec=grid_spec,
        out_shape=jax.ShapeDtypeStruct((n_slots, half), jnp.int32),
        compiler_params=_params(1),
        name="experts",
    )(*items, xs_sorted, w1, w3, w2)


def _combine_kernel(x1_ref, ya_ref, yb_ref, route_ref, mod_ref, fg_ref, o_ref, *, final):
    half = ya_ref.shape[1]
    route = route_ref[...]
    wa, wb = route[:, 2:3], route[:, 3:4]
    a_hi, a_lo = _unpack_bf16_pairs(ya_ref[...])
    b_hi, b_lo = _unpack_bf16_pairs(yb_ref[...])
    gate = mod_ref[0][5:6]
    out_hi = x1_ref[:, :half] + gate[:, :half] * (wa * a_hi + wb * b_hi)
    out_lo = x1_ref[:, half:] + gate[:, half:] * (wa * a_lo + wb * b_lo)
    if final:
        total = jnp.sum(out_hi * out_hi, axis=-1, keepdims=True) + jnp.sum(out_lo * out_lo, axis=-1, keepdims=True)
        scale = lax.rsqrt(total / (2 * half) + EPS)
        out_hi = out_hi * scale * fg_ref[:, :half]
        out_lo = out_lo * scale * fg_ref[:, half:]
    o_ref[:, :half] = out_hi
    o_ref[:, half:] = out_lo


def _combine(x1, y_pairs, route, mod, final_g, seg, final):
    n_rows, d = x1.shape
    tm = seg["tm"]
    row, mod_map = _row_maps(seg, tm)
    second = n_rows // tm
    return pl.pallas_call(
        functools.partial(_combine_kernel, final=final),
        grid=(n_rows // tm,),
        in_specs=[pl.BlockSpec((tm, d), row), pl.BlockSpec((tm, d // 2), row),
                  pl.BlockSpec((tm, d // 2), lambda i: (second + i, 0)),
                  pl.BlockSpec((tm, LANES), row), pl.BlockSpec((1, 6, d), mod_map),
                  pl.BlockSpec((1, d), lambda i: (0, 0))],
        out_specs=pl.BlockSpec((tm, d), row),
        out_shape=jax.ShapeDtypeStruct((n_rows, d), F32),
        compiler_params=_params(1),
        name="moe_combine",
    )(x1, y_pairs, y_pairs, route, mod, final_g)


def _moe_layer(xs, mix, mod, g2, wo, wr, br, w1, w3, w2, final_g, seg, n_rows, final):
    x1, h_pairs, route = _router(xs, mix, mod, g2, wo, wr, br, seg, n_rows)
    slot_a, slot_b, token_of_slot, items = _route_plan(route, MOE_ROW_TILE)
    y_sorted = _experts(_gather_rows(h_pairs, token_of_slot), items, w1, w3, w2)
    y_pairs = _gather_rows(y_sorted, jnp.concatenate([slot_a, slot_b]))
    return _combine(x1, y_pairs, route, mod, final_g, seg, final)


def kernel(x, c, ctx, c_ctx, norm1_g, norm2_g, w_mod, b_mod, w_in, conv_w, conv_b, b_gates, attn_sink,
           g_att, g_ml, w_out, ffn_w1, ffn_w3, ffn_w2, w_router, b_router, exp_w1, exp_w3, exp_w2,
           final_g):
    b_, s_, d = x.shape
    lc = ctx.shape[1]
    depth = w_in.shape[0]
    n_lat, n_ctx = b_ * s_, b_ * lc
    tm = min(ROW_TILE, s_)
    assert s_ % tm == 0 and n_ctx % tm == 0
    assert (TOP_K * n_lat) % MOE_ROW_TILE == 0 and (TOP_K * n_ctx) % MOE_ROW_TILE == 0
    assert s_ % ML_CHUNK == 0 and lc % ML_CHUNK == 0 and n_lat % lc == 0 and b_ < MOD_ROWS
    seg = dict(B=b_, S=s_, Lc=lc, n_lat=n_lat, tm=tm)

    cond = jnp.zeros((MOD_ROWS, d), F32).at[:b_].set(c).at[b_].set(c_ctx)
    mods = _modulation(cond, w_mod, b_mod).reshape(depth, MOD_ROWS, 6, d)
    rope = _rope_tables(s_, tm)
    xs = jnp.concatenate([x.reshape(n_lat, d), ctx.reshape(n_ctx, d)], axis=0)
    final_row = final_g.reshape(1, d)

    for layer in range(depth):
        last = layer == depth - 1
        w_l = w_in[layer]
        w_gates = w_l[:, MAIN_WIDTH:]
        w_gc = jnp.pad(w_gates, ((0, 0), (0, LANES - ML_GATES))).astype(BF16)
        b_gc = jnp.pad(b_gates[layer], (0, LANES - ML_GATES)).reshape(1, LANES)
        qa, ka, va, qm, km, vm, om, gc, gr = _input_projection(
            xs, mods[layer], norm1_g[layer].reshape(1, d), w_l[:, :MAIN_WIDTH].astype(BF16), w_gc,
            w_gates.T.astype(BF16), b_gc, b_gates[layer].reshape(ML_GATES, 1),
            conv_w[layer], conv_b[layer].reshape(1, -1), rope, seg)
        att = _attention(qa, ka, va, attn_sink[layer], g_att[layer].reshape(1, ATT_WIDTH), seg, not last)
        hf, hb = _mlstm(qm, km, vm, gc, gr, seg)
        mix = (att, hf, hb, om, g_ml[layer].reshape(1, ML_WIDTH))
        n_rows = n_lat if last else n_lat + n_ctx
        g2 = norm2_g[layer].reshape(1, d)
        wo = w_out[layer].astype(BF16)
        i = layer // 2
        if layer % 2 == 0:
            xs = _dense_layer(xs, mix, mods[layer], g2, wo, ffn_w1[i].astype(BF16),
                              ffn_w3[i].astype(BF16), ffn_w2[i].astype(BF16), final_row, seg, n_rows, last)
        else:
            wr = jnp.pad(w_router[i], ((0, 0), (0, LANES - N_EXPERTS)))
            br = jnp.pad(b_router[i], (0, LANES - N_EXPERTS)).reshape(1, LANES)
            xs = _moe_layer(xs, mix, mods[layer], g2, wo, wr, br, exp_w1[i].astype(BF16),
                            exp_w3[i].astype(BF16), exp_w2[i].astype(BF16), final_row, seg, n_rows, last)
    return xs[:n_lat].reshape(b_, s_, d)
```

```python
import functools

import jax
import jax.numpy as jnp
import numpy as np
from jax import lax
from jax.experimental import pallas as pl
from jax.experimental.pallas import tpu as pltpu
from jax.experimental.pallas import tpu_sc as plsc

F32 = jnp.float32
BF16 = jnp.bfloat16

GRID_W = 64
ATT_HEADS = 8
ATT_KV_HEADS = 2
ATT_HEAD_DIM = 64
ATT_GROUP = ATT_HEADS // ATT_KV_HEADS
WINDOW = 128
ATT_BLOCK = 128
ROPE_THETA = 10000.0
ML_HEADS = 4
ML_QK_DIM = 64
ML_V_DIM = 128
ML_CONV = 5
GATE_CAP = 15.0
ATT_WIDTH = ATT_HEADS * ATT_HEAD_DIM
ATT_KV_WIDTH = ATT_KV_HEADS * ATT_HEAD_DIM
ML_QK_WIDTH = ML_HEADS * ML_QK_DIM
ML_WIDTH = ML_HEADS * ML_V_DIM
ML_GATES = 4 * ML_HEADS
MAIN_WIDTH = ATT_WIDTH + 2 * ATT_KV_WIDTH + 2 * ML_QK_WIDTH + 2 * ML_WIDTH
N_EXPERTS = 8
TOP_K = 2
EPS = 1e-6

LANES = 128
SUBLANES = 8
VMEM_LIMIT = 56 * 1024 * 1024
NEG = -1e30
SC_CORES = 2
SC_SUBCORES = 16
SC_GATHER_ROWS = 64

ROW_TILE = 512
MOE_ROW_TILE = 512
FFN_CHUNKS = 2
ML_CHUNK = 128
CONV_HALO = SUBLANES
MOD_ROWS = 16
MOD_COL_TILE = 1536


def _dot(a, b):
    return jnp.dot(a, b, preferred_element_type=F32)


def _dot_nt(a, b):
    return lax.dot_general(a, b, (((1,), (1,)), ((), ())), preferred_element_type=F32)


def _dot_tn(a, b):
    return lax.dot_general(a, b, (((0,), (0,)), ((), ())), preferred_element_type=F32)


def _dot_f32(a, b):
    return jnp.dot(a, b, preferred_element_type=F32, precision=lax.Precision.HIGHEST)


def _sigmoid(x):
    return 1.0 / (1.0 + jnp.exp(-x))


def _rms(x, g):
    return x * lax.rsqrt(jnp.mean(x * x, axis=-1, keepdims=True) + EPS) * g


def _adaln(x, g, shift, scale):
    return _rms(x, g) * (1.0 + scale) + shift


def _params(n_axes):
    return pltpu.CompilerParams(dimension_semantics=("arbitrary",) * n_axes,
                                vmem_limit_bytes=VMEM_LIMIT)


def _resident(shape):
    zeros = (0,) * len(shape)
    return pl.BlockSpec(shape, lambda *_: zeros, pipeline_mode=pl.Buffered(1))


def _mod_kernel(c_ref, w_ref, b_ref, o_ref):
    c = c_ref[...]
    a = (c * _sigmoid(c)).astype(BF16)
    o_ref[0] = _dot(a, w_ref[0].astype(BF16)) + b_ref[0]


def _modulation(cond, w_mod, b_mod):
    depth, d, width = w_mod.shape
    tn = MOD_COL_TILE if width % MOD_COL_TILE == 0 else width
    return pl.pallas_call(
        _mod_kernel,
        grid=(depth, width // tn),
        in_specs=[pl.BlockSpec((MOD_ROWS, d), lambda l, j: (0, 0)),
                  pl.BlockSpec((1, d, tn), lambda l, j: (l, 0, j)),
                  pl.BlockSpec((1, 1, tn), lambda l, j: (l, 0, j))],
        out_specs=pl.BlockSpec((1, MOD_ROWS, tn), lambda l, j: (l, 0, j)),
        out_shape=jax.ShapeDtypeStruct((depth, MOD_ROWS, width), F32),
        compiler_params=_params(2),
        name="modulation",
    )(cond, w_mod, b_mod.reshape(depth, 1, width))


def _gate_act(u, is_forget):
    g = GATE_CAP * jnp.tanh(u / GATE_CAP)
    log_sig = jnp.minimum(g, 0.0) - jnp.log1p(jnp.exp(-jnp.abs(g)))
    return jnp.where(is_forget, log_sig, g)


def _conv_silu(xe, cw, cb, rows):
    n_ext = xe.shape[0]
    mid = ML_CONV // 2
    y = cb + cw[mid:mid + 1] * xe[CONV_HALO:CONV_HALO + rows]
    for tap in range(ML_CONV):
        if tap != mid:
            y = y + cw[tap:tap + 1] * pltpu.roll(xe, (mid - tap) % n_ext, 0)[CONV_HALO:CONV_HALO + rows]
    return y * _sigmoid(y)


def _inproj_kernel(x_ref, xp_ref, xn_ref, mod_ref, g_ref, w_ref, wgc_ref, wgr_ref, bgc_ref, bgr_ref,
                   cw_ref, cb_ref, cos_ref, sa_ref, sb_ref,
                   qa_ref, ka_ref, va_ref, qm_ref, km_ref, vm_ref, om_ref, gc_ref, gr_ref,
                   *, n_lat_tiles, seq_lat, seq_ctx):
    mod = mod_ref[0]
    tm = x_ref.shape[0]
    normed = lambda ref: _adaln(ref[...], g_ref[...], mod[0:1], mod[1:2]).astype(BF16)
    hx = normed(x_ref)
    cos, sa, sb = cos_ref[...], sa_ref[...], sb_ref[...]
    quarter = ATT_HEAD_DIM // 4

    def rope(u):
        return (u * cos + pltpu.roll(u, LANES - quarter, 1) * sa + pltpu.roll(u, quarter, 1) * sb)

    c0 = 0
    q = _dot(hx, w_ref[:, c0:c0 + ATT_WIDTH])
    for j in range(ATT_WIDTH // LANES):
        sl = slice(j * LANES, (j + 1) * LANES)
        qa_ref[:, sl] = (rope(q[:, sl]) * (ATT_HEAD_DIM ** -0.5)).astype(BF16)
    c0 += ATT_WIDTH
    kv = _dot(hx, w_ref[:, c0:c0 + 2 * ATT_KV_WIDTH])
    ka_ref[...] = rope(kv[:, :ATT_KV_WIDTH]).astype(BF16)
    va_ref[...] = kv[:, ATT_KV_WIDTH:].astype(BF16)
    c0 += 2 * ATT_KV_WIDTH
    w_qk = w_ref[:, c0:c0 + 2 * ML_QK_WIDTH]
    qk = _dot(hx, w_qk)
    qk_prev = _dot(normed(xp_ref), w_qk)
    qk_next = _dot(normed(xn_ref), w_qk)
    i = pl.program_id(0)
    seq_len = jnp.where(i < n_lat_tiles, seq_lat, seq_ctx)
    seg = min(tm, seq_ctx)
    cw, cb = cw_ref[...], cb_ref[...]
    for j in range(tm // seg):
        first_row = i * tm + j * seg
        has_prev = (lax.rem(first_row, seq_len) != 0).astype(F32)
        has_next = (lax.rem(first_row + seg, seq_len) != 0).astype(F32)
        prev = qk_prev if j == 0 else qk[j * seg - CONV_HALO:j * seg]
        nxt = qk_next if (j + 1) * seg == tm else qk[(j + 1) * seg:(j + 1) * seg + CONV_HALO]
        xe = jnp.concatenate([prev * has_prev, qk[j * seg:(j + 1) * seg], nxt * has_next], axis=0)
        y = _conv_silu(xe, cw, cb, seg)
        qm_ref[j * seg:(j + 1) * seg, :] = (y[:, :ML_QK_WIDTH] * (ML_QK_DIM ** -0.5)).astype(BF16)
        km_ref[j * seg:(j + 1) * seg, :] = y[:, ML_QK_WIDTH:].astype(BF16)
    c0 += 2 * ML_QK_WIDTH
    vm_ref[...] = _dot(hx, w_ref[:, c0:c0 + ML_WIDTH]).astype(BF16)
    c0 += ML_WIDTH
    om_ref[...] = _dot(hx, w_ref[:, c0:c0 + ML_WIDTH]).astype(BF16)
    gc = _dot(hx, wgc_ref[...]) + bgc_ref[...]
    lane = lax.broadcasted_iota(jnp.int32, gc.shape, 1)
    gc_ref[...] = _gate_act(gc, (lane // ML_HEADS) % 2 == 1)
    gr = _dot_nt(wgr_ref[...], hx) + bgr_ref[...]
    sub = lax.broadcasted_iota(jnp.int32, gr.shape, 0)
    gr_ref[...] = _gate_act(gr, (sub // ML_HEADS) % 2 == 1)


def _input_projection(xs, mod, g1, w_main, w_gc, w_gr, b_gc, b_gr, conv_w, conv_b, rope, seg):
    n, d = xs.shape
    tm = seg["tm"]
    nlat = seg["n_lat"] // tm
    s_tiles = seg["S"] // tm
    halos_per_tile = tm // CONV_HALO
    row = lambda i: (i, 0)
    prev = lambda i: (jnp.maximum(i * halos_per_tile - 1, 0), 0)
    nxt = lambda i: (jnp.minimum((i + 1) * halos_per_tile, n // CONV_HALO - 1), 0)
    mod_map = lambda i: (jnp.where(i < nlat, i // s_tiles, seg["B"]), 0, 0)
    rope_map = lambda i: (jnp.where(i < nlat, i % s_tiles, s_tiles), 0)
    widths = [(ATT_WIDTH, BF16), (ATT_KV_WIDTH, BF16), (ATT_KV_WIDTH, BF16), (ML_QK_WIDTH, BF16),
              (ML_QK_WIDTH, BF16), (ML_WIDTH, BF16), (ML_WIDTH, BF16), (LANES, F32)]
    out_shape = [jax.ShapeDtypeStruct((n, w), t) for w, t in widths]
    out_specs = [pl.BlockSpec((tm, w), row) for w, _ in widths]
    out_shape.append(jax.ShapeDtypeStruct((ML_GATES, n), F32))
    out_specs.append(pl.BlockSpec((ML_GATES, tm), lambda i: (0, i)))
    return pl.pallas_call(
        functools.partial(_inproj_kernel, n_lat_tiles=nlat, seq_lat=seg["S"], seq_ctx=seg["Lc"]),
        grid=(n // tm,),
        in_specs=[pl.BlockSpec((tm, d), row),
                  pl.BlockSpec((CONV_HALO, d), prev),
                  pl.BlockSpec((CONV_HALO, d), nxt),
                  pl.BlockSpec((1, 6, d), mod_map),
                  _resident((1, d)),
                  _resident(w_main.shape), _resident(w_gc.shape), _resident(w_gr.shape),
                  _resident(b_gc.shape), _resident(b_gr.shape),
                  _resident(conv_w.shape), _resident(conv_b.shape),
                  pl.BlockSpec((tm, LANES), rope_map),
                  pl.BlockSpec((tm, LANES), rope_map),
                  pl.BlockSpec((tm, LANES), rope_map)],
        out_specs=out_specs,
        out_shape=out_shape,
        compiler_params=_params(1),
        name="input_projection",
    )(xs, xs, xs, mod, g1, w_main, w_gc, w_gr, b_gc, b_gr, conv_w, conv_b, *rope)


def _rope_tables(s, tm):
    quarter = ATT_HEAD_DIM // 4
    t = jnp.arange(s)
    row = (t // GRID_W).astype(F32)
    col = (t % GRID_W).astype(F32)
    inv = ROPE_THETA ** (-jnp.arange(quarter, dtype=F32) / quarter)
    ang_r = row[:, None] * inv[None, :]
    ang_c = col[:, None] * inv[None, :]
    zero = jnp.zeros_like(ang_r)
    cos = jnp.concatenate([jnp.cos(ang_r)] * 2 + [jnp.cos(ang_c)] * 2, axis=1)
    sin_up = jnp.concatenate([-jnp.sin(ang_r), zero, -jnp.sin(ang_c), zero], axis=1)
    sin_dn = jnp.concatenate([zero, jnp.sin(ang_r), zero, jnp.sin(ang_c)], axis=1)
    reps = LANES // ATT_HEAD_DIM
    ident = [jnp.ones((tm, LANES), F32), jnp.zeros((tm, LANES), F32), jnp.zeros((tm, LANES), F32)]
    return tuple(jnp.concatenate([jnp.tile(a, (1, reps)), i], axis=0)
                 for a, i in zip((cos, sin_up, sin_dn), ident))


def _attn_kernel(sink_ref, q_ref, kp_ref, kc_ref, kn_ref, vp_ref, vc_ref, vn_ref, kx_ref, vx_ref,
                 g_ref, o_ref, *, n_lat_blocks):
    j = pl.program_id(1)
    is_lat = j < n_lat_blocks
    blk = ATT_BLOCK
    rows = lax.broadcasted_iota(jnp.int32, (blk, blk), 0)
    cols = lax.broadcasted_iota(jnp.int32, (blk, blk), 1)
    ok_p = jnp.logical_and(cols >= rows, jnp.logical_and(is_lat, j > 0))
    ok_c = jnp.logical_and(cols >= 0, is_lat)
    ok_n = jnp.logical_and(cols <= rows, jnp.logical_and(is_lat, j < n_lat_blocks - 1))
    n_ctx = kx_ref.shape[0]
    bias = jnp.concatenate([jnp.where(ok_p, 0.0, NEG), jnp.where(ok_c, 0.0, NEG),
                            jnp.where(ok_n, 0.0, NEG), jnp.zeros((blk, n_ctx), F32)], axis=1)
    q = q_ref[...]
    k_all = jnp.concatenate([kp_ref[...], kc_ref[...], kn_ref[...], kx_ref[...]], axis=0)
    v_all = jnp.concatenate([vp_ref[...], vc_ref[...], vn_ref[...], vx_ref[...]], axis=0)
    dh = ATT_HEAD_DIM
    outs = []
    for h in range(ATT_KV_HEADS):
        k_h = k_all[:, h * dh:(h + 1) * dh]
        v_h = v_all[:, h * dh:(h + 1) * dh]
        q_h = jnp.concatenate([q[:, (h * ATT_GROUP + g) * dh:(h * ATT_GROUP + g + 1) * dh]
                               for g in range(ATT_GROUP)], axis=0)
        s_all = _dot_nt(q_h, k_h)
        p_parts, inv_parts = [], []
        for g in range(ATT_GROUP):
            sink = sink_ref[h * ATT_GROUP + g]
            s = s_all[g * blk:(g + 1) * blk] + bias
            m = jnp.maximum(jnp.max(s, axis=-1, keepdims=True), sink)
            p = jnp.exp(s - m)
            denom = jnp.sum(p, axis=-1, keepdims=True) + jnp.exp(sink - m)
            p_parts.append(p.astype(BF16))
            inv_parts.append(1.0 / denom)
        o = _dot(jnp.concatenate(p_parts, axis=0), v_h)
        for g in range(ATT_GROUP):
            outs.append(o[g * blk:(g + 1) * blk] * inv_parts[g])
    att = jnp.concatenate(outs, axis=1)
    o_ref[...] = _rms(att, g_ref[...]).astype(o_ref.dtype)


def _attention(qa, ka, va, sink, g_att, seg, with_ctx):
    n = qa.shape[0]
    b_, s_, lc = seg["B"], seg["S"], seg["Lc"]
    blk = ATT_BLOCK
    nqb = s_ // blk
    ncb = lc // blk if with_ctx else 0
    lat_blocks = b_ * nqb

    def q_map(b, j, *_):
        return (jnp.where(j < nqb, b * nqb + j, lat_blocks + b * (lc // blk) + (j - nqb)), 0)

    def win_map(off):
        def f(b, j, *_):
            jj = jnp.clip(jnp.where(j < nqb, j, 0) + off, 0, nqb - 1)
            return (b * nqb + jj, 0)
        return f

    ctx_map = lambda b, j, *_: (b_ * s_ // lc + b, 0)
    kv_specs = [pl.BlockSpec((blk, ATT_KV_WIDTH), win_map(o)) for o in (-1, 0, 1)]
    grid_spec = pltpu.PrefetchScalarGridSpec(
        num_scalar_prefetch=1,
        grid=(b_, nqb + ncb),
        in_specs=[pl.BlockSpec((blk, ATT_WIDTH), q_map)] + kv_specs + kv_specs
                 + [pl.BlockSpec((lc, ATT_KV_WIDTH), ctx_map), pl.BlockSpec((lc, ATT_KV_WIDTH), ctx_map),
                    pl.BlockSpec((1, ATT_WIDTH), lambda b, j, *_: (0, 0))],
        out_specs=pl.BlockSpec((blk, ATT_WIDTH), q_map),
    )
    return pl.pallas_call(
        functools.partial(_attn_kernel, n_lat_blocks=nqb),
        grid_spec=grid_spec,
        out_shape=jax.ShapeDtypeStruct((n, ATT_WIDTH), BF16),
        compiler_params=_params(2),
        name="window_attention",
    )(sink, qa, ka, ka, ka, va, va, va, ka, va, g_att)


def _mlstm_direction(q_ref, k_ref, v_ref, gc_ref, gr_ref, out_ref, state_ref, *, reverse):
    chunk = q_ref.shape[0]
    rows = lax.broadcasted_iota(jnp.int32, (chunk, chunk), 0)
    cols = lax.broadcasted_iota(jnp.int32, (chunk, chunk), 1)
    lower = rows >= cols
    upper = rows <= cols
    seen = upper if reverse else lower
    gc = gc_ref[...]
    gr = gr_ref[...]
    b_col = _dot_f32(seen.astype(F32), gc)
    b_row = _dot_f32(gr, (lower if reverse else upper).astype(F32))
    b_end = jnp.sum(gc, axis=0, keepdims=True)
    base = 2 * ML_HEADS if reverse else 0
    pair_width = 2 * ML_QK_DIM
    lane = lax.broadcasted_iota(jnp.int32, (chunk, pair_width), 1)
    state_row = lax.broadcasted_iota(jnp.int32, (pair_width, 1), 0)
    ones = jnp.ones((chunk, ML_V_DIM), BF16)

    for pair in range(ML_HEADS // 2):
        q_pair = q_ref[:, pair * pair_width:(pair + 1) * pair_width]
        k_pair = k_ref[:, pair * pair_width:(pair + 1) * pair_width]
        state = state_ref[pair]
        state_bf = state.astype(BF16)
        update = None
        decays = []
        for sub in range(2):
            h = 2 * pair + sub
            i_idx = base + h
            f_idx = base + ML_HEADS + h
            own = (lane >= ML_QK_DIM) if sub else (lane < ML_QK_DIM)
            q_h = jnp.where(own, q_pair, jnp.zeros_like(q_pair))
            vx = jnp.concatenate([v_ref[:, h * ML_V_DIM:(h + 1) * ML_V_DIM], ones], axis=1)
            bc = b_col[:, f_idx:f_idx + 1]
            d = bc - b_row[f_idx:f_idx + 1, :] + gr[i_idx:i_idx + 1, :]
            w = jnp.exp(jnp.where(seen, d, NEG))
            s = _dot_nt(q_h, k_pair) * w
            tot = _dot(s.astype(BF16), vx) + jnp.exp(bc) * _dot(q_h, state_bf)
            h_out = tot[:, :ML_V_DIM] / jnp.maximum(jnp.abs(tot[:, ML_V_DIM:]), 1.0)
            out_ref[:, h * ML_V_DIM:(h + 1) * ML_V_DIM] = h_out.astype(out_ref.dtype)

            be = b_end[:, f_idx:f_idx + 1]
            kw = jnp.where(own, k_pair.astype(F32) * jnp.exp(be - bc + gc[:, i_idx:i_idx + 1]), 0.0)
            part = _dot_tn(kw.astype(BF16), vx)
            update = part if update is None else update + part
            decays.append(jnp.exp(be))
        decay = jnp.where(state_row < ML_QK_DIM, decays[0], decays[1])
        state_ref[pair] = decay * state + update


def _mlstm_kernel(qf_ref, kf_ref, vf_ref, gcf_ref, grf_ref, qb_ref, kb_ref, vb_ref, gcb_ref, grb_ref,
                  hf_ref, hb_ref, sf_ref, sb_ref):
    @pl.when(pl.program_id(1) == 0)
    def _():
        sf_ref[...] = jnp.zeros_like(sf_ref)
        sb_ref[...] = jnp.zeros_like(sb_ref)

    _mlstm_direction(qf_ref, kf_ref, vf_ref, gcf_ref, grf_ref, hf_ref, sf_ref, reverse=False)
    _mlstm_direction(qb_ref, kb_ref, vb_ref, gcb_ref, grb_ref, hb_ref, sb_ref, reverse=True)


def _mlstm(qm, km, vm, gc, gr, seg):
    n = qm.shape[0]
    b_, s_, lc = seg["B"], seg["S"], seg["Lc"]
    chunk = ML_CHUNK
    ncc, ncl = lc // chunk, s_ // chunk
    lat_chunks = b_ * ncl

    def blk(b, c, reverse):
        pos = jnp.where(c < ncc, c, c - ncc)
        if reverse:
            pos = jnp.where(c < ncc, ncc, ncl) - 1 - pos
        return jnp.where(c < ncc, lat_chunks + b * ncc + pos, b * ncl + pos)

    def specs(reverse):
        cur = lambda b, c: (blk(b, c, reverse), 0)
        return [pl.BlockSpec((chunk, ML_QK_WIDTH), cur), pl.BlockSpec((chunk, ML_QK_WIDTH), cur),
                pl.BlockSpec((chunk, ML_WIDTH), cur), pl.BlockSpec((chunk, LANES), cur),
                pl.BlockSpec((ML_GATES, chunk), lambda b, c: (0, blk(b, c, reverse)))]

    out = lambda reverse: pl.BlockSpec((chunk, ML_WIDTH), lambda b, c: (blk(b, c, reverse), 0))
    state = pltpu.VMEM((ML_HEADS // 2, 2 * ML_QK_DIM, 2 * ML_V_DIM), F32)
    return pl.pallas_call(
        _mlstm_kernel,
        grid=(b_, ncc + ncl),
        in_specs=specs(False) + specs(True),
        out_specs=[out(False), out(True)],
        out_shape=[jax.ShapeDtypeStruct((n, ML_WIDTH), BF16)] * 2,
        scratch_shapes=[state, state],
        compiler_params=_params(2),
        name="mlstm_scan",
    )(qm, km, vm, gc, gr, qm, km, vm, gc, gr)


def _swiglu(h, w1, w3, w2):
    a = _dot(h, w1)
    b = _dot(h, w3)
    return _dot((a * _sigmoid(a) * b).astype(BF16), w2)


def _mix_residual(x_ref, mix_refs, mod, wo_ref):
    att_ref, hf_ref, hb_ref, om_ref, gml_ref = mix_refs
    mix = _dot(att_ref[...], wo_ref[:ATT_WIDTH, :])
    for h in range(ML_HEADS):
        sl = slice(h * ML_V_DIM, (h + 1) * ML_V_DIM)
        tot = hf_ref[:, sl].astype(F32) + hb_ref[:, sl].astype(F32)
        ml = _rms(tot, gml_ref[:, sl]) * _sigmoid(om_ref[:, sl].astype(F32))
        mix = mix + _dot(ml.astype(BF16), wo_ref[ATT_WIDTH + h * ML_V_DIM:ATT_WIDTH + (h + 1) * ML_V_DIM, :])
    return x_ref[...] + mod[2:3] * mix


def _mix_specs(tm, row):
    return [pl.BlockSpec((tm, ATT_WIDTH), row)] + [pl.BlockSpec((tm, ML_WIDTH), row)] * 3 \
        + [_resident((1, ML_WIDTH))]


def _dense_layer_kernel(x_ref, att_ref, hf_ref, hb_ref, om_ref, gml_ref, mod_ref, g_ref, wo_ref,
                        w1_ref, w3_ref, w2_ref, fg_ref, o_ref, *, final):
    mod = mod_ref[0]
    x1 = _mix_residual(x_ref, (att_ref, hf_ref, hb_ref, om_ref, gml_ref), mod, wo_ref)
    hx = _adaln(x1, g_ref[...], mod[3:4], mod[4:5]).astype(BF16)
    out = x1 + mod[5:6] * _swiglu(hx, w1_ref[...], w3_ref[...], w2_ref[...])
    if final:
        out = _rms(out, fg_ref[...])
    o_ref[...] = out


def _row_maps(seg, tm):
    nlat = seg["n_lat"] // tm
    s_tiles = seg["S"] // tm
    return (lambda i, *_: (i, 0)), (lambda i, *_: (jnp.where(i < nlat, i // s_tiles, seg["B"]), 0, 0))


def _dense_layer(xs, mix, mod, g2, wo, w1, w3, w2, final_g, seg, n_rows, final):
    d = xs.shape[1]
    tm = seg["tm"]
    row, mod_map = _row_maps(seg, tm)
    return pl.pallas_call(
        functools.partial(_dense_layer_kernel, final=final),
        grid=(n_rows // tm,),
        in_specs=[pl.BlockSpec((tm, d), row)] + _mix_specs(tm, row)
                 + [pl.BlockSpec((1, 6, d), mod_map),
                    _resident((1, d)), _resident(wo.shape), _resident(w1.shape), _resident(w3.shape),
                    _resident(w2.shape), _resident((1, d))],
        out_specs=pl.BlockSpec((tm, d), row),
        out_shape=jax.ShapeDtypeStruct((n_rows, d), F32),
        compiler_params=_params(1),
        name="dense_layer",
    )(xs, *mix, mod, g2, wo, w1, w3, w2, final_g)


def _router_kernel(x_ref, att_ref, hf_ref, hb_ref, om_ref, gml_ref, mod_ref, g_ref, wo_ref, wr_ref, br_ref,
                   x1_ref, h_ref, route_ref):
    mod = mod_ref[0]
    x1 = _mix_residual(x_ref, (att_ref, hf_ref, hb_ref, om_ref, gml_ref), mod, wo_ref)
    x1_ref[...] = x1
    hx = _adaln(x1, g_ref[...], mod[3:4], mod[4:5])
    h_ref[...] = _pack_bf16_pairs(hx)
    hx_hi = hx.astype(BF16)
    hx_lo = (hx - hx_hi.astype(F32)).astype(BF16)
    both = _dot(hx_hi, wr_ref[...])
    logits = both[:, :LANES] + both[:, LANES:] + _dot(hx_lo, wr_ref[:, :LANES]) + br_ref[...]
    lane = lax.broadcasted_iota(jnp.int32, logits.shape, 1)
    logits = jnp.where(lane < N_EXPERTS, logits, -jnp.inf)
    top1 = jnp.max(logits, axis=-1, keepdims=True)
    idx1 = jnp.min(jnp.where(logits == top1, lane, LANES), axis=-1, keepdims=True)
    rest = jnp.where(lane == idx1, -jnp.inf, logits)
    top2 = jnp.max(rest, axis=-1, keepdims=True)
    idx2 = jnp.min(jnp.where(rest == top2, lane, LANES), axis=-1, keepdims=True)
    e2 = jnp.exp(top2 - top1)
    w_first = 1.0 / (1.0 + e2)
    route_ref[...] = jnp.where(lane == 0, idx1.astype(F32),
                               jnp.where(lane == 1, idx2.astype(F32),
                                         jnp.where(lane == 2, w_first,
                                                   jnp.where(lane == 3, e2 * w_first, 0.0))))


def _router(xs, mix, mod, g2, wo, wr, br, seg, n_rows):
    d = xs.shape[1]
    tm = seg["tm"]
    row, mod_map = _row_maps(seg, tm)
    return pl.pallas_call(
        _router_kernel,
        grid=(n_rows // tm,),
        in_specs=[pl.BlockSpec((tm, d), row)] + _mix_specs(tm, row)
                 + [pl.BlockSpec((1, 6, d), mod_map),
                    _resident((1, d)), _resident(wo.shape), _resident(wr.shape), _resident(br.shape)],
        out_specs=[pl.BlockSpec((tm, d), row), pl.BlockSpec((tm, d // 2), row),
                   pl.BlockSpec((tm, LANES), row)],
        out_shape=[jax.ShapeDtypeStruct((n_rows, d), F32), jax.ShapeDtypeStruct((n_rows, d // 2), jnp.int32),
                   jax.ShapeDtypeStruct((n_rows, LANES), F32)],
        compiler_params=_params(1),
        name="mix_router",
    )(xs, *mix, mod, g2, wo, wr, br)


def _pack_bf16_pairs(h):
    half = h.shape[1] // 2
    hi = lax.bitcast_convert_type(h[:, :half].astype(BF16).astype(F32), jnp.int32)
    lo = lax.bitcast_convert_type(h[:, half:].astype(BF16).astype(F32), jnp.int32)
    return (hi & jnp.int32(-65536)) | lax.shift_right_logical(lo, 16)


def _unpack_bf16_pairs(p):
    hi = lax.bitcast_convert_type(p & jnp.int32(-65536), F32)
    lo = lax.bitcast_convert_type(lax.shift_left(p, 16), F32)
    return hi, lo


def _route_plan(route, tm):
    n_rows = route.shape[0]
    n_slots = TOP_K * n_rows
    idx1 = route[:, 0].astype(jnp.int32)
    idx2 = route[:, 1].astype(jnp.int32)
    hot = jax.nn.one_hot(idx1, N_EXPERTS, dtype=jnp.int32) + jax.nn.one_hot(idx2, N_EXPERTS, dtype=jnp.int32)
    incl = jnp.cumsum(hot, axis=0)
    offs = jnp.concatenate([jnp.zeros((1,), jnp.int32), jnp.cumsum(incl[-1])])
    rank = incl - hot
    slot_a = offs[idx1] + jnp.take_along_axis(rank, idx1[:, None], axis=1)[:, 0]
    slot_b = offs[idx2] + jnp.take_along_axis(rank, idx2[:, None], axis=1)[:, 0]
    n_tiles = n_slots // tm
    t_start = jnp.arange(n_tiles, dtype=jnp.int32) * tm
    e_first = jnp.searchsorted(offs[1:], t_start, side="right").astype(jnp.int32)
    base_hi = jnp.minimum(t_start + tm, offs[e_first + 1])
    e_next = jnp.arange(1, N_EXPERTS, dtype=jnp.int32)
    start = offs[1:N_EXPERTS]
    x_tile = jnp.minimum(start // tm, n_tiles - 1)
    x_hi = jnp.where(start % tm != 0, jnp.minimum(offs[2:], (x_tile + 1) * tm), start)
    tiles = jnp.concatenate([t_start // tm, x_tile])
    experts = jnp.concatenate([e_first, e_next])
    lo = jnp.concatenate([t_start, start])
    hi = jnp.concatenate([base_hi, x_hi])
    order = jnp.argsort(tiles * (2 * N_EXPERTS) + experts)
    tiles, experts, lo, hi = tiles[order], experts[order], lo[order], hi[order]
    change = tiles[1:] != tiles[:-1]
    one = jnp.ones((1,), bool)
    first = jnp.concatenate([one, change]).astype(jnp.int32)
    last = jnp.concatenate([change, one]).astype(jnp.int32)
    return slot_a, slot_b, (tiles, experts, lo, hi, first, last)


def _scatter_rows(rows, idx_a, idx_b):
    n_rows, width = rows.shape
    workers = SC_CORES * SC_SUBCORES
    per_worker = n_rows // workers
    assert n_rows % (workers * SC_GATHER_ROWS) == 0
    mesh = plsc.VectorSubcoreMesh(core_axis_name="c", subcore_axis_name="s")

    @functools.partial(
        pl.kernel, mesh=mesh,
        out_type=jax.ShapeDtypeStruct((TOP_K * n_rows, width), rows.dtype),
        scratch_types=[pltpu.VMEM((TOP_K, SC_GATHER_ROWS), jnp.int32),
                       pltpu.VMEM((SC_GATHER_ROWS, width), rows.dtype),
                       pltpu.SemaphoreType.DMA],
        name="scatter_rows")
    def scatter(rows_hbm, idx_a_hbm, idx_b_hbm, out_hbm, idx_v, rows_v, sem):
        base = (lax.axis_index("s") * SC_CORES + lax.axis_index("c")) * per_worker

        @pl.loop(0, per_worker // SC_GATHER_ROWS)
        def _(i):
            off = pl.multiple_of(base + i * SC_GATHER_ROWS, SC_GATHER_ROWS)
            pltpu.sync_copy(idx_a_hbm.at[pl.ds(off, SC_GATHER_ROWS)], idx_v.at[0])
            pltpu.sync_copy(idx_b_hbm.at[pl.ds(off, SC_GATHER_ROWS)], idx_v.at[1])
            pltpu.sync_copy(rows_hbm.at[pl.ds(off, SC_GATHER_ROWS)], rows_v)
            pltpu.async_copy(rows_v, out_hbm.at[idx_v.at[0]], sem).wait()
            pltpu.async_copy(rows_v, out_hbm.at[idx_v.at[1]], sem).wait()

    return scatter(rows, idx_a, idx_b)


def _gather_rows(table, idx):
    n_idx = idx.shape[0]
    width = table.shape[1]
    workers = SC_CORES * SC_SUBCORES
    per_worker = n_idx // workers
    assert n_idx % (workers * SC_GATHER_ROWS) == 0
    mesh = plsc.VectorSubcoreMesh(core_axis_name="c", subcore_axis_name="s")

    @functools.partial(
        pl.kernel, mesh=mesh,
        out_type=jax.ShapeDtypeStruct((n_idx, width), table.dtype),
        scratch_types=[pltpu.VMEM((SC_GATHER_ROWS,), jnp.int32),
                       pltpu.VMEM((SC_GATHER_ROWS, width), table.dtype),
                       pltpu.SemaphoreType.DMA],
        name="gather_rows")
    def gather(table_hbm, idx_hbm, out_hbm, idx_v, rows_v, sem):
        base = (lax.axis_index("s") * SC_CORES + lax.axis_index("c")) * per_worker

        @pl.loop(0, per_worker // SC_GATHER_ROWS)
        def _(i):
            off = pl.multiple_of(base + i * SC_GATHER_ROWS, SC_GATHER_ROWS)
            pltpu.sync_copy(idx_hbm.at[pl.ds(off, SC_GATHER_ROWS)], idx_v)
            pltpu.async_copy(table_hbm.at[idx_v], rows_v, sem).wait()
            pltpu.sync_copy(rows_v, out_hbm.at[pl.ds(off, SC_GATHER_ROWS)])

    return gather(table, idx)


def _experts_kernel(tile_ref, exp_ref, lo_ref, hi_ref, first_ref, last_ref,
                    x_ref, w1_ref, w3_ref, w2_ref, o_ref, acc_ref):
    i = pl.program_id(0)
    tm, half = x_ref.shape
    f = w1_ref.shape[2]
    fc = f // FFN_CHUNKS

    @pl.when(first_ref[i] == 1)
    def _():
        acc_ref[...] = jnp.zeros_like(acc_ref)

    lo, hi = lo_ref[i], hi_ref[i]

    @pl.when(hi > lo)
    def _():
        x_hi, x_lo = _unpack_bf16_pairs(x_ref[...])
        x_hi, x_lo = x_hi.astype(BF16), x_lo.astype(BF16)
        y = None
        for c in range(FFN_CHUNKS):
            cols = slice(c * fc, (c + 1) * fc)
            a = _dot(x_hi, w1_ref[0, :half, cols]) + _dot(x_lo, w1_ref[0, half:, cols])
            b = _dot(x_hi, w3_ref[0, :half, cols]) + _dot(x_lo, w3_ref[0, half:, cols])
            part = _dot((a * _sigmoid(a) * b).astype(BF16), w2_ref[0, cols, :])
            y = part if y is None else y + part
        rows = tile_ref[i] * tm + lax.broadcasted_iota(jnp.int32, (tm, 1), 0)
        keep = jnp.logical_and(rows >= lo, rows < hi)
        acc_ref[...] += jnp.where(keep, y, 0.0)

    @pl.when(last_ref[i] == 1)
    def _():
        o_ref[...] = _pack_bf16_pairs(acc_ref[...])


def _experts(xs_sorted, items, w1, w3, w2):
    n_slots, half = xs_sorted.shape
    n_exp, d, f = w1.shape
    tm = MOE_ROW_TILE
    assert f % (FFN_CHUNKS * LANES) == 0
    tile_map = lambda i, tiles, *_: (tiles[i], 0)
    exp_map = lambda i, tiles, experts, *_: (experts[i], 0, 0)
    grid_spec = pltpu.PrefetchScalarGridSpec(
        num_scalar_prefetch=len(items),
        grid=(items[0].shape[0],),
        in_specs=[pl.BlockSpec((tm, half), tile_map),
                  pl.BlockSpec((1, d, f), exp_map), pl.BlockSpec((1, d, f), exp_map),
                  pl.BlockSpec((1, f, d), exp_map)],
        out_specs=pl.BlockSpec((tm, half), tile_map),
        scratch_shapes=[pltpu.VMEM((tm, d), F32)],
    )
    return pl.pallas_call(
        _experts_kernel,
        grid_spec=grid_spec,
        out_shape=jax.ShapeDtypeStruct((n_slots, half), jnp.int32),
        compiler_params=_params(1),
        name="experts",
    )(*items, xs_sorted, w1, w3, w2)


def _combine_kernel(x1_ref, ya_ref, yb_ref, route_ref, mod_ref, fg_ref, o_ref, *, final):
    half = ya_ref.shape[1]
    route = route_ref[...]
    wa, wb = route[:, 2:3], route[:, 3:4]
    a_hi, a_lo = _unpack_bf16_pairs(ya_ref[...])
    b_hi, b_lo = _unpack_bf16_pairs(yb_ref[...])
    gate = mod_ref[0][5:6]
    out_hi = x1_ref[:, :half] + gate[:, :half] * (wa * a_hi + wb * b_hi)
    out_lo = x1_ref[:, half:] + gate[:, half:] * (wa * a_lo + wb * b_lo)
    if final:
        total = jnp.sum(out_hi * out_hi, axis=-1, keepdims=True) + jnp.sum(out_lo * out_lo, axis=-1, keepdims=True)
        scale = lax.rsqrt(total / (2 * half) + EPS)
        out_hi = out_hi * scale * fg_ref[:, :half]
        out_lo = out_lo * scale * fg_ref[:, half:]
    o_ref[:, :half] = out_hi
    o_ref[:, half:] = out_lo


def _combine(x1, y_pairs, route, mod, final_g, seg, final):
    n_rows, d = x1.shape
    tm = seg["tm"]
    row, mod_map = _row_maps(seg, tm)
    second = n_rows // tm
    return pl.pallas_call(
        functools.partial(_combine_kernel, final=final),
        grid=(n_rows // tm,),
        in_specs=[pl.BlockSpec((tm, d), row), pl.BlockSpec((tm, d // 2), row),
                  pl.BlockSpec((tm, d // 2), lambda i: (second + i, 0)),
                  pl.BlockSpec((tm, LANES), row), pl.BlockSpec((1, 6, d), mod_map),
                  pl.BlockSpec((1, d), lambda i: (0, 0))],
        out_specs=pl.BlockSpec((tm, d), row),
        out_shape=jax.ShapeDtypeStruct((n_rows, d), F32),
        compiler_params=_params(1),
        name="moe_combine",
    )(x1, y_pairs, y_pairs, route, mod, final_g)


def _moe_layer(xs, mix, mod, g2, wo, wr, br, w1, w3, w2, final_g, seg, n_rows, final):
    x1, h_pairs, route = _router(xs, mix, mod, g2, wo, wr, br, seg, n_rows)
    slot_a, slot_b, items = _route_plan(route, MOE_ROW_TILE)
    y_sorted = _experts(_scatter_rows(h_pairs, slot_a, slot_b), items, w1, w3, w2)
    y_pairs = _gather_rows(y_sorted, jnp.concatenate([slot_a, slot_b]))
    return _combine(x1, y_pairs, route, mod, final_g, seg, final)


def kernel(x, c, ctx, c_ctx, norm1_g, norm2_g, w_mod, b_mod, w_in, conv_w, conv_b, b_gates, attn_sink,
           g_att, g_ml, w_out, ffn_w1, ffn_w3, ffn_w2, w_router, b_router, exp_w1, exp_w3, exp_w2,
           final_g):
    b_, s_, d = x.shape
    lc = ctx.shape[1]
    depth = w_in.shape[0]
    n_lat, n_ctx = b_ * s_, b_ * lc
    tm = min(ROW_TILE, s_)
    assert s_ % tm == 0 and n_ctx % tm == 0
    assert (TOP_K * n_lat) % MOE_ROW_TILE == 0 and (TOP_K * n_ctx) % MOE_ROW_TILE == 0
    assert s_ % ML_CHUNK == 0 and lc % ML_CHUNK == 0 and n_lat % lc == 0 and b_ < MOD_ROWS
    seg = dict(B=b_, S=s_, Lc=lc, n_lat=n_lat, tm=tm)

    cond = jnp.zeros((MOD_ROWS, d), F32).at[:b_].set(c).at[b_].set(c_ctx)
    mods = _modulation(cond, w_mod, b_mod).reshape(depth, MOD_ROWS, 6, d)
    rope = _rope_tables(s_, tm)
    xs = jnp.concatenate([x.reshape(n_lat, d), ctx.reshape(n_ctx, d)], axis=0)
    final_row = final_g.reshape(1, d)

    for layer in range(depth):
        last = layer == depth - 1
        w_l = w_in[layer]
        w_gates = w_l[:, MAIN_WIDTH:]
        w_gc = jnp.pad(w_gates, ((0, 0), (0, LANES - ML_GATES))).astype(BF16)
        b_gc = jnp.pad(b_gates[layer], (0, LANES - ML_GATES)).reshape(1, LANES)
        qa, ka, va, qm, km, vm, om, gc, gr = _input_projection(
            xs, mods[layer], norm1_g[layer].reshape(1, d), w_l[:, :MAIN_WIDTH].astype(BF16), w_gc,
            w_gates.T.astype(BF16), b_gc, b_gates[layer].reshape(ML_GATES, 1),
            conv_w[layer], conv_b[layer].reshape(1, -1), rope, seg)
        att = _attention(qa, ka, va, attn_sink[layer], g_att[layer].reshape(1, ATT_WIDTH), seg, not last)
        hf, hb = _mlstm(qm, km, vm, gc, gr, seg)
        mix = (att, hf, hb, om, g_ml[layer].reshape(1, ML_WIDTH))
        n_rows = n_lat if last else n_lat + n_ctx
        g2 = norm2_g[layer].reshape(1, d)
        wo = w_out[layer].astype(BF16)
        i = layer // 2
        if layer % 2 == 0:
            xs = _dense_layer(xs, mix, mods[layer], g2, wo, ffn_w1[i].astype(BF16),
                              ffn_w3[i].astype(BF16), ffn_w2[i].astype(BF16), final_row, seg, n_rows, last)
        else:
            wr = jnp.pad(w_router[i], ((0, 0), (0, LANES - N_EXPERTS)))
            wr_hi = wr.astype(BF16)
            wr = jnp.concatenate([wr_hi, (wr - wr_hi.astype(F32)).astype(BF16)], axis=1)
            br = jnp.pad(b_router[i], (0, LANES - N_EXPERTS)).reshape(1, LANES)
            xs = _moe_layer(xs, mix, mods[layer], g2, wo, wr, br, exp_w1[i].astype(BF16),
                            exp_w3[i].astype(BF16), exp_w2[i].astype(BF16), final_row, seg, n_rows, last)
    return xs[:n_lat].reshape(b_, s_, d)
```

```python
import functools

import jax
import jax.numpy as jnp
import numpy as np
from jax import lax
from jax.experimental import pallas as pl
from jax.experimental.pallas import tpu as pltpu
from jax.experimental.pallas import tpu_sc as plsc

F32 = jnp.float32
BF16 = jnp.bfloat16

GRID_W = 64
ATT_HEADS = 8
ATT_KV_HEADS = 2
ATT_HEAD_DIM = 64
ATT_GROUP = ATT_HEADS // ATT_KV_HEADS
WINDOW = 128
ATT_BLOCK = 128
ROPE_THETA = 10000.0
ML_HEADS = 4
ML_QK_DIM = 64
ML_V_DIM = 128
ML_CONV = 5
GATE_CAP = 15.0
ATT_WIDTH = ATT_HEADS * ATT_HEAD_DIM
ATT_KV_WIDTH = ATT_KV_HEADS * ATT_HEAD_DIM
ML_QK_WIDTH = ML_HEADS * ML_QK_DIM
ML_WIDTH = ML_HEADS * ML_V_DIM
ML_GATES = 4 * ML_HEADS
MAIN_WIDTH = ATT_WIDTH + 2 * ATT_KV_WIDTH + 2 * ML_QK_WIDTH + 2 * ML_WIDTH
N_EXPERTS = 8
TOP_K = 2
EPS = 1e-6

LANES = 128
SUBLANES = 8
VMEM_LIMIT = 56 * 1024 * 1024
NEG = -1e30
SC_CORES = 2
SC_SUBCORES = 16
SC_GATHER_ROWS = 64

ROW_TILE = 512
MOE_ROW_TILE = 512
FFN_CHUNK_COLS = 256
ML_CHUNK = 128
CONV_HALO = SUBLANES
MOD_ROWS = 16
MOD_COL_TILE = 1536


def _dot(a, b):
    return jnp.dot(a, b, preferred_element_type=F32)


def _dot_nt(a, b):
    return lax.dot_general(a, b, (((1,), (1,)), ((), ())), preferred_element_type=F32)


def _dot_tn(a, b):
    return lax.dot_general(a, b, (((0,), (0,)), ((), ())), preferred_element_type=F32)


def _dot_f32(a, b):
    return jnp.dot(a, b, preferred_element_type=F32, precision=lax.Precision.HIGHEST)


def _sigmoid(x):
    return 1.0 / (1.0 + jnp.exp(-x))


def _rms(x, g):
    return x * lax.rsqrt(jnp.mean(x * x, axis=-1, keepdims=True) + EPS) * g


def _adaln(x, g, shift, scale):
    return _rms(x, g) * (1.0 + scale) + shift


def _params(n_axes):
    return pltpu.CompilerParams(dimension_semantics=("arbitrary",) * n_axes,
                                vmem_limit_bytes=VMEM_LIMIT)


def _resident(shape):
    zeros = (0,) * len(shape)
    return pl.BlockSpec(shape, lambda *_: zeros, pipeline_mode=pl.Buffered(1))


def _layer_resident(stacked, layer, block=None):
    block = tuple(stacked.shape[1:]) if block is None else block
    index = (layer,) + (0,) * len(block)
    return pl.BlockSpec((None,) + block, lambda *_: index, pipeline_mode=pl.Buffered(1))


def _mod_kernel(c_ref, w_ref, b_ref, o_ref):
    c = c_ref[...]
    a = (c * _sigmoid(c)).astype(BF16)
    o_ref[0] = _dot(a, w_ref[0].astype(BF16)) + b_ref[0]


def _modulation(cond, w_mod, b_mod):
    depth, d, width = w_mod.shape
    tn = MOD_COL_TILE if width % MOD_COL_TILE == 0 else width
    return pl.pallas_call(
        _mod_kernel,
        grid=(depth, width // tn),
        in_specs=[pl.BlockSpec((MOD_ROWS, d), lambda l, j: (0, 0)),
                  pl.BlockSpec((1, d, tn), lambda l, j: (l, 0, j)),
                  pl.BlockSpec((1, 1, tn), lambda l, j: (l, 0, j))],
        out_specs=pl.BlockSpec((1, MOD_ROWS, tn), lambda l, j: (l, 0, j)),
        out_shape=jax.ShapeDtypeStruct((depth, MOD_ROWS, width), F32),
        compiler_params=_params(2),
        name="modulation",
    )(cond, w_mod, b_mod.reshape(depth, 1, width))


def _gate_act(u, is_forget):
    g = GATE_CAP * jnp.tanh(u / GATE_CAP)
    log_sig = jnp.minimum(g, 0.0) - jnp.log1p(jnp.exp(-jnp.abs(g)))
    return jnp.where(is_forget, log_sig, g)


def _conv_silu(xe, cw, cb, rows):
    n_ext = xe.shape[0]
    mid = ML_CONV // 2
    y = cb + cw[mid:mid + 1] * xe[CONV_HALO:CONV_HALO + rows]
    for tap in range(ML_CONV):
        if tap != mid:
            y = y + cw[tap:tap + 1] * pltpu.roll(xe, (mid - tap) % n_ext, 0)[CONV_HALO:CONV_HALO + rows]
    return y * _sigmoid(y)


def _inproj_kernel(x_ref, xp_ref, xn_ref, mod_ref, g_ref, w_ref, wgc_ref, wgr_ref, bgc_ref, bgr_ref,
                   cw_ref, cb_ref, cos_ref, sa_ref, sb_ref,
                   qa_ref, ka_ref, va_ref, qm_ref, km_ref, vm_ref, om_ref, gc_ref, gr_ref,
                   *, n_lat_tiles, seq_lat, seq_ctx):
    mod = mod_ref[0]
    tm = x_ref.shape[0]
    seg = min(tm, seq_ctx)
    n_seg = tm // seg
    norm_gain = g_ref[...]
    normed = lambda rows: _adaln(rows, norm_gain, mod[0:1], mod[1:2]).astype(BF16)
    quarter = ATT_HEAD_DIM // 4

    def rope(u, rows):
        return (u * cos_ref[rows, :] + pltpu.roll(u, LANES - quarter, 1) * sa_ref[rows, :]
                + pltpu.roll(u, quarter, 1) * sb_ref[rows, :])

    c_kv = ATT_WIDTH
    c_qk = c_kv + 2 * ATT_KV_WIDTH
    c_vm = c_qk + 2 * ML_QK_WIDTH
    c_om = c_vm + ML_WIDTH
    w_qk = w_ref[:, c_qk:c_vm]
    i = pl.program_id(0)
    seq_len = jnp.where(i < n_lat_tiles, seq_lat, seq_ctx)
    cw, cb = cw_ref[...], cb_ref[...]
    hx = [normed(x_ref[j * seg:(j + 1) * seg, :]) for j in range(n_seg)]
    qk = [_dot(h, w_qk) for h in hx]
    qk_prev = _dot(normed(xp_ref[...]), w_qk)
    qk_next = _dot(normed(xn_ref[...]), w_qk)
    for j in range(n_seg):
        rows = slice(j * seg, (j + 1) * seg)
        first_row = i * tm + j * seg
        has_prev = (lax.rem(first_row, seq_len) != 0).astype(F32)
        has_next = (lax.rem(first_row + seg, seq_len) != 0).astype(F32)
        prev = qk_prev if j == 0 else qk[j - 1][seg - CONV_HALO:]
        nxt = qk_next if j == n_seg - 1 else qk[j + 1][:CONV_HALO]
        xe = jnp.concatenate([prev * has_prev, qk[j], nxt * has_next], axis=0)
        y = _conv_silu(xe, cw, cb, seg)
        qm_ref[rows, :] = (y[:, :ML_QK_WIDTH] * (ML_QK_DIM ** -0.5)).astype(BF16)
        km_ref[rows, :] = y[:, ML_QK_WIDTH:].astype(BF16)
        q = _dot(hx[j], w_ref[:, :c_kv])
        for c in range(ATT_WIDTH // LANES):
            sl = slice(c * LANES, (c + 1) * LANES)
            qa_ref[rows, sl] = (rope(q[:, sl], rows) * (ATT_HEAD_DIM ** -0.5)).astype(BF16)
        kv = _dot(hx[j], w_ref[:, c_kv:c_qk])
        ka_ref[rows, :] = rope(kv[:, :ATT_KV_WIDTH], rows).astype(BF16)
        va_ref[rows, :] = kv[:, ATT_KV_WIDTH:].astype(BF16)
        vm_ref[rows, :] = _dot(hx[j], w_ref[:, c_vm:c_om]).astype(BF16)
        om_ref[rows, :] = _dot(hx[j], w_ref[:, c_om:c_om + ML_WIDTH]).astype(BF16)
        gc = _dot(hx[j], wgc_ref[...]) + bgc_ref[...]
        lane = lax.broadcasted_iota(jnp.int32, gc.shape, 1)
        gc_ref[rows, :] = _gate_act(gc, (lane // ML_HEADS) % 2 == 1)
        gr = _dot_nt(wgr_ref[...], hx[j]) + bgr_ref[...]
        sub = lax.broadcasted_iota(jnp.int32, gr.shape, 0)
        gr_ref[:, rows] = _gate_act(gr, (sub // ML_HEADS) % 2 == 1)


def _input_projection(xs, mod, g1, w_in, layer, w_gc, w_gr, b_gc, b_gr, conv_w, conv_b, rope, seg):
    n, d = xs.shape
    tm = seg["tm"]
    nlat = seg["n_lat"] // tm
    s_tiles = seg["S"] // tm
    halos_per_tile = tm // CONV_HALO
    row = lambda i: (i, 0)
    prev = lambda i: (jnp.maximum(i * halos_per_tile - 1, 0), 0)
    nxt = lambda i: (jnp.minimum((i + 1) * halos_per_tile, n // CONV_HALO - 1), 0)
    mod_map = lambda i: (jnp.where(i < nlat, i // s_tiles, seg["B"]), 0, 0)
    rope_map = lambda i: (jnp.where(i < nlat, i % s_tiles, s_tiles), 0)
    widths = [(ATT_WIDTH, BF16), (ATT_KV_WIDTH, BF16), (ATT_KV_WIDTH, BF16), (ML_QK_WIDTH, BF16),
              (ML_QK_WIDTH, BF16), (ML_WIDTH, BF16), (ML_WIDTH, BF16), (LANES, F32)]
    out_shape = [jax.ShapeDtypeStruct((n, w), t) for w, t in widths]
    out_specs = [pl.BlockSpec((tm, w), row) for w, _ in widths]
    out_shape.append(jax.ShapeDtypeStruct((ML_GATES, n), F32))
    out_specs.append(pl.BlockSpec((ML_GATES, tm), lambda i: (0, i)))
    return pl.pallas_call(
        functools.partial(_inproj_kernel, n_lat_tiles=nlat, seq_lat=seg["S"], seq_ctx=seg["Lc"]),
        grid=(n // tm,),
        in_specs=[pl.BlockSpec((tm, d), row),
                  pl.BlockSpec((CONV_HALO, d), prev),
                  pl.BlockSpec((CONV_HALO, d), nxt),
                  pl.BlockSpec((1, 6, d), mod_map),
                  _resident((1, d)),
                  _layer_resident(w_in, layer, (d, MAIN_WIDTH)), _resident(w_gc.shape), _resident(w_gr.shape),
                  _resident(b_gc.shape), _resident(b_gr.shape),
                  _resident(conv_w.shape), _resident(conv_b.shape),
                  pl.BlockSpec((tm, LANES), rope_map),
                  pl.BlockSpec((tm, LANES), rope_map),
                  pl.BlockSpec((tm, LANES), rope_map)],
        out_specs=out_specs,
        out_shape=out_shape,
        compiler_params=_params(1),
        name="input_projection",
    )(xs, xs, xs, mod, g1, w_in, w_gc, w_gr, b_gc, b_gr, conv_w, conv_b, *rope)


def _rope_tables(s, tm):
    quarter = ATT_HEAD_DIM // 4
    t = jnp.arange(s)
    row = (t // GRID_W).astype(F32)
    col = (t % GRID_W).astype(F32)
    inv = ROPE_THETA ** (-jnp.arange(quarter, dtype=F32) / quarter)
    ang_r = row[:, None] * inv[None, :]
    ang_c = col[:, None] * inv[None, :]
    zero = jnp.zeros_like(ang_r)
    cos = jnp.concatenate([jnp.cos(ang_r)] * 2 + [jnp.cos(ang_c)] * 2, axis=1)
    sin_up = jnp.concatenate([-jnp.sin(ang_r), zero, -jnp.sin(ang_c), zero], axis=1)
    sin_dn = jnp.concatenate([zero, jnp.sin(ang_r), zero, jnp.sin(ang_c)], axis=1)
    reps = LANES // ATT_HEAD_DIM
    ident = [jnp.ones((tm, LANES), F32), jnp.zeros((tm, LANES), F32), jnp.zeros((tm, LANES), F32)]
    return tuple(jnp.concatenate([jnp.tile(a, (1, reps)), i], axis=0)
                 for a, i in zip((cos, sin_up, sin_dn), ident))


def _attn_kernel(sink_ref, q_ref, kp_ref, kc_ref, kn_ref, vp_ref, vc_ref, vn_ref, kx_ref, vx_ref,
                 g_ref, o_ref, *, n_lat_blocks):
    j = pl.program_id(1)
    is_lat = j < n_lat_blocks
    blk = ATT_BLOCK
    rows = lax.broadcasted_iota(jnp.int32, (blk, blk), 0)
    cols = lax.broadcasted_iota(jnp.int32, (blk, blk), 1)
    ok_p = jnp.logical_and(cols >= rows, jnp.logical_and(is_lat, j > 0))
    ok_c = jnp.logical_and(cols >= 0, is_lat)
    ok_n = jnp.logical_and(cols <= rows, jnp.logical_and(is_lat, j < n_lat_blocks - 1))
    n_ctx = kx_ref.shape[0]
    bias = jnp.concatenate([jnp.where(ok_p, 0.0, NEG), jnp.where(ok_c, 0.0, NEG),
                            jnp.where(ok_n, 0.0, NEG), jnp.zeros((blk, n_ctx), F32)], axis=1)
    q = q_ref[...]
    k_all = jnp.concatenate([kp_ref[...], kc_ref[...], kn_ref[...], kx_ref[...]], axis=0)
    v_all = jnp.concatenate([vp_ref[...], vc_ref[...], vn_ref[...], vx_ref[...]], axis=0)
    dh = ATT_HEAD_DIM
    outs = []
    for h in range(ATT_KV_HEADS):
        k_h = k_all[:, h * dh:(h + 1) * dh]
        v_h = v_all[:, h * dh:(h + 1) * dh]
        q_h = jnp.concatenate([q[:, (h * ATT_GROUP + g) * dh:(h * ATT_GROUP + g + 1) * dh]
                               for g in range(ATT_GROUP)], axis=0)
        s_all = _dot_nt(q_h, k_h)
        p_parts, inv_parts = [], []
        for g in range(ATT_GROUP):
            sink = sink_ref[h * ATT_GROUP + g]
            s = s_all[g * blk:(g + 1) * blk] + bias
            m = jnp.maximum(jnp.max(s, axis=-1, keepdims=True), sink)
            p = jnp.exp(s - m)
            denom = jnp.sum(p, axis=-1, keepdims=True) + jnp.exp(sink - m)
            p_parts.append(p.astype(BF16))
            inv_parts.append(1.0 / denom)
        o = _dot(jnp.concatenate(p_parts, axis=0), v_h)
        for g in range(ATT_GROUP):
            outs.append(o[g * blk:(g + 1) * blk] * inv_parts[g])
    att = jnp.concatenate(outs, axis=1)
    o_ref[...] = _rms(att, g_ref[...]).astype(o_ref.dtype)


def _attention(qa, ka, va, sink, g_att, seg, with_ctx):
    n = qa.shape[0]
    b_, s_, lc = seg["B"], seg["S"], seg["Lc"]
    blk = ATT_BLOCK
    nqb = s_ // blk
    ncb = lc // blk if with_ctx else 0
    lat_blocks = b_ * nqb

    def q_map(b, j, *_):
        return (jnp.where(j < nqb, b * nqb + j, lat_blocks + b * (lc // blk) + (j - nqb)), 0)

    def win_map(off):
        def f(b, j, *_):
            jj = jnp.clip(jnp.where(j < nqb, j, 0) + off, 0, nqb - 1)
            return (b * nqb + jj, 0)
        return f

    ctx_map = lambda b, j, *_: (b_ * s_ // lc + b, 0)
    kv_specs = [pl.BlockSpec((blk, ATT_KV_WIDTH), win_map(o)) for o in (-1, 0, 1)]
    grid_spec = pltpu.PrefetchScalarGridSpec(
        num_scalar_prefetch=1,
        grid=(b_, nqb + ncb),
        in_specs=[pl.BlockSpec((blk, ATT_WIDTH), q_map)] + kv_specs + kv_specs
                 + [pl.BlockSpec((lc, ATT_KV_WIDTH), ctx_map), pl.BlockSpec((lc, ATT_KV_WIDTH), ctx_map),
                    pl.BlockSpec((1, ATT_WIDTH), lambda b, j, *_: (0, 0))],
        out_specs=pl.BlockSpec((blk, ATT_WIDTH), q_map),
    )
    return pl.pallas_call(
        functools.partial(_attn_kernel, n_lat_blocks=nqb),
        grid_spec=grid_spec,
        out_shape=jax.ShapeDtypeStruct((n, ATT_WIDTH), BF16),
        compiler_params=_params(2),
        name="window_attention",
    )(sink, qa, ka, ka, ka, va, va, va, ka, va, g_att)


def _mlstm_direction(q_ref, k_ref, v_ref, gc_ref, gr_ref, out_ref, state_ref, *, reverse):
    chunk = q_ref.shape[0]
    rows = lax.broadcasted_iota(jnp.int32, (chunk, chunk), 0)
    cols = lax.broadcasted_iota(jnp.int32, (chunk, chunk), 1)
    lower = rows >= cols
    upper = rows <= cols
    seen = upper if reverse else lower
    gc = gc_ref[...]
    gr = gr_ref[...]
    b_col = _dot_f32(seen.astype(F32), gc)
    b_row = _dot_f32(gr, (lower if reverse else upper).astype(F32))
    b_end = jnp.sum(gc, axis=0, keepdims=True)
    base = 2 * ML_HEADS if reverse else 0
    pair_width = 2 * ML_QK_DIM
    lane = lax.broadcasted_iota(jnp.int32, (chunk, pair_width), 1)
    state_row = lax.broadcasted_iota(jnp.int32, (pair_width, 1), 0)
    ones = jnp.ones((chunk, ML_V_DIM), BF16)

    for pair in range(ML_HEADS // 2):
        q_pair = q_ref[:, pair * pair_width:(pair + 1) * pair_width]
        k_pair = k_ref[:, pair * pair_width:(pair + 1) * pair_width]
        state = state_ref[pair]
        state_bf = state.astype(BF16)
        update = None
        decays = []
        for sub in range(2):
            h = 2 * pair + sub
            i_idx = base + h
            f_idx = base + ML_HEADS + h
            own = (lane >= ML_QK_DIM) if sub else (lane < ML_QK_DIM)
            q_h = jnp.where(own, q_pair, jnp.zeros_like(q_pair))
            vx = jnp.concatenate([v_ref[:, h * ML_V_DIM:(h + 1) * ML_V_DIM], ones], axis=1)
            bc = b_col[:, f_idx:f_idx + 1]
            d = bc - b_row[f_idx:f_idx + 1, :] + gr[i_idx:i_idx + 1, :]
            w = jnp.exp(jnp.where(seen, d, NEG))
            s = _dot_nt(q_h, k_pair) * w
            tot = _dot(s.astype(BF16), vx) + jnp.exp(bc) * _dot(q_h, state_bf)
            h_out = tot[:, :ML_V_DIM] / jnp.maximum(jnp.abs(tot[:, ML_V_DIM:]), 1.0)
            out_ref[:, h * ML_V_DIM:(h + 1) * ML_V_DIM] = h_out.astype(out_ref.dtype)

            be = b_end[:, f_idx:f_idx + 1]
            kw = jnp.where(own, k_pair.astype(F32) * jnp.exp(be - bc + gc[:, i_idx:i_idx + 1]), 0.0)
            part = _dot_tn(kw.astype(BF16), vx)
            update = part if update is None else update + part
            decays.append(jnp.exp(be))
        decay = jnp.where(state_row < ML_QK_DIM, decays[0], decays[1])
        state_ref[pair] = decay * state + update


def _mlstm_kernel(qf_ref, kf_ref, vf_ref, gcf_ref, grf_ref, qb_ref, kb_ref, vb_ref, gcb_ref, grb_ref,
                  hf_ref, hb_ref, sf_ref, sb_ref):
    @pl.when(pl.program_id(1) == 0)
    def _():
        sf_ref[...] = jnp.zeros_like(sf_ref)
        sb_ref[...] = jnp.zeros_like(sb_ref)

    _mlstm_direction(qf_ref, kf_ref, vf_ref, gcf_ref, grf_ref, hf_ref, sf_ref, reverse=False)
    _mlstm_direction(qb_ref, kb_ref, vb_ref, gcb_ref, grb_ref, hb_ref, sb_ref, reverse=True)


def _mlstm(qm, km, vm, gc, gr, seg):
    n = qm.shape[0]
    b_, s_, lc = seg["B"], seg["S"], seg["Lc"]
    chunk = ML_CHUNK
    ncc, ncl = lc // chunk, s_ // chunk
    lat_chunks = b_ * ncl

    def blk(b, c, reverse):
        pos = jnp.where(c < ncc, c, c - ncc)
        if reverse:
            pos = jnp.where(c < ncc, ncc, ncl) - 1 - pos
        return jnp.where(c < ncc, lat_chunks + b * ncc + pos, b * ncl + pos)

    def specs(reverse):
        cur = lambda b, c: (blk(b, c, reverse), 0)
        return [pl.BlockSpec((chunk, ML_QK_WIDTH), cur), pl.BlockSpec((chunk, ML_QK_WIDTH), cur),
                pl.BlockSpec((chunk, ML_WIDTH), cur), pl.BlockSpec((chunk, LANES), cur),
                pl.BlockSpec((ML_GATES, chunk), lambda b, c: (0, blk(b, c, reverse)))]

    out = lambda reverse: pl.BlockSpec((chunk, ML_WIDTH), lambda b, c: (blk(b, c, reverse), 0))
    state = pltpu.VMEM((ML_HEADS // 2, 2 * ML_QK_DIM, 2 * ML_V_DIM), F32)
    return pl.pallas_call(
        _mlstm_kernel,
        grid=(b_, ncc + ncl),
        in_specs=specs(False) + specs(True),
        out_specs=[out(False), out(True)],
        out_shape=[jax.ShapeDtypeStruct((n, ML_WIDTH), BF16)] * 2,
        scratch_shapes=[state, state],
        compiler_params=_params(2),
        name="mlstm_scan",
    )(qm, km, vm, gc, gr, qm, km, vm, gc, gr)


def _swiglu(h, w1_ref, w3_ref, w2_ref):
    f = w1_ref.shape[1]
    fc = FFN_CHUNK_COLS
    y = None
    for c in range(f // fc):
        cols = slice(c * fc, (c + 1) * fc)
        a = _dot(h, w1_ref[:, cols])
        b = _dot(h, w3_ref[:, cols])
        part = _dot((a * _sigmoid(a) * b).astype(BF16), w2_ref[cols, :])
        y = part if y is None else y + part
    return y


def _mix_residual(x_ref, mix_refs, mod, wo_ref):
    att_ref, hf_ref, hb_ref, om_ref, gml_ref = mix_refs
    mix = _dot(att_ref[...], wo_ref[:ATT_WIDTH, :])
    for h in range(ML_HEADS):
        sl = slice(h * ML_V_DIM, (h + 1) * ML_V_DIM)
        tot = hf_ref[:, sl].astype(F32) + hb_ref[:, sl].astype(F32)
        ml = _rms(tot, gml_ref[:, sl]) * _sigmoid(om_ref[:, sl].astype(F32))
        mix = mix + _dot(ml.astype(BF16), wo_ref[ATT_WIDTH + h * ML_V_DIM:ATT_WIDTH + (h + 1) * ML_V_DIM, :])
    return x_ref[...] + mod[2:3] * mix


def _mix_specs(tm, row):
    return [pl.BlockSpec((tm, ATT_WIDTH), row)] + [pl.BlockSpec((tm, ML_WIDTH), row)] * 3 \
        + [_resident((1, ML_WIDTH))]


def _dense_layer_kernel(x_ref, att_ref, hf_ref, hb_ref, om_ref, gml_ref, mod_ref, g_ref, wo_ref,
                        w1_ref, w3_ref, w2_ref, fg_ref, o_ref, *, final):
    mod = mod_ref[0]
    x1 = _mix_residual(x_ref, (att_ref, hf_ref, hb_ref, om_ref, gml_ref), mod, wo_ref)
    hx = _adaln(x1, g_ref[...], mod[3:4], mod[4:5]).astype(BF16)
    out = x1 + mod[5:6] * _swiglu(hx, w1_ref, w3_ref, w2_ref)
    if final:
        out = _rms(out, fg_ref[...])
    o_ref[...] = out


def _row_maps(seg, tm):
    nlat = seg["n_lat"] // tm
    s_tiles = seg["S"] // tm
    return (lambda i, *_: (i, 0)), (lambda i, *_: (jnp.where(i < nlat, i // s_tiles, seg["B"]), 0, 0))


def _dense_layer(xs, mix, mod, g2, wo, layer, w1, w3, w2, ffn_index, final_g, seg, n_rows, final):
    d = xs.shape[1]
    tm = seg["tm"]
    row, mod_map = _row_maps(seg, tm)
    return pl.pallas_call(
        functools.partial(_dense_layer_kernel, final=final),
        grid=(n_rows // tm,),
        in_specs=[pl.BlockSpec((tm, d), row)] + _mix_specs(tm, row)
                 + [pl.BlockSpec((1, 6, d), mod_map),
                    _resident((1, d)), _layer_resident(wo, layer), _layer_resident(w1, ffn_index),
                    _layer_resident(w3, ffn_index), _layer_resident(w2, ffn_index), _resident((1, d))],
        out_specs=pl.BlockSpec((tm, d), row),
        out_shape=jax.ShapeDtypeStruct((n_rows, d), F32),
        compiler_params=_params(1),
        name="dense_layer",
    )(xs, *mix, mod, g2, wo, w1, w3, w2, final_g)


def _router_kernel(x_ref, att_ref, hf_ref, hb_ref, om_ref, gml_ref, mod_ref, g_ref, wo_ref, wr_ref, br_ref,
                   x1_ref, h_ref, route_ref):
    mod = mod_ref[0]
    x1 = _mix_residual(x_ref, (att_ref, hf_ref, hb_ref, om_ref, gml_ref), mod, wo_ref)
    x1_ref[...] = x1
    hx = _adaln(x1, g_ref[...], mod[3:4], mod[4:5])
    h_ref[...] = _pack_bf16_pairs(hx)
    hx_hi = hx.astype(BF16)
    hx_lo = (hx - hx_hi.astype(F32)).astype(BF16)
    both = _dot(hx_hi, wr_ref[...])
    logits = both[:, :LANES] + both[:, LANES:] + _dot(hx_lo, wr_ref[:, :LANES]) + br_ref[...]
    lane = lax.broadcasted_iota(jnp.int32, logits.shape, 1)
    logits = jnp.where(lane < N_EXPERTS, logits, -jnp.inf)
    top1 = jnp.max(logits, axis=-1, keepdims=True)
    idx1 = jnp.min(jnp.where(logits == top1, lane, LANES), axis=-1, keepdims=True)
    rest = jnp.where(lane == idx1, -jnp.inf, logits)
    top2 = jnp.max(rest, axis=-1, keepdims=True)
    idx2 = jnp.min(jnp.where(rest == top2, lane, LANES), axis=-1, keepdims=True)
    e2 = jnp.exp(top2 - top1)
    w_first = 1.0 / (1.0 + e2)
    route_ref[...] = jnp.where(lane == 0, idx1.astype(F32),
                               jnp.where(lane == 1, idx2.astype(F32),
                                         jnp.where(lane == 2, w_first,
                                                   jnp.where(lane == 3, e2 * w_first, 0.0))))


def _router(xs, mix, mod, g2, wo, layer, wr, br, seg, n_rows):
    d = xs.shape[1]
    tm = seg["tm"]
    row, mod_map = _row_maps(seg, tm)
    return pl.pallas_call(
        _router_kernel,
        grid=(n_rows // tm,),
        in_specs=[pl.BlockSpec((tm, d), row)] + _mix_specs(tm, row)
                 + [pl.BlockSpec((1, 6, d), mod_map),
                    _resident((1, d)), _layer_resident(wo, layer), _resident(wr.shape), _resident(br.shape)],
        out_specs=[pl.BlockSpec((tm, d), row), pl.BlockSpec((tm, d // 2), row),
                   pl.BlockSpec((tm, LANES), row)],
        out_shape=[jax.ShapeDtypeStruct((n_rows, d), F32), jax.ShapeDtypeStruct((n_rows, d // 2), jnp.int32),
                   jax.ShapeDtypeStruct((n_rows, LANES), F32)],
        compiler_params=_params(1),
        name="mix_router",
    )(xs, *mix, mod, g2, wo, wr, br)


def _pack_bf16_pairs(h):
    half = h.shape[1] // 2
    hi = lax.bitcast_convert_type(h[:, :half].astype(BF16).astype(F32), jnp.int32)
    lo = lax.bitcast_convert_type(h[:, half:].astype(BF16).astype(F32), jnp.int32)
    return (hi & jnp.int32(-65536)) | lax.shift_right_logical(lo, 16)


def _unpack_bf16_pairs(p):
    hi = lax.bitcast_convert_type(p & jnp.int32(-65536), F32)
    lo = lax.bitcast_convert_type(lax.shift_left(p, 16), F32)
    return hi, lo


def _route_plan(route, tm):
    n_rows = route.shape[0]
    n_slots = TOP_K * n_rows
    idx1 = route[:, 0].astype(jnp.int32)
    idx2 = route[:, 1].astype(jnp.int32)
    hot = jax.nn.one_hot(idx1, N_EXPERTS, dtype=jnp.int32) + jax.nn.one_hot(idx2, N_EXPERTS, dtype=jnp.int32)
    incl = jnp.cumsum(hot, axis=0)
    offs = jnp.concatenate([jnp.zeros((1,), jnp.int32), jnp.cumsum(incl[-1])])
    rank = incl - hot
    slot_a = offs[idx1] + jnp.take_along_axis(rank, idx1[:, None], axis=1)[:, 0]
    slot_b = offs[idx2] + jnp.take_along_axis(rank, idx2[:, None], axis=1)[:, 0]
    n_tiles = n_slots // tm
    t_start = jnp.arange(n_tiles, dtype=jnp.int32) * tm
    e_first = jnp.searchsorted(offs[1:], t_start, side="right").astype(jnp.int32)
    base_hi = jnp.minimum(t_start + tm, offs[e_first + 1])
    e_next = jnp.arange(1, N_EXPERTS, dtype=jnp.int32)
    start = offs[1:N_EXPERTS]
    x_tile = jnp.minimum(start // tm, n_tiles - 1)
    x_hi = jnp.where(start % tm != 0, jnp.minimum(offs[2:], (x_tile + 1) * tm), start)
    tiles = jnp.concatenate([t_start // tm, x_tile])
    experts = jnp.concatenate([e_first, e_next])
    lo = jnp.concatenate([t_start, start])
    hi = jnp.concatenate([base_hi, x_hi])
    order = jnp.argsort(tiles * (2 * N_EXPERTS) + experts)
    tiles, experts, lo, hi = tiles[order], experts[order], lo[order], hi[order]
    change = tiles[1:] != tiles[:-1]
    one = jnp.ones((1,), bool)
    first = jnp.concatenate([one, change]).astype(jnp.int32)
    last = jnp.concatenate([change, one]).astype(jnp.int32)
    return slot_a, slot_b, (tiles, experts, lo, hi, first, last)


def _scatter_rows(rows, idx_a, idx_b):
    n_rows, width = rows.shape
    workers = SC_CORES * SC_SUBCORES
    per_worker = n_rows // workers
    assert n_rows % (workers * SC_GATHER_ROWS) == 0
    mesh = plsc.VectorSubcoreMesh(core_axis_name="c", subcore_axis_name="s")

    @functools.partial(
        pl.kernel, mesh=mesh,
        out_type=jax.ShapeDtypeStruct((TOP_K * n_rows, width), rows.dtype),
        scratch_types=[pltpu.VMEM((TOP_K, SC_GATHER_ROWS), jnp.int32),
                       pltpu.VMEM((SC_GATHER_ROWS, width), rows.dtype),
                       pltpu.SemaphoreType.DMA],
        name="scatter_rows")
    def scatter(rows_hbm, idx_a_hbm, idx_b_hbm, out_hbm, idx_v, rows_v, sem):
        base = (lax.axis_index("s") * SC_CORES + lax.axis_index("c")) * per_worker

        @pl.loop(0, per_worker // SC_GATHER_ROWS)
        def _(i):
            off = pl.multiple_of(base + i * SC_GATHER_ROWS, SC_GATHER_ROWS)
            pltpu.sync_copy(idx_a_hbm.at[pl.ds(off, SC_GATHER_ROWS)], idx_v.at[0])
            pltpu.sync_copy(idx_b_hbm.at[pl.ds(off, SC_GATHER_ROWS)], idx_v.at[1])
            pltpu.sync_copy(rows_hbm.at[pl.ds(off, SC_GATHER_ROWS)], rows_v)
            pltpu.async_copy(rows_v, out_hbm.at[idx_v.at[0]], sem).wait()
            pltpu.async_copy(rows_v, out_hbm.at[idx_v.at[1]], sem).wait()

    return scatter(rows, idx_a, idx_b)


def _gather_rows(table, idx):
    n_idx = idx.shape[0]
    width = table.shape[1]
    workers = SC_CORES * SC_SUBCORES
    per_worker = n_idx // workers
    assert n_idx % (workers * SC_GATHER_ROWS) == 0
    mesh = plsc.VectorSubcoreMesh(core_axis_name="c", subcore_axis_name="s")

    @functools.partial(
        pl.kernel, mesh=mesh,
        out_type=jax.ShapeDtypeStruct((n_idx, width), table.dtype),
        scratch_types=[pltpu.VMEM((SC_GATHER_ROWS,), jnp.int32),
                       pltpu.VMEM((SC_GATHER_ROWS, width), table.dtype),
                       pltpu.SemaphoreType.DMA],
        name="gather_rows")
    def gather(table_hbm, idx_hbm, out_hbm, idx_v, rows_v, sem):
        base = (lax.axis_index("s") * SC_CORES + lax.axis_index("c")) * per_worker

        @pl.loop(0, per_worker // SC_GATHER_ROWS)
        def _(i):
            off = pl.multiple_of(base + i * SC_GATHER_ROWS, SC_GATHER_ROWS)
            pltpu.sync_copy(idx_hbm.at[pl.ds(off, SC_GATHER_ROWS)], idx_v)
            pltpu.async_copy(table_hbm.at[idx_v], rows_v, sem).wait()
            pltpu.sync_copy(rows_v, out_hbm.at[pl.ds(off, SC_GATHER_ROWS)])

    return gather(table, idx)


def _experts_kernel(tile_ref, exp_ref, lo_ref, hi_ref, first_ref, last_ref,
                    x_ref, w1_ref, w3_ref, w2_ref, o_ref, acc_ref):
    i = pl.program_id(0)
    tm, half = x_ref.shape
    f = w1_ref.shape[2]
    fc = FFN_CHUNK_COLS

    @pl.when(first_ref[i] == 1)
    def _():
        acc_ref[...] = jnp.zeros_like(acc_ref)

    lo, hi = lo_ref[i], hi_ref[i]

    @pl.when(hi > lo)
    def _():
        x_hi, x_lo = _unpack_bf16_pairs(x_ref[...])
        x_hi, x_lo = x_hi.astype(BF16), x_lo.astype(BF16)
        y = None
        for c in range(f // fc):
            cols = slice(c * fc, (c + 1) * fc)
            a = _dot(x_hi, w1_ref[0, :half, cols]) + _dot(x_lo, w1_ref[0, half:, cols])
            b = _dot(x_hi, w3_ref[0, :half, cols]) + _dot(x_lo, w3_ref[0, half:, cols])
            part = _dot((a * _sigmoid(a) * b).astype(BF16), w2_ref[0, cols, :])
            y = part if y is None else y + part
        rows = tile_ref[i] * tm + lax.broadcasted_iota(jnp.int32, (tm, 1), 0)
        keep = jnp.logical_and(rows >= lo, rows < hi)
        acc_ref[...] += jnp.where(keep, y, 0.0)

    @pl.when(last_ref[i] == 1)
    def _():
        o_ref[...] = _pack_bf16_pairs(acc_ref[...])


def _experts(xs_sorted, items, w1, w3, w2, moe_index):
    n_slots, half = xs_sorted.shape
    _, n_exp, d, f = w1.shape
    tm = MOE_ROW_TILE
    assert f % FFN_CHUNK_COLS == 0
    tile_map = lambda i, tiles, *_: (tiles[i], 0)
    exp_map = lambda i, tiles, experts, *_: (moe_index, experts[i], 0, 0)
    grid_spec = pltpu.PrefetchScalarGridSpec(
        num_scalar_prefetch=len(items),
        grid=(items[0].shape[0],),
        in_specs=[pl.BlockSpec((tm, half), tile_map),
                  pl.BlockSpec((None, 1, d, f), exp_map), pl.BlockSpec((None, 1, d, f), exp_map),
                  pl.BlockSpec((None, 1, f, d), exp_map)],
        out_specs=pl.BlockSpec((tm, half), tile_map),
        scratch_shapes=[pltpu.VMEM((tm, d), F32)],
    )
    return pl.pallas_call(
        _experts_kernel,
        grid_spec=grid_spec,
        out_shape=jax.ShapeDtypeStruct((n_slots, half), jnp.int32),
        compiler_params=_params(1),
        name="experts",
    )(*items, xs_sorted, w1, w3, w2)


def _combine_kernel(x1_ref, ya_ref, yb_ref, route_ref, mod_ref, fg_ref, o_ref, *, final):
    half = ya_ref.shape[1]
    route = route_ref[...]
    wa, wb = route[:, 2:3], route[:, 3:4]
    a_hi, a_lo = _unpack_bf16_pairs(ya_ref[...])
    b_hi, b_lo = _unpack_bf16_pairs(yb_ref[...])
    gate = mod_ref[0][5:6]
    out_hi = x1_ref[:, :half] + gate[:, :half] * (wa * a_hi + wb * b_hi)
    out_lo = x1_ref[:, half:] + gate[:, half:] * (wa * a_lo + wb * b_lo)
    if final:
        total = jnp.sum(out_hi * out_hi, axis=-1, keepdims=True) + jnp.sum(out_lo * out_lo, axis=-1, keepdims=True)
        scale = lax.rsqrt(total / (2 * half) + EPS)
        out_hi = out_hi * scale * fg_ref[:, :half]
        out_lo = out_lo * scale * fg_ref[:, half:]
    o_ref[:, :half] = out_hi
    o_ref[:, half:] = out_lo


def _combine(x1, y_pairs, route, mod, final_g, seg, final):
    n_rows, d = x1.shape
    tm = seg["tm"]
    row, mod_map = _row_maps(seg, tm)
    second = n_rows // tm
    return pl.pallas_call(
        functools.partial(_combine_kernel, final=final),
        grid=(n_rows // tm,),
        in_specs=[pl.BlockSpec((tm, d), row), pl.BlockSpec((tm, d // 2), row),
                  pl.BlockSpec((tm, d // 2), lambda i: (second + i, 0)),
                  pl.BlockSpec((tm, LANES), row), pl.BlockSpec((1, 6, d), mod_map),
                  pl.BlockSpec((1, d), lambda i: (0, 0))],
        out_specs=pl.BlockSpec((tm, d), row),
        out_shape=jax.ShapeDtypeStruct((n_rows, d), F32),
        compiler_params=_params(1),
        name="moe_combine",
    )(x1, y_pairs, y_pairs, route, mod, final_g)


def _moe_layer(xs, mix, mod, g2, wo, layer, wr, br, w1, w3, w2, moe_index, final_g, seg, n_rows, final):
    x1, h_pairs, route = _router(xs, mix, mod, g2, wo, layer, wr, br, seg, n_rows)
    slot_a, slot_b, items = _route_plan(route, MOE_ROW_TILE)
    y_sorted = _experts(_scatter_rows(h_pairs, slot_a, slot_b), items, w1, w3, w2, moe_index)
    y_pairs = _gather_rows(y_sorted, jnp.concatenate([slot_a, slot_b]))
    return _combine(x1, y_pairs, route, mod, final_g, seg, final)


def kernel(x, c, ctx, c_ctx, norm1_g, norm2_g, w_mod, b_mod, w_in, conv_w, conv_b, b_gates, attn_sink,
           g_att, g_ml, w_out, ffn_w1, ffn_w3, ffn_w2, w_router, b_router, exp_w1, exp_w3, exp_w2,
           final_g):
    b_, s_, d = x.shape
    lc = ctx.shape[1]
    depth = w_in.shape[0]
    n_lat, n_ctx = b_ * s_, b_ * lc
    tm = min(ROW_TILE, s_)
    assert s_ % tm == 0 and n_ctx % tm == 0
    assert (TOP_K * n_lat) % MOE_ROW_TILE == 0 and (TOP_K * n_ctx) % MOE_ROW_TILE == 0
    assert s_ % ML_CHUNK == 0 and lc % ML_CHUNK == 0 and n_lat % lc == 0 and b_ < MOD_ROWS
    seg = dict(B=b_, S=s_, Lc=lc, n_lat=n_lat, tm=tm)

    cond = jnp.zeros((MOD_ROWS, d), F32).at[:b_].set(c).at[b_].set(c_ctx)
    mods = _modulation(cond, w_mod, b_mod).reshape(depth, MOD_ROWS, 6, d)
    rope = _rope_tables(s_, tm)
    xs = jnp.concatenate([x.reshape(n_lat, d), ctx.reshape(n_ctx, d)], axis=0)
    final_row = final_g.reshape(1, d)

    w_in_b, w_out_b = w_in.astype(BF16), w_out.astype(BF16)
    ffn_b = (ffn_w1.astype(BF16), ffn_w3.astype(BF16), ffn_w2.astype(BF16))
    exp_b = (exp_w1.astype(BF16), exp_w3.astype(BF16), exp_w2.astype(BF16))

    for layer in range(depth):
        last = layer == depth - 1
        w_gates = w_in[layer][:, MAIN_WIDTH:]
        w_gc = jnp.pad(w_gates, ((0, 0), (0, LANES - ML_GATES))).astype(BF16)
        b_gc = jnp.pad(b_gates[layer], (0, LANES - ML_GATES)).reshape(1, LANES)
        qa, ka, va, qm, km, vm, om, gc, gr = _input_projection(
            xs, mods[layer], norm1_g[layer].reshape(1, d), w_in_b, layer, w_gc,
            w_gates.T.astype(BF16), b_gc, b_gates[layer].reshape(ML_GATES, 1),
            conv_w[layer], conv_b[layer].reshape(1, -1), rope, seg)
        att = _attention(qa, ka, va, attn_sink[layer], g_att[layer].reshape(1, ATT_WIDTH), seg, not last)
        hf, hb = _mlstm(qm, km, vm, gc, gr, seg)
        mix = (att, hf, hb, om, g_ml[layer].reshape(1, ML_WIDTH))
        n_rows = n_lat if last else n_lat + n_ctx
        g2 = norm2_g[layer].reshape(1, d)
        i = layer // 2
        if layer % 2 == 0:
            xs = _dense_layer(xs, mix, mods[layer], g2, w_out_b, layer, *ffn_b, i, final_row, seg, n_rows, last)
        else:
            wr = jnp.pad(w_router[i], ((0, 0), (0, LANES - N_EXPERTS)))
            wr_hi = wr.astype(BF16)
            wr = jnp.concatenate([wr_hi, (wr - wr_hi.astype(F32)).astype(BF16)], axis=1)
            br = jnp.pad(b_router[i], (0, LANES - N_EXPERTS)).reshape(1, LANES)
            xs = _moe_layer(xs, mix, mods[layer], g2, w_out_b, layer, wr, br, *exp_b, i, final_row, seg,
                            n_rows, last)
    return xs[:n_lat].reshape(b_, s_, d)
```

```python
import functools

import jax
import jax.numpy as jnp
import numpy as np
from jax import lax
from jax.experimental import pallas as pl
from jax.experimental.pallas import tpu as pltpu
from jax.experimental.pallas import tpu_sc as plsc

F32 = jnp.float32
BF16 = jnp.bfloat16

GRID_W = 64
ATT_HEADS = 8
ATT_KV_HEADS = 2
ATT_HEAD_DIM = 64
ATT_GROUP = ATT_HEADS // ATT_KV_HEADS
WINDOW = 128
ATT_BLOCK = 128
ROPE_THETA = 10000.0
ML_HEADS = 4
ML_QK_DIM = 64
ML_V_DIM = 128
ML_CONV = 5
GATE_CAP = 15.0
ATT_WIDTH = ATT_HEADS * ATT_HEAD_DIM
ATT_KV_WIDTH = ATT_KV_HEADS * ATT_HEAD_DIM
ML_QK_WIDTH = ML_HEADS * ML_QK_DIM
ML_WIDTH = ML_HEADS * ML_V_DIM
ML_GATES = 4 * ML_HEADS
MAIN_WIDTH = ATT_WIDTH + 2 * ATT_KV_WIDTH + 2 * ML_QK_WIDTH + 2 * ML_WIDTH
N_EXPERTS = 8
TOP_K = 2
EPS = 1e-6

LANES = 128
SUBLANES = 8
VMEM_LIMIT = 56 * 1024 * 1024
NEG = -1e30
SC_CORES = 2
SC_SUBCORES = 16
SC_GATHER_ROWS = 64

ROW_TILE = 512
MOE_ROW_TILE = 512
FFN_CHUNK_COLS = 256
ATT_STEP_BLOCKS = 4
ML_CHUNK = 128
ML_STEP_CHUNKS = 2
CONV_HALO = SUBLANES
MOD_ROWS = 16
MOD_COL_TILE = 1536


def _dot(a, b):
    return jnp.dot(a, b, preferred_element_type=F32)


def _dot_nt(a, b):
    return lax.dot_general(a, b, (((1,), (1,)), ((), ())), preferred_element_type=F32)


def _dot_tn(a, b):
    return lax.dot_general(a, b, (((0,), (0,)), ((), ())), preferred_element_type=F32)


def _dot_f32(a, b):
    return jnp.dot(a, b, preferred_element_type=F32, precision=lax.Precision.HIGHEST)


def _sigmoid(x):
    return 1.0 / (1.0 + jnp.exp(-x))


def _rms(x, g):
    return x * lax.rsqrt(jnp.mean(x * x, axis=-1, keepdims=True) + EPS) * g


def _adaln(x, g, shift, scale):
    return _rms(x, g) * (1.0 + scale) + shift


def _params(n_axes):
    return pltpu.CompilerParams(dimension_semantics=("arbitrary",) * n_axes,
                                vmem_limit_bytes=VMEM_LIMIT)


def _resident(shape):
    zeros = (0,) * len(shape)
    return pl.BlockSpec(shape, lambda *_: zeros, pipeline_mode=pl.Buffered(1))


def _layer_resident(stacked, layer, block=None):
    block = tuple(stacked.shape[1:]) if block is None else block
    index = (layer,) + (0,) * len(block)
    return pl.BlockSpec((None,) + block, lambda *_: index, pipeline_mode=pl.Buffered(1))


def _mod_kernel(c_ref, w_ref, b_ref, o_ref):
    c = c_ref[...]
    a = (c * _sigmoid(c)).astype(BF16)
    o_ref[0] = _dot(a, w_ref[0].astype(BF16)) + b_ref[0]


def _modulation(cond, w_mod, b_mod):
    depth, d, width = w_mod.shape
    tn = MOD_COL_TILE if width % MOD_COL_TILE == 0 else width
    return pl.pallas_call(
        _mod_kernel,
        grid=(depth, width // tn),
        in_specs=[pl.BlockSpec((MOD_ROWS, d), lambda l, j: (0, 0)),
                  pl.BlockSpec((1, d, tn), lambda l, j: (l, 0, j)),
                  pl.BlockSpec((1, 1, tn), lambda l, j: (l, 0, j))],
        out_specs=pl.BlockSpec((1, MOD_ROWS, tn), lambda l, j: (l, 0, j)),
        out_shape=jax.ShapeDtypeStruct((depth, MOD_ROWS, width), F32),
        compiler_params=_params(2),
        name="modulation",
    )(cond, w_mod, b_mod.reshape(depth, 1, width))


def _gate_act(u, is_forget):
    g = GATE_CAP * jnp.tanh(u / GATE_CAP)
    log_sig = jnp.minimum(g, 0.0) - jnp.log1p(jnp.exp(-jnp.abs(g)))
    return jnp.where(is_forget, log_sig, g)


def _conv_silu(xe, cw, cb, rows):
    n_ext = xe.shape[0]
    mid = ML_CONV // 2
    y = cb + cw[mid:mid + 1] * xe[CONV_HALO:CONV_HALO + rows]
    for tap in range(ML_CONV):
        if tap != mid:
            y = y + cw[tap:tap + 1] * pltpu.roll(xe, (mid - tap) % n_ext, 0)[CONV_HALO:CONV_HALO + rows]
    return y * _sigmoid(y)


def _inproj_kernel(x_ref, xp_ref, xn_ref, mod_ref, g_ref, w_ref, wgc_ref, wgr_ref, bgc_ref, bgr_ref,
                   cw_ref, cb_ref, cos_ref, sa_ref, sb_ref,
                   qa_ref, ka_ref, va_ref, qm_ref, km_ref, vm_ref, om_ref, gc_ref, gr_ref,
                   *, n_lat_tiles, seq_lat, seq_ctx):
    mod = mod_ref[0]
    tm = x_ref.shape[0]
    seg = min(tm, seq_ctx)
    n_seg = tm // seg
    norm_gain = g_ref[...]
    normed = lambda rows: _adaln(rows, norm_gain, mod[0:1], mod[1:2]).astype(BF16)
    quarter = ATT_HEAD_DIM // 4

    def rope(u, rows):
        return (u * cos_ref[rows, :] + pltpu.roll(u, LANES - quarter, 1) * sa_ref[rows, :]
                + pltpu.roll(u, quarter, 1) * sb_ref[rows, :])

    c_kv = ATT_WIDTH
    c_qk = c_kv + 2 * ATT_KV_WIDTH
    c_vm = c_qk + 2 * ML_QK_WIDTH
    c_om = c_vm + ML_WIDTH
    w_qk = w_ref[:, c_qk:c_vm]
    i = pl.program_id(0)
    seq_len = jnp.where(i < n_lat_tiles, seq_lat, seq_ctx)
    cw, cb = cw_ref[...], cb_ref[...]
    hx = [normed(x_ref[j * seg:(j + 1) * seg, :]) for j in range(n_seg)]
    qk = [_dot(h, w_qk) for h in hx]
    qk_prev = _dot(normed(xp_ref[...]), w_qk)
    qk_next = _dot(normed(xn_ref[...]), w_qk)
    for j in range(n_seg):
        rows = slice(j * seg, (j + 1) * seg)
        first_row = i * tm + j * seg
        has_prev = (lax.rem(first_row, seq_len) != 0).astype(F32)
        has_next = (lax.rem(first_row + seg, seq_len) != 0).astype(F32)
        prev = qk_prev if j == 0 else qk[j - 1][seg - CONV_HALO:]
        nxt = qk_next if j == n_seg - 1 else qk[j + 1][:CONV_HALO]
        xe = jnp.concatenate([prev * has_prev, qk[j], nxt * has_next], axis=0)
        y = _conv_silu(xe, cw, cb, seg)
        qm_ref[rows, :] = (y[:, :ML_QK_WIDTH] * (ML_QK_DIM ** -0.5)).astype(BF16)
        km_ref[rows, :] = y[:, ML_QK_WIDTH:].astype(BF16)
        q = _dot(hx[j], w_ref[:, :c_kv])
        for c in range(ATT_WIDTH // LANES):
            sl = slice(c * LANES, (c + 1) * LANES)
            qa_ref[rows, sl] = (rope(q[:, sl], rows) * (ATT_HEAD_DIM ** -0.5)).astype(BF16)
        kv = _dot(hx[j], w_ref[:, c_kv:c_qk])
        ka_ref[rows, :] = rope(kv[:, :ATT_KV_WIDTH], rows).astype(BF16)
        va_ref[rows, :] = kv[:, ATT_KV_WIDTH:].astype(BF16)
        vm_ref[rows, :] = _dot(hx[j], w_ref[:, c_vm:c_om]).astype(BF16)
        om_ref[rows, :] = _dot(hx[j], w_ref[:, c_om:c_om + ML_WIDTH]).astype(BF16)
        gc = _dot(hx[j], wgc_ref[...]) + bgc_ref[...]
        lane = lax.broadcasted_iota(jnp.int32, gc.shape, 1)
        gc_ref[rows, :] = _gate_act(gc, (lane // ML_HEADS) % 2 == 1)
        gr = _dot_nt(wgr_ref[...], hx[j]) + bgr_ref[...]
        sub = lax.broadcasted_iota(jnp.int32, gr.shape, 0)
        gr_ref[:, rows] = _gate_act(gr, (sub // ML_HEADS) % 2 == 1)


def _input_projection(xs, mod, g1, w_in, layer, w_gc, w_gr, b_gc, b_gr, conv_w, conv_b, rope, seg):
    n, d = xs.shape
    tm = seg["tm"]
    nlat = seg["n_lat"] // tm
    s_tiles = seg["S"] // tm
    halos_per_tile = tm // CONV_HALO
    row = lambda i: (i, 0)
    prev = lambda i: (jnp.maximum(i * halos_per_tile - 1, 0), 0)
    nxt = lambda i: (jnp.minimum((i + 1) * halos_per_tile, n // CONV_HALO - 1), 0)
    mod_map = lambda i: (jnp.where(i < nlat, i // s_tiles, seg["B"]), 0, 0)
    rope_map = lambda i: (jnp.where(i < nlat, i % s_tiles, s_tiles), 0)
    widths = [(ATT_WIDTH, BF16), (ATT_KV_WIDTH, BF16), (ATT_KV_WIDTH, BF16), (ML_QK_WIDTH, BF16),
              (ML_QK_WIDTH, BF16), (ML_WIDTH, BF16), (ML_WIDTH, BF16), (LANES, F32)]
    out_shape = [jax.ShapeDtypeStruct((n, w), t) for w, t in widths]
    out_specs = [pl.BlockSpec((tm, w), row) for w, _ in widths]
    out_shape.append(jax.ShapeDtypeStruct((ML_GATES, n), F32))
    out_specs.append(pl.BlockSpec((ML_GATES, tm), lambda i: (0, i)))
    return pl.pallas_call(
        functools.partial(_inproj_kernel, n_lat_tiles=nlat, seq_lat=seg["S"], seq_ctx=seg["Lc"]),
        grid=(n // tm,),
        in_specs=[pl.BlockSpec((tm, d), row),
                  pl.BlockSpec((CONV_HALO, d), prev),
                  pl.BlockSpec((CONV_HALO, d), nxt),
                  pl.BlockSpec((1, 6, d), mod_map),
                  _resident((1, d)),
                  _layer_resident(w_in, layer, (d, MAIN_WIDTH)), _resident(w_gc.shape), _resident(w_gr.shape),
                  _resident(b_gc.shape), _resident(b_gr.shape),
                  _resident(conv_w.shape), _resident(conv_b.shape),
                  pl.BlockSpec((tm, LANES), rope_map),
                  pl.BlockSpec((tm, LANES), rope_map),
                  pl.BlockSpec((tm, LANES), rope_map)],
        out_specs=out_specs,
        out_shape=out_shape,
        compiler_params=_params(1),
        name="input_projection",
    )(xs, xs, xs, mod, g1, w_in, w_gc, w_gr, b_gc, b_gr, conv_w, conv_b, *rope)


def _rope_tables(s, tm):
    quarter = ATT_HEAD_DIM // 4
    t = jnp.arange(s)
    row = (t // GRID_W).astype(F32)
    col = (t % GRID_W).astype(F32)
    inv = ROPE_THETA ** (-jnp.arange(quarter, dtype=F32) / quarter)
    ang_r = row[:, None] * inv[None, :]
    ang_c = col[:, None] * inv[None, :]
    zero = jnp.zeros_like(ang_r)
    cos = jnp.concatenate([jnp.cos(ang_r)] * 2 + [jnp.cos(ang_c)] * 2, axis=1)
    sin_up = jnp.concatenate([-jnp.sin(ang_r), zero, -jnp.sin(ang_c), zero], axis=1)
    sin_dn = jnp.concatenate([zero, jnp.sin(ang_r), zero, jnp.sin(ang_c)], axis=1)
    reps = LANES // ATT_HEAD_DIM
    ident = [jnp.ones((tm, LANES), F32), jnp.zeros((tm, LANES), F32), jnp.zeros((tm, LANES), F32)]
    return tuple(jnp.concatenate([jnp.tile(a, (1, reps)), i], axis=0)
                 for a, i in zip((cos, sin_up, sin_dn), ident))


def _attn_kernel(sink_ref, *refs, window, blocks):
    blk = ATT_BLOCK
    if window:
        q_ref, kp_ref, kc_ref, kn_ref, vp_ref, vc_ref, vn_ref, kx_ref, vx_ref, g_ref, o_ref = refs
        j = pl.program_id(1)
        k_own, v_own = kc_ref[...], vc_ref[...]
        k_blocks = [kp_ref[...]] + [k_own[t * blk:(t + 1) * blk] for t in range(blocks)] + [kn_ref[...]]
        v_blocks = [vp_ref[...]] + [v_own[t * blk:(t + 1) * blk] for t in range(blocks)] + [vn_ref[...]]
    else:
        q_ref, kx_ref, vx_ref, g_ref, _, o_ref = refs
    for t in range(blocks):
        if window:
            has_prev = j > 0 if t == 0 else True
            has_next = j < pl.num_programs(1) - 1 if t == blocks - 1 else True
            win = (k_blocks[t:t + 3], v_blocks[t:t + 3], has_prev, has_next)
        else:
            win = None
        att = _attend_block(sink_ref, q_ref[t * blk:(t + 1) * blk, :], win, kx_ref[...], vx_ref[...])
        o_ref[t * blk:(t + 1) * blk, :] = _rms(att, g_ref[...]).astype(o_ref.dtype)


def _attend_block(sink_ref, q, win, k_ctx, v_ctx):
    blk = ATT_BLOCK
    if win is None:
        k_all, v_all, bias = k_ctx, v_ctx, None
    else:
        k_win, v_win, has_prev, has_next = win
        rows = lax.broadcasted_iota(jnp.int32, (blk, blk), 0)
        cols = lax.broadcasted_iota(jnp.int32, (blk, blk), 1)
        ok_p = jnp.logical_and(cols >= rows, has_prev)
        ok_n = jnp.logical_and(cols <= rows, has_next)
        bias = jnp.concatenate([jnp.where(ok_p, 0.0, NEG), jnp.zeros((blk, blk), F32),
                                jnp.where(ok_n, 0.0, NEG), jnp.zeros((blk, k_ctx.shape[0]), F32)], axis=1)
        k_all = jnp.concatenate(list(k_win) + [k_ctx], axis=0)
        v_all = jnp.concatenate(list(v_win) + [v_ctx], axis=0)
    dh = ATT_HEAD_DIM
    outs = []
    for h in range(ATT_KV_HEADS):
        k_h = k_all[:, h * dh:(h + 1) * dh]
        v_h = v_all[:, h * dh:(h + 1) * dh]
        q_h = jnp.concatenate([q[:, (h * ATT_GROUP + g) * dh:(h * ATT_GROUP + g + 1) * dh]
                               for g in range(ATT_GROUP)], axis=0)
        s_all = _dot_nt(q_h, k_h)
        p_parts, inv_parts = [], []
        for g in range(ATT_GROUP):
            sink = sink_ref[h * ATT_GROUP + g]
            s = s_all[g * blk:(g + 1) * blk]
            if bias is not None:
                s = s + bias
            m = jnp.maximum(jnp.max(s, axis=-1, keepdims=True), sink)
            p = jnp.exp(s - m)
            denom = jnp.sum(p, axis=-1, keepdims=True) + jnp.exp(sink - m)
            p_parts.append(p.astype(BF16))
            inv_parts.append(1.0 / denom)
        o = _dot(jnp.concatenate(p_parts, axis=0), v_h)
        for g in range(ATT_GROUP):
            outs.append(o[g * blk:(g + 1) * blk] * inv_parts[g])
    return jnp.concatenate(outs, axis=1)


def _attention(qa, ka, va, sink, g_att, seg, with_ctx):
    n = qa.shape[0]
    b_, s_, lc = seg["B"], seg["S"], seg["Lc"]
    blk = ATT_BLOCK
    blocks_per_seq = s_ // blk
    ctx_map = lambda b, j, *_: (b_ * s_ // lc + b, 0)
    ctx_specs = [pl.BlockSpec((lc, ATT_KV_WIDTH), ctx_map)] * 2
    g_spec = pl.BlockSpec((1, ATT_WIDTH), lambda b, j, *_: (0, 0))

    def call(window, blocks, steps, q_map, in_specs, args, aliases):
        grid_spec = pltpu.PrefetchScalarGridSpec(
            num_scalar_prefetch=1,
            grid=(b_, steps),
            in_specs=[pl.BlockSpec((blocks * blk, ATT_WIDTH), q_map)] + in_specs,
            out_specs=pl.BlockSpec((blocks * blk, ATT_WIDTH), q_map))
        return pl.pallas_call(
            functools.partial(_attn_kernel, window=window, blocks=blocks),
            grid_spec=grid_spec,
            out_shape=jax.ShapeDtypeStruct((n, ATT_WIDTH), BF16),
            input_output_aliases=aliases,
            compiler_params=_params(2),
            name="window_attention" if window else "context_attention",
        )(sink, qa, *args)

    qb = min(ATT_STEP_BLOCKS, blocks_per_seq)
    assert blocks_per_seq % qb == 0
    nqs = blocks_per_seq // qb

    def edge_map(off):
        return lambda b, j, *_: (b * blocks_per_seq + jnp.clip(j * qb + off, 0, blocks_per_seq - 1), 0)

    kv_specs = [pl.BlockSpec((blk, ATT_KV_WIDTH), edge_map(-1)),
                pl.BlockSpec((qb * blk, ATT_KV_WIDTH), lambda b, j, *_: (b * nqs + j, 0)),
                pl.BlockSpec((blk, ATT_KV_WIDTH), edge_map(qb))]
    att = call(True, qb, nqs, lambda b, j, *_: (b * nqs + j, 0), kv_specs + kv_specs + ctx_specs + [g_spec],
               (ka, ka, ka, va, va, va, ka, va, g_att), {})
    if with_ctx:
        cb = min(ATT_STEP_BLOCKS, lc // blk)
        assert (lc // blk) % cb == 0 and (b_ * s_) % (cb * blk) == 0
        ncs = lc // (cb * blk)
        first = b_ * s_ // (cb * blk)
        att = call(False, cb, ncs, lambda b, j, *_: (first + b * ncs + j, 0),
                   ctx_specs + [g_spec, pl.BlockSpec(memory_space=pl.ANY)], (ka, va, g_att, att), {5: 0})
    return att


def _mlstm_direction(q_ref, k_ref, v_ref, gc_ref, gr_ref, out_ref, state_ref, tok, *, reverse):
    chunk = ML_CHUNK
    rows = lax.broadcasted_iota(jnp.int32, (chunk, chunk), 0)
    cols = lax.broadcasted_iota(jnp.int32, (chunk, chunk), 1)
    lower = rows >= cols
    upper = rows <= cols
    seen = upper if reverse else lower
    gc = gc_ref[tok, :]
    gr = gr_ref[:, tok]
    b_col = _dot_f32(seen.astype(F32), gc)
    b_row = _dot_f32(gr, (lower if reverse else upper).astype(F32))
    b_end = jnp.sum(gc, axis=0, keepdims=True)
    base = 2 * ML_HEADS if reverse else 0
    pair_width = 2 * ML_QK_DIM
    lane = lax.broadcasted_iota(jnp.int32, (chunk, pair_width), 1)
    state_row = lax.broadcasted_iota(jnp.int32, (pair_width, 1), 0)
    ones = jnp.ones((chunk, ML_V_DIM), BF16)

    for pair in range(ML_HEADS // 2):
        q_pair = q_ref[tok, pair * pair_width:(pair + 1) * pair_width]
        k_pair = k_ref[tok, pair * pair_width:(pair + 1) * pair_width]
        state = state_ref[pair]
        state_bf = state.astype(BF16)
        update = None
        decays = []
        for sub in range(2):
            h = 2 * pair + sub
            i_idx = base + h
            f_idx = base + ML_HEADS + h
            own = (lane >= ML_QK_DIM) if sub else (lane < ML_QK_DIM)
            q_h = jnp.where(own, q_pair, jnp.zeros_like(q_pair))
            vx = jnp.concatenate([v_ref[tok, h * ML_V_DIM:(h + 1) * ML_V_DIM], ones], axis=1)
            bc = b_col[:, f_idx:f_idx + 1]
            d = bc - b_row[f_idx:f_idx + 1, :] + gr[i_idx:i_idx + 1, :]
            w = jnp.exp(jnp.where(seen, d, NEG))
            s = _dot_nt(q_h, k_pair) * w
            tot = _dot(s.astype(BF16), vx) + jnp.exp(bc) * _dot(q_h, state_bf)
            h_out = tot[:, :ML_V_DIM] / jnp.maximum(jnp.abs(tot[:, ML_V_DIM:]), 1.0)
            out_ref[tok, h * ML_V_DIM:(h + 1) * ML_V_DIM] = h_out.astype(out_ref.dtype)

            be = b_end[:, f_idx:f_idx + 1]
            kw = jnp.where(own, k_pair.astype(F32) * jnp.exp(be - bc + gc[:, i_idx:i_idx + 1]), 0.0)
            part = _dot_tn(kw.astype(BF16), vx)
            update = part if update is None else update + part
            decays.append(jnp.exp(be))
        decay = jnp.where(state_row < ML_QK_DIM, decays[0], decays[1])
        state_ref[pair] = decay * state + update


def _mlstm_kernel(qf_ref, kf_ref, vf_ref, gcf_ref, grf_ref, qb_ref, kb_ref, vb_ref, gcb_ref, grb_ref,
                  hf_ref, hb_ref, sf_ref, sb_ref):
    @pl.when(pl.program_id(1) == 0)
    def _():
        sf_ref[...] = jnp.zeros_like(sf_ref)
        sb_ref[...] = jnp.zeros_like(sb_ref)

    n_chunks = qf_ref.shape[0] // ML_CHUNK
    for t in range(n_chunks):
        fwd = slice(t * ML_CHUNK, (t + 1) * ML_CHUNK)
        bwd = slice((n_chunks - 1 - t) * ML_CHUNK, (n_chunks - t) * ML_CHUNK)
        _mlstm_direction(qf_ref, kf_ref, vf_ref, gcf_ref, grf_ref, hf_ref, sf_ref, fwd, reverse=False)
        _mlstm_direction(qb_ref, kb_ref, vb_ref, gcb_ref, grb_ref, hb_ref, sb_ref, bwd, reverse=True)


def _mlstm(qm, km, vm, gc, gr, seg):
    n = qm.shape[0]
    b_, s_, lc = seg["B"], seg["S"], seg["Lc"]
    chunk = ML_STEP_CHUNKS * ML_CHUNK
    assert s_ % chunk == 0 and lc % chunk == 0
    ncc, ncl = lc // chunk, s_ // chunk
    lat_chunks = b_ * ncl

    def blk(b, c, reverse):
        pos = jnp.where(c < ncc, c, c - ncc)
        if reverse:
            pos = jnp.where(c < ncc, ncc, ncl) - 1 - pos
        return jnp.where(c < ncc, lat_chunks + b * ncc + pos, b * ncl + pos)

    def specs(reverse):
        cur = lambda b, c: (blk(b, c, reverse), 0)
        return [pl.BlockSpec((chunk, ML_QK_WIDTH), cur), pl.BlockSpec((chunk, ML_QK_WIDTH), cur),
                pl.BlockSpec((chunk, ML_WIDTH), cur), pl.BlockSpec((chunk, LANES), cur),
                pl.BlockSpec((ML_GATES, chunk), lambda b, c: (0, blk(b, c, reverse)))]

    out = lambda reverse: pl.BlockSpec((chunk, ML_WIDTH), lambda b, c: (blk(b, c, reverse), 0))
    state = pltpu.VMEM((ML_HEADS // 2, 2 * ML_QK_DIM, 2 * ML_V_DIM), F32)
    return pl.pallas_call(
        _mlstm_kernel,
        grid=(b_, ncc + ncl),
        in_specs=specs(False) + specs(True),
        out_specs=[out(False), out(True)],
        out_shape=[jax.ShapeDtypeStruct((n, ML_WIDTH), BF16)] * 2,
        scratch_shapes=[state, state],
        compiler_params=_params(2),
        name="mlstm_scan",
    )(qm, km, vm, gc, gr, qm, km, vm, gc, gr)


def _swiglu(h, w1_ref, w3_ref, w2_ref):
    f = w1_ref.shape[1]
    fc = FFN_CHUNK_COLS
    y = None
    for c in range(f // fc):
        cols = slice(c * fc, (c + 1) * fc)
        a = _dot(h, w1_ref[:, cols])
        b = _dot(h, w3_ref[:, cols])
        part = _dot((a * _sigmoid(a) * b).astype(BF16), w2_ref[cols, :])
        y = part if y is None else y + part
    return y


def _mix_residual(x_ref, mix_refs, mod, wo_ref):
    att_ref, hf_ref, hb_ref, om_ref, gml_ref = mix_refs
    mix = _dot(att_ref[...], wo_ref[:ATT_WIDTH, :])
    for h in range(ML_HEADS):
        sl = slice(h * ML_V_DIM, (h + 1) * ML_V_DIM)
        tot = hf_ref[:, sl].astype(F32) + hb_ref[:, sl].astype(F32)
        ml = _rms(tot, gml_ref[:, sl]) * _sigmoid(om_ref[:, sl].astype(F32))
        mix = mix + _dot(ml.astype(BF16), wo_ref[ATT_WIDTH + h * ML_V_DIM:ATT_WIDTH + (h + 1) * ML_V_DIM, :])
    return x_ref[...] + mod[2:3] * mix


def _mix_specs(tm, row):
    return [pl.BlockSpec((tm, ATT_WIDTH), row)] + [pl.BlockSpec((tm, ML_WIDTH), row)] * 3 \
        + [_resident((1, ML_WIDTH))]


def _dense_layer_kernel(x_ref, att_ref, hf_ref, hb_ref, om_ref, gml_ref, mod_ref, g_ref, wo_ref,
                        w1_ref, w3_ref, w2_ref, fg_ref, o_ref, *, final):
    mod = mod_ref[0]
    x1 = _mix_residual(x_ref, (att_ref, hf_ref, hb_ref, om_ref, gml_ref), mod, wo_ref)
    hx = _adaln(x1, g_ref[...], mod[3:4], mod[4:5]).astype(BF16)
    out = x1 + mod[5:6] * _swiglu(hx, w1_ref, w3_ref, w2_ref)
    if final:
        out = _rms(out, fg_ref[...])
    o_ref[...] = out


def _row_maps(seg, tm):
    nlat = seg["n_lat"] // tm
    s_tiles = seg["S"] // tm
    return (lambda i, *_: (i, 0)), (lambda i, *_: (jnp.where(i < nlat, i // s_tiles, seg["B"]), 0, 0))


def _dense_layer(xs, mix, mod, g2, wo, layer, w1, w3, w2, ffn_index, final_g, seg, n_rows, final):
    d = xs.shape[1]
    tm = seg["tm"]
    row, mod_map = _row_maps(seg, tm)
    return pl.pallas_call(
        functools.partial(_dense_layer_kernel, final=final),
        grid=(n_rows // tm,),
        in_specs=[pl.BlockSpec((tm, d), row)] + _mix_specs(tm, row)
                 + [pl.BlockSpec((1, 6, d), mod_map),
                    _resident((1, d)), _layer_resident(wo, layer), _layer_resident(w1, ffn_index),
                    _layer_resident(w3, ffn_index), _layer_resident(w2, ffn_index), _resident((1, d))],
        out_specs=pl.BlockSpec((tm, d), row),
        out_shape=jax.ShapeDtypeStruct((n_rows, d), F32),
        compiler_params=_params(1),
        name="dense_layer",
    )(xs, *mix, mod, g2, wo, w1, w3, w2, final_g)


def _router_kernel(x_ref, att_ref, hf_ref, hb_ref, om_ref, gml_ref, mod_ref, g_ref, wo_ref, wr_ref, br_ref,
                   x1_ref, h_ref, route_ref):
    mod = mod_ref[0]
    x1 = _mix_residual(x_ref, (att_ref, hf_ref, hb_ref, om_ref, gml_ref), mod, wo_ref)
    x1_ref[...] = x1
    hx = _adaln(x1, g_ref[...], mod[3:4], mod[4:5])
    h_ref[...] = _pack_bf16_pairs(hx)
    hx_hi = hx.astype(BF16)
    hx_lo = (hx - hx_hi.astype(F32)).astype(BF16)
    both = _dot(hx_hi, wr_ref[...])
    logits = both[:, :LANES] + both[:, LANES:] + _dot(hx_lo, wr_ref[:, :LANES]) + br_ref[...]
    lane = lax.broadcasted_iota(jnp.int32, logits.shape, 1)
    logits = jnp.where(lane < N_EXPERTS, logits, -jnp.inf)
    top1 = jnp.max(logits, axis=-1, keepdims=True)
    idx1 = jnp.min(jnp.where(logits == top1, lane, LANES), axis=-1, keepdims=True)
    rest = jnp.where(lane == idx1, -jnp.inf, logits)
    top2 = jnp.max(rest, axis=-1, keepdims=True)
    idx2 = jnp.min(jnp.where(rest == top2, lane, LANES), axis=-1, keepdims=True)
    e2 = jnp.exp(top2 - top1)
    w_first = 1.0 / (1.0 + e2)
    route_ref[...] = jnp.where(lane == 0, idx1.astype(F32),
                               jnp.where(lane == 1, idx2.astype(F32),
                                         jnp.where(lane == 2, w_first,
                                                   jnp.where(lane == 3, e2 * w_first, 0.0))))


def _router(xs, mix, mod, g2, wo, layer, wr, br, seg, n_rows):
    d = xs.shape[1]
    tm = seg["tm"]
    row, mod_map = _row_maps(seg, tm)
    return pl.pallas_call(
        _router_kernel,
        grid=(n_rows // tm,),
        in_specs=[pl.BlockSpec((tm, d), row)] + _mix_specs(tm, row)
                 + [pl.BlockSpec((1, 6, d), mod_map),
                    _resident((1, d)), _layer_resident(wo, layer), _resident(wr.shape), _resident(br.shape)],
        out_specs=[pl.BlockSpec((tm, d), row), pl.BlockSpec((tm, d // 2), row),
                   pl.BlockSpec((tm, LANES), row)],
        out_shape=[jax.ShapeDtypeStruct((n_rows, d), F32), jax.ShapeDtypeStruct((n_rows, d // 2), jnp.int32),
                   jax.ShapeDtypeStruct((n_rows, LANES), F32)],
        compiler_params=_params(1),
        name="mix_router",
    )(xs, *mix, mod, g2, wo, wr, br)


def _pack_bf16_pairs(h):
    half = h.shape[1] // 2
    hi = lax.bitcast_convert_type(h[:, :half].astype(BF16).astype(F32), jnp.int32)
    lo = lax.bitcast_convert_type(h[:, half:].astype(BF16).astype(F32), jnp.int32)
    return (hi & jnp.int32(-65536)) | lax.shift_right_logical(lo, 16)


def _unpack_bf16_pairs(p):
    hi = lax.bitcast_convert_type(p & jnp.int32(-65536), F32)
    lo = lax.bitcast_convert_type(lax.shift_left(p, 16), F32)
    return hi, lo


def _route_plan(route, tm):
    n_rows = route.shape[0]
    n_slots = TOP_K * n_rows
    idx1 = route[:, 0].astype(jnp.int32)
    idx2 = route[:, 1].astype(jnp.int32)
    hot = jax.nn.one_hot(idx1, N_EXPERTS, dtype=jnp.int32) + jax.nn.one_hot(idx2, N_EXPERTS, dtype=jnp.int32)
    incl = jnp.cumsum(hot, axis=0)
    offs = jnp.concatenate([jnp.zeros((1,), jnp.int32), jnp.cumsum(incl[-1])])
    rank = incl - hot
    slot_a = offs[idx1] + jnp.take_along_axis(rank, idx1[:, None], axis=1)[:, 0]
    slot_b = offs[idx2] + jnp.take_along_axis(rank, idx2[:, None], axis=1)[:, 0]
    n_tiles = n_slots // tm
    t_start = jnp.arange(n_tiles, dtype=jnp.int32) * tm
    e_first = jnp.searchsorted(offs[1:], t_start, side="right").astype(jnp.int32)
    base_hi = jnp.minimum(t_start + tm, offs[e_first + 1])
    e_next = jnp.arange(1, N_EXPERTS, dtype=jnp.int32)
    start = offs[1:N_EXPERTS]
    x_tile = jnp.minimum(start // tm, n_tiles - 1)
    x_hi = jnp.where(start % tm != 0, jnp.minimum(offs[2:], (x_tile + 1) * tm), start)
    tiles = jnp.concatenate([t_start // tm, x_tile])
    experts = jnp.concatenate([e_first, e_next])
    lo = jnp.concatenate([t_start, start])
    hi = jnp.concatenate([base_hi, x_hi])
    order = jnp.argsort(tiles * (2 * N_EXPERTS) + experts)
    tiles, experts, lo, hi = tiles[order], experts[order], lo[order], hi[order]
    change = tiles[1:] != tiles[:-1]
    one = jnp.ones((1,), bool)
    first = jnp.concatenate([one, change]).astype(jnp.int32)
    last = jnp.concatenate([change, one]).astype(jnp.int32)
    return slot_a, slot_b, (tiles, experts, lo, hi, first, last)


def _scatter_rows(rows, idx_a, idx_b):
    n_rows, width = rows.shape
    workers = SC_CORES * SC_SUBCORES
    per_worker = n_rows // workers
    assert n_rows % (workers * SC_GATHER_ROWS) == 0
    mesh = plsc.VectorSubcoreMesh(core_axis_name="c", subcore_axis_name="s")

    @functools.partial(
        pl.kernel, mesh=mesh,
        out_type=jax.ShapeDtypeStruct((TOP_K * n_rows, width), rows.dtype),
        scratch_types=[pltpu.VMEM((TOP_K, SC_GATHER_ROWS), jnp.int32),
                       pltpu.VMEM((SC_GATHER_ROWS, width), rows.dtype),
                       pltpu.SemaphoreType.DMA],
        name="scatter_rows")
    def scatter(rows_hbm, idx_a_hbm, idx_b_hbm, out_hbm, idx_v, rows_v, sem):
        base = (lax.axis_index("s") * SC_CORES + lax.axis_index("c")) * per_worker

        @pl.loop(0, per_worker // SC_GATHER_ROWS)
        def _(i):
            off = pl.multiple_of(base + i * SC_GATHER_ROWS, SC_GATHER_ROWS)
            pltpu.sync_copy(idx_a_hbm.at[pl.ds(off, SC_GATHER_ROWS)], idx_v.at[0])
            pltpu.sync_copy(idx_b_hbm.at[pl.ds(off, SC_GATHER_ROWS)], idx_v.at[1])
            pltpu.sync_copy(rows_hbm.at[pl.ds(off, SC_GATHER_ROWS)], rows_v)
            pltpu.async_copy(rows_v, out_hbm.at[idx_v.at[0]], sem).wait()
            pltpu.async_copy(rows_v, out_hbm.at[idx_v.at[1]], sem).wait()

    return scatter(rows, idx_a, idx_b)


def _gather_rows(table, idx):
    n_idx = idx.shape[0]
    width = table.shape[1]
    workers = SC_CORES * SC_SUBCORES
    per_worker = n_idx // workers
    assert n_idx % (workers * SC_GATHER_ROWS) == 0
    mesh = plsc.VectorSubcoreMesh(core_axis_name="c", subcore_axis_name="s")

    @functools.partial(
        pl.kernel, mesh=mesh,
        out_type=jax.ShapeDtypeStruct((n_idx, width), table.dtype),
        scratch_types=[pltpu.VMEM((SC_GATHER_ROWS,), jnp.int32),
                       pltpu.VMEM((SC_GATHER_ROWS, width), table.dtype),
                       pltpu.SemaphoreType.DMA],
        name="gather_rows")
    def gather(table_hbm, idx_hbm, out_hbm, idx_v, rows_v, sem):
        base = (lax.axis_index("s") * SC_CORES + lax.axis_index("c")) * per_worker

        @pl.loop(0, per_worker // SC_GATHER_ROWS)
        def _(i):
            off = pl.multiple_of(base + i * SC_GATHER_ROWS, SC_GATHER_ROWS)
            pltpu.sync_copy(idx_hbm.at[pl.ds(off, SC_GATHER_ROWS)], idx_v)
            pltpu.async_copy(table_hbm.at[idx_v], rows_v, sem).wait()
            pltpu.sync_copy(rows_v, out_hbm.at[pl.ds(off, SC_GATHER_ROWS)])

    return gather(table, idx)


def _experts_kernel(tile_ref, exp_ref, lo_ref, hi_ref, first_ref, last_ref,
                    x_ref, w1_ref, w3_ref, w2_ref, o_ref, acc_ref):
    i = pl.program_id(0)
    tm, half = x_ref.shape
    f = w1_ref.shape[2]
    fc = FFN_CHUNK_COLS

    @pl.when(first_ref[i] == 1)
    def _():
        acc_ref[...] = jnp.zeros_like(acc_ref)

    lo, hi = lo_ref[i], hi_ref[i]

    @pl.when(hi > lo)
    def _():
        x_hi, x_lo = _unpack_bf16_pairs(x_ref[...])
        x_hi, x_lo = x_hi.astype(BF16), x_lo.astype(BF16)
        y = None
        for c in range(f // fc):
            cols = slice(c * fc, (c + 1) * fc)
            a = _dot(x_hi, w1_ref[0, :half, cols]) + _dot(x_lo, w1_ref[0, half:, cols])
            b = _dot(x_hi, w3_ref[0, :half, cols]) + _dot(x_lo, w3_ref[0, half:, cols])
            part = _dot((a * _sigmoid(a) * b).astype(BF16), w2_ref[0, cols, :])
            y = part if y is None else y + part
        rows = tile_ref[i] * tm + lax.broadcasted_iota(jnp.int32, (tm, 1), 0)
        keep = jnp.logical_and(rows >= lo, rows < hi)
        acc_ref[...] += jnp.where(keep, y, 0.0)

    @pl.when(last_ref[i] == 1)
    def _():
        o_ref[...] = _pack_bf16_pairs(acc_ref[...])


def _experts(xs_sorted, items, w1, w3, w2, moe_index):
    n_slots, half = xs_sorted.shape
    _, n_exp, d, f = w1.shape
    tm = MOE_ROW_TILE
    assert f % FFN_CHUNK_COLS == 0
    tile_map = lambda i, tiles, *_: (tiles[i], 0)
    exp_map = lambda i, tiles, experts, *_: (moe_index, experts[i], 0, 0)
    grid_spec = pltpu.PrefetchScalarGridSpec(
        num_scalar_prefetch=len(items),
        grid=(items[0].shape[0],),
        in_specs=[pl.BlockSpec((tm, half), tile_map),
                  pl.BlockSpec((None, 1, d, f), exp_map), pl.BlockSpec((None, 1, d, f), exp_map),
                  pl.BlockSpec((None, 1, f, d), exp_map)],
        out_specs=pl.BlockSpec((tm, half), tile_map),
        scratch_shapes=[pltpu.VMEM((tm, d), F32)],
    )
    return pl.pallas_call(
        _experts_kernel,
        grid_spec=grid_spec,
        out_shape=jax.ShapeDtypeStruct((n_slots, half), jnp.int32),
        compiler_params=_params(1),
        name="experts",
    )(*items, xs_sorted, w1, w3, w2)


def _combine_kernel(x1_ref, ya_ref, yb_ref, route_ref, mod_ref, fg_ref, o_ref, *, final):
    half = ya_ref.shape[1]
    route = route_ref[...]
    wa, wb = route[:, 2:3], route[:, 3:4]
    a_hi, a_lo = _unpack_bf16_pairs(ya_ref[...])
    b_hi, b_lo = _unpack_bf16_pairs(yb_ref[...])
    gate = mod_ref[0][5:6]
    out_hi = x1_ref[:, :half] + gate[:, :half] * (wa * a_hi + wb * b_hi)
    out_lo = x1_ref[:, half:] + gate[:, half:] * (wa * a_lo + wb * b_lo)
    if final:
        total = jnp.sum(out_hi * out_hi, axis=-1, keepdims=True) + jnp.sum(out_lo * out_lo, axis=-1, keepdims=True)
        scale = lax.rsqrt(total / (2 * half) + EPS)
        out_hi = out_hi * scale * fg_ref[:, :half]
        out_lo = out_lo * scale * fg_ref[:, half:]
    o_ref[:, :half] = out_hi
    o_ref[:, half:] = out_lo


def _combine(x1, y_pairs, route, mod, final_g, seg, final):
    n_rows, d = x1.shape
    tm = seg["tm"]
    row, mod_map = _row_maps(seg, tm)
    second = n_rows // tm
    return pl.pallas_call(
        functools.partial(_combine_kernel, final=final),
        grid=(n_rows // tm,),
        in_specs=[pl.BlockSpec((tm, d), row), pl.BlockSpec((tm, d // 2), row),
                  pl.BlockSpec((tm, d // 2), lambda i: (second + i, 0)),
                  pl.BlockSpec((tm, LANES), row), pl.BlockSpec((1, 6, d), mod_map),
                  pl.BlockSpec((1, d), lambda i: (0, 0))],
        out_specs=pl.BlockSpec((tm, d), row),
        out_shape=jax.ShapeDtypeStruct((n_rows, d), F32),
        compiler_params=_params(1),
        name="moe_combine",
    )(x1, y_pairs, y_pairs, route, mod, final_g)


def _moe_layer(xs, mix, mod, g2, wo, layer, wr, br, w1, w3, w2, moe_index, final_g, seg, n_rows, final):
    x1, h_pairs, route = _router(xs, mix, mod, g2, wo, layer, wr, br, seg, n_rows)
    slot_a, slot_b, items = _route_plan(route, MOE_ROW_TILE)
    y_sorted = _experts(_scatter_rows(h_pairs, slot_a, slot_b), items, w1, w3, w2, moe_index)
    y_pairs = _gather_rows(y_sorted, jnp.concatenate([slot_a, slot_b]))
    return _combine(x1, y_pairs, route, mod, final_g, seg, final)


def kernel(x, c, ctx, c_ctx, norm1_g, norm2_g, w_mod, b_mod, w_in, conv_w, conv_b, b_gates, attn_sink,
           g_att, g_ml, w_out, ffn_w1, ffn_w3, ffn_w2, w_router, b_router, exp_w1, exp_w3, exp_w2,
           final_g):
    b_, s_, d = x.shape
    lc = ctx.shape[1]
    depth = w_in.shape[0]
    n_lat, n_ctx = b_ * s_, b_ * lc
    tm = min(ROW_TILE, s_)
    assert s_ % tm == 0 and n_ctx % tm == 0
    assert (TOP_K * n_lat) % MOE_ROW_TILE == 0 and (TOP_K * n_ctx) % MOE_ROW_TILE == 0
    assert s_ % ML_CHUNK == 0 and lc % ML_CHUNK == 0 and n_lat % lc == 0 and b_ < MOD_ROWS
    seg = dict(B=b_, S=s_, Lc=lc, n_lat=n_lat, tm=tm)

    cond = jnp.zeros((MOD_ROWS, d), F32).at[:b_].set(c).at[b_].set(c_ctx)
    mods = _modulation(cond, w_mod, b_mod).reshape(depth, MOD_ROWS, 6, d)
    rope = _rope_tables(s_, tm)
    xs = jnp.concatenate([x.reshape(n_lat, d), ctx.reshape(n_ctx, d)], axis=0)
    final_row = final_g.reshape(1, d)

    w_in_b, w_out_b = w_in.astype(BF16), w_out.astype(BF16)
    ffn_b = (ffn_w1.astype(BF16), ffn_w3.astype(BF16), ffn_w2.astype(BF16))
    exp_b = (exp_w1.astype(BF16), exp_w3.astype(BF16), exp_w2.astype(BF16))

    for layer in range(depth):
        last = layer == depth - 1
        w_gates = w_in[layer][:, MAIN_WIDTH:]
        w_gc = jnp.pad(w_gates, ((0, 0), (0, LANES - ML_GATES))).astype(BF16)
        b_gc = jnp.pad(b_gates[layer], (0, LANES - ML_GATES)).reshape(1, LANES)
        qa, ka, va, qm, km, vm, om, gc, gr = _input_projection(
            xs, mods[layer], norm1_g[layer].reshape(1, d), w_in_b, layer, w_gc,
            w_gates.T.astype(BF16), b_gc, b_gates[layer].reshape(ML_GATES, 1),
            conv_w[layer], conv_b[layer].reshape(1, -1), rope, seg)
        att = _attention(qa, ka, va, attn_sink[layer], g_att[layer].reshape(1, ATT_WIDTH), seg, not last)
        hf, hb = _mlstm(qm, km, vm, gc, gr, seg)
        mix = (att, hf, hb, om, g_ml[layer].reshape(1, ML_WIDTH))
        n_rows = n_lat if last else n_lat + n_ctx
        g2 = norm2_g[layer].reshape(1, d)
        i = layer // 2
        if layer % 2 == 0:
            xs = _dense_layer(xs, mix, mods[layer], g2, w_out_b, layer, *ffn_b, i, final_row, seg, n_rows, last)
        else:
            wr = jnp.pad(w_router[i], ((0, 0), (0, LANES - N_EXPERTS)))
            wr_hi = wr.astype(BF16)
            wr = jnp.concatenate([wr_hi, (wr - wr_hi.astype(F32)).astype(BF16)], axis=1)
            br = jnp.pad(b_router[i], (0, LANES - N_EXPERTS)).reshape(1, LANES)
            xs = _moe_layer(xs, mix, mods[layer], g2, w_out_b, layer, wr, br, *exp_b, i, final_row, seg,
                            n_rows, last)
    return xs[:n_lat].reshape(b_, s_, d)
```

```python
import functools

import jax
import jax.numpy as jnp
import numpy as np
from jax import lax
from jax.experimental import pallas as pl
from jax.experimental.pallas import tpu as pltpu
from jax.experimental.pallas import tpu_sc as plsc

F32 = jnp.float32
BF16 = jnp.bfloat16

GRID_W = 64
ATT_HEADS = 8
ATT_KV_HEADS = 2
ATT_HEAD_DIM = 64
ATT_GROUP = ATT_HEADS // ATT_KV_HEADS
WINDOW = 128
ATT_BLOCK = 128
ROPE_THETA = 10000.0
ML_HEADS = 4
ML_QK_DIM = 64
ML_V_DIM = 128
ML_CONV = 5
GATE_CAP = 15.0
ATT_WIDTH = ATT_HEADS * ATT_HEAD_DIM
ATT_KV_WIDTH = ATT_KV_HEADS * ATT_HEAD_DIM
ML_QK_WIDTH = ML_HEADS * ML_QK_DIM
ML_WIDTH = ML_HEADS * ML_V_DIM
ML_GATES = 4 * ML_HEADS
MAIN_WIDTH = ATT_WIDTH + 2 * ATT_KV_WIDTH + 2 * ML_QK_WIDTH + 2 * ML_WIDTH
N_EXPERTS = 8
TOP_K = 2
EPS = 1e-6

LANES = 128
SUBLANES = 8
VMEM_LIMIT = 56 * 1024 * 1024
NEG = -1e30
SC_CORES = 2
SC_SUBCORES = 16
SC_GATHER_ROWS = 64

ROW_TILE = 512
MOE_ROW_TILE = 512
FFN_CHUNK_COLS = 256
ATT_STEP_BLOCKS = 8
ML_CHUNK = 128
ML_STEP_CHUNKS = 2
CONV_HALO = SUBLANES
MOD_ROWS = 16
MOD_COL_TILE = 1536


def _dot(a, b):
    return jnp.dot(a, b, preferred_element_type=F32)


def _dot_nt(a, b):
    return lax.dot_general(a, b, (((1,), (1,)), ((), ())), preferred_element_type=F32)


def _dot_tn(a, b):
    return lax.dot_general(a, b, (((0,), (0,)), ((), ())), preferred_element_type=F32)


def _dot_f32(a, b):
    return jnp.dot(a, b, preferred_element_type=F32, precision=lax.Precision.HIGHEST)


def _sigmoid(x):
    return 1.0 / (1.0 + jnp.exp(-x))


def _rms(x, g):
    return x * lax.rsqrt(jnp.mean(x * x, axis=-1, keepdims=True) + EPS) * g


def _adaln(x, g, shift, scale):
    return _rms(x, g) * (1.0 + scale) + shift


def _params(n_axes):
    return pltpu.CompilerParams(dimension_semantics=("arbitrary",) * n_axes,
                                vmem_limit_bytes=VMEM_LIMIT)


def _resident(shape):
    zeros = (0,) * len(shape)
    return pl.BlockSpec(shape, lambda *_: zeros, pipeline_mode=pl.Buffered(1))


def _layer_resident(stacked, layer, block=None):
    block = tuple(stacked.shape[1:]) if block is None else block
    index = (layer,) + (0,) * len(block)
    return pl.BlockSpec((None,) + block, lambda *_: index, pipeline_mode=pl.Buffered(1))


def _mod_kernel(c_ref, w_ref, b_ref, o_ref):
    c = c_ref[...]
    a = (c * _sigmoid(c)).astype(BF16)
    o_ref[0] = _dot(a, w_ref[0].astype(BF16)) + b_ref[0]


def _modulation(cond, w_mod, b_mod):
    depth, d, width = w_mod.shape
    tn = MOD_COL_TILE if width % MOD_COL_TILE == 0 else width
    return pl.pallas_call(
        _mod_kernel,
        grid=(depth, width // tn),
        in_specs=[pl.BlockSpec((MOD_ROWS, d), lambda l, j: (0, 0)),
                  pl.BlockSpec((1, d, tn), lambda l, j: (l, 0, j)),
                  pl.BlockSpec((1, 1, tn), lambda l, j: (l, 0, j))],
        out_specs=pl.BlockSpec((1, MOD_ROWS, tn), lambda l, j: (l, 0, j)),
        out_shape=jax.ShapeDtypeStruct((depth, MOD_ROWS, width), F32),
        compiler_params=_params(2),
        name="modulation",
    )(cond, w_mod, b_mod.reshape(depth, 1, width))


def _gate_act(u, is_forget):
    g = GATE_CAP * jnp.tanh(u / GATE_CAP)
    log_sig = jnp.minimum(g, 0.0) - jnp.log1p(jnp.exp(-jnp.abs(g)))
    return jnp.where(is_forget, log_sig, g)


def _conv_silu(xe, cw, cb, rows):
    n_ext = xe.shape[0]
    mid = ML_CONV // 2
    y = cb + cw[mid:mid + 1] * xe[CONV_HALO:CONV_HALO + rows]
    for tap in range(ML_CONV):
        if tap != mid:
            y = y + cw[tap:tap + 1] * pltpu.roll(xe, (mid - tap) % n_ext, 0)[CONV_HALO:CONV_HALO + rows]
    return y * _sigmoid(y)


def _inproj_kernel(x_ref, xp_ref, xn_ref, mod_ref, g_ref, w_ref, wgc_ref, wgr_ref, bgc_ref, bgr_ref,
                   cw_ref, cb_ref, cos_ref, sa_ref, sb_ref,
                   qa_ref, ka_ref, va_ref, qm_ref, km_ref, vm_ref, om_ref, gc_ref, gr_ref,
                   *, n_lat_tiles, seq_lat, seq_ctx):
    mod = mod_ref[0]
    tm = x_ref.shape[0]
    seg = min(tm, seq_ctx)
    n_seg = tm // seg
    norm_gain = g_ref[...]
    normed = lambda rows: _adaln(rows, norm_gain, mod[0:1], mod[1:2]).astype(BF16)
    quarter = ATT_HEAD_DIM // 4

    def rope(u, rows):
        return (u * cos_ref[rows, :] + pltpu.roll(u, LANES - quarter, 1) * sa_ref[rows, :]
                + pltpu.roll(u, quarter, 1) * sb_ref[rows, :])

    c_kv = ATT_WIDTH
    c_qk = c_kv + 2 * ATT_KV_WIDTH
    c_vm = c_qk + 2 * ML_QK_WIDTH
    c_om = c_vm + ML_WIDTH
    w_qk = w_ref[:, c_qk:c_vm]
    i = pl.program_id(0)
    seq_len = jnp.where(i < n_lat_tiles, seq_lat, seq_ctx)
    cw, cb = cw_ref[...], cb_ref[...]
    hx = [normed(x_ref[j * seg:(j + 1) * seg, :]) for j in range(n_seg)]
    qk = [_dot(h, w_qk) for h in hx]
    qk_prev = _dot(normed(xp_ref[...]), w_qk)
    qk_next = _dot(normed(xn_ref[...]), w_qk)
    for j in range(n_seg):
        rows = slice(j * seg, (j + 1) * seg)
        first_row = i * tm + j * seg
        has_prev = (lax.rem(first_row, seq_len) != 0).astype(F32)
        has_next = (lax.rem(first_row + seg, seq_len) != 0).astype(F32)
        prev = qk_prev if j == 0 else qk[j - 1][seg - CONV_HALO:]
        nxt = qk_next if j == n_seg - 1 else qk[j + 1][:CONV_HALO]
        xe = jnp.concatenate([prev * has_prev, qk[j], nxt * has_next], axis=0)
        y = _conv_silu(xe, cw, cb, seg)
        qm_ref[rows, :] = (y[:, :ML_QK_WIDTH] * (ML_QK_DIM ** -0.5)).astype(BF16)
        km_ref[rows, :] = y[:, ML_QK_WIDTH:].astype(BF16)
        q = _dot(hx[j], w_ref[:, :c_kv])
        for c in range(ATT_WIDTH // LANES):
            sl = slice(c * LANES, (c + 1) * LANES)
            qa_ref[rows, sl] = (rope(q[:, sl], rows) * (ATT_HEAD_DIM ** -0.5)).astype(BF16)
        kv = _dot(hx[j], w_ref[:, c_kv:c_qk])
        ka_ref[rows, :] = rope(kv[:, :ATT_KV_WIDTH], rows).astype(BF16)
        va_ref[rows, :] = kv[:, ATT_KV_WIDTH:].astype(BF16)
        vm_ref[rows, :] = _dot(hx[j], w_ref[:, c_vm:c_om]).astype(BF16)
        om_ref[rows, :] = _dot(hx[j], w_ref[:, c_om:c_om + ML_WIDTH]).astype(BF16)
        gc = _dot(hx[j], wgc_ref[...]) + bgc_ref[...]
        lane = lax.broadcasted_iota(jnp.int32, gc.shape, 1)
        gc_ref[rows, :] = _gate_act(gc, (lane // ML_HEADS) % 2 == 1)
        gr = _dot_nt(wgr_ref[...], hx[j]) + bgr_ref[...]
        sub = lax.broadcasted_iota(jnp.int32, gr.shape, 0)
        gr_ref[:, rows] = _gate_act(gr, (sub // ML_HEADS) % 2 == 1)


def _input_projection(xs, mod, g1, w_in, layer, w_gc, w_gr, b_gc, b_gr, conv_w, conv_b, rope, seg):
    n, d = xs.shape
    tm = seg["tm"]
    nlat = seg["n_lat"] // tm
    s_tiles = seg["S"] // tm
    halos_per_tile = tm // CONV_HALO
    row = lambda i: (i, 0)
    prev = lambda i: (jnp.maximum(i * halos_per_tile - 1, 0), 0)
    nxt = lambda i: (jnp.minimum((i + 1) * halos_per_tile, n // CONV_HALO - 1), 0)
    mod_map = lambda i: (jnp.where(i < nlat, i // s_tiles, seg["B"]), 0, 0)
    rope_map = lambda i: (jnp.where(i < nlat, i % s_tiles, s_tiles), 0)
    widths = [(ATT_WIDTH, BF16), (ATT_KV_WIDTH, BF16), (ATT_KV_WIDTH, BF16), (ML_QK_WIDTH, BF16),
              (ML_QK_WIDTH, BF16), (ML_WIDTH, BF16), (ML_WIDTH, BF16), (LANES, F32)]
    out_shape = [jax.ShapeDtypeStruct((n, w), t) for w, t in widths]
    out_specs = [pl.BlockSpec((tm, w), row) for w, _ in widths]
    out_shape.append(jax.ShapeDtypeStruct((ML_GATES, n), F32))
    out_specs.append(pl.BlockSpec((ML_GATES, tm), lambda i: (0, i)))
    return pl.pallas_call(
        functools.partial(_inproj_kernel, n_lat_tiles=nlat, seq_lat=seg["S"], seq_ctx=seg["Lc"]),
        grid=(n // tm,),
        in_specs=[pl.BlockSpec((tm, d), row),
                  pl.BlockSpec((CONV_HALO, d), prev),
                  pl.BlockSpec((CONV_HALO, d), nxt),
                  pl.BlockSpec((1, 6, d), mod_map),
                  _resident((1, d)),
                  _layer_resident(w_in, layer, (d, MAIN_WIDTH)), _resident(w_gc.shape), _resident(w_gr.shape),
                  _resident(b_gc.shape), _resident(b_gr.shape),
                  _resident(conv_w.shape), _resident(conv_b.shape),
                  pl.BlockSpec((tm, LANES), rope_map),
                  pl.BlockSpec((tm, LANES), rope_map),
                  pl.BlockSpec((tm, LANES), rope_map)],
        out_specs=out_specs,
        out_shape=out_shape,
        compiler_params=_params(1),
        name="input_projection",
    )(xs, xs, xs, mod, g1, w_in, w_gc, w_gr, b_gc, b_gr, conv_w, conv_b, *rope)


def _rope_tables(s, tm):
    quarter = ATT_HEAD_DIM // 4
    t = jnp.arange(s)
    row = (t // GRID_W).astype(F32)
    col = (t % GRID_W).astype(F32)
    inv = ROPE_THETA ** (-jnp.arange(quarter, dtype=F32) / quarter)
    ang_r = row[:, None] * inv[None, :]
    ang_c = col[:, None] * inv[None, :]
    zero = jnp.zeros_like(ang_r)
    cos = jnp.concatenate([jnp.cos(ang_r)] * 2 + [jnp.cos(ang_c)] * 2, axis=1)
    sin_up = jnp.concatenate([-jnp.sin(ang_r), zero, -jnp.sin(ang_c), zero], axis=1)
    sin_dn = jnp.concatenate([zero, jnp.sin(ang_r), zero, jnp.sin(ang_c)], axis=1)
    reps = LANES // ATT_HEAD_DIM
    ident = [jnp.ones((tm, LANES), F32), jnp.zeros((tm, LANES), F32), jnp.zeros((tm, LANES), F32)]
    return tuple(jnp.concatenate([jnp.tile(a, (1, reps)), i], axis=0)
                 for a, i in zip((cos, sin_up, sin_dn), ident))


def _attn_kernel(sink_ref, *refs, window, blocks):
    blk = ATT_BLOCK
    if window:
        q_ref, kp_ref, kc_ref, kn_ref, vp_ref, vc_ref, vn_ref, kx_ref, vx_ref, g_ref, o_ref = refs
        j = pl.program_id(1)
        k_own, v_own = kc_ref[...], vc_ref[...]
        k_blocks = [kp_ref[...]] + [k_own[t * blk:(t + 1) * blk] for t in range(blocks)] + [kn_ref[...]]
        v_blocks = [vp_ref[...]] + [v_own[t * blk:(t + 1) * blk] for t in range(blocks)] + [vn_ref[...]]
    else:
        q_ref, kx_ref, vx_ref, g_ref, _, o_ref = refs
    for t in range(blocks):
        if window:
            has_prev = j > 0 if t == 0 else True
            has_next = j < pl.num_programs(1) - 1 if t == blocks - 1 else True
            win = (k_blocks[t:t + 3], v_blocks[t:t + 3], has_prev, has_next)
        else:
            win = None
        att = _attend_block(sink_ref, q_ref[t * blk:(t + 1) * blk, :], win, kx_ref[...], vx_ref[...])
        o_ref[t * blk:(t + 1) * blk, :] = _rms(att, g_ref[...]).astype(o_ref.dtype)


def _attend_block(sink_ref, q, win, k_ctx, v_ctx):
    blk = ATT_BLOCK
    if win is None:
        k_all, v_all, bias = k_ctx, v_ctx, None
    else:
        k_win, v_win, has_prev, has_next = win
        rows = lax.broadcasted_iota(jnp.int32, (blk, blk), 0)
        cols = lax.broadcasted_iota(jnp.int32, (blk, blk), 1)
        ok_p = jnp.logical_and(cols >= rows, has_prev)
        ok_n = jnp.logical_and(cols <= rows, has_next)
        bias = jnp.concatenate([jnp.where(ok_p, 0.0, NEG), jnp.zeros((blk, blk), F32),
                                jnp.where(ok_n, 0.0, NEG), jnp.zeros((blk, k_ctx.shape[0]), F32)], axis=1)
        k_all = jnp.concatenate(list(k_win) + [k_ctx], axis=0)
        v_all = jnp.concatenate(list(v_win) + [v_ctx], axis=0)
    dh = ATT_HEAD_DIM
    outs = []
    for h in range(ATT_KV_HEADS):
        k_h = k_all[:, h * dh:(h + 1) * dh]
        v_h = v_all[:, h * dh:(h + 1) * dh]
        q_h = jnp.concatenate([q[:, (h * ATT_GROUP + g) * dh:(h * ATT_GROUP + g + 1) * dh]
                               for g in range(ATT_GROUP)], axis=0)
        s_all = _dot_nt(q_h, k_h)
        p_parts, inv_parts = [], []
        for g in range(ATT_GROUP):
            sink = sink_ref[h * ATT_GROUP + g]
            s = s_all[g * blk:(g + 1) * blk]
            if bias is not None:
                s = s + bias
            m = jnp.maximum(jnp.max(s, axis=-1, keepdims=True), sink)
            p = jnp.exp(s - m)
            denom = jnp.sum(p, axis=-1, keepdims=True) + jnp.exp(sink - m)
            p_parts.append(p.astype(BF16))
            inv_parts.append(1.0 / denom)
        o = _dot(jnp.concatenate(p_parts, axis=0), v_h)
        for g in range(ATT_GROUP):
            outs.append(o[g * blk:(g + 1) * blk] * inv_parts[g])
    return jnp.concatenate(outs, axis=1)


def _attention(qa, ka, va, sink, g_att, seg, with_ctx):
    n = qa.shape[0]
    b_, s_, lc = seg["B"], seg["S"], seg["Lc"]
    blk = ATT_BLOCK
    blocks_per_seq = s_ // blk
    ctx_map = lambda b, j, *_: (b_ * s_ // lc + b, 0)
    ctx_specs = [pl.BlockSpec((lc, ATT_KV_WIDTH), ctx_map)] * 2
    g_spec = pl.BlockSpec((1, ATT_WIDTH), lambda b, j, *_: (0, 0))

    def call(window, blocks, steps, q_map, in_specs, args, aliases):
        grid_spec = pltpu.PrefetchScalarGridSpec(
            num_scalar_prefetch=1,
            grid=(b_, steps),
            in_specs=[pl.BlockSpec((blocks * blk, ATT_WIDTH), q_map)] + in_specs,
            out_specs=pl.BlockSpec((blocks * blk, ATT_WIDTH), q_map))
        return pl.pallas_call(
            functools.partial(_attn_kernel, window=window, blocks=blocks),
            grid_spec=grid_spec,
            out_shape=jax.ShapeDtypeStruct((n, ATT_WIDTH), BF16),
            input_output_aliases=aliases,
            compiler_params=_params(2),
            name="window_attention" if window else "context_attention",
        )(sink, qa, *args)

    qb = min(ATT_STEP_BLOCKS, blocks_per_seq)
    assert blocks_per_seq % qb == 0
    nqs = blocks_per_seq // qb

    def edge_map(off):
        return lambda b, j, *_: (b * blocks_per_seq + jnp.clip(j * qb + off, 0, blocks_per_seq - 1), 0)

    kv_specs = [pl.BlockSpec((blk, ATT_KV_WIDTH), edge_map(-1)),
                pl.BlockSpec((qb * blk, ATT_KV_WIDTH), lambda b, j, *_: (b * nqs + j, 0)),
                pl.BlockSpec((blk, ATT_KV_WIDTH), edge_map(qb))]
    att = call(True, qb, nqs, lambda b, j, *_: (b * nqs + j, 0), kv_specs + kv_specs + ctx_specs + [g_spec],
               (ka, ka, ka, va, va, va, ka, va, g_att), {})
    if with_ctx:
        cb = min(ATT_STEP_BLOCKS, lc // blk)
        assert (lc // blk) % cb == 0 and (b_ * s_) % (cb * blk) == 0
        ncs = lc // (cb * blk)
        first = b_ * s_ // (cb * blk)
        att = call(False, cb, ncs, lambda b, j, *_: (first + b * ncs + j, 0),
                   ctx_specs + [g_spec, pl.BlockSpec(memory_space=pl.ANY)], (ka, va, g_att, att), {5: 0})
    return att


def _mlstm_direction(q_ref, k_ref, v_ref, gc_ref, gr_ref, out_ref, state_ref, tok, *, reverse):
    chunk = ML_CHUNK
    rows = lax.broadcasted_iota(jnp.int32, (chunk, chunk), 0)
    cols = lax.broadcasted_iota(jnp.int32, (chunk, chunk), 1)
    lower = rows >= cols
    upper = rows <= cols
    seen = upper if reverse else lower
    gc = gc_ref[tok, :]
    gr = gr_ref[:, tok]
    b_col = _dot_f32(seen.astype(F32), gc)
    b_row = _dot_f32(gr, (lower if reverse else upper).astype(F32))
    b_end = jnp.sum(gc, axis=0, keepdims=True)
    base = 2 * ML_HEADS if reverse else 0
    pair_width = 2 * ML_QK_DIM
    lane = lax.broadcasted_iota(jnp.int32, (chunk, pair_width), 1)
    state_row = lax.broadcasted_iota(jnp.int32, (pair_width, 1), 0)
    ones = jnp.ones((chunk, ML_V_DIM), BF16)

    for pair in range(ML_HEADS // 2):
        q_pair = q_ref[tok, pair * pair_width:(pair + 1) * pair_width]
        k_pair = k_ref[tok, pair * pair_width:(pair + 1) * pair_width]
        state = state_ref[pair]
        state_bf = state.astype(BF16)
        update = None
        decays = []
        for sub in range(2):
            h = 2 * pair + sub
            i_idx = base + h
            f_idx = base + ML_HEADS + h
            own = (lane >= ML_QK_DIM) if sub else (lane < ML_QK_DIM)
            q_h = jnp.where(own, q_pair, jnp.zeros_like(q_pair))
            vx = jnp.concatenate([v_ref[tok, h * ML_V_DIM:(h + 1) * ML_V_DIM], ones], axis=1)
            bc = b_col[:, f_idx:f_idx + 1]
            d = bc - b_row[f_idx:f_idx + 1, :] + gr[i_idx:i_idx + 1, :]
            w = jnp.exp(jnp.where(seen, d, NEG))
            s = _dot_nt(q_h, k_pair) * w
            tot = _dot(s.astype(BF16), vx) + jnp.exp(bc) * _dot(q_h, state_bf)
            h_out = tot[:, :ML_V_DIM] / jnp.maximum(jnp.abs(tot[:, ML_V_DIM:]), 1.0)
            out_ref[tok, h * ML_V_DIM:(h + 1) * ML_V_DIM] = h_out.astype(out_ref.dtype)

            be = b_end[:, f_idx:f_idx + 1]
            kw = jnp.where(own, k_pair.astype(F32) * jnp.exp(be - bc + gc[:, i_idx:i_idx + 1]), 0.0)
            part = _dot_tn(kw.astype(BF16), vx)
            update = part if update is None else update + part
            decays.append(jnp.exp(be))
        decay = jnp.where(state_row < ML_QK_DIM, decays[0], decays[1])
        state_ref[pair] = decay * state + update


def _mlstm_kernel(qf_ref, kf_ref, vf_ref, gcf_ref, grf_ref, qb_ref, kb_ref, vb_ref, gcb_ref, grb_ref,
                  hf_ref, hb_ref, sf_ref, sb_ref):
    @pl.when(pl.program_id(1) == 0)
    def _():
        sf_ref[...] = jnp.zeros_like(sf_ref)
        sb_ref[...] = jnp.zeros_like(sb_ref)

    n_chunks = qf_ref.shape[0] // ML_CHUNK
    for t in range(n_chunks):
        fwd = slice(t * ML_CHUNK, (t + 1) * ML_CHUNK)
        bwd = slice((n_chunks - 1 - t) * ML_CHUNK, (n_chunks - t) * ML_CHUNK)
        _mlstm_direction(qf_ref, kf_ref, vf_ref, gcf_ref, grf_ref, hf_ref, sf_ref, fwd, reverse=False)
        _mlstm_direction(qb_ref, kb_ref, vb_ref, gcb_ref, grb_ref, hb_ref, sb_ref, bwd, reverse=True)


def _mlstm(qm, km, vm, gc, gr, seg):
    n = qm.shape[0]
    b_, s_, lc = seg["B"], seg["S"], seg["Lc"]
    chunk = ML_STEP_CHUNKS * ML_CHUNK
    assert s_ % chunk == 0 and lc % chunk == 0
    ncc, ncl = lc // chunk, s_ // chunk
    lat_chunks = b_ * ncl

    def blk(b, c, reverse):
        pos = jnp.where(c < ncc, c, c - ncc)
        if reverse:
            pos = jnp.where(c < ncc, ncc, ncl) - 1 - pos
        return jnp.where(c < ncc, lat_chunks + b * ncc + pos, b * ncl + pos)

    def specs(reverse):
        cur = lambda b, c: (blk(b, c, reverse), 0)
        return [pl.BlockSpec((chunk, ML_QK_WIDTH), cur), pl.BlockSpec((chunk, ML_QK_WIDTH), cur),
                pl.BlockSpec((chunk, ML_WIDTH), cur), pl.BlockSpec((chunk, LANES), cur),
                pl.BlockSpec((ML_GATES, chunk), lambda b, c: (0, blk(b, c, reverse)))]

    out = lambda reverse: pl.BlockSpec((chunk, ML_WIDTH), lambda b, c: (blk(b, c, reverse), 0))
    state = pltpu.VMEM((ML_HEADS // 2, 2 * ML_QK_DIM, 2 * ML_V_DIM), F32)
    return pl.pallas_call(
        _mlstm_kernel,
        grid=(b_, ncc + ncl),
        in_specs=specs(False) + specs(True),
        out_specs=[out(False), out(True)],
        out_shape=[jax.ShapeDtypeStruct((n, ML_WIDTH), BF16)] * 2,
        scratch_shapes=[state, state],
        compiler_params=_params(2),
        name="mlstm_scan",
    )(qm, km, vm, gc, gr, qm, km, vm, gc, gr)


def _swiglu(h, w1_ref, w3_ref, w2_ref):
    f = w1_ref.shape[1]
    fc = FFN_CHUNK_COLS
    y = None
    for c in range(f // fc):
        cols = slice(c * fc, (c + 1) * fc)
        a = _dot(h, w1_ref[:, cols])
        b = _dot(h, w3_ref[:, cols])
        part = _dot((a * _sigmoid(a) * b).astype(BF16), w2_ref[cols, :])
        y = part if y is None else y + part
    return y


def _mix_residual(x_ref, mix_refs, mod, wo_ref):
    att_ref, hf_ref, hb_ref, om_ref, gml_ref = mix_refs
    mix = _dot(att_ref[...], wo_ref[:ATT_WIDTH, :])
    for h in range(ML_HEADS):
        sl = slice(h * ML_V_DIM, (h + 1) * ML_V_DIM)
        tot = hf_ref[:, sl].astype(F32) + hb_ref[:, sl].astype(F32)
        ml = _rms(tot, gml_ref[:, sl]) * _sigmoid(om_ref[:, sl].astype(F32))
        mix = mix + _dot(ml.astype(BF16), wo_ref[ATT_WIDTH + h * ML_V_DIM:ATT_WIDTH + (h + 1) * ML_V_DIM, :])
    return x_ref[...] + mod[2:3] * mix


def _mix_specs(tm, row):
    return [pl.BlockSpec((tm, ATT_WIDTH), row)] + [pl.BlockSpec((tm, ML_WIDTH), row)] * 3 \
        + [_resident((1, ML_WIDTH))]


def _dense_layer_kernel(x_ref, att_ref, hf_ref, hb_ref, om_ref, gml_ref, mod_ref, g_ref, wo_ref,
                        w1_ref, w3_ref, w2_ref, fg_ref, o_ref, *, final):
    mod = mod_ref[0]
    x1 = _mix_residual(x_ref, (att_ref, hf_ref, hb_ref, om_ref, gml_ref), mod, wo_ref)
    hx = _adaln(x1, g_ref[...], mod[3:4], mod[4:5]).astype(BF16)
    out = x1 + mod[5:6] * _swiglu(hx, w1_ref, w3_ref, w2_ref)
    if final:
        out = _rms(out, fg_ref[...])
    o_ref[...] = out


def _row_maps(seg, tm):
    nlat = seg["n_lat"] // tm
    s_tiles = seg["S"] // tm
    return (lambda i, *_: (i, 0)), (lambda i, *_: (jnp.where(i < nlat, i // s_tiles, seg["B"]), 0, 0))


def _dense_layer(xs, mix, mod, g2, wo, layer, w1, w3, w2, ffn_index, final_g, seg, n_rows, final):
    d = xs.shape[1]
    tm = seg["tm"]
    row, mod_map = _row_maps(seg, tm)
    return pl.pallas_call(
        functools.partial(_dense_layer_kernel, final=final),
        grid=(n_rows // tm,),
        in_specs=[pl.BlockSpec((tm, d), row)] + _mix_specs(tm, row)
                 + [pl.BlockSpec((1, 6, d), mod_map),
                    _resident((1, d)), _layer_resident(wo, layer), _layer_resident(w1, ffn_index),
                    _layer_resident(w3, ffn_index), _layer_resident(w2, ffn_index), _resident((1, d))],
        out_specs=pl.BlockSpec((tm, d), row),
        out_shape=jax.ShapeDtypeStruct((n_rows, d), F32),
        compiler_params=_params(1),
        name="dense_layer",
    )(xs, *mix, mod, g2, wo, w1, w3, w2, final_g)


def _router_kernel(x_ref, att_ref, hf_ref, hb_ref, om_ref, gml_ref, mod_ref, g_ref, wo_ref, wr_ref, br_ref,
                   earlier_ref,
                   x1_ref, h_ref, route_ref, counts_ref, count_ref):
    mod = mod_ref[0]
    x1 = _mix_residual(x_ref, (att_ref, hf_ref, hb_ref, om_ref, gml_ref), mod, wo_ref)
    x1_ref[...] = x1
    hx = _adaln(x1, g_ref[...], mod[3:4], mod[4:5])
    h_ref[...] = _pack_bf16_pairs(hx)
    hx_hi = hx.astype(BF16)
    hx_lo = (hx - hx_hi.astype(F32)).astype(BF16)
    both = _dot(hx_hi, wr_ref[...])
    logits = both[:, :LANES] + both[:, LANES:] + _dot(hx_lo, wr_ref[:, :LANES]) + br_ref[...]
    lane = lax.broadcasted_iota(jnp.int32, logits.shape, 1)
    logits = jnp.where(lane < N_EXPERTS, logits, -jnp.inf)
    top1 = jnp.max(logits, axis=-1, keepdims=True)
    idx1 = jnp.min(jnp.where(logits == top1, lane, LANES), axis=-1, keepdims=True)
    rest = jnp.where(lane == idx1, -jnp.inf, logits)
    top2 = jnp.max(rest, axis=-1, keepdims=True)
    idx2 = jnp.min(jnp.where(rest == top2, lane, LANES), axis=-1, keepdims=True)
    e2 = jnp.exp(top2 - top1)
    w_first = 1.0 / (1.0 + e2)
    @pl.when(pl.program_id(0) == 0)
    def _():
        count_ref[...] = jnp.zeros_like(count_ref)

    hot1 = (lane == idx1).astype(F32)
    hot2 = (lane == idx2).astype(F32)
    hot = hot1 + hot2
    before = _dot(earlier_ref[...], hot.astype(BF16)) + count_ref[...]
    rank1 = jnp.sum(hot1 * before, axis=-1, keepdims=True)
    rank2 = jnp.sum(hot2 * before, axis=-1, keepdims=True)
    count_ref[...] += jnp.sum(hot, axis=0, keepdims=True)
    counts_ref[...] = count_ref[...]
    route = jnp.where(lane == 0, idx1.astype(F32), jnp.where(lane == 1, idx2.astype(F32), 0.0))
    route = jnp.where(lane == 2, w_first, jnp.where(lane == 3, e2 * w_first, route))
    route_ref[...] = jnp.where(lane == 4, rank1, jnp.where(lane == 5, rank2, route))


def _router(xs, mix, mod, g2, wo, layer, wr, br, seg, n_rows):
    d = xs.shape[1]
    tm = seg["tm"]
    row, mod_map = _row_maps(seg, tm)
    return pl.pallas_call(
        _router_kernel,
        grid=(n_rows // tm,),
        in_specs=[pl.BlockSpec((tm, d), row)] + _mix_specs(tm, row)
                 + [pl.BlockSpec((1, 6, d), mod_map),
                    _resident((1, d)), _layer_resident(wo, layer), _resident(wr.shape), _resident(br.shape),
                    _resident((tm, tm))],
        out_specs=[pl.BlockSpec((tm, d), row), pl.BlockSpec((tm, d // 2), row),
                   pl.BlockSpec((tm, LANES), row), pl.BlockSpec((1, LANES), lambda i: (0, 0))],
        out_shape=[jax.ShapeDtypeStruct((n_rows, d), F32), jax.ShapeDtypeStruct((n_rows, d // 2), jnp.int32),
                   jax.ShapeDtypeStruct((n_rows, LANES), F32), jax.ShapeDtypeStruct((1, LANES), F32)],
        scratch_shapes=[pltpu.VMEM((1, LANES), F32)],
        compiler_params=_params(1),
        name="mix_router",
    )(xs, *mix, mod, g2, wo, wr, br, jnp.tril(jnp.ones((tm, tm), BF16), -1))


def _pack_bf16_pairs(h):
    half = h.shape[1] // 2
    hi = lax.bitcast_convert_type(h[:, :half].astype(BF16).astype(F32), jnp.int32)
    lo = lax.bitcast_convert_type(h[:, half:].astype(BF16).astype(F32), jnp.int32)
    return (hi & jnp.int32(-65536)) | lax.shift_right_logical(lo, 16)


def _unpack_bf16_pairs(p):
    hi = lax.bitcast_convert_type(p & jnp.int32(-65536), F32)
    lo = lax.bitcast_convert_type(lax.shift_left(p, 16), F32)
    return hi, lo


def _route_plan(route, counts, tm):
    n_rows = route.shape[0]
    n_slots = TOP_K * n_rows
    picks = route[:, :2 * TOP_K + 2].astype(jnp.int32)
    idx1, idx2, rank1, rank2 = picks[:, 0], picks[:, 1], picks[:, 4], picks[:, 5]
    offs = jnp.concatenate([jnp.zeros((1,), jnp.int32), jnp.cumsum(counts[0, :N_EXPERTS].astype(jnp.int32))])
    slot_a = offs[idx1] + rank1
    slot_b = offs[idx2] + rank2
    n_tiles = n_slots // tm
    t_start = jnp.arange(n_tiles, dtype=jnp.int32) * tm
    e_first = jnp.searchsorted(offs[1:], t_start, side="right").astype(jnp.int32)
    base_hi = jnp.minimum(t_start + tm, offs[e_first + 1])
    e_next = jnp.arange(1, N_EXPERTS, dtype=jnp.int32)
    start = offs[1:N_EXPERTS]
    x_tile = jnp.minimum(start // tm, n_tiles - 1)
    x_hi = jnp.where(start % tm != 0, jnp.minimum(offs[2:], (x_tile + 1) * tm), start)
    tiles = jnp.concatenate([t_start // tm, x_tile])
    experts = jnp.concatenate([e_first, e_next])
    lo = jnp.concatenate([t_start, start])
    hi = jnp.concatenate([base_hi, x_hi])
    order = jnp.argsort(tiles * (2 * N_EXPERTS) + experts)
    tiles, experts, lo, hi = tiles[order], experts[order], lo[order], hi[order]
    change = tiles[1:] != tiles[:-1]
    one = jnp.ones((1,), bool)
    first = jnp.concatenate([one, change]).astype(jnp.int32)
    last = jnp.concatenate([change, one]).astype(jnp.int32)
    return slot_a, slot_b, (tiles, experts, lo, hi, first, last)


def _scatter_rows(rows, idx_a, idx_b):
    n_rows, width = rows.shape
    workers = SC_CORES * SC_SUBCORES
    per_worker = n_rows // workers
    assert n_rows % (workers * SC_GATHER_ROWS) == 0
    mesh = plsc.VectorSubcoreMesh(core_axis_name="c", subcore_axis_name="s")

    @functools.partial(
        pl.kernel, mesh=mesh,
        out_type=jax.ShapeDtypeStruct((TOP_K * n_rows, width), rows.dtype),
        scratch_types=[pltpu.VMEM((TOP_K, SC_GATHER_ROWS), jnp.int32),
                       pltpu.VMEM((SC_GATHER_ROWS, width), rows.dtype),
                       pltpu.SemaphoreType.DMA],
        name="scatter_rows")
    def scatter(rows_hbm, idx_a_hbm, idx_b_hbm, out_hbm, idx_v, rows_v, sem):
        base = (lax.axis_index("s") * SC_CORES + lax.axis_index("c")) * per_worker

        @pl.loop(0, per_worker // SC_GATHER_ROWS)
        def _(i):
            off = pl.multiple_of(base + i * SC_GATHER_ROWS, SC_GATHER_ROWS)
            pltpu.sync_copy(idx_a_hbm.at[pl.ds(off, SC_GATHER_ROWS)], idx_v.at[0])
            pltpu.sync_copy(idx_b_hbm.at[pl.ds(off, SC_GATHER_ROWS)], idx_v.at[1])
            pltpu.sync_copy(rows_hbm.at[pl.ds(off, SC_GATHER_ROWS)], rows_v)
            pltpu.async_copy(rows_v, out_hbm.at[idx_v.at[0]], sem).wait()
            pltpu.async_copy(rows_v, out_hbm.at[idx_v.at[1]], sem).wait()

    return scatter(rows, idx_a, idx_b)


def _gather_rows(table, idx):
    n_idx = idx.shape[0]
    width = table.shape[1]
    workers = SC_CORES * SC_SUBCORES
    per_worker = n_idx // workers
    assert n_idx % (workers * SC_GATHER_ROWS) == 0
    mesh = plsc.VectorSubcoreMesh(core_axis_name="c", subcore_axis_name="s")

    @functools.partial(
        pl.kernel, mesh=mesh,
        out_type=jax.ShapeDtypeStruct((n_idx, width), table.dtype),
        scratch_types=[pltpu.VMEM((SC_GATHER_ROWS,), jnp.int32),
                       pltpu.VMEM((SC_GATHER_ROWS, width), table.dtype),
                       pltpu.SemaphoreType.DMA],
        name="gather_rows")
    def gather(table_hbm, idx_hbm, out_hbm, idx_v, rows_v, sem):
        base = (lax.axis_index("s") * SC_CORES + lax.axis_index("c")) * per_worker

        @pl.loop(0, per_worker // SC_GATHER_ROWS)
        def _(i):
            off = pl.multiple_of(base + i * SC_GATHER_ROWS, SC_GATHER_ROWS)
            pltpu.sync_copy(idx_hbm.at[pl.ds(off, SC_GATHER_ROWS)], idx_v)
            pltpu.async_copy(table_hbm.at[idx_v], rows_v, sem).wait()
            pltpu.sync_copy(rows_v, out_hbm.at[pl.ds(off, SC_GATHER_ROWS)])

    return gather(table, idx)


def _experts_kernel(tile_ref, exp_ref, lo_ref, hi_ref, first_ref, last_ref,
                    x_ref, w1_ref, w3_ref, w2_ref, o_ref, acc_ref):
    i = pl.program_id(0)
    tm, half = x_ref.shape
    f = w1_ref.shape[2]
    fc = FFN_CHUNK_COLS

    @pl.when(first_ref[i] == 1)
    def _():
        acc_ref[...] = jnp.zeros_like(acc_ref)

    lo, hi = lo_ref[i], hi_ref[i]

    @pl.when(hi > lo)
    def _():
        x_hi, x_lo = _unpack_bf16_pairs(x_ref[...])
        x_hi, x_lo = x_hi.astype(BF16), x_lo.astype(BF16)
        y = None
        for c in range(f // fc):
            cols = slice(c * fc, (c + 1) * fc)
            a = _dot(x_hi, w1_ref[0, :half, cols]) + _dot(x_lo, w1_ref[0, half:, cols])
            b = _dot(x_hi, w3_ref[0, :half, cols]) + _dot(x_lo, w3_ref[0, half:, cols])
            part = _dot((a * _sigmoid(a) * b).astype(BF16), w2_ref[0, cols, :])
            y = part if y is None else y + part
        rows = tile_ref[i] * tm + lax.broadcasted_iota(jnp.int32, (tm, 1), 0)
        keep = jnp.logical_and(rows >= lo, rows < hi)
        acc_ref[...] += jnp.where(keep, y, 0.0)

    @pl.when(last_ref[i] == 1)
    def _():
        o_ref[...] = _pack_bf16_pairs(acc_ref[...])


def _experts(xs_sorted, items, w1, w3, w2, moe_index):
    n_slots, half = xs_sorted.shape
    _, n_exp, d, f = w1.shape
    tm = MOE_ROW_TILE
    assert f % FFN_CHUNK_COLS == 0
    tile_map = lambda i, tiles, *_: (tiles[i], 0)
    exp_map = lambda i, tiles, experts, *_: (moe_index, experts[i], 0, 0)
    grid_spec = pltpu.PrefetchScalarGridSpec(
        num_scalar_prefetch=len(items),
        grid=(items[0].shape[0],),
        in_specs=[pl.BlockSpec((tm, half), tile_map),
                  pl.BlockSpec((None, 1, d, f), exp_map), pl.BlockSpec((None, 1, d, f), exp_map),
                  pl.BlockSpec((None, 1, f, d), exp_map)],
        out_specs=pl.BlockSpec((tm, half), tile_map),
        scratch_shapes=[pltpu.VMEM((tm, d), F32)],
    )
    return pl.pallas_call(
        _experts_kernel,
        grid_spec=grid_spec,
        out_shape=jax.ShapeDtypeStruct((n_slots, half), jnp.int32),
        compiler_params=_params(1),
        name="experts",
    )(*items, xs_sorted, w1, w3, w2)


def _combine_kernel(x1_ref, ya_ref, yb_ref, route_ref, mod_ref, fg_ref, o_ref, *, final):
    half = ya_ref.shape[1]
    route = route_ref[...]
    wa, wb = route[:, 2:3], route[:, 3:4]
    a_hi, a_lo = _unpack_bf16_pairs(ya_ref[...])
    b_hi, b_lo = _unpack_bf16_pairs(yb_ref[...])
    gate = mod_ref[0][5:6]
    out_hi = x1_ref[:, :half] + gate[:, :half] * (wa * a_hi + wb * b_hi)
    out_lo = x1_ref[:, half:] + gate[:, half:] * (wa * a_lo + wb * b_lo)
    if final:
        total = jnp.sum(out_hi * out_hi, axis=-1, keepdims=True) + jnp.sum(out_lo * out_lo, axis=-1, keepdims=True)
        scale = lax.rsqrt(total / (2 * half) + EPS)
        out_hi = out_hi * scale * fg_ref[:, :half]
        out_lo = out_lo * scale * fg_ref[:, half:]
    o_ref[:, :half] = out_hi
    o_ref[:, half:] = out_lo


def _combine(x1, y_pairs, route, mod, final_g, seg, final):
    n_rows, d = x1.shape
    tm = seg["tm"]
    row, mod_map = _row_maps(seg, tm)
    second = n_rows // tm
    return pl.pallas_call(
        functools.partial(_combine_kernel, final=final),
        grid=(n_rows // tm,),
        in_specs=[pl.BlockSpec((tm, d), row), pl.BlockSpec((tm, d // 2), row),
                  pl.BlockSpec((tm, d // 2), lambda i: (second + i, 0)),
                  pl.BlockSpec((tm, LANES), row), pl.BlockSpec((1, 6, d), mod_map),
                  pl.BlockSpec((1, d), lambda i: (0, 0))],
        out_specs=pl.BlockSpec((tm, d), row),
        out_shape=jax.ShapeDtypeStruct((n_rows, d), F32),
        compiler_params=_params(1),
        name="moe_combine",
    )(x1, y_pairs, y_pairs, route, mod, final_g)


def _moe_layer(xs, mix, mod, g2, wo, layer, wr, br, w1, w3, w2, moe_index, final_g, seg, n_rows, final):
    x1, h_pairs, route, counts = _router(xs, mix, mod, g2, wo, layer, wr, br, seg, n_rows)
    slot_a, slot_b, items = _route_plan(route, counts, MOE_ROW_TILE)
    y_sorted = _experts(_scatter_rows(h_pairs, slot_a, slot_b), items, w1, w3, w2, moe_index)
    y_pairs = _gather_rows(y_sorted, jnp.concatenate([slot_a, slot_b]))
    return _combine(x1, y_pairs, route, mod, final_g, seg, final)


def kernel(x, c, ctx, c_ctx, norm1_g, norm2_g, w_mod, b_mod, w_in, conv_w, conv_b, b_gates, attn_sink,
           g_att, g_ml, w_out, ffn_w1, ffn_w3, ffn_w2, w_router, b_router, exp_w1, exp_w3, exp_w2,
           final_g):
    b_, s_, d = x.shape
    lc = ctx.shape[1]
    depth = w_in.shape[0]
    n_lat, n_ctx = b_ * s_, b_ * lc
    tm = min(ROW_TILE, s_)
    assert s_ % tm == 0 and n_ctx % tm == 0
    assert (TOP_K * n_lat) % MOE_ROW_TILE == 0 and (TOP_K * n_ctx) % MOE_ROW_TILE == 0
    assert s_ % ML_CHUNK == 0 and lc % ML_CHUNK == 0 and n_lat % lc == 0 and b_ < MOD_ROWS
    seg = dict(B=b_, S=s_, Lc=lc, n_lat=n_lat, tm=tm)

    cond = jnp.zeros((MOD_ROWS, d), F32).at[:b_].set(c).at[b_].set(c_ctx)
    mods = _modulation(cond, w_mod, b_mod).reshape(depth, MOD_ROWS, 6, d)
    rope = _rope_tables(s_, tm)
    xs = jnp.concatenate([x.reshape(n_lat, d), ctx.reshape(n_ctx, d)], axis=0)
    final_row = final_g.reshape(1, d)

    w_in_b, w_out_b = w_in.astype(BF16), w_out.astype(BF16)
    ffn_b = (ffn_w1.astype(BF16), ffn_w3.astype(BF16), ffn_w2.astype(BF16))
    exp_b = (exp_w1.astype(BF16), exp_w3.astype(BF16), exp_w2.astype(BF16))

    for layer in range(depth):
        last = layer == depth - 1
        w_gates = w_in[layer][:, MAIN_WIDTH:]
        w_gc = jnp.pad(w_gates, ((0, 0), (0, LANES - ML_GATES))).astype(BF16)
        b_gc = jnp.pad(b_gates[layer], (0, LANES - ML_GATES)).reshape(1, LANES)
        qa, ka, va, qm, km, vm, om, gc, gr = _input_projection(
            xs, mods[layer], norm1_g[layer].reshape(1, d), w_in_b, layer, w_gc,
            w_gates.T.astype(BF16), b_gc, b_gates[layer].reshape(ML_GATES, 1),
            conv_w[layer], conv_b[layer].reshape(1, -1), rope, seg)
        att = _attention(qa, ka, va, attn_sink[layer], g_att[layer].reshape(1, ATT_WIDTH), seg, not last)
        hf, hb = _mlstm(qm, km, vm, gc, gr, seg)
        mix = (att, hf, hb, om, g_ml[layer].reshape(1, ML_WIDTH))
        n_rows = n_lat if last else n_lat + n_ctx
        g2 = norm2_g[layer].reshape(1, d)
        i = layer // 2
        if layer % 2 == 0:
            xs = _dense_layer(xs, mix, mods[layer], g2, w_out_b, layer, *ffn_b, i, final_row, seg, n_rows, last)
        else:
            wr = jnp.pad(w_router[i], ((0, 0), (0, LANES - N_EXPERTS)))
            wr_hi = wr.astype(BF16)
            wr = jnp.concatenate([wr_hi, (wr - wr_hi.astype(F32)).astype(BF16)], axis=1)
            br = jnp.pad(b_router[i], (0, LANES - N_EXPERTS)).reshape(1, LANES)
            xs = _moe_layer(xs, mix, mods[layer], g2, w_out_b, layer, wr, br, *exp_b, i, final_row, seg,
                            n_rows, last)
    return xs[:n_lat].reshape(b_, s_, d)
```

```python
import functools

import jax
import jax.numpy as jnp
import numpy as np
from jax import lax
from jax.experimental import pallas as pl
from jax.experimental.pallas import tpu as pltpu
from jax.experimental.pallas import tpu_sc as plsc

F32 = jnp.float32
BF16 = jnp.bfloat16

GRID_W = 64
ATT_HEADS = 8
ATT_KV_HEADS = 2
ATT_HEAD_DIM = 64
ATT_GROUP = ATT_HEADS // ATT_KV_HEADS
WINDOW = 128
ATT_BLOCK = 128
ROPE_THETA = 10000.0
ML_HEADS = 4
ML_QK_DIM = 64
ML_V_DIM = 128
ML_CONV = 5
GATE_CAP = 15.0
ATT_WIDTH = ATT_HEADS * ATT_HEAD_DIM
ATT_KV_WIDTH = ATT_KV_HEADS * ATT_HEAD_DIM
ML_QK_WIDTH = ML_HEADS * ML_QK_DIM
ML_WIDTH = ML_HEADS * ML_V_DIM
ML_GATES = 4 * ML_HEADS
MAIN_WIDTH = ATT_WIDTH + 2 * ATT_KV_WIDTH + 2 * ML_QK_WIDTH + 2 * ML_WIDTH
N_EXPERTS = 8
TOP_K = 2
EPS = 1e-6

LANES = 128
SUBLANES = 8
VMEM_LIMIT = 56 * 1024 * 1024
NEG = -1e30
SC_CORES = 2
SC_SUBCORES = 16
SC_GATHER_ROWS = 64

ROW_TILE = 512
MOE_ROW_TILE = 512
FFN_CHUNK_COLS = 256
ATT_STEP_BLOCKS = 8
ML_CHUNK = 128
ML_STEP_CHUNKS = 2
CONV_HALO = SUBLANES
MOD_ROWS = 16
MOD_COL_TILE = 1536


def _dot(a, b):
    return jnp.dot(a, b, preferred_element_type=F32)


def _dot_nt(a, b):
    return lax.dot_general(a, b, (((1,), (1,)), ((), ())), preferred_element_type=F32)


def _dot_tn(a, b):
    return lax.dot_general(a, b, (((0,), (0,)), ((), ())), preferred_element_type=F32)


def _dot_f32(a, b):
    return jnp.dot(a, b, preferred_element_type=F32, precision=lax.Precision.HIGHEST)


def _sigmoid(x):
    return 1.0 / (1.0 + jnp.exp(-x))


def _rms(x, g):
    return x * lax.rsqrt(jnp.mean(x * x, axis=-1, keepdims=True) + EPS) * g


def _adaln(x, g, shift, scale):
    return _rms(x, g) * (1.0 + scale) + shift


def _params(n_axes):
    return pltpu.CompilerParams(dimension_semantics=("arbitrary",) * n_axes,
                                vmem_limit_bytes=VMEM_LIMIT)


def _resident(shape):
    zeros = (0,) * len(shape)
    return pl.BlockSpec(shape, lambda *_: zeros, pipeline_mode=pl.Buffered(1))


def _layer_resident(stacked, layer, block=None):
    block = tuple(stacked.shape[1:]) if block is None else block
    index = (layer,) + (0,) * len(block)
    return pl.BlockSpec((None,) + block, lambda *_: index, pipeline_mode=pl.Buffered(1))


def _mod_kernel(c_ref, w_ref, b_ref, o_ref):
    c = c_ref[...]
    a = (c * _sigmoid(c)).astype(BF16)
    o_ref[0] = _dot(a, w_ref[0].astype(BF16)) + b_ref[0]


def _modulation(cond, w_mod, b_mod):
    depth, d, width = w_mod.shape
    tn = MOD_COL_TILE if width % MOD_COL_TILE == 0 else width
    return pl.pallas_call(
        _mod_kernel,
        grid=(depth, width // tn),
        in_specs=[pl.BlockSpec((MOD_ROWS, d), lambda l, j: (0, 0)),
                  pl.BlockSpec((1, d, tn), lambda l, j: (l, 0, j)),
                  pl.BlockSpec((1, 1, tn), lambda l, j: (l, 0, j))],
        out_specs=pl.BlockSpec((1, MOD_ROWS, tn), lambda l, j: (l, 0, j)),
        out_shape=jax.ShapeDtypeStruct((depth, MOD_ROWS, width), F32),
        compiler_params=_params(2),
        name="modulation",
    )(cond, w_mod, b_mod.reshape(depth, 1, width))


def _gate_act(u, is_forget):
    g = GATE_CAP * jnp.tanh(u / GATE_CAP)
    log_sig = jnp.minimum(g, 0.0) - jnp.log1p(jnp.exp(-jnp.abs(g)))
    return jnp.where(is_forget, log_sig, g)


def _conv_silu(xe, cw, cb, rows):
    n_ext = xe.shape[0]
    mid = ML_CONV // 2
    y = cb + cw[mid:mid + 1] * xe[CONV_HALO:CONV_HALO + rows]
    for tap in range(ML_CONV):
        if tap != mid:
            y = y + cw[tap:tap + 1] * pltpu.roll(xe, (mid - tap) % n_ext, 0)[CONV_HALO:CONV_HALO + rows]
    return y * _sigmoid(y)


def _inproj_kernel(x_ref, xp_ref, xn_ref, mod_ref, g_ref, w_ref, wgc_ref, wgr_ref, bgc_ref, bgr_ref,
                   cw_ref, cb_ref, cos_ref, sa_ref, sb_ref,
                   qa_ref, ka_ref, va_ref, qm_ref, km_ref, vm_ref, om_ref, gc_ref, gr_ref,
                   *, n_lat_tiles, seq_lat, seq_ctx):
    mod = mod_ref[0]
    tm = x_ref.shape[0]
    seg = min(tm, seq_ctx)
    n_seg = tm // seg
    norm_gain = g_ref[...]
    normed = lambda rows: _adaln(rows, norm_gain, mod[0:1], mod[1:2]).astype(BF16)
    quarter = ATT_HEAD_DIM // 4

    def rope(u, rows):
        return (u * cos_ref[rows, :] + pltpu.roll(u, LANES - quarter, 1) * sa_ref[rows, :]
                + pltpu.roll(u, quarter, 1) * sb_ref[rows, :])

    c_kv = ATT_WIDTH
    c_qk = c_kv + 2 * ATT_KV_WIDTH
    c_vm = c_qk + 2 * ML_QK_WIDTH
    c_om = c_vm + ML_WIDTH
    w_qk = w_ref[:, c_qk:c_vm]
    i = pl.program_id(0)
    seq_len = jnp.where(i < n_lat_tiles, seq_lat, seq_ctx)
    cw, cb = cw_ref[...], cb_ref[...]
    hx = [normed(x_ref[j * seg:(j + 1) * seg, :]) for j in range(n_seg)]
    qk = [_dot(h, w_qk) for h in hx]
    qk_prev = _dot(normed(xp_ref[...]), w_qk)
    qk_next = _dot(normed(xn_ref[...]), w_qk)
    for j in range(n_seg):
        rows = slice(j * seg, (j + 1) * seg)
        first_row = i * tm + j * seg
        has_prev = (lax.rem(first_row, seq_len) != 0).astype(F32)
        has_next = (lax.rem(first_row + seg, seq_len) != 0).astype(F32)
        prev = qk_prev if j == 0 else qk[j - 1][seg - CONV_HALO:]
        nxt = qk_next if j == n_seg - 1 else qk[j + 1][:CONV_HALO]
        xe = jnp.concatenate([prev * has_prev, qk[j], nxt * has_next], axis=0)
        y = _conv_silu(xe, cw, cb, seg)
        qm_ref[rows, :] = (y[:, :ML_QK_WIDTH] * (ML_QK_DIM ** -0.5)).astype(BF16)
        km_ref[rows, :] = y[:, ML_QK_WIDTH:].astype(BF16)
        q = _dot(hx[j], w_ref[:, :c_kv])
        for c in range(ATT_WIDTH // LANES):
            sl = slice(c * LANES, (c + 1) * LANES)
            qa_ref[rows, sl] = (rope(q[:, sl], rows) * (ATT_HEAD_DIM ** -0.5)).astype(BF16)
        kv = _dot(hx[j], w_ref[:, c_kv:c_qk])
        ka_ref[rows, :] = rope(kv[:, :ATT_KV_WIDTH], rows).astype(BF16)
        va_ref[rows, :] = kv[:, ATT_KV_WIDTH:].astype(BF16)
        vm_ref[rows, :] = _dot(hx[j], w_ref[:, c_vm:c_om]).astype(BF16)
        om_ref[rows, :] = _dot(hx[j], w_ref[:, c_om:c_om + ML_WIDTH]).astype(BF16)
        gc = _dot(hx[j], wgc_ref[...]) + bgc_ref[...]
        lane = lax.broadcasted_iota(jnp.int32, gc.shape, 1)
        gc_ref[rows, :] = _gate_act(gc, (lane // ML_HEADS) % 2 == 1)
        gr = _dot_nt(wgr_ref[...], hx[j]) + bgr_ref[...]
        sub = lax.broadcasted_iota(jnp.int32, gr.shape, 0)
        gr_ref[:, rows] = _gate_act(gr, (sub // ML_HEADS) % 2 == 1)


def _input_projection(xs, mod, g1, w_in, layer, w_gc, w_gr, b_gc, b_gr, conv_w, conv_b, rope, seg):
    n, d = xs.shape
    tm = seg["tm"]
    nlat = seg["n_lat"] // tm
    s_tiles = seg["S"] // tm
    halos_per_tile = tm // CONV_HALO
    row = lambda i: (i, 0)
    prev = lambda i: (jnp.maximum(i * halos_per_tile - 1, 0), 0)
    nxt = lambda i: (jnp.minimum((i + 1) * halos_per_tile, n // CONV_HALO - 1), 0)
    mod_map = lambda i: (jnp.where(i < nlat, i // s_tiles, seg["B"]), 0, 0)
    rope_map = lambda i: (jnp.where(i < nlat, i % s_tiles, s_tiles), 0)
    widths = [(ATT_WIDTH, BF16), (ATT_KV_WIDTH, BF16), (ATT_KV_WIDTH, BF16), (ML_QK_WIDTH, BF16),
              (ML_QK_WIDTH, BF16), (ML_WIDTH, BF16), (ML_WIDTH, BF16), (LANES, F32)]
    out_shape = [jax.ShapeDtypeStruct((n, w), t) for w, t in widths]
    out_specs = [pl.BlockSpec((tm, w), row) for w, _ in widths]
    out_shape.append(jax.ShapeDtypeStruct((ML_GATES, n), F32))
    out_specs.append(pl.BlockSpec((ML_GATES, tm), lambda i: (0, i)))
    return pl.pallas_call(
        functools.partial(_inproj_kernel, n_lat_tiles=nlat, seq_lat=seg["S"], seq_ctx=seg["Lc"]),
        grid=(n // tm,),
        in_specs=[pl.BlockSpec((tm, d), row),
                  pl.BlockSpec((CONV_HALO, d), prev),
                  pl.BlockSpec((CONV_HALO, d), nxt),
                  pl.BlockSpec((1, 6, d), mod_map),
                  _resident((1, d)),
                  _layer_resident(w_in, layer, (d, MAIN_WIDTH)), _resident(w_gc.shape), _resident(w_gr.shape),
                  _resident(b_gc.shape), _resident(b_gr.shape),
                  _resident(conv_w.shape), _resident(conv_b.shape),
                  pl.BlockSpec((tm, LANES), rope_map),
                  pl.BlockSpec((tm, LANES), rope_map),
                  pl.BlockSpec((tm, LANES), rope_map)],
        out_specs=out_specs,
        out_shape=out_shape,
        compiler_params=_params(1),
        name="input_projection",
    )(xs, xs, xs, mod, g1, w_in, w_gc, w_gr, b_gc, b_gr, conv_w, conv_b, *rope)


def _rope_tables(s, tm):
    quarter = ATT_HEAD_DIM // 4
    t = jnp.arange(s)
    row = (t // GRID_W).astype(F32)
    col = (t % GRID_W).astype(F32)
    inv = ROPE_THETA ** (-jnp.arange(quarter, dtype=F32) / quarter)
    ang_r = row[:, None] * inv[None, :]
    ang_c = col[:, None] * inv[None, :]
    zero = jnp.zeros_like(ang_r)
    cos = jnp.concatenate([jnp.cos(ang_r)] * 2 + [jnp.cos(ang_c)] * 2, axis=1)
    sin_up = jnp.concatenate([-jnp.sin(ang_r), zero, -jnp.sin(ang_c), zero], axis=1)
    sin_dn = jnp.concatenate([zero, jnp.sin(ang_r), zero, jnp.sin(ang_c)], axis=1)
    reps = LANES // ATT_HEAD_DIM
    ident = [jnp.ones((tm, LANES), F32), jnp.zeros((tm, LANES), F32), jnp.zeros((tm, LANES), F32)]
    return tuple(jnp.concatenate([jnp.tile(a, (1, reps)), i], axis=0)
                 for a, i in zip((cos, sin_up, sin_dn), ident))


def _attn_kernel(sink_ref, *refs, window, blocks):
    blk = ATT_BLOCK
    if window:
        q_ref, kp_ref, kc_ref, kn_ref, vp_ref, vc_ref, vn_ref, kx_ref, vx_ref, g_ref, o_ref = refs
        j = pl.program_id(1)
        k_own, v_own = kc_ref[...], vc_ref[...]
        k_blocks = [kp_ref[...]] + [k_own[t * blk:(t + 1) * blk] for t in range(blocks)] + [kn_ref[...]]
        v_blocks = [vp_ref[...]] + [v_own[t * blk:(t + 1) * blk] for t in range(blocks)] + [vn_ref[...]]
    else:
        q_ref, kx_ref, vx_ref, g_ref, _, o_ref = refs
    for t in range(blocks):
        if window:
            has_prev = j > 0 if t == 0 else True
            has_next = j < pl.num_programs(1) - 1 if t == blocks - 1 else True
            win = (k_blocks[t:t + 3], v_blocks[t:t + 3], has_prev, has_next)
        else:
            win = None
        att = _attend_block(sink_ref, q_ref[t * blk:(t + 1) * blk, :], win, kx_ref[...], vx_ref[...])
        o_ref[t * blk:(t + 1) * blk, :] = _rms(att, g_ref[...]).astype(o_ref.dtype)


def _attend_block(sink_ref, q, win, k_ctx, v_ctx):
    blk = ATT_BLOCK
    if win is None:
        k_all, v_all, bias = k_ctx, v_ctx, None
    else:
        k_win, v_win, has_prev, has_next = win
        rows = lax.broadcasted_iota(jnp.int32, (blk, blk), 0)
        cols = lax.broadcasted_iota(jnp.int32, (blk, blk), 1)
        ok_p = jnp.logical_and(cols >= rows, has_prev)
        ok_n = jnp.logical_and(cols <= rows, has_next)
        bias = jnp.concatenate([jnp.where(ok_p, 0.0, NEG), jnp.zeros((blk, blk), F32),
                                jnp.where(ok_n, 0.0, NEG), jnp.zeros((blk, k_ctx.shape[0]), F32)], axis=1)
        k_all = jnp.concatenate(list(k_win) + [k_ctx], axis=0)
        v_all = jnp.concatenate(list(v_win) + [v_ctx], axis=0)
    dh = ATT_HEAD_DIM
    outs = []
    for h in range(ATT_KV_HEADS):
        k_h = k_all[:, h * dh:(h + 1) * dh]
        v_h = v_all[:, h * dh:(h + 1) * dh]
        q_h = jnp.concatenate([q[:, (h * ATT_GROUP + g) * dh:(h * ATT_GROUP + g + 1) * dh]
                               for g in range(ATT_GROUP)], axis=0)
        s_all = _dot_nt(q_h, k_h)
        p_parts, inv_parts = [], []
        for g in range(ATT_GROUP):
            sink = sink_ref[h * ATT_GROUP + g]
            s = s_all[g * blk:(g + 1) * blk]
            if bias is not None:
                s = s + bias
            m = jnp.maximum(jnp.max(s, axis=-1, keepdims=True), sink)
            p = jnp.exp(s - m)
            denom = jnp.sum(p, axis=-1, keepdims=True) + jnp.exp(sink - m)
            p_parts.append(p.astype(BF16))
            inv_parts.append(1.0 / denom)
        o = _dot(jnp.concatenate(p_parts, axis=0), v_h)
        for g in range(ATT_GROUP):
            outs.append(o[g * blk:(g + 1) * blk] * inv_parts[g])
    return jnp.concatenate(outs, axis=1)


def _attention(qa, ka, va, sink, g_att, seg, with_ctx):
    n = qa.shape[0]
    b_, s_, lc = seg["B"], seg["S"], seg["Lc"]
    blk = ATT_BLOCK
    blocks_per_seq = s_ // blk
    ctx_map = lambda b, j, *_: (b_ * s_ // lc + b, 0)
    ctx_specs = [pl.BlockSpec((lc, ATT_KV_WIDTH), ctx_map)] * 2
    g_spec = pl.BlockSpec((1, ATT_WIDTH), lambda b, j, *_: (0, 0))

    def call(window, blocks, steps, q_map, in_specs, args, aliases):
        grid_spec = pltpu.PrefetchScalarGridSpec(
            num_scalar_prefetch=1,
            grid=(b_, steps),
            in_specs=[pl.BlockSpec((blocks * blk, ATT_WIDTH), q_map)] + in_specs,
            out_specs=pl.BlockSpec((blocks * blk, ATT_WIDTH), q_map))
        return pl.pallas_call(
            functools.partial(_attn_kernel, window=window, blocks=blocks),
            grid_spec=grid_spec,
            out_shape=jax.ShapeDtypeStruct((n, ATT_WIDTH), BF16),
            input_output_aliases=aliases,
            compiler_params=_params(2),
            name="window_attention" if window else "context_attention",
        )(sink, qa, *args)

    qb = min(ATT_STEP_BLOCKS, blocks_per_seq)
    assert blocks_per_seq % qb == 0
    nqs = blocks_per_seq // qb

    def edge_map(off):
        return lambda b, j, *_: (b * blocks_per_seq + jnp.clip(j * qb + off, 0, blocks_per_seq - 1), 0)

    kv_specs = [pl.BlockSpec((blk, ATT_KV_WIDTH), edge_map(-1)),
                pl.BlockSpec((qb * blk, ATT_KV_WIDTH), lambda b, j, *_: (b * nqs + j, 0)),
                pl.BlockSpec((blk, ATT_KV_WIDTH), edge_map(qb))]
    att = call(True, qb, nqs, lambda b, j, *_: (b * nqs + j, 0), kv_specs + kv_specs + ctx_specs + [g_spec],
               (ka, ka, ka, va, va, va, ka, va, g_att), {})
    if with_ctx:
        cb = min(ATT_STEP_BLOCKS, lc // blk)
        assert (lc // blk) % cb == 0 and (b_ * s_) % (cb * blk) == 0
        ncs = lc // (cb * blk)
        first = b_ * s_ // (cb * blk)
        att = call(False, cb, ncs, lambda b, j, *_: (first + b * ncs + j, 0),
                   ctx_specs + [g_spec, pl.BlockSpec(memory_space=pl.ANY)], (ka, va, g_att, att), {5: 0})
    return att


def _mlstm_direction(q_ref, k_ref, v_ref, gc_ref, gr_ref, out_ref, state_ref, tok, *, reverse):
    chunk = ML_CHUNK
    rows = lax.broadcasted_iota(jnp.int32, (chunk, chunk), 0)
    cols = lax.broadcasted_iota(jnp.int32, (chunk, chunk), 1)
    lower = rows >= cols
    upper = rows <= cols
    seen = upper if reverse else lower
    gc = gc_ref[tok, :]
    gr = gr_ref[:, tok]
    b_col = _dot_f32(seen.astype(F32), gc)
    b_row = _dot_f32(gr, (lower if reverse else upper).astype(F32))
    b_end = jnp.sum(gc, axis=0, keepdims=True)
    base = 2 * ML_HEADS if reverse else 0
    pair_width = 2 * ML_QK_DIM
    lane = lax.broadcasted_iota(jnp.int32, (chunk, pair_width), 1)
    state_row = lax.broadcasted_iota(jnp.int32, (pair_width, 1), 0)
    ones = jnp.ones((chunk, ML_V_DIM), BF16)

    for pair in range(ML_HEADS // 2):
        q_pair = q_ref[tok, pair * pair_width:(pair + 1) * pair_width]
        k_pair = k_ref[tok, pair * pair_width:(pair + 1) * pair_width]
        state = state_ref[pair]
        state_bf = state.astype(BF16)
        update = None
        decays = []
        for sub in range(2):
            h = 2 * pair + sub
            i_idx = base + h
            f_idx = base + ML_HEADS + h
            own = (lane >= ML_QK_DIM) if sub else (lane < ML_QK_DIM)
            q_h = jnp.where(own, q_pair, jnp.zeros_like(q_pair))
            vx = jnp.concatenate([v_ref[tok, h * ML_V_DIM:(h + 1) * ML_V_DIM], ones], axis=1)
            bc = b_col[:, f_idx:f_idx + 1]
            d = bc - b_row[f_idx:f_idx + 1, :] + gr[i_idx:i_idx + 1, :]
            w = jnp.exp(jnp.where(seen, d, NEG))
            s = _dot_nt(q_h, k_pair) * w
            tot = _dot(s.astype(BF16), vx) + jnp.exp(bc) * _dot(q_h, state_bf)
            h_out = tot[:, :ML_V_DIM] / jnp.maximum(jnp.abs(tot[:, ML_V_DIM:]), 1.0)
            out_ref[tok, h * ML_V_DIM:(h + 1) * ML_V_DIM] = h_out.astype(out_ref.dtype)

            be = b_end[:, f_idx:f_idx + 1]
            kw = jnp.where(own, k_pair.astype(F32) * jnp.exp(be - bc + gc[:, i_idx:i_idx + 1]), 0.0)
            part = _dot_tn(kw.astype(BF16), vx)
            update = part if update is None else update + part
            decays.append(jnp.exp(be))
        decay = jnp.where(state_row < ML_QK_DIM, decays[0], decays[1])
        state_ref[pair] = decay * state + update


def _mlstm_kernel(qf_ref, kf_ref, vf_ref, gcf_ref, grf_ref, qb_ref, kb_ref, vb_ref, gcb_ref, grb_ref,
                  hf_ref, hb_ref, sf_ref, sb_ref):
    @pl.when(pl.program_id(1) == 0)
    def _():
        sf_ref[...] = jnp.zeros_like(sf_ref)
        sb_ref[...] = jnp.zeros_like(sb_ref)

    n_chunks = qf_ref.shape[0] // ML_CHUNK
    for t in range(n_chunks):
        fwd = slice(t * ML_CHUNK, (t + 1) * ML_CHUNK)
        bwd = slice((n_chunks - 1 - t) * ML_CHUNK, (n_chunks - t) * ML_CHUNK)
        _mlstm_direction(qf_ref, kf_ref, vf_ref, gcf_ref, grf_ref, hf_ref, sf_ref, fwd, reverse=False)
        _mlstm_direction(qb_ref, kb_ref, vb_ref, gcb_ref, grb_ref, hb_ref, sb_ref, bwd, reverse=True)


def _mlstm(qm, km, vm, gc, gr, seg):
    n = qm.shape[0]
    b_, s_, lc = seg["B"], seg["S"], seg["Lc"]
    chunk = ML_STEP_CHUNKS * ML_CHUNK
    assert s_ % chunk == 0 and lc % chunk == 0
    ncc, ncl = lc // chunk, s_ // chunk
    lat_chunks = b_ * ncl

    def blk(b, c, reverse):
        pos = jnp.where(c < ncc, c, c - ncc)
        if reverse:
            pos = jnp.where(c < ncc, ncc, ncl) - 1 - pos
        return jnp.where(c < ncc, lat_chunks + b * ncc + pos, b * ncl + pos)

    def specs(reverse):
        cur = lambda b, c: (blk(b, c, reverse), 0)
        return [pl.BlockSpec((chunk, ML_QK_WIDTH), cur), pl.BlockSpec((chunk, ML_QK_WIDTH), cur),
                pl.BlockSpec((chunk, ML_WIDTH), cur), pl.BlockSpec((chunk, LANES), cur),
                pl.BlockSpec((ML_GATES, chunk), lambda b, c: (0, blk(b, c, reverse)))]

    out = lambda reverse: pl.BlockSpec((chunk, ML_WIDTH), lambda b, c: (blk(b, c, reverse), 0))
    state = pltpu.VMEM((ML_HEADS // 2, 2 * ML_QK_DIM, 2 * ML_V_DIM), F32)
    return pl.pallas_call(
        _mlstm_kernel,
        grid=(b_, ncc + ncl),
        in_specs=specs(False) + specs(True),
        out_specs=[out(False), out(True)],
        out_shape=[jax.ShapeDtypeStruct((n, ML_WIDTH), BF16)] * 2,
        scratch_shapes=[state, state],
        compiler_params=_params(2),
        name="mlstm_scan",
    )(qm, km, vm, gc, gr, qm, km, vm, gc, gr)


def _swiglu(h, w1_ref, w3_ref, w2_ref):
    f = w1_ref.shape[1]
    fc = FFN_CHUNK_COLS
    y = None
    for c in range(f // fc):
        cols = slice(c * fc, (c + 1) * fc)
        a = _dot(h, w1_ref[:, cols])
        b = _dot(h, w3_ref[:, cols])
        part = _dot((a * _sigmoid(a) * b).astype(BF16), w2_ref[cols, :])
        y = part if y is None else y + part
    return y


def _mix_residual(x_ref, mix_refs, mod, wo_ref):
    att_ref, hf_ref, hb_ref, om_ref, gml_ref = mix_refs
    mix = _dot(att_ref[...], wo_ref[:ATT_WIDTH, :])
    for h in range(ML_HEADS):
        sl = slice(h * ML_V_DIM, (h + 1) * ML_V_DIM)
        tot = hf_ref[:, sl].astype(F32) + hb_ref[:, sl].astype(F32)
        ml = _rms(tot, gml_ref[:, sl]) * _sigmoid(om_ref[:, sl].astype(F32))
        mix = mix + _dot(ml.astype(BF16), wo_ref[ATT_WIDTH + h * ML_V_DIM:ATT_WIDTH + (h + 1) * ML_V_DIM, :])
    return x_ref[...] + mod[2:3] * mix


def _mix_specs(tm, row):
    return [pl.BlockSpec((tm, ATT_WIDTH), row)] + [pl.BlockSpec((tm, ML_WIDTH), row)] * 3 \
        + [_resident((1, ML_WIDTH))]


def _dense_layer_kernel(x_ref, att_ref, hf_ref, hb_ref, om_ref, gml_ref, mod_ref, g_ref, wo_ref,
                        w1_ref, w3_ref, w2_ref, fg_ref, *rest, final):
    n_cast = (len(rest) - 1) // 2
    o_ref = rest[n_cast]
    for src, dst in zip(rest[:n_cast], rest[n_cast + 1:]):
        dst[...] = src[...].astype(dst.dtype)
    mod = mod_ref[0]
    x1 = _mix_residual(x_ref, (att_ref, hf_ref, hb_ref, om_ref, gml_ref), mod, wo_ref)
    hx = _adaln(x1, g_ref[...], mod[3:4], mod[4:5]).astype(BF16)
    out = x1 + mod[5:6] * _swiglu(hx, w1_ref, w3_ref, w2_ref)
    if final:
        out = _rms(out, fg_ref[...])
    o_ref[...] = out


def _row_maps(seg, tm):
    nlat = seg["n_lat"] // tm
    s_tiles = seg["S"] // tm
    return (lambda i, *_: (i, 0)), (lambda i, *_: (jnp.where(i < nlat, i // s_tiles, seg["B"]), 0, 0))


def _cast_slabs(stacked, index, n_steps):
    rows_per_layer = int(np.prod(stacked.shape[1:-1]))
    width = stacked.shape[-1]
    slab = next(r for r in range(2 * SUBLANES, rows_per_layer + 1, 2 * SUBLANES)
                if rows_per_layer % r == 0 and rows_per_layer // r <= n_steps)
    n_slabs = rows_per_layer // slab
    src = pl.BlockSpec((slab, width), lambda i: (index * n_slabs + jnp.minimum(i, n_slabs - 1), 0))
    dst = pl.BlockSpec((slab, width), lambda i: (jnp.minimum(i, n_slabs - 1), 0))
    flat = stacked.reshape(-1, width)
    return src, dst, flat, jax.ShapeDtypeStruct((rows_per_layer, width), BF16)


def _dense_layer(xs, mix, mod, g2, wo, layer, w1, w3, w2, ffn_index, final_g, seg, n_rows, final, cast=None):
    d = xs.shape[1]
    tm = seg["tm"]
    row, mod_map = _row_maps(seg, tm)
    n_steps = n_rows // tm
    cast_specs = [_cast_slabs(p, cast[1], n_steps) for p in cast[0]] if cast else []
    outs = pl.pallas_call(
        functools.partial(_dense_layer_kernel, final=final),
        grid=(n_steps,),
        in_specs=[pl.BlockSpec((tm, d), row)] + _mix_specs(tm, row)
                 + [pl.BlockSpec((1, 6, d), mod_map),
                    _resident((1, d)), _layer_resident(wo, layer), _layer_resident(w1, ffn_index),
                    _layer_resident(w3, ffn_index), _layer_resident(w2, ffn_index), _resident((1, d))]
                 + [c[0] for c in cast_specs],
        out_specs=[pl.BlockSpec((tm, d), row)] + [c[1] for c in cast_specs],
        out_shape=[jax.ShapeDtypeStruct((n_rows, d), F32)] + [c[3] for c in cast_specs],
        compiler_params=_params(1),
        name="dense_layer",
    )(xs, *mix, mod, g2, wo, w1, w3, w2, final_g, *[c[2] for c in cast_specs])
    if not cast:
        return outs[0], None
    return outs[0], tuple(o.reshape(p.shape[1:]) for o, p in zip(outs[1:], cast[0]))


def _router_kernel(x_ref, att_ref, hf_ref, hb_ref, om_ref, gml_ref, mod_ref, g_ref, wo_ref, wr_ref, br_ref,
                   earlier_ref,
                   x1_ref, h_ref, route_ref, counts_ref, count_ref):
    mod = mod_ref[0]
    x1 = _mix_residual(x_ref, (att_ref, hf_ref, hb_ref, om_ref, gml_ref), mod, wo_ref)
    x1_ref[...] = x1
    hx = _adaln(x1, g_ref[...], mod[3:4], mod[4:5])
    h_ref[...] = _pack_bf16_pairs(hx)
    hx_hi = hx.astype(BF16)
    hx_lo = (hx - hx_hi.astype(F32)).astype(BF16)
    both = _dot(hx_hi, wr_ref[...])
    logits = both[:, :LANES] + both[:, LANES:] + _dot(hx_lo, wr_ref[:, :LANES]) + br_ref[...]
    lane = lax.broadcasted_iota(jnp.int32, logits.shape, 1)
    logits = jnp.where(lane < N_EXPERTS, logits, -jnp.inf)
    top1 = jnp.max(logits, axis=-1, keepdims=True)
    idx1 = jnp.min(jnp.where(logits == top1, lane, LANES), axis=-1, keepdims=True)
    rest = jnp.where(lane == idx1, -jnp.inf, logits)
    top2 = jnp.max(rest, axis=-1, keepdims=True)
    idx2 = jnp.min(jnp.where(rest == top2, lane, LANES), axis=-1, keepdims=True)
    e2 = jnp.exp(top2 - top1)
    w_first = 1.0 / (1.0 + e2)
    @pl.when(pl.program_id(0) == 0)
    def _():
        count_ref[...] = jnp.zeros_like(count_ref)

    hot1 = (lane == idx1).astype(F32)
    hot2 = (lane == idx2).astype(F32)
    hot = hot1 + hot2
    before = _dot(earlier_ref[...], hot.astype(BF16)) + count_ref[...]
    rank1 = jnp.sum(hot1 * before, axis=-1, keepdims=True)
    rank2 = jnp.sum(hot2 * before, axis=-1, keepdims=True)
    count_ref[...] += jnp.sum(hot, axis=0, keepdims=True)
    counts_ref[...] = count_ref[...]
    route = jnp.where(lane == 0, idx1.astype(F32), jnp.where(lane == 1, idx2.astype(F32), 0.0))
    route = jnp.where(lane == 2, w_first, jnp.where(lane == 3, e2 * w_first, route))
    route_ref[...] = jnp.where(lane == 4, rank1, jnp.where(lane == 5, rank2, route))


def _router(xs, mix, mod, g2, wo, layer, wr, br, seg, n_rows):
    d = xs.shape[1]
    tm = seg["tm"]
    row, mod_map = _row_maps(seg, tm)
    return pl.pallas_call(
        _router_kernel,
        grid=(n_rows // tm,),
        in_specs=[pl.BlockSpec((tm, d), row)] + _mix_specs(tm, row)
                 + [pl.BlockSpec((1, 6, d), mod_map),
                    _resident((1, d)), _layer_resident(wo, layer), _resident(wr.shape), _resident(br.shape),
                    _resident((tm, tm))],
        out_specs=[pl.BlockSpec((tm, d), row), pl.BlockSpec((tm, d // 2), row),
                   pl.BlockSpec((tm, LANES), row), pl.BlockSpec((1, LANES), lambda i: (0, 0))],
        out_shape=[jax.ShapeDtypeStruct((n_rows, d), F32), jax.ShapeDtypeStruct((n_rows, d // 2), jnp.int32),
                   jax.ShapeDtypeStruct((n_rows, LANES), F32), jax.ShapeDtypeStruct((1, LANES), F32)],
        scratch_shapes=[pltpu.VMEM((1, LANES), F32)],
        compiler_params=_params(1),
        name="mix_router",
    )(xs, *mix, mod, g2, wo, wr, br, jnp.tril(jnp.ones((tm, tm), BF16), -1))


def _pack_bf16_pairs(h):
    half = h.shape[1] // 2
    hi = lax.bitcast_convert_type(h[:, :half].astype(BF16).astype(F32), jnp.int32)
    lo = lax.bitcast_convert_type(h[:, half:].astype(BF16).astype(F32), jnp.int32)
    return (hi & jnp.int32(-65536)) | lax.shift_right_logical(lo, 16)


def _unpack_bf16_pairs(p):
    hi = lax.bitcast_convert_type(p & jnp.int32(-65536), F32)
    lo = lax.bitcast_convert_type(lax.shift_left(p, 16), F32)
    return hi, lo


def _route_plan(route, counts, tm):
    n_rows = route.shape[0]
    n_slots = TOP_K * n_rows
    picks = route[:, :2 * TOP_K + 2].astype(jnp.int32)
    idx1, idx2, rank1, rank2 = picks[:, 0], picks[:, 1], picks[:, 4], picks[:, 5]
    offs = jnp.concatenate([jnp.zeros((1,), jnp.int32), jnp.cumsum(counts[0, :N_EXPERTS].astype(jnp.int32))])
    slot_a = offs[idx1] + rank1
    slot_b = offs[idx2] + rank2
    n_tiles = n_slots // tm
    t_start = jnp.arange(n_tiles, dtype=jnp.int32) * tm
    e_first = jnp.searchsorted(offs[1:], t_start, side="right").astype(jnp.int32)
    base_hi = jnp.minimum(t_start + tm, offs[e_first + 1])
    e_next = jnp.arange(1, N_EXPERTS, dtype=jnp.int32)
    start = offs[1:N_EXPERTS]
    x_tile = jnp.minimum(start // tm, n_tiles - 1)
    x_hi = jnp.where(start % tm != 0, jnp.minimum(offs[2:], (x_tile + 1) * tm), start)
    tiles = jnp.concatenate([t_start // tm, x_tile])
    experts = jnp.concatenate([e_first, e_next])
    lo = jnp.concatenate([t_start, start])
    hi = jnp.concatenate([base_hi, x_hi])
    order = jnp.argsort(tiles * (2 * N_EXPERTS) + experts)
    tiles, experts, lo, hi = tiles[order], experts[order], lo[order], hi[order]
    change = tiles[1:] != tiles[:-1]
    one = jnp.ones((1,), bool)
    first = jnp.concatenate([one, change]).astype(jnp.int32)
    last = jnp.concatenate([change, one]).astype(jnp.int32)
    return slot_a, slot_b, (tiles, experts, lo, hi, first, last)


def _scatter_rows(rows, idx_a, idx_b):
    n_rows, width = rows.shape
    workers = SC_CORES * SC_SUBCORES
    per_worker = n_rows // workers
    assert n_rows % (workers * SC_GATHER_ROWS) == 0
    mesh = plsc.VectorSubcoreMesh(core_axis_name="c", subcore_axis_name="s")

    @functools.partial(
        pl.kernel, mesh=mesh,
        out_type=jax.ShapeDtypeStruct((TOP_K * n_rows, width), rows.dtype),
        scratch_types=[pltpu.VMEM((TOP_K, SC_GATHER_ROWS), jnp.int32),
                       pltpu.VMEM((SC_GATHER_ROWS, width), rows.dtype),
                       pltpu.SemaphoreType.DMA],
        name="scatter_rows")
    def scatter(rows_hbm, idx_a_hbm, idx_b_hbm, out_hbm, idx_v, rows_v, sem):
        base = (lax.axis_index("s") * SC_CORES + lax.axis_index("c")) * per_worker

        @pl.loop(0, per_worker // SC_GATHER_ROWS)
        def _(i):
            off = pl.multiple_of(base + i * SC_GATHER_ROWS, SC_GATHER_ROWS)
            pltpu.sync_copy(idx_a_hbm.at[pl.ds(off, SC_GATHER_ROWS)], idx_v.at[0])
            pltpu.sync_copy(idx_b_hbm.at[pl.ds(off, SC_GATHER_ROWS)], idx_v.at[1])
            pltpu.sync_copy(rows_hbm.at[pl.ds(off, SC_GATHER_ROWS)], rows_v)
            pltpu.async_copy(rows_v, out_hbm.at[idx_v.at[0]], sem).wait()
            pltpu.async_copy(rows_v, out_hbm.at[idx_v.at[1]], sem).wait()

    return scatter(rows, idx_a, idx_b)


def _gather_rows(table, idx):
    n_idx = idx.shape[0]
    width = table.shape[1]
    workers = SC_CORES * SC_SUBCORES
    per_worker = n_idx // workers
    assert n_idx % (workers * SC_GATHER_ROWS) == 0
    mesh = plsc.VectorSubcoreMesh(core_axis_name="c", subcore_axis_name="s")

    @functools.partial(
        pl.kernel, mesh=mesh,
        out_type=jax.ShapeDtypeStruct((n_idx, width), table.dtype),
        scratch_types=[pltpu.VMEM((SC_GATHER_ROWS,), jnp.int32),
                       pltpu.VMEM((SC_GATHER_ROWS, width), table.dtype),
                       pltpu.SemaphoreType.DMA],
        name="gather_rows")
    def gather(table_hbm, idx_hbm, out_hbm, idx_v, rows_v, sem):
        base = (lax.axis_index("s") * SC_CORES + lax.axis_index("c")) * per_worker

        @pl.loop(0, per_worker // SC_GATHER_ROWS)
        def _(i):
            off = pl.multiple_of(base + i * SC_GATHER_ROWS, SC_GATHER_ROWS)
            pltpu.sync_copy(idx_hbm.at[pl.ds(off, SC_GATHER_ROWS)], idx_v)
            pltpu.async_copy(table_hbm.at[idx_v], rows_v, sem).wait()
            pltpu.sync_copy(rows_v, out_hbm.at[pl.ds(off, SC_GATHER_ROWS)])

    return gather(table, idx)


def _experts_kernel(tile_ref, exp_ref, lo_ref, hi_ref, first_ref, last_ref,
                    x_ref, w1_ref, w3_ref, w2_ref, o_ref, acc_ref):
    i = pl.program_id(0)
    tm, half = x_ref.shape
    f = w1_ref.shape[2]
    fc = FFN_CHUNK_COLS

    @pl.when(first_ref[i] == 1)
    def _():
        acc_ref[...] = jnp.zeros_like(acc_ref)

    lo, hi = lo_ref[i], hi_ref[i]

    @pl.when(hi > lo)
    def _():
        x_hi, x_lo = _unpack_bf16_pairs(x_ref[...])
        x_hi, x_lo = x_hi.astype(BF16), x_lo.astype(BF16)
        y = None
        for c in range(f // fc):
            cols = slice(c * fc, (c + 1) * fc)
            a = _dot(x_hi, w1_ref[0, :half, cols]) + _dot(x_lo, w1_ref[0, half:, cols])
            b = _dot(x_hi, w3_ref[0, :half, cols]) + _dot(x_lo, w3_ref[0, half:, cols])
            part = _dot((a * _sigmoid(a) * b).astype(BF16), w2_ref[0, cols, :])
            y = part if y is None else y + part
        rows = tile_ref[i] * tm + lax.broadcasted_iota(jnp.int32, (tm, 1), 0)
        keep = jnp.logical_and(rows >= lo, rows < hi)
        acc_ref[...] += jnp.where(keep, y, 0.0)

    @pl.when(last_ref[i] == 1)
    def _():
        o_ref[...] = _pack_bf16_pairs(acc_ref[...])


def _experts(xs_sorted, items, w1, w3, w2):
    n_slots, half = xs_sorted.shape
    n_exp, d, f = w1.shape
    tm = MOE_ROW_TILE
    assert f % FFN_CHUNK_COLS == 0
    tile_map = lambda i, tiles, *_: (tiles[i], 0)
    exp_map = lambda i, tiles, experts, *_: (experts[i], 0, 0)
    grid_spec = pltpu.PrefetchScalarGridSpec(
        num_scalar_prefetch=len(items),
        grid=(items[0].shape[0],),
        in_specs=[pl.BlockSpec((tm, half), tile_map),
                  pl.BlockSpec((1, d, f), exp_map), pl.BlockSpec((1, d, f), exp_map),
                  pl.BlockSpec((1, f, d), exp_map)],
        out_specs=pl.BlockSpec((tm, half), tile_map),
        scratch_shapes=[pltpu.VMEM((tm, d), F32)],
    )
    return pl.pallas_call(
        _experts_kernel,
        grid_spec=grid_spec,
        out_shape=jax.ShapeDtypeStruct((n_slots, half), jnp.int32),
        compiler_params=_params(1),
        name="experts",
    )(*items, xs_sorted, w1, w3, w2)


def _combine_kernel(x1_ref, ya_ref, yb_ref, route_ref, mod_ref, fg_ref, o_ref, *, final):
    half = ya_ref.shape[1]
    route = route_ref[...]
    wa, wb = route[:, 2:3], route[:, 3:4]
    a_hi, a_lo = _unpack_bf16_pairs(ya_ref[...])
    b_hi, b_lo = _unpack_bf16_pairs(yb_ref[...])
    gate = mod_ref[0][5:6]
    out_hi = x1_ref[:, :half] + gate[:, :half] * (wa * a_hi + wb * b_hi)
    out_lo = x1_ref[:, half:] + gate[:, half:] * (wa * a_lo + wb * b_lo)
    if final:
        total = jnp.sum(out_hi * out_hi, axis=-1, keepdims=True) + jnp.sum(out_lo * out_lo, axis=-1, keepdims=True)
        scale = lax.rsqrt(total / (2 * half) + EPS)
        out_hi = out_hi * scale * fg_ref[:, :half]
        out_lo = out_lo * scale * fg_ref[:, half:]
    o_ref[:, :half] = out_hi
    o_ref[:, half:] = out_lo


def _combine(x1, y_pairs, route, mod, final_g, seg, final):
    n_rows, d = x1.shape
    tm = seg["tm"]
    row, mod_map = _row_maps(seg, tm)
    second = n_rows // tm
    return pl.pallas_call(
        functools.partial(_combine_kernel, final=final),
        grid=(n_rows // tm,),
        in_specs=[pl.BlockSpec((tm, d), row), pl.BlockSpec((tm, d // 2), row),
                  pl.BlockSpec((tm, d // 2), lambda i: (second + i, 0)),
                  pl.BlockSpec((tm, LANES), row), pl.BlockSpec((1, 6, d), mod_map),
                  pl.BlockSpec((1, d), lambda i: (0, 0))],
        out_specs=pl.BlockSpec((tm, d), row),
        out_shape=jax.ShapeDtypeStruct((n_rows, d), F32),
        compiler_params=_params(1),
        name="moe_combine",
    )(x1, y_pairs, y_pairs, route, mod, final_g)


def _moe_layer(xs, mix, mod, g2, wo, layer, wr, br, w1, w3, w2, final_g, seg, n_rows, final):
    x1, h_pairs, route, counts = _router(xs, mix, mod, g2, wo, layer, wr, br, seg, n_rows)
    slot_a, slot_b, items = _route_plan(route, counts, MOE_ROW_TILE)
    y_sorted = _experts(_scatter_rows(h_pairs, slot_a, slot_b), items, w1, w3, w2)
    y_pairs = _gather_rows(y_sorted, jnp.concatenate([slot_a, slot_b]))
    return _combine(x1, y_pairs, route, mod, final_g, seg, final)


def kernel(x, c, ctx, c_ctx, norm1_g, norm2_g, w_mod, b_mod, w_in, conv_w, conv_b, b_gates, attn_sink,
           g_att, g_ml, w_out, ffn_w1, ffn_w3, ffn_w2, w_router, b_router, exp_w1, exp_w3, exp_w2,
           final_g):
    b_, s_, d = x.shape
    lc = ctx.shape[1]
    depth = w_in.shape[0]
    n_lat, n_ctx = b_ * s_, b_ * lc
    tm = min(ROW_TILE, s_)
    assert s_ % tm == 0 and n_ctx % tm == 0
    assert (TOP_K * n_lat) % MOE_ROW_TILE == 0 and (TOP_K * n_ctx) % MOE_ROW_TILE == 0
    assert s_ % ML_CHUNK == 0 and lc % ML_CHUNK == 0 and n_lat % lc == 0 and b_ < MOD_ROWS
    seg = dict(B=b_, S=s_, Lc=lc, n_lat=n_lat, tm=tm)

    cond = jnp.zeros((MOD_ROWS, d), F32).at[:b_].set(c).at[b_].set(c_ctx)
    mods = _modulation(cond, w_mod, b_mod).reshape(depth, MOD_ROWS, 6, d)
    rope = _rope_tables(s_, tm)
    xs = jnp.concatenate([x.reshape(n_lat, d), ctx.reshape(n_ctx, d)], axis=0)
    final_row = final_g.reshape(1, d)

    w_in_b, w_out_b = w_in.astype(BF16), w_out.astype(BF16)
    ffn_b = (ffn_w1.astype(BF16), ffn_w3.astype(BF16), ffn_w2.astype(BF16))
    exp_b = None

    for layer in range(depth):
        last = layer == depth - 1
        w_gates = w_in[layer][:, MAIN_WIDTH:]
        w_gc = jnp.pad(w_gates, ((0, 0), (0, LANES - ML_GATES))).astype(BF16)
        b_gc = jnp.pad(b_gates[layer], (0, LANES - ML_GATES)).reshape(1, LANES)
        qa, ka, va, qm, km, vm, om, gc, gr = _input_projection(
            xs, mods[layer], norm1_g[layer].reshape(1, d), w_in_b, layer, w_gc,
            w_gates.T.astype(BF16), b_gc, b_gates[layer].reshape(ML_GATES, 1),
            conv_w[layer], conv_b[layer].reshape(1, -1), rope, seg)
        att = _attention(qa, ka, va, attn_sink[layer], g_att[layer].reshape(1, ATT_WIDTH), seg, not last)
        hf, hb = _mlstm(qm, km, vm, gc, gr, seg)
        mix = (att, hf, hb, om, g_ml[layer].reshape(1, ML_WIDTH))
        n_rows = n_lat if last else n_lat + n_ctx
        g2 = norm2_g[layer].reshape(1, d)
        i = layer // 2
        if layer % 2 == 0:
            cast = None if last else ((exp_w1, exp_w3, exp_w2), i)
            xs, exp_b = _dense_layer(xs, mix, mods[layer], g2, w_out_b, layer, *ffn_b, i, final_row, seg,
                                     n_rows, last, cast)
        else:
            wr = jnp.pad(w_router[i], ((0, 0), (0, LANES - N_EXPERTS)))
            wr_hi = wr.astype(BF16)
            wr = jnp.concatenate([wr_hi, (wr - wr_hi.astype(F32)).astype(BF16)], axis=1)
            br = jnp.pad(b_router[i], (0, LANES - N_EXPERTS)).reshape(1, LANES)
            xs = _moe_layer(xs, mix, mods[layer], g2, w_out_b, layer, wr, br, *exp_b, final_row, seg,
                            n_rows, last)
    return xs[:n_lat].reshape(b_, s_, d)
```

```python
import functools

import jax
import jax.numpy as jnp
import numpy as np
from jax import lax
from jax.experimental import pallas as pl
from jax.experimental.pallas import tpu as pltpu
from jax.experimental.pallas import tpu_sc as plsc

F32 = jnp.float32
BF16 = jnp.bfloat16

GRID_W = 64
ATT_HEADS = 8
ATT_KV_HEADS = 2
ATT_HEAD_DIM = 64
ATT_GROUP = ATT_HEADS // ATT_KV_HEADS
WINDOW = 128
ATT_BLOCK = 128
ROPE_THETA = 10000.0
ML_HEADS = 4
ML_QK_DIM = 64
ML_V_DIM = 128
ML_CONV = 5
GATE_CAP = 15.0
ATT_WIDTH = ATT_HEADS * ATT_HEAD_DIM
ATT_KV_WIDTH = ATT_KV_HEADS * ATT_HEAD_DIM
ML_QK_WIDTH = ML_HEADS * ML_QK_DIM
ML_WIDTH = ML_HEADS * ML_V_DIM
ML_GATES = 4 * ML_HEADS
MAIN_WIDTH = ATT_WIDTH + 2 * ATT_KV_WIDTH + 2 * ML_QK_WIDTH + 2 * ML_WIDTH
N_EXPERTS = 8
TOP_K = 2
EPS = 1e-6

LANES = 128
SUBLANES = 8
VMEM_LIMIT = 56 * 1024 * 1024
NEG = -1e30
SC_CORES = 2
SC_SUBCORES = 16
SC_GATHER_ROWS = 64

ROW_TILE = 512
MOE_ROW_TILE = 512
FFN_CHUNK_COLS = 256
ATT_STEP_BLOCKS = 8
ML_CHUNK = 128
ML_STEP_CHUNKS = 2
CONV_HALO = SUBLANES
MOD_ROWS = 16
MOD_COL_TILE = 1536


def _dot(a, b):
    return jnp.dot(a, b, preferred_element_type=F32)


def _dot_nt(a, b):
    return lax.dot_general(a, b, (((1,), (1,)), ((), ())), preferred_element_type=F32)


def _dot_tn(a, b):
    return lax.dot_general(a, b, (((0,), (0,)), ((), ())), preferred_element_type=F32)


def _dot_f32(a, b):
    return jnp.dot(a, b, preferred_element_type=F32, precision=lax.Precision.HIGHEST)


def _sigmoid(x):
    return 1.0 / (1.0 + jnp.exp(-x))


def _rms(x, g):
    return x * lax.rsqrt(jnp.mean(x * x, axis=-1, keepdims=True) + EPS) * g


def _adaln(x, g, shift, scale):
    return _rms(x, g) * (1.0 + scale) + shift


def _params(n_axes):
    return pltpu.CompilerParams(dimension_semantics=("arbitrary",) * n_axes,
                                vmem_limit_bytes=VMEM_LIMIT)


def _resident(shape):
    zeros = (0,) * len(shape)
    return pl.BlockSpec(shape, lambda *_: zeros, pipeline_mode=pl.Buffered(1))


def _layer_resident(stacked, layer, block=None):
    block = tuple(stacked.shape[1:]) if block is None else block
    index = (layer,) + (0,) * len(block)
    return pl.BlockSpec((None,) + block, lambda *_: index, pipeline_mode=pl.Buffered(1))


def _mod_kernel(c_ref, w_ref, b_ref, o_ref):
    c = c_ref[...]
    a = (c * _sigmoid(c)).astype(BF16)
    o_ref[0] = _dot(a, w_ref[0].astype(BF16)) + b_ref[0]


def _modulation(cond, w_mod, b_mod):
    depth, d, width = w_mod.shape
    tn = MOD_COL_TILE if width % MOD_COL_TILE == 0 else width
    return pl.pallas_call(
        _mod_kernel,
        grid=(depth, width // tn),
        in_specs=[pl.BlockSpec((MOD_ROWS, d), lambda l, j: (0, 0)),
                  pl.BlockSpec((1, d, tn), lambda l, j: (l, 0, j)),
                  pl.BlockSpec((1, 1, tn), lambda l, j: (l, 0, j))],
        out_specs=pl.BlockSpec((1, MOD_ROWS, tn), lambda l, j: (l, 0, j)),
        out_shape=jax.ShapeDtypeStruct((depth, MOD_ROWS, width), F32),
        compiler_params=_params(2),
        name="modulation",
    )(cond, w_mod, b_mod.reshape(depth, 1, width))


def _gate_act(u, is_forget):
    g = GATE_CAP * jnp.tanh(u / GATE_CAP)
    log_sig = jnp.minimum(g, 0.0) - jnp.log1p(jnp.exp(-jnp.abs(g)))
    return jnp.where(is_forget, log_sig, g)


def _conv_silu(xe, cw, cb, rows):
    n_ext = xe.shape[0]
    mid = ML_CONV // 2
    y = cb + cw[mid:mid + 1] * xe[CONV_HALO:CONV_HALO + rows]
    for tap in range(ML_CONV):
        if tap != mid:
            y = y + cw[tap:tap + 1] * pltpu.roll(xe, (mid - tap) % n_ext, 0)[CONV_HALO:CONV_HALO + rows]
    return y * _sigmoid(y)


def _inproj_kernel(x_ref, xp_ref, xn_ref, mod_ref, g_ref, w_ref, wgc_ref, wgr_ref, bgc_ref, bgr_ref,
                   cw_ref, cb_ref, cos_ref, sa_ref, sb_ref,
                   qa_ref, ka_ref, va_ref, qm_ref, km_ref, vm_ref, om_ref, gc_ref, gr_ref,
                   *, n_lat_tiles, seq_lat, seq_ctx):
    mod = mod_ref[0]
    tm = x_ref.shape[0]
    seg = min(tm, seq_ctx)
    n_seg = tm // seg
    norm_gain = g_ref[...]
    normed = lambda rows: _adaln(rows, norm_gain, mod[0:1], mod[1:2]).astype(BF16)
    quarter = ATT_HEAD_DIM // 4

    def rope(u, rows):
        return (u * cos_ref[rows, :] + pltpu.roll(u, LANES - quarter, 1) * sa_ref[rows, :]
                + pltpu.roll(u, quarter, 1) * sb_ref[rows, :])

    c_kv = ATT_WIDTH
    c_qk = c_kv + 2 * ATT_KV_WIDTH
    c_vm = c_qk + 2 * ML_QK_WIDTH
    c_om = c_vm + ML_WIDTH
    w_qk = w_ref[:, c_qk:c_vm]
    i = pl.program_id(0)
    seq_len = jnp.where(i < n_lat_tiles, seq_lat, seq_ctx)
    cw, cb = cw_ref[...], cb_ref[...]
    hx = [normed(x_ref[j * seg:(j + 1) * seg, :]) for j in range(n_seg)]
    qk = [_dot(h, w_qk) for h in hx]
    qk_prev = _dot(normed(xp_ref[...]), w_qk)
    qk_next = _dot(normed(xn_ref[...]), w_qk)
    for j in range(n_seg):
        rows = slice(j * seg, (j + 1) * seg)
        first_row = i * tm + j * seg
        has_prev = (lax.rem(first_row, seq_len) != 0).astype(F32)
        has_next = (lax.rem(first_row + seg, seq_len) != 0).astype(F32)
        prev = qk_prev if j == 0 else qk[j - 1][seg - CONV_HALO:]
        nxt = qk_next if j == n_seg - 1 else qk[j + 1][:CONV_HALO]
        xe = jnp.concatenate([prev * has_prev, qk[j], nxt * has_next], axis=0)
        y = _conv_silu(xe, cw, cb, seg)
        qm_ref[rows, :] = (y[:, :ML_QK_WIDTH] * (ML_QK_DIM ** -0.5)).astype(BF16)
        km_ref[rows, :] = y[:, ML_QK_WIDTH:].astype(BF16)
        q = _dot(hx[j], w_ref[:, :c_kv])
        for c in range(ATT_WIDTH // LANES):
            sl = slice(c * LANES, (c + 1) * LANES)
            qa_ref[rows, sl] = (rope(q[:, sl], rows) * (ATT_HEAD_DIM ** -0.5)).astype(BF16)
        kv = _dot(hx[j], w_ref[:, c_kv:c_qk])
        ka_ref[rows, :] = rope(kv[:, :ATT_KV_WIDTH], rows).astype(BF16)
        va_ref[rows, :] = kv[:, ATT_KV_WIDTH:].astype(BF16)
        vm_ref[rows, :] = _dot(hx[j], w_ref[:, c_vm:c_om]).astype(BF16)
        om_ref[rows, :] = _dot(hx[j], w_ref[:, c_om:c_om + ML_WIDTH]).astype(BF16)
        gc = _dot(hx[j], wgc_ref[...]) + bgc_ref[...]
        lane = lax.broadcasted_iota(jnp.int32, gc.shape, 1)
        gc_ref[rows, :] = _gate_act(gc, (lane // ML_HEADS) % 2 == 1)
        gr = _dot_nt(wgr_ref[...], hx[j]) + bgr_ref[...]
        sub = lax.broadcasted_iota(jnp.int32, gr.shape, 0)
        gr_ref[:, rows] = _gate_act(gr, (sub // ML_HEADS) % 2 == 1)


def _input_projection(xs, mod, g1, w_in, layer, w_gc, w_gr, b_gc, b_gr, conv_w, conv_b, rope, seg):
    n, d = xs.shape
    tm = seg["tm"]
    nlat = seg["n_lat"] // tm
    s_tiles = seg["S"] // tm
    halos_per_tile = tm // CONV_HALO
    row = lambda i: (i, 0)
    prev = lambda i: (jnp.maximum(i * halos_per_tile - 1, 0), 0)
    nxt = lambda i: (jnp.minimum((i + 1) * halos_per_tile, n // CONV_HALO - 1), 0)
    mod_map = lambda i: (jnp.where(i < nlat, i // s_tiles, seg["B"]), 0, 0)
    rope_map = lambda i: (jnp.where(i < nlat, i % s_tiles, s_tiles), 0)
    widths = [(ATT_WIDTH, BF16), (ATT_KV_WIDTH, BF16), (ATT_KV_WIDTH, BF16), (ML_QK_WIDTH, BF16),
              (ML_QK_WIDTH, BF16), (ML_WIDTH, BF16), (ML_WIDTH, BF16), (LANES, F32)]
    out_shape = [jax.ShapeDtypeStruct((n, w), t) for w, t in widths]
    out_specs = [pl.BlockSpec((tm, w), row) for w, _ in widths]
    out_shape.append(jax.ShapeDtypeStruct((ML_GATES, n), F32))
    out_specs.append(pl.BlockSpec((ML_GATES, tm), lambda i: (0, i)))
    return pl.pallas_call(
        functools.partial(_inproj_kernel, n_lat_tiles=nlat, seq_lat=seg["S"], seq_ctx=seg["Lc"]),
        grid=(n // tm,),
        in_specs=[pl.BlockSpec((tm, d), row),
                  pl.BlockSpec((CONV_HALO, d), prev),
                  pl.BlockSpec((CONV_HALO, d), nxt),
                  pl.BlockSpec((1, 6, d), mod_map),
                  _resident((1, d)),
                  _layer_resident(w_in, layer, (d, MAIN_WIDTH)), _resident(w_gc.shape), _resident(w_gr.shape),
                  _resident(b_gc.shape), _resident(b_gr.shape),
                  _resident(conv_w.shape), _resident(conv_b.shape),
                  pl.BlockSpec((tm, LANES), rope_map),
                  pl.BlockSpec((tm, LANES), rope_map),
                  pl.BlockSpec((tm, LANES), rope_map)],
        out_specs=out_specs,
        out_shape=out_shape,
        compiler_params=_params(1),
        name="input_projection",
    )(xs, xs, xs, mod, g1, w_in, w_gc, w_gr, b_gc, b_gr, conv_w, conv_b, *rope)


def _rope_tables(s, tm):
    quarter = ATT_HEAD_DIM // 4
    t = jnp.arange(s)
    row = (t // GRID_W).astype(F32)
    col = (t % GRID_W).astype(F32)
    inv = ROPE_THETA ** (-jnp.arange(quarter, dtype=F32) / quarter)
    ang_r = row[:, None] * inv[None, :]
    ang_c = col[:, None] * inv[None, :]
    zero = jnp.zeros_like(ang_r)
    cos = jnp.concatenate([jnp.cos(ang_r)] * 2 + [jnp.cos(ang_c)] * 2, axis=1)
    sin_up = jnp.concatenate([-jnp.sin(ang_r), zero, -jnp.sin(ang_c), zero], axis=1)
    sin_dn = jnp.concatenate([zero, jnp.sin(ang_r), zero, jnp.sin(ang_c)], axis=1)
    reps = LANES // ATT_HEAD_DIM
    ident = [jnp.ones((tm, LANES), F32), jnp.zeros((tm, LANES), F32), jnp.zeros((tm, LANES), F32)]
    return tuple(jnp.concatenate([jnp.tile(a, (1, reps)), i], axis=0)
                 for a, i in zip((cos, sin_up, sin_dn), ident))


def _attn_kernel(sink_ref, *refs, window, blocks):
    blk = ATT_BLOCK
    if window:
        q_ref, kp_ref, kc_ref, kn_ref, vp_ref, vc_ref, vn_ref, kx_ref, vx_ref, g_ref, o_ref = refs
        j = pl.program_id(1)
        k_own, v_own = kc_ref[...], vc_ref[...]
        k_blocks = [kp_ref[...]] + [k_own[t * blk:(t + 1) * blk] for t in range(blocks)] + [kn_ref[...]]
        v_blocks = [vp_ref[...]] + [v_own[t * blk:(t + 1) * blk] for t in range(blocks)] + [vn_ref[...]]
    else:
        q_ref, kx_ref, vx_ref, g_ref, _, o_ref = refs
    for t in range(blocks):
        if window:
            has_prev = j > 0 if t == 0 else True
            has_next = j < pl.num_programs(1) - 1 if t == blocks - 1 else True
            win = (k_blocks[t:t + 3], v_blocks[t:t + 3], has_prev, has_next)
        else:
            win = None
        att = _attend_block(sink_ref, q_ref[t * blk:(t + 1) * blk, :], win, kx_ref[...], vx_ref[...])
        o_ref[t * blk:(t + 1) * blk, :] = _rms(att, g_ref[...]).astype(o_ref.dtype)


def _attend_block(sink_ref, q, win, k_ctx, v_ctx):
    blk = ATT_BLOCK
    if win is None:
        k_all, v_all, bias = k_ctx, v_ctx, None
    else:
        k_win, v_win, has_prev, has_next = win
        rows = lax.broadcasted_iota(jnp.int32, (blk, blk), 0)
        cols = lax.broadcasted_iota(jnp.int32, (blk, blk), 1)
        ok_p = jnp.logical_and(cols >= rows, has_prev)
        ok_n = jnp.logical_and(cols <= rows, has_next)
        bias = jnp.concatenate([jnp.where(ok_p, 0.0, NEG), jnp.zeros((blk, blk), F32),
                                jnp.where(ok_n, 0.0, NEG), jnp.zeros((blk, k_ctx.shape[0]), F32)], axis=1)
        k_all = jnp.concatenate(list(k_win) + [k_ctx], axis=0)
        v_all = jnp.concatenate(list(v_win) + [v_ctx], axis=0)
    dh = ATT_HEAD_DIM
    outs = []
    for h in range(ATT_KV_HEADS):
        k_h = k_all[:, h * dh:(h + 1) * dh]
        v_h = v_all[:, h * dh:(h + 1) * dh]
        q_h = jnp.concatenate([q[:, (h * ATT_GROUP + g) * dh:(h * ATT_GROUP + g + 1) * dh]
                               for g in range(ATT_GROUP)], axis=0)
        s_all = _dot_nt(q_h, k_h)
        p_parts, inv_parts = [], []
        for g in range(ATT_GROUP):
            sink = sink_ref[h * ATT_GROUP + g]
            s = s_all[g * blk:(g + 1) * blk]
            if bias is not None:
                s = s + bias
            m = jnp.maximum(jnp.max(s, axis=-1, keepdims=True), sink)
            p = jnp.exp(s - m)
            denom = jnp.sum(p, axis=-1, keepdims=True) + jnp.exp(sink - m)
            p_parts.append(p.astype(BF16))
            inv_parts.append(1.0 / denom)
        o = _dot(jnp.concatenate(p_parts, axis=0), v_h)
        for g in range(ATT_GROUP):
            outs.append(o[g * blk:(g + 1) * blk] * inv_parts[g])
    return jnp.concatenate(outs, axis=1)


def _attention(qa, ka, va, sink, g_att, seg, with_ctx):
    n = qa.shape[0]
    b_, s_, lc = seg["B"], seg["S"], seg["Lc"]
    blk = ATT_BLOCK
    blocks_per_seq = s_ // blk
    ctx_map = lambda b, j, *_: (b_ * s_ // lc + b, 0)
    ctx_specs = [pl.BlockSpec((lc, ATT_KV_WIDTH), ctx_map)] * 2
    g_spec = pl.BlockSpec((1, ATT_WIDTH), lambda b, j, *_: (0, 0))

    def call(window, blocks, steps, q_map, in_specs, args, aliases):
        grid_spec = pltpu.PrefetchScalarGridSpec(
            num_scalar_prefetch=1,
            grid=(b_, steps),
            in_specs=[pl.BlockSpec((blocks * blk, ATT_WIDTH), q_map)] + in_specs,
            out_specs=pl.BlockSpec((blocks * blk, ATT_WIDTH), q_map))
        return pl.pallas_call(
            functools.partial(_attn_kernel, window=window, blocks=blocks),
            grid_spec=grid_spec,
            out_shape=jax.ShapeDtypeStruct((n, ATT_WIDTH), BF16),
            input_output_aliases=aliases,
            compiler_params=_params(2),
            name="window_attention" if window else "context_attention",
        )(sink, qa, *args)

    qb = min(ATT_STEP_BLOCKS, blocks_per_seq)
    assert blocks_per_seq % qb == 0
    nqs = blocks_per_seq // qb

    def edge_map(off):
        return lambda b, j, *_: (b * blocks_per_seq + jnp.clip(j * qb + off, 0, blocks_per_seq - 1), 0)

    kv_specs = [pl.BlockSpec((blk, ATT_KV_WIDTH), edge_map(-1)),
                pl.BlockSpec((qb * blk, ATT_KV_WIDTH), lambda b, j, *_: (b * nqs + j, 0)),
                pl.BlockSpec((blk, ATT_KV_WIDTH), edge_map(qb))]
    att = call(True, qb, nqs, lambda b, j, *_: (b * nqs + j, 0), kv_specs + kv_specs + ctx_specs + [g_spec],
               (ka, ka, ka, va, va, va, ka, va, g_att), {})
    if with_ctx:
        cb = min(ATT_STEP_BLOCKS, lc // blk)
        assert (lc // blk) % cb == 0 and (b_ * s_) % (cb * blk) == 0
        ncs = lc // (cb * blk)
        first = b_ * s_ // (cb * blk)
        att = call(False, cb, ncs, lambda b, j, *_: (first + b * ncs + j, 0),
                   ctx_specs + [g_spec, pl.BlockSpec(memory_space=pl.ANY)], (ka, va, g_att, att), {5: 0})
    return att


def _mlstm_direction(q_ref, k_ref, v_ref, gc_ref, gr_ref, out_ref, state_ref, tok, *, reverse):
    chunk = ML_CHUNK
    rows = lax.broadcasted_iota(jnp.int32, (chunk, chunk), 0)
    cols = lax.broadcasted_iota(jnp.int32, (chunk, chunk), 1)
    lower = rows >= cols
    upper = rows <= cols
    seen = upper if reverse else lower
    gc = gc_ref[tok, :]
    gr = gr_ref[:, tok]
    b_col = _dot_f32(seen.astype(F32), gc)
    b_row = _dot_f32(gr, (lower if reverse else upper).astype(F32))
    b_end = jnp.sum(gc, axis=0, keepdims=True)
    base = 2 * ML_HEADS if reverse else 0
    pair_width = 2 * ML_QK_DIM
    lane = lax.broadcasted_iota(jnp.int32, (chunk, pair_width), 1)
    state_row = lax.broadcasted_iota(jnp.int32, (pair_width, 1), 0)
    ones = jnp.ones((chunk, ML_V_DIM), BF16)

    for pair in range(ML_HEADS // 2):
        q_pair = q_ref[tok, pair * pair_width:(pair + 1) * pair_width]
        k_pair = k_ref[tok, pair * pair_width:(pair + 1) * pair_width]
        state = state_ref[pair]
        state_bf = state.astype(BF16)
        update = None
        decays = []
        for sub in range(2):
            h = 2 * pair + sub
            i_idx = base + h
            f_idx = base + ML_HEADS + h
            own = (lane >= ML_QK_DIM) if sub else (lane < ML_QK_DIM)
            q_h = jnp.where(own, q_pair, jnp.zeros_like(q_pair))
            vx = jnp.concatenate([v_ref[tok, h * ML_V_DIM:(h + 1) * ML_V_DIM], ones], axis=1)
            bc = b_col[:, f_idx:f_idx + 1]
            d = bc - b_row[f_idx:f_idx + 1, :] + gr[i_idx:i_idx + 1, :]
            w = jnp.exp(jnp.where(seen, d, NEG))
            s = _dot_nt(q_h, k_pair) * w
            tot = _dot(s.astype(BF16), vx) + jnp.exp(bc) * _dot(q_h, state_bf)
            h_out = tot[:, :ML_V_DIM] / jnp.maximum(jnp.abs(tot[:, ML_V_DIM:]), 1.0)
            out_ref[tok, h * ML_V_DIM:(h + 1) * ML_V_DIM] = h_out.astype(out_ref.dtype)

            be = b_end[:, f_idx:f_idx + 1]
            kw = jnp.where(own, k_pair.astype(F32) * jnp.exp(be - bc + gc[:, i_idx:i_idx + 1]), 0.0)
            part = _dot_tn(kw.astype(BF16), vx)
            update = part if update is None else update + part
            decays.append(jnp.exp(be))
        decay = jnp.where(state_row < ML_QK_DIM, decays[0], decays[1])
        state_ref[pair] = decay * state + update


def _mlstm_kernel(qf_ref, kf_ref, vf_ref, gcf_ref, grf_ref, qb_ref, kb_ref, vb_ref, gcb_ref, grb_ref,
                  hf_ref, hb_ref, sf_ref, sb_ref):
    @pl.when(pl.program_id(1) == 0)
    def _():
        sf_ref[...] = jnp.zeros_like(sf_ref)
        sb_ref[...] = jnp.zeros_like(sb_ref)

    n_chunks = qf_ref.shape[0] // ML_CHUNK
    for t in range(n_chunks):
        fwd = slice(t * ML_CHUNK, (t + 1) * ML_CHUNK)
        bwd = slice((n_chunks - 1 - t) * ML_CHUNK, (n_chunks - t) * ML_CHUNK)
        _mlstm_direction(qf_ref, kf_ref, vf_ref, gcf_ref, grf_ref, hf_ref, sf_ref, fwd, reverse=False)
        _mlstm_direction(qb_ref, kb_ref, vb_ref, gcb_ref, grb_ref, hb_ref, sb_ref, bwd, reverse=True)


def _mlstm(qm, km, vm, gc, gr, seg):
    n = qm.shape[0]
    b_, s_, lc = seg["B"], seg["S"], seg["Lc"]
    chunk = ML_STEP_CHUNKS * ML_CHUNK
    assert s_ % chunk == 0 and lc % chunk == 0
    ncc, ncl = lc // chunk, s_ // chunk
    lat_chunks = b_ * ncl

    def blk(b, c, reverse):
        pos = jnp.where(c < ncc, c, c - ncc)
        if reverse:
            pos = jnp.where(c < ncc, ncc, ncl) - 1 - pos
        return jnp.where(c < ncc, lat_chunks + b * ncc + pos, b * ncl + pos)

    def specs(reverse):
        cur = lambda b, c: (blk(b, c, reverse), 0)
        return [pl.BlockSpec((chunk, ML_QK_WIDTH), cur), pl.BlockSpec((chunk, ML_QK_WIDTH), cur),
                pl.BlockSpec((chunk, ML_WIDTH), cur), pl.BlockSpec((chunk, LANES), cur),
                pl.BlockSpec((ML_GATES, chunk), lambda b, c: (0, blk(b, c, reverse)))]

    out = lambda reverse: pl.BlockSpec((chunk, ML_WIDTH), lambda b, c: (blk(b, c, reverse), 0))
    state = pltpu.VMEM((ML_HEADS // 2, 2 * ML_QK_DIM, 2 * ML_V_DIM), F32)
    return pl.pallas_call(
        _mlstm_kernel,
        grid=(b_, ncc + ncl),
        in_specs=specs(False) + specs(True),
        out_specs=[out(False), out(True)],
        out_shape=[jax.ShapeDtypeStruct((n, ML_WIDTH), BF16)] * 2,
        scratch_shapes=[state, state],
        compiler_params=_params(2),
        name="mlstm_scan",
    )(qm, km, vm, gc, gr, qm, km, vm, gc, gr)


def _swiglu(h, w1_ref, w3_ref, w2_ref):
    f = w1_ref.shape[1]
    fc = FFN_CHUNK_COLS
    y = None
    for c in range(f // fc):
        cols = slice(c * fc, (c + 1) * fc)
        a = _dot(h, w1_ref[:, cols])
        b = _dot(h, w3_ref[:, cols])
        part = _dot((a * _sigmoid(a) * b).astype(BF16), w2_ref[cols, :])
        y = part if y is None else y + part
    return y


def _mix_residual(x_ref, mix_refs, mod, wo_ref):
    att_ref, hf_ref, hb_ref, om_ref, gml_ref = mix_refs
    mix = _dot(att_ref[...], wo_ref[:ATT_WIDTH, :])
    for h in range(ML_HEADS):
        sl = slice(h * ML_V_DIM, (h + 1) * ML_V_DIM)
        tot = hf_ref[:, sl].astype(F32) + hb_ref[:, sl].astype(F32)
        ml = _rms(tot, gml_ref[:, sl]) * _sigmoid(om_ref[:, sl].astype(F32))
        mix = mix + _dot(ml.astype(BF16), wo_ref[ATT_WIDTH + h * ML_V_DIM:ATT_WIDTH + (h + 1) * ML_V_DIM, :])
    return x_ref[...] + mod[2:3] * mix


def _mix_specs(tm, row):
    return [pl.BlockSpec((tm, ATT_WIDTH), row)] + [pl.BlockSpec((tm, ML_WIDTH), row)] * 3 \
        + [_resident((1, ML_WIDTH))]


def _dense_layer_kernel(x_ref, att_ref, hf_ref, hb_ref, om_ref, gml_ref, mod_ref, g_ref, wo_ref,
                        w1_ref, w3_ref, w2_ref, fg_ref, *rest, final):
    n_cast = (len(rest) - 1) // 2
    o_ref = rest[n_cast]
    for src, dst in zip(rest[:n_cast], rest[n_cast + 1:]):
        dst[...] = src[...].astype(dst.dtype)
    mod = mod_ref[0]
    x1 = _mix_residual(x_ref, (att_ref, hf_ref, hb_ref, om_ref, gml_ref), mod, wo_ref)
    hx = _adaln(x1, g_ref[...], mod[3:4], mod[4:5]).astype(BF16)
    out = x1 + mod[5:6] * _swiglu(hx, w1_ref, w3_ref, w2_ref)
    if final:
        out = _rms(out, fg_ref[...])
    o_ref[...] = out


def _row_maps(seg, tm):
    nlat = seg["n_lat"] // tm
    s_tiles = seg["S"] // tm
    return (lambda i, *_: (i, 0)), (lambda i, *_: (jnp.where(i < nlat, i // s_tiles, seg["B"]), 0, 0))


def _cast_slabs(stacked, index, n_steps):
    rows_per_layer = int(np.prod(stacked.shape[1:-1]))
    width = stacked.shape[-1]
    slab = next(r for r in range(2 * SUBLANES, rows_per_layer + 1, 2 * SUBLANES)
                if rows_per_layer % r == 0 and rows_per_layer // r <= n_steps)
    n_slabs = rows_per_layer // slab
    src = pl.BlockSpec((slab, width), lambda i: (index * n_slabs + jnp.minimum(i, n_slabs - 1), 0))
    dst = pl.BlockSpec((slab, width), lambda i: (jnp.minimum(i, n_slabs - 1), 0))
    flat = stacked.reshape(-1, width)
    return src, dst, flat, jax.ShapeDtypeStruct((rows_per_layer, width), BF16)


def _dense_layer(xs, mix, mod, g2, wo, layer, w1, w3, w2, ffn_index, final_g, seg, n_rows, final, cast=None):
    d = xs.shape[1]
    tm = seg["tm"]
    row, mod_map = _row_maps(seg, tm)
    n_steps = n_rows // tm
    cast_specs = [_cast_slabs(p, cast[1], n_steps) for p in cast[0]] if cast else []
    outs = pl.pallas_call(
        functools.partial(_dense_layer_kernel, final=final),
        grid=(n_steps,),
        in_specs=[pl.BlockSpec((tm, d), row)] + _mix_specs(tm, row)
                 + [pl.BlockSpec((1, 6, d), mod_map),
                    _resident((1, d)), _layer_resident(wo, layer), _layer_resident(w1, ffn_index),
                    _layer_resident(w3, ffn_index), _layer_resident(w2, ffn_index), _resident((1, d))]
                 + [c[0] for c in cast_specs],
        out_specs=[pl.BlockSpec((tm, d), row)] + [c[1] for c in cast_specs],
        out_shape=[jax.ShapeDtypeStruct((n_rows, d), F32)] + [c[3] for c in cast_specs],
        compiler_params=_params(1),
        name="dense_layer",
    )(xs, *mix, mod, g2, wo, w1, w3, w2, final_g, *[c[2] for c in cast_specs])
    if not cast:
        return outs[0], None
    return outs[0], tuple(o.reshape(p.shape[1:]) for o, p in zip(outs[1:], cast[0]))


def _router_kernel(x_ref, att_ref, hf_ref, hb_ref, om_ref, gml_ref, mod_ref, g_ref, wo_ref, wr_ref, br_ref,
                   earlier_ref,
                   x1_ref, h_ref, route_ref, counts_ref, count_ref):
    mod = mod_ref[0]
    x1 = _mix_residual(x_ref, (att_ref, hf_ref, hb_ref, om_ref, gml_ref), mod, wo_ref)
    x1_ref[...] = x1
    hx = _adaln(x1, g_ref[...], mod[3:4], mod[4:5])
    h_ref[...] = _pack_bf16_pairs(hx)
    hx_hi = hx.astype(BF16)
    hx_lo = (hx - hx_hi.astype(F32)).astype(BF16)
    both = _dot_nt(wr_ref[...], hx_hi)
    logits = (both[:N_EXPERTS] + both[N_EXPERTS:] + _dot_nt(wr_ref[:N_EXPERTS, :], hx_lo)
              + br_ref[...])
    sub = lax.broadcasted_iota(jnp.int32, logits.shape, 0)
    top1 = jnp.max(logits, axis=0, keepdims=True)
    idx1 = jnp.min(jnp.where(logits == top1, sub, N_EXPERTS), axis=0, keepdims=True)
    rest = jnp.where(sub == idx1, -jnp.inf, logits)
    top2 = jnp.max(rest, axis=0, keepdims=True)
    idx2 = jnp.min(jnp.where(rest == top2, sub, N_EXPERTS), axis=0, keepdims=True)
    e2 = jnp.exp(top2 - top1)
    w_first = 1.0 / (1.0 + e2)
    @pl.when(pl.program_id(0) == 0)
    def _():
        count_ref[...] = jnp.zeros_like(count_ref)

    first, second = sub == idx1, sub == idx2
    hot = jnp.logical_or(first, second).astype(F32)
    before = _dot(hot.astype(BF16), earlier_ref[...]) + count_ref[:, :1]
    rank1 = jnp.sum(jnp.where(first, before, 0.0), axis=0, keepdims=True)
    rank2 = jnp.sum(jnp.where(second, before, 0.0), axis=0, keepdims=True)
    count_ref[...] += jnp.sum(hot, axis=1, keepdims=True)
    counts_ref[...] = count_ref[...]
    route = jnp.where(sub == 0, idx1.astype(F32), jnp.where(sub == 1, idx2.astype(F32), 0.0))
    route = jnp.where(sub == 2, w_first, jnp.where(sub == 3, e2 * w_first, route))
    route_ref[...] = jnp.where(sub == 4, rank1, jnp.where(sub == 5, rank2, route))


def _router(xs, mix, mod, g2, wo, layer, wr, br, seg, n_rows):
    d = xs.shape[1]
    tm = seg["tm"]
    row, mod_map = _row_maps(seg, tm)
    return pl.pallas_call(
        _router_kernel,
        grid=(n_rows // tm,),
        in_specs=[pl.BlockSpec((tm, d), row)] + _mix_specs(tm, row)
                 + [pl.BlockSpec((1, 6, d), mod_map),
                    _resident((1, d)), _layer_resident(wo, layer), _resident(wr.shape), _resident(br.shape),
                    _resident((tm, tm))],
        out_specs=[pl.BlockSpec((tm, d), row), pl.BlockSpec((tm, d // 2), row),
                   pl.BlockSpec((N_EXPERTS, tm), lambda i: (0, i)),
                   pl.BlockSpec((N_EXPERTS, LANES), lambda i: (0, 0))],
        out_shape=[jax.ShapeDtypeStruct((n_rows, d), F32), jax.ShapeDtypeStruct((n_rows, d // 2), jnp.int32),
                   jax.ShapeDtypeStruct((N_EXPERTS, n_rows), F32),
                   jax.ShapeDtypeStruct((N_EXPERTS, LANES), F32)],
        scratch_shapes=[pltpu.VMEM((N_EXPERTS, LANES), F32)],
        compiler_params=_params(1),
        name="mix_router",
    )(xs, *mix, mod, g2, wo, wr, br, jnp.triu(jnp.ones((tm, tm), BF16), 1))


def _pack_bf16_pairs(h):
    half = h.shape[1] // 2
    hi = lax.bitcast_convert_type(h[:, :half].astype(BF16).astype(F32), jnp.int32)
    lo = lax.bitcast_convert_type(h[:, half:].astype(BF16).astype(F32), jnp.int32)
    return (hi & jnp.int32(-65536)) | lax.shift_right_logical(lo, 16)


def _unpack_bf16_pairs(p):
    hi = lax.bitcast_convert_type(p & jnp.int32(-65536), F32)
    lo = lax.bitcast_convert_type(lax.shift_left(p, 16), F32)
    return hi, lo


def _route_plan(route, counts, tm):
    n_rows = route.shape[1]
    n_slots = TOP_K * n_rows
    idx1, idx2, rank1, rank2 = (route[r].astype(jnp.int32) for r in (0, 1, 4, 5))
    offs = jnp.concatenate([jnp.zeros((1,), jnp.int32), jnp.cumsum(counts[:, 0].astype(jnp.int32))])
    first_slot = lambda idx: sum(jnp.where(idx == e, offs[e], 0) for e in range(N_EXPERTS))
    slot_a = first_slot(idx1) + rank1
    slot_b = first_slot(idx2) + rank2
    n_tiles = n_slots // tm
    t_start = jnp.arange(n_tiles, dtype=jnp.int32) * tm
    e_first = jnp.sum(offs[None, 1:] <= t_start[:, None], axis=1).astype(jnp.int32)
    base_hi = jnp.minimum(t_start + tm, offs[e_first + 1])
    e_next = jnp.arange(1, N_EXPERTS, dtype=jnp.int32)
    start = offs[1:N_EXPERTS]
    x_tile = jnp.minimum(start // tm, n_tiles - 1)
    x_hi = jnp.where(start % tm != 0, jnp.minimum(offs[2:], (x_tile + 1) * tm), start)
    tiles = jnp.concatenate([t_start // tm, x_tile])
    experts = jnp.concatenate([e_first, e_next])
    lo = jnp.concatenate([t_start, start])
    hi = jnp.concatenate([base_hi, x_hi])
    order = jnp.argsort(tiles * (2 * N_EXPERTS) + experts)
    tiles, experts, lo, hi = tiles[order], experts[order], lo[order], hi[order]
    change = tiles[1:] != tiles[:-1]
    one = jnp.ones((1,), bool)
    first = jnp.concatenate([one, change]).astype(jnp.int32)
    last = jnp.concatenate([change, one]).astype(jnp.int32)
    return slot_a, slot_b, (tiles, experts, lo, hi, first, last)


def _scatter_rows(rows, idx_a, idx_b):
    n_rows, width = rows.shape
    workers = SC_CORES * SC_SUBCORES
    per_worker = n_rows // workers
    assert n_rows % (workers * SC_GATHER_ROWS) == 0
    mesh = plsc.VectorSubcoreMesh(core_axis_name="c", subcore_axis_name="s")

    @functools.partial(
        pl.kernel, mesh=mesh,
        out_type=jax.ShapeDtypeStruct((TOP_K * n_rows, width), rows.dtype),
        scratch_types=[pltpu.VMEM((TOP_K, SC_GATHER_ROWS), jnp.int32),
                       pltpu.VMEM((SC_GATHER_ROWS, width), rows.dtype),
                       pltpu.SemaphoreType.DMA],
        name="scatter_rows")
    def scatter(rows_hbm, idx_a_hbm, idx_b_hbm, out_hbm, idx_v, rows_v, sem):
        base = (lax.axis_index("s") * SC_CORES + lax.axis_index("c")) * per_worker

        @pl.loop(0, per_worker // SC_GATHER_ROWS)
        def _(i):
            off = pl.multiple_of(base + i * SC_GATHER_ROWS, SC_GATHER_ROWS)
            pltpu.sync_copy(idx_a_hbm.at[pl.ds(off, SC_GATHER_ROWS)], idx_v.at[0])
            pltpu.sync_copy(idx_b_hbm.at[pl.ds(off, SC_GATHER_ROWS)], idx_v.at[1])
            pltpu.sync_copy(rows_hbm.at[pl.ds(off, SC_GATHER_ROWS)], rows_v)
            pltpu.async_copy(rows_v, out_hbm.at[idx_v.at[0]], sem).wait()
            pltpu.async_copy(rows_v, out_hbm.at[idx_v.at[1]], sem).wait()

    return scatter(rows, idx_a, idx_b)


def _gather_rows(table, idx):
    n_idx = idx.shape[0]
    width = table.shape[1]
    workers = SC_CORES * SC_SUBCORES
    per_worker = n_idx // workers
    assert n_idx % (workers * SC_GATHER_ROWS) == 0
    mesh = plsc.VectorSubcoreMesh(core_axis_name="c", subcore_axis_name="s")

    @functools.partial(
        pl.kernel, mesh=mesh,
        out_type=jax.ShapeDtypeStruct((n_idx, width), table.dtype),
        scratch_types=[pltpu.VMEM((SC_GATHER_ROWS,), jnp.int32),
                       pltpu.VMEM((SC_GATHER_ROWS, width), table.dtype),
                       pltpu.SemaphoreType.DMA],
        name="gather_rows")
    def gather(table_hbm, idx_hbm, out_hbm, idx_v, rows_v, sem):
        base = (lax.axis_index("s") * SC_CORES + lax.axis_index("c")) * per_worker

        @pl.loop(0, per_worker // SC_GATHER_ROWS)
        def _(i):
            off = pl.multiple_of(base + i * SC_GATHER_ROWS, SC_GATHER_ROWS)
            pltpu.sync_copy(idx_hbm.at[pl.ds(off, SC_GATHER_ROWS)], idx_v)
            pltpu.async_copy(table_hbm.at[idx_v], rows_v, sem).wait()
            pltpu.sync_copy(rows_v, out_hbm.at[pl.ds(off, SC_GATHER_ROWS)])

    return gather(table, idx)


def _experts_kernel(tile_ref, exp_ref, lo_ref, hi_ref, first_ref, last_ref,
                    x_ref, w1_ref, w3_ref, w2_ref, o_ref, acc_ref):
    i = pl.program_id(0)
    tm, half = x_ref.shape
    f = w1_ref.shape[2]
    fc = FFN_CHUNK_COLS

    @pl.when(first_ref[i] == 1)
    def _():
        acc_ref[...] = jnp.zeros_like(acc_ref)

    lo, hi = lo_ref[i], hi_ref[i]

    @pl.when(hi > lo)
    def _():
        x_hi, x_lo = _unpack_bf16_pairs(x_ref[...])
        x_hi, x_lo = x_hi.astype(BF16), x_lo.astype(BF16)
        y = None
        for c in range(f // fc):
            cols = slice(c * fc, (c + 1) * fc)
            a = _dot(x_hi, w1_ref[0, :half, cols]) + _dot(x_lo, w1_ref[0, half:, cols])
            b = _dot(x_hi, w3_ref[0, :half, cols]) + _dot(x_lo, w3_ref[0, half:, cols])
            part = _dot((a * _sigmoid(a) * b).astype(BF16), w2_ref[0, cols, :])
            y = part if y is None else y + part
        rows = tile_ref[i] * tm + lax.broadcasted_iota(jnp.int32, (tm, 1), 0)
        keep = jnp.logical_and(rows >= lo, rows < hi)
        acc_ref[...] += jnp.where(keep, y, 0.0)

    @pl.when(last_ref[i] == 1)
    def _():
        o_ref[...] = _pack_bf16_pairs(acc_ref[...])


def _experts(xs_sorted, items, w1, w3, w2):
    n_slots, half = xs_sorted.shape
    n_exp, d, f = w1.shape
    tm = MOE_ROW_TILE
    assert f % FFN_CHUNK_COLS == 0
    tile_map = lambda i, tiles, *_: (tiles[i], 0)
    exp_map = lambda i, tiles, experts, *_: (experts[i], 0, 0)
    grid_spec = pltpu.PrefetchScalarGridSpec(
        num_scalar_prefetch=len(items),
        grid=(items[0].shape[0],),
        in_specs=[pl.BlockSpec((tm, half), tile_map),
                  pl.BlockSpec((1, d, f), exp_map), pl.BlockSpec((1, d, f), exp_map),
                  pl.BlockSpec((1, f, d), exp_map)],
        out_specs=pl.BlockSpec((tm, half), tile_map),
        scratch_shapes=[pltpu.VMEM((tm, d), F32)],
    )
    return pl.pallas_call(
        _experts_kernel,
        grid_spec=grid_spec,
        out_shape=jax.ShapeDtypeStruct((n_slots, half), jnp.int32),
        compiler_params=_params(1),
        name="experts",
    )(*items, xs_sorted, w1, w3, w2)


def _combine_kernel(x1_ref, ya_ref, yb_ref, route_ref, mod_ref, fg_ref, o_ref, *, final):
    half = ya_ref.shape[1]
    route = route_ref[...]
    padded = jnp.concatenate([route, jnp.zeros((LANES - route.shape[0], route.shape[1]), F32)], axis=0)
    route = jnp.transpose(padded)
    wa, wb = route[:, 2:3], route[:, 3:4]
    a_hi, a_lo = _unpack_bf16_pairs(ya_ref[...])
    b_hi, b_lo = _unpack_bf16_pairs(yb_ref[...])
    gate = mod_ref[0][5:6]
    out_hi = x1_ref[:, :half] + gate[:, :half] * (wa * a_hi + wb * b_hi)
    out_lo = x1_ref[:, half:] + gate[:, half:] * (wa * a_lo + wb * b_lo)
    if final:
        total = jnp.sum(out_hi * out_hi, axis=-1, keepdims=True) + jnp.sum(out_lo * out_lo, axis=-1, keepdims=True)
        scale = lax.rsqrt(total / (2 * half) + EPS)
        out_hi = out_hi * scale * fg_ref[:, :half]
        out_lo = out_lo * scale * fg_ref[:, half:]
    o_ref[:, :half] = out_hi
    o_ref[:, half:] = out_lo


def _combine(x1, y_pairs, route, mod, final_g, seg, final):
    n_rows, d = x1.shape
    tm = seg["tm"]
    row, mod_map = _row_maps(seg, tm)
    second = n_rows // tm
    return pl.pallas_call(
        functools.partial(_combine_kernel, final=final),
        grid=(n_rows // tm,),
        in_specs=[pl.BlockSpec((tm, d), row), pl.BlockSpec((tm, d // 2), row),
                  pl.BlockSpec((tm, d // 2), lambda i: (second + i, 0)),
                  pl.BlockSpec((N_EXPERTS, tm), lambda i: (0, i)), pl.BlockSpec((1, 6, d), mod_map),
                  pl.BlockSpec((1, d), lambda i: (0, 0))],
        out_specs=pl.BlockSpec((tm, d), row),
        out_shape=jax.ShapeDtypeStruct((n_rows, d), F32),
        compiler_params=_params(1),
        name="moe_combine",
    )(x1, y_pairs, y_pairs, route, mod, final_g)


def _moe_layer(xs, mix, mod, g2, wo, layer, wr, br, w1, w3, w2, final_g, seg, n_rows, final):
    x1, h_pairs, route, counts = _router(xs, mix, mod, g2, wo, layer, wr, br, seg, n_rows)
    slot_a, slot_b, items = _route_plan(route, counts, MOE_ROW_TILE)
    y_sorted = _experts(_scatter_rows(h_pairs, slot_a, slot_b), items, w1, w3, w2)
    y_pairs = _gather_rows(y_sorted, jnp.concatenate([slot_a, slot_b]))
    return _combine(x1, y_pairs, route, mod, final_g, seg, final)


def kernel(x, c, ctx, c_ctx, norm1_g, norm2_g, w_mod, b_mod, w_in, conv_w, conv_b, b_gates, attn_sink,
           g_att, g_ml, w_out, ffn_w1, ffn_w3, ffn_w2, w_router, b_router, exp_w1, exp_w3, exp_w2,
           final_g):
    b_, s_, d = x.shape
    lc = ctx.shape[1]
    depth = w_in.shape[0]
    n_lat, n_ctx = b_ * s_, b_ * lc
    tm = min(ROW_TILE, s_)
    assert s_ % tm == 0 and n_ctx % tm == 0
    assert (TOP_K * n_lat) % MOE_ROW_TILE == 0 and (TOP_K * n_ctx) % MOE_ROW_TILE == 0
    assert s_ % ML_CHUNK == 0 and lc % ML_CHUNK == 0 and n_lat % lc == 0 and b_ < MOD_ROWS
    seg = dict(B=b_, S=s_, Lc=lc, n_lat=n_lat, tm=tm)

    cond = jnp.zeros((MOD_ROWS, d), F32).at[:b_].set(c).at[b_].set(c_ctx)
    mods = _modulation(cond, w_mod, b_mod).reshape(depth, MOD_ROWS, 6, d)
    rope = _rope_tables(s_, tm)
    xs = jnp.concatenate([x.reshape(n_lat, d), ctx.reshape(n_ctx, d)], axis=0)
    final_row = final_g.reshape(1, d)

    w_in_b, w_out_b = w_in.astype(BF16), w_out.astype(BF16)
    ffn_b = (ffn_w1.astype(BF16), ffn_w3.astype(BF16), ffn_w2.astype(BF16))
    exp_b = None

    for layer in range(depth):
        last = layer == depth - 1
        w_gates = w_in[layer][:, MAIN_WIDTH:]
        w_gc = jnp.pad(w_gates, ((0, 0), (0, LANES - ML_GATES))).astype(BF16)
        b_gc = jnp.pad(b_gates[layer], (0, LANES - ML_GATES)).reshape(1, LANES)
        qa, ka, va, qm, km, vm, om, gc, gr = _input_projection(
            xs, mods[layer], norm1_g[layer].reshape(1, d), w_in_b, layer, w_gc,
            w_gates.T.astype(BF16), b_gc, b_gates[layer].reshape(ML_GATES, 1),
            conv_w[layer], conv_b[layer].reshape(1, -1), rope, seg)
        att = _attention(qa, ka, va, attn_sink[layer], g_att[layer].reshape(1, ATT_WIDTH), seg, not last)
        hf, hb = _mlstm(qm, km, vm, gc, gr, seg)
        mix = (att, hf, hb, om, g_ml[layer].reshape(1, ML_WIDTH))
        n_rows = n_lat if last else n_lat + n_ctx
        g2 = norm2_g[layer].reshape(1, d)
        i = layer // 2
        if layer % 2 == 0:
            cast = None if last else ((exp_w1, exp_w3, exp_w2), i)
            xs, exp_b = _dense_layer(xs, mix, mods[layer], g2, w_out_b, layer, *ffn_b, i, final_row, seg,
                                     n_rows, last, cast)
        else:
            wr = w_router[i].T
            wr_hi = wr.astype(BF16)
            wr = jnp.concatenate([wr_hi, (wr - wr_hi.astype(F32)).astype(BF16)], axis=0)
            br = b_router[i].reshape(N_EXPERTS, 1)
            xs = _moe_layer(xs, mix, mods[layer], g2, w_out_b, layer, wr, br, *exp_b, final_row, seg,
                            n_rows, last)
    return xs[:n_lat].reshape(b_, s_, d)
```

```python
import functools

import jax
import jax.numpy as jnp
import numpy as np
from jax import lax
from jax.experimental import pallas as pl
from jax.experimental.pallas import tpu as pltpu
from jax.experimental.pallas import tpu_sc as plsc

F32 = jnp.float32
BF16 = jnp.bfloat16

GRID_W = 64
ATT_HEADS = 8
ATT_KV_HEADS = 2
ATT_HEAD_DIM = 64
ATT_GROUP = ATT_HEADS // ATT_KV_HEADS
WINDOW = 128
ATT_BLOCK = 128
ROPE_THETA = 10000.0
ML_HEADS = 4
ML_QK_DIM = 64
ML_V_DIM = 128
ML_CONV = 5
GATE_CAP = 15.0
ATT_WIDTH = ATT_HEADS * ATT_HEAD_DIM
ATT_KV_WIDTH = ATT_KV_HEADS * ATT_HEAD_DIM
ML_QK_WIDTH = ML_HEADS * ML_QK_DIM
ML_WIDTH = ML_HEADS * ML_V_DIM
ML_GATES = 4 * ML_HEADS
MAIN_WIDTH = ATT_WIDTH + 2 * ATT_KV_WIDTH + 2 * ML_QK_WIDTH + 2 * ML_WIDTH
N_EXPERTS = 8
TOP_K = 2
EPS = 1e-6

LANES = 128
SUBLANES = 8
VMEM_LIMIT = 56 * 1024 * 1024
NEG = -1e30
SC_CORES = 2
SC_SUBCORES = 16
SC_GATHER_ROWS = 64

ROW_TILE = 512
INPROJ_TILE = 1024
MOE_ROW_TILE = 512
FFN_CHUNK_COLS = 256
ATT_STEP_BLOCKS = 8
ML_CHUNK = 128
ML_STEP_CHUNKS = 2
CONV_HALO = SUBLANES
MOD_ROWS = 16
MOD_COL_TILE = 1536


def _dot(a, b):
    return jnp.dot(a, b, preferred_element_type=F32)


def _dot_nt(a, b):
    return lax.dot_general(a, b, (((1,), (1,)), ((), ())), preferred_element_type=F32)


def _dot_tn(a, b):
    return lax.dot_general(a, b, (((0,), (0,)), ((), ())), preferred_element_type=F32)


def _dot_f32(a, b):
    return jnp.dot(a, b, preferred_element_type=F32, precision=lax.Precision.HIGHEST)


def _sigmoid(x):
    return 1.0 / (1.0 + jnp.exp(-x))


def _rms(x, g):
    return x * lax.rsqrt(jnp.mean(x * x, axis=-1, keepdims=True) + EPS) * g


def _adaln(x, g, shift, scale):
    return _rms(x, g) * (1.0 + scale) + shift


def _params(n_axes):
    return pltpu.CompilerParams(dimension_semantics=("arbitrary",) * n_axes,
                                vmem_limit_bytes=VMEM_LIMIT)


def _resident(shape):
    zeros = (0,) * len(shape)
    return pl.BlockSpec(shape, lambda *_: zeros, pipeline_mode=pl.Buffered(1))


def _layer_resident(stacked, layer, block=None):
    block = tuple(stacked.shape[1:]) if block is None else block
    index = (layer,) + (0,) * len(block)
    return pl.BlockSpec((None,) + block, lambda *_: index, pipeline_mode=pl.Buffered(1))


def _mod_kernel(c_ref, w_ref, b_ref, o_ref):
    c = c_ref[...]
    a = (c * _sigmoid(c)).astype(BF16)
    o_ref[0] = _dot(a, w_ref[0].astype(BF16)) + b_ref[0]


def _modulation(cond, w_mod, b_mod):
    depth, d, width = w_mod.shape
    tn = MOD_COL_TILE if width % MOD_COL_TILE == 0 else width
    return pl.pallas_call(
        _mod_kernel,
        grid=(depth, width // tn),
        in_specs=[pl.BlockSpec((MOD_ROWS, d), lambda l, j: (0, 0)),
                  pl.BlockSpec((1, d, tn), lambda l, j: (l, 0, j)),
                  pl.BlockSpec((1, 1, tn), lambda l, j: (l, 0, j))],
        out_specs=pl.BlockSpec((1, MOD_ROWS, tn), lambda l, j: (l, 0, j)),
        out_shape=jax.ShapeDtypeStruct((depth, MOD_ROWS, width), F32),
        compiler_params=_params(2),
        name="modulation",
    )(cond, w_mod, b_mod.reshape(depth, 1, width))


def _gate_act(u, is_forget):
    g = GATE_CAP * jnp.tanh(u / GATE_CAP)
    log_sig = jnp.minimum(g, 0.0) - jnp.log1p(jnp.exp(-jnp.abs(g)))
    return jnp.where(is_forget, log_sig, g)


def _conv_silu(xe, cw, cb, rows):
    n_ext = xe.shape[0]
    mid = ML_CONV // 2
    y = cb + cw[mid:mid + 1] * xe[CONV_HALO:CONV_HALO + rows]
    for tap in range(ML_CONV):
        if tap != mid:
            y = y + cw[tap:tap + 1] * pltpu.roll(xe, (mid - tap) % n_ext, 0)[CONV_HALO:CONV_HALO + rows]
    return y * _sigmoid(y)


def _inproj_kernel(*refs, n_lat_tiles, seq_lat, seq_ctx, split):
    n_x = 6 if split else 3
    x_refs, refs = refs[:n_x], refs[n_x:]
    (mod_ref, g_ref, w_ref, wgc_ref, wgr_ref, bgc_ref, bgr_ref, cw_ref, cb_ref, cos_ref, sa_ref, sb_ref,
     qa_ref, ka_ref, va_ref, qm_ref, km_ref, vm_ref, om_ref, gc_ref, gr_ref) = refs[:21]
    i = pl.program_id(0)
    if split:
        in_lat = i < n_lat_tiles
        x_tile, x_prev, x_next = (jnp.where(in_lat, x_refs[2 * k][...], x_refs[2 * k + 1][...]) for k in range(3))
        refs[21][...] = x_tile
    else:
        x_tile, x_prev, x_next = (r[...] for r in x_refs)
    mod = mod_ref[0]
    tm = x_tile.shape[0]
    seg = min(tm, seq_ctx)
    n_seg = tm // seg
    norm_gain = g_ref[...]
    normed = lambda rows: _adaln(rows, norm_gain, mod[0:1], mod[1:2]).astype(BF16)
    quarter = ATT_HEAD_DIM // 4

    def rope(u, rows):
        return (u * cos_ref[rows, :] + pltpu.roll(u, LANES - quarter, 1) * sa_ref[rows, :]
                + pltpu.roll(u, quarter, 1) * sb_ref[rows, :])

    c_kv = ATT_WIDTH
    c_qk = c_kv + 2 * ATT_KV_WIDTH
    c_vm = c_qk + 2 * ML_QK_WIDTH
    c_om = c_vm + ML_WIDTH
    w_qk = w_ref[:, c_qk:c_vm]
    seq_len = jnp.where(i < n_lat_tiles, seq_lat, seq_ctx)
    cw, cb = cw_ref[...], cb_ref[...]
    hx = [normed(x_tile[j * seg:(j + 1) * seg, :]) for j in range(n_seg)]
    qk = [_dot(h, w_qk) for h in hx]
    qk_prev = _dot(normed(x_prev), w_qk)
    qk_next = _dot(normed(x_next), w_qk)
    for j in range(n_seg):
        rows = slice(j * seg, (j + 1) * seg)
        first_row = i * tm + j * seg
        has_prev = (lax.rem(first_row, seq_len) != 0).astype(F32)
        has_next = (lax.rem(first_row + seg, seq_len) != 0).astype(F32)
        prev = qk_prev if j == 0 else qk[j - 1][seg - CONV_HALO:]
        nxt = qk_next if j == n_seg - 1 else qk[j + 1][:CONV_HALO]
        xe = jnp.concatenate([prev * has_prev, qk[j], nxt * has_next], axis=0)
        y = _conv_silu(xe, cw, cb, seg)
        qm_ref[rows, :] = (y[:, :ML_QK_WIDTH] * (ML_QK_DIM ** -0.5)).astype(BF16)
        km_ref[rows, :] = y[:, ML_QK_WIDTH:].astype(BF16)
        q = _dot(hx[j], w_ref[:, :c_kv])
        for c in range(ATT_WIDTH // LANES):
            sl = slice(c * LANES, (c + 1) * LANES)
            qa_ref[rows, sl] = (rope(q[:, sl], rows) * (ATT_HEAD_DIM ** -0.5)).astype(BF16)
        kv = _dot(hx[j], w_ref[:, c_kv:c_qk])
        ka_ref[rows, :] = rope(kv[:, :ATT_KV_WIDTH], rows).astype(BF16)
        va_ref[rows, :] = kv[:, ATT_KV_WIDTH:].astype(BF16)
        vm_ref[rows, :] = _dot(hx[j], w_ref[:, c_vm:c_om]).astype(BF16)
        om_ref[rows, :] = _dot(hx[j], w_ref[:, c_om:c_om + ML_WIDTH]).astype(BF16)
        gc = _dot(hx[j], wgc_ref[...]) + bgc_ref[...]
        lane = lax.broadcasted_iota(jnp.int32, gc.shape, 1)
        gc_ref[rows, :] = _gate_act(gc, (lane // ML_HEADS) % 2 == 1)
        gr = _dot_nt(wgr_ref[...], hx[j]) + bgr_ref[...]
        sub = lax.broadcasted_iota(jnp.int32, gr.shape, 0)
        gr_ref[:, rows] = _gate_act(gr, (sub // ML_HEADS) % 2 == 1)


def _input_projection(xs, mod, g1, w_in, layer, w_gc, w_gr, b_gc, b_gr, conv_w, conv_b, rope, seg):
    split = isinstance(xs, tuple)
    d = (xs[0] if split else xs).shape[1]
    tm = seg["tm"]
    nlat = seg["n_lat"] // tm
    n = sum(a.shape[0] for a in xs) if split else xs.shape[0]
    s_tiles = seg["S"] // tm
    halos_per_tile = tm // CONV_HALO
    row = lambda i: (i, 0)

    def x_specs(n_rows, first_tile):
        n_tiles = n_rows // tm
        t = lambda i: jnp.clip(i - first_tile, 0, n_tiles - 1)
        return [pl.BlockSpec((tm, d), lambda i: (t(i), 0)),
                pl.BlockSpec((CONV_HALO, d), lambda i: (jnp.maximum(t(i) * halos_per_tile - 1, 0), 0)),
                pl.BlockSpec((CONV_HALO, d),
                             lambda i: (jnp.minimum((t(i) + 1) * halos_per_tile, n_rows // CONV_HALO - 1), 0))]

    if split:
        lat, ctx = x_specs(xs[0].shape[0], 0), x_specs(xs[1].shape[0], nlat)
        x_in_specs = [s for pair in zip(lat, ctx) for s in pair]
        x_args = [xs[0], xs[1]] * 3
    else:
        x_in_specs, x_args = x_specs(n, 0), [xs] * 3
    mod_map = lambda i: (jnp.where(i < nlat, i // s_tiles, seg["B"]), 0, 0)
    rope_map = lambda i: (jnp.where(i < nlat, i % s_tiles, s_tiles), 0)
    widths = [(ATT_WIDTH, BF16), (ATT_KV_WIDTH, BF16), (ATT_KV_WIDTH, BF16), (ML_QK_WIDTH, BF16),
              (ML_QK_WIDTH, BF16), (ML_WIDTH, BF16), (ML_WIDTH, BF16), (LANES, F32)]
    out_shape = [jax.ShapeDtypeStruct((n, w), t) for w, t in widths]
    out_specs = [pl.BlockSpec((tm, w), row) for w, _ in widths]
    out_shape.append(jax.ShapeDtypeStruct((ML_GATES, n), F32))
    out_specs.append(pl.BlockSpec((ML_GATES, tm), lambda i: (0, i)))
    if split:
        out_shape.append(jax.ShapeDtypeStruct((n, d), F32))
        out_specs.append(pl.BlockSpec((tm, d), row))
    return pl.pallas_call(
        functools.partial(_inproj_kernel, n_lat_tiles=nlat, seq_lat=seg["S"], seq_ctx=seg["Lc"], split=split),
        grid=(n // tm,),
        in_specs=x_in_specs
                 + [pl.BlockSpec((1, 6, d), mod_map),
                  _resident((1, d)),
                  _layer_resident(w_in, layer, (d, MAIN_WIDTH)), _resident(w_gc.shape), _resident(w_gr.shape),
                  _resident(b_gc.shape), _resident(b_gr.shape),
                  _resident(conv_w.shape), _resident(conv_b.shape),
                  pl.BlockSpec((tm, LANES), rope_map),
                  pl.BlockSpec((tm, LANES), rope_map),
                  pl.BlockSpec((tm, LANES), rope_map)],
        out_specs=out_specs,
        out_shape=out_shape,
        compiler_params=_params(1),
        name="input_projection",
    )(*x_args, mod, g1, w_in, w_gc, w_gr, b_gc, b_gr, conv_w, conv_b, *rope)


def _rope_tables(s, tm):
    quarter = ATT_HEAD_DIM // 4
    t = jnp.arange(s)
    row = (t // GRID_W).astype(F32)
    col = (t % GRID_W).astype(F32)
    inv = ROPE_THETA ** (-jnp.arange(quarter, dtype=F32) / quarter)
    ang_r = row[:, None] * inv[None, :]
    ang_c = col[:, None] * inv[None, :]
    zero = jnp.zeros_like(ang_r)
    cos = jnp.concatenate([jnp.cos(ang_r)] * 2 + [jnp.cos(ang_c)] * 2, axis=1)
    sin_up = jnp.concatenate([-jnp.sin(ang_r), zero, -jnp.sin(ang_c), zero], axis=1)
    sin_dn = jnp.concatenate([zero, jnp.sin(ang_r), zero, jnp.sin(ang_c)], axis=1)
    reps = LANES // ATT_HEAD_DIM
    ident = [jnp.ones((tm, LANES), F32), jnp.zeros((tm, LANES), F32), jnp.zeros((tm, LANES), F32)]
    return tuple(jnp.concatenate([jnp.tile(a, (1, reps)), i], axis=0)
                 for a, i in zip((cos, sin_up, sin_dn), ident))


def _attn_kernel(sink_ref, *refs, window, blocks):
    blk = ATT_BLOCK
    if window:
        q_ref, kp_ref, kc_ref, kn_ref, vp_ref, vc_ref, vn_ref, kx_ref, vx_ref, g_ref, o_ref = refs
        j = pl.program_id(1)
        k_own, v_own = kc_ref[...], vc_ref[...]
        k_blocks = [kp_ref[...]] + [k_own[t * blk:(t + 1) * blk] for t in range(blocks)] + [kn_ref[...]]
        v_blocks = [vp_ref[...]] + [v_own[t * blk:(t + 1) * blk] for t in range(blocks)] + [vn_ref[...]]
    else:
        q_ref, kx_ref, vx_ref, g_ref, _, o_ref = refs
    for t in range(blocks):
        if window:
            has_prev = j > 0 if t == 0 else True
            has_next = j < pl.num_programs(1) - 1 if t == blocks - 1 else True
            win = (k_blocks[t:t + 3], v_blocks[t:t + 3], has_prev, has_next)
        else:
            win = None
        att = _attend_block(sink_ref, q_ref[t * blk:(t + 1) * blk, :], win, kx_ref[...], vx_ref[...])
        o_ref[t * blk:(t + 1) * blk, :] = _rms(att, g_ref[...]).astype(o_ref.dtype)


def _attend_block(sink_ref, q, win, k_ctx, v_ctx):
    blk = ATT_BLOCK
    if win is None:
        k_all, v_all, bias = k_ctx, v_ctx, None
    else:
        k_win, v_win, has_prev, has_next = win
        rows = lax.broadcasted_iota(jnp.int32, (blk, blk), 0)
        cols = lax.broadcasted_iota(jnp.int32, (blk, blk), 1)
        ok_p = jnp.logical_and(cols >= rows, has_prev)
        ok_n = jnp.logical_and(cols <= rows, has_next)
        bias = jnp.concatenate([jnp.where(ok_p, 0.0, NEG), jnp.zeros((blk, blk), F32),
                                jnp.where(ok_n, 0.0, NEG), jnp.zeros((blk, k_ctx.shape[0]), F32)], axis=1)
        k_all = jnp.concatenate(list(k_win) + [k_ctx], axis=0)
        v_all = jnp.concatenate(list(v_win) + [v_ctx], axis=0)
    dh = ATT_HEAD_DIM
    outs = []
    for h in range(ATT_KV_HEADS):
        k_h = k_all[:, h * dh:(h + 1) * dh]
        v_h = v_all[:, h * dh:(h + 1) * dh]
        q_h = jnp.concatenate([q[:, (h * ATT_GROUP + g) * dh:(h * ATT_GROUP + g + 1) * dh]
                               for g in range(ATT_GROUP)], axis=0)
        s_all = _dot_nt(q_h, k_h)
        p_parts, inv_parts = [], []
        for g in range(ATT_GROUP):
            sink = sink_ref[h * ATT_GROUP + g]
            s = s_all[g * blk:(g + 1) * blk]
            if bias is not None:
                s = s + bias
            m = jnp.maximum(jnp.max(s, axis=-1, keepdims=True), sink)
            p = jnp.exp(s - m)
            denom = jnp.sum(p, axis=-1, keepdims=True) + jnp.exp(sink - m)
            p_parts.append(p.astype(BF16))
            inv_parts.append(1.0 / denom)
        o = _dot(jnp.concatenate(p_parts, axis=0), v_h)
        for g in range(ATT_GROUP):
            outs.append(o[g * blk:(g + 1) * blk] * inv_parts[g])
    return jnp.concatenate(outs, axis=1)


def _attention(qa, ka, va, sink, g_att, seg, with_ctx):
    n = qa.shape[0]
    b_, s_, lc = seg["B"], seg["S"], seg["Lc"]
    blk = ATT_BLOCK
    blocks_per_seq = s_ // blk
    ctx_map = lambda b, j, *_: (b_ * s_ // lc + b, 0)
    ctx_specs = [pl.BlockSpec((lc, ATT_KV_WIDTH), ctx_map)] * 2
    g_spec = pl.BlockSpec((1, ATT_WIDTH), lambda b, j, *_: (0, 0))

    def call(window, blocks, steps, q_map, in_specs, args, aliases):
        grid_spec = pltpu.PrefetchScalarGridSpec(
            num_scalar_prefetch=1,
            grid=(b_, steps),
            in_specs=[pl.BlockSpec((blocks * blk, ATT_WIDTH), q_map)] + in_specs,
            out_specs=pl.BlockSpec((blocks * blk, ATT_WIDTH), q_map))
        return pl.pallas_call(
            functools.partial(_attn_kernel, window=window, blocks=blocks),
            grid_spec=grid_spec,
            out_shape=jax.ShapeDtypeStruct((n, ATT_WIDTH), BF16),
            input_output_aliases=aliases,
            compiler_params=_params(2),
            name="window_attention" if window else "context_attention",
        )(sink, qa, *args)

    qb = min(ATT_STEP_BLOCKS, blocks_per_seq)
    assert blocks_per_seq % qb == 0
    nqs = blocks_per_seq // qb

    def edge_map(off):
        return lambda b, j, *_: (b * blocks_per_seq + jnp.clip(j * qb + off, 0, blocks_per_seq - 1), 0)

    kv_specs = [pl.BlockSpec((blk, ATT_KV_WIDTH), edge_map(-1)),
                pl.BlockSpec((qb * blk, ATT_KV_WIDTH), lambda b, j, *_: (b * nqs + j, 0)),
                pl.BlockSpec((blk, ATT_KV_WIDTH), edge_map(qb))]
    att = call(True, qb, nqs, lambda b, j, *_: (b * nqs + j, 0), kv_specs + kv_specs + ctx_specs + [g_spec],
               (ka, ka, ka, va, va, va, ka, va, g_att), {})
    if with_ctx:
        cb = min(ATT_STEP_BLOCKS, lc // blk)
        assert (lc // blk) % cb == 0 and (b_ * s_) % (cb * blk) == 0
        ncs = lc // (cb * blk)
        first = b_ * s_ // (cb * blk)
        att = call(False, cb, ncs, lambda b, j, *_: (first + b * ncs + j, 0),
                   ctx_specs + [g_spec, pl.BlockSpec(memory_space=pl.ANY)], (ka, va, g_att, att), {5: 0})
    return att


def _mlstm_direction(q_ref, k_ref, v_ref, gc_ref, gr_ref, out_ref, state_ref, tok, *, reverse):
    chunk = ML_CHUNK
    rows = lax.broadcasted_iota(jnp.int32, (chunk, chunk), 0)
    cols = lax.broadcasted_iota(jnp.int32, (chunk, chunk), 1)
    lower = rows >= cols
    upper = rows <= cols
    seen = upper if reverse else lower
    gc = gc_ref[tok, :]
    gr = gr_ref[:, tok]
    b_col = _dot_f32(seen.astype(F32), gc)
    b_row = _dot_f32(gr, (lower if reverse else upper).astype(F32))
    b_end = jnp.sum(gc, axis=0, keepdims=True)
    base = 2 * ML_HEADS if reverse else 0
    pair_width = 2 * ML_QK_DIM
    lane = lax.broadcasted_iota(jnp.int32, (chunk, pair_width), 1)
    state_row = lax.broadcasted_iota(jnp.int32, (pair_width, 1), 0)
    ones = jnp.ones((chunk, ML_V_DIM), BF16)

    for pair in range(ML_HEADS // 2):
        q_pair = q_ref[tok, pair * pair_width:(pair + 1) * pair_width]
        k_pair = k_ref[tok, pair * pair_width:(pair + 1) * pair_width]
        state = state_ref[pair]
        state_bf = state.astype(BF16)
        update = None
        decays = []
        for sub in range(2):
            h = 2 * pair + sub
            i_idx = base + h
            f_idx = base + ML_HEADS + h
            own = (lane >= ML_QK_DIM) if sub else (lane < ML_QK_DIM)
            q_h = jnp.where(own, q_pair, jnp.zeros_like(q_pair))
            vx = jnp.concatenate([v_ref[tok, h * ML_V_DIM:(h + 1) * ML_V_DIM], ones], axis=1)
            bc = b_col[:, f_idx:f_idx + 1]
            d = bc - b_row[f_idx:f_idx + 1, :] + gr[i_idx:i_idx + 1, :]
            w = jnp.exp(jnp.where(seen, d, NEG))
            s = _dot_nt(q_h, k_pair) * w
            tot = _dot(s.astype(BF16), vx) + jnp.exp(bc) * _dot(q_h, state_bf)
            h_out = tot[:, :ML_V_DIM] / jnp.maximum(jnp.abs(tot[:, ML_V_DIM:]), 1.0)
            out_ref[tok, h * ML_V_DIM:(h + 1) * ML_V_DIM] = h_out.astype(out_ref.dtype)

            be = b_end[:, f_idx:f_idx + 1]
            kw = jnp.where(own, k_pair.astype(F32) * jnp.exp(be - bc + gc[:, i_idx:i_idx + 1]), 0.0)
            part = _dot_tn(kw.astype(BF16), vx)
            update = part if update is None else update + part
            decays.append(jnp.exp(be))
        decay = jnp.where(state_row < ML_QK_DIM, decays[0], decays[1])
        state_ref[pair] = decay * state + update


def _mlstm_kernel(qf_ref, kf_ref, vf_ref, gcf_ref, grf_ref, qb_ref, kb_ref, vb_ref, gcb_ref, grb_ref,
                  hf_ref, hb_ref, sf_ref, sb_ref):
    @pl.when(pl.program_id(1) == 0)
    def _():
        sf_ref[...] = jnp.zeros_like(sf_ref)
        sb_ref[...] = jnp.zeros_like(sb_ref)

    n_chunks = qf_ref.shape[0] // ML_CHUNK
    for t in range(n_chunks):
        fwd = slice(t * ML_CHUNK, (t + 1) * ML_CHUNK)
        bwd = slice((n_chunks - 1 - t) * ML_CHUNK, (n_chunks - t) * ML_CHUNK)
        _mlstm_direction(qf_ref, kf_ref, vf_ref, gcf_ref, grf_ref, hf_ref, sf_ref, fwd, reverse=False)
        _mlstm_direction(qb_ref, kb_ref, vb_ref, gcb_ref, grb_ref, hb_ref, sb_ref, bwd, reverse=True)


def _mlstm(qm, km, vm, gc, gr, seg):
    n = qm.shape[0]
    b_, s_, lc = seg["B"], seg["S"], seg["Lc"]
    chunk = ML_STEP_CHUNKS * ML_CHUNK
    assert s_ % chunk == 0 and lc % chunk == 0
    ncc, ncl = lc // chunk, s_ // chunk
    lat_chunks = b_ * ncl

    def blk(b, c, reverse):
        pos = jnp.where(c < ncc, c, c - ncc)
        if reverse:
            pos = jnp.where(c < ncc, ncc, ncl) - 1 - pos
        return jnp.where(c < ncc, lat_chunks + b * ncc + pos, b * ncl + pos)

    def specs(reverse):
        cur = lambda b, c: (blk(b, c, reverse), 0)
        return [pl.BlockSpec((chunk, ML_QK_WIDTH), cur), pl.BlockSpec((chunk, ML_QK_WIDTH), cur),
                pl.BlockSpec((chunk, ML_WIDTH), cur), pl.BlockSpec((chunk, LANES), cur),
                pl.BlockSpec((ML_GATES, chunk), lambda b, c: (0, blk(b, c, reverse)))]

    out = lambda reverse: pl.BlockSpec((chunk, ML_WIDTH), lambda b, c: (blk(b, c, reverse), 0))
    state = pltpu.VMEM((ML_HEADS // 2, 2 * ML_QK_DIM, 2 * ML_V_DIM), F32)
    return pl.pallas_call(
        _mlstm_kernel,
        grid=(b_, ncc + ncl),
        in_specs=specs(False) + specs(True),
        out_specs=[out(False), out(True)],
        out_shape=[jax.ShapeDtypeStruct((n, ML_WIDTH), BF16)] * 2,
        scratch_shapes=[state, state],
        compiler_params=_params(2),
        name="mlstm_scan",
    )(qm, km, vm, gc, gr, qm, km, vm, gc, gr)


def _swiglu(h, w1_ref, w3_ref, w2_ref):
    f = w1_ref.shape[1]
    fc = FFN_CHUNK_COLS
    y = None
    for c in range(f // fc):
        cols = slice(c * fc, (c + 1) * fc)
        a = _dot(h, w1_ref[:, cols])
        b = _dot(h, w3_ref[:, cols])
        part = _dot((a * _sigmoid(a) * b).astype(BF16), w2_ref[cols, :])
        y = part if y is None else y + part
    return y


def _mix_residual(x_ref, mix_refs, mod, wo_ref):
    att_ref, hf_ref, hb_ref, om_ref, gml_ref = mix_refs
    mix = _dot(att_ref[...], wo_ref[:ATT_WIDTH, :])
    for h in range(ML_HEADS):
        sl = slice(h * ML_V_DIM, (h + 1) * ML_V_DIM)
        tot = hf_ref[:, sl].astype(F32) + hb_ref[:, sl].astype(F32)
        ml = _rms(tot, gml_ref[:, sl]) * _sigmoid(om_ref[:, sl].astype(F32))
        mix = mix + _dot(ml.astype(BF16), wo_ref[ATT_WIDTH + h * ML_V_DIM:ATT_WIDTH + (h + 1) * ML_V_DIM, :])
    return x_ref[...] + mod[2:3] * mix


def _mix_specs(tm, row):
    return [pl.BlockSpec((tm, ATT_WIDTH), row)] + [pl.BlockSpec((tm, ML_WIDTH), row)] * 3 \
        + [_resident((1, ML_WIDTH))]


def _dense_layer_kernel(x_ref, att_ref, hf_ref, hb_ref, om_ref, gml_ref, mod_ref, g_ref, wo_ref,
                        w1_ref, w3_ref, w2_ref, fg_ref, *rest, final):
    n_cast = (len(rest) - 1) // 2
    o_ref = rest[n_cast]
    for src, dst in zip(rest[:n_cast], rest[n_cast + 1:]):
        dst[...] = src[...].astype(dst.dtype)
    mod = mod_ref[0]
    x1 = _mix_residual(x_ref, (att_ref, hf_ref, hb_ref, om_ref, gml_ref), mod, wo_ref)
    hx = _adaln(x1, g_ref[...], mod[3:4], mod[4:5]).astype(BF16)
    out = x1 + mod[5:6] * _swiglu(hx, w1_ref, w3_ref, w2_ref)
    if final:
        out = _rms(out, fg_ref[...])
    o_ref[...] = out


def _row_maps(seg, tm):
    nlat = seg["n_lat"] // tm
    s_tiles = seg["S"] // tm
    return (lambda i, *_: (i, 0)), (lambda i, *_: (jnp.where(i < nlat, i // s_tiles, seg["B"]), 0, 0))


def _cast_slabs(stacked, index, n_steps):
    rows_per_layer = int(np.prod(stacked.shape[1:-1]))
    width = stacked.shape[-1]
    slab = next(r for r in range(2 * SUBLANES, rows_per_layer + 1, 2 * SUBLANES)
                if rows_per_layer % r == 0 and rows_per_layer // r <= n_steps)
    n_slabs = rows_per_layer // slab
    src = pl.BlockSpec((slab, width), lambda i: (index * n_slabs + jnp.minimum(i, n_slabs - 1), 0))
    dst = pl.BlockSpec((slab, width), lambda i: (jnp.minimum(i, n_slabs - 1), 0))
    flat = stacked.reshape(-1, width)
    return src, dst, flat, jax.ShapeDtypeStruct((rows_per_layer, width), BF16)


def _dense_layer(xs, mix, mod, g2, wo, layer, w1, w3, w2, ffn_index, final_g, seg, n_rows, final, cast=None):
    d = xs.shape[1]
    tm = seg["tm"]
    row, mod_map = _row_maps(seg, tm)
    n_steps = n_rows // tm
    cast_specs = [_cast_slabs(p, cast[1], n_steps) for p in cast[0]] if cast else []
    outs = pl.pallas_call(
        functools.partial(_dense_layer_kernel, final=final),
        grid=(n_steps,),
        in_specs=[pl.BlockSpec((tm, d), row)] + _mix_specs(tm, row)
                 + [pl.BlockSpec((1, 6, d), mod_map),
                    _resident((1, d)), _layer_resident(wo, layer), _layer_resident(w1, ffn_index),
                    _layer_resident(w3, ffn_index), _layer_resident(w2, ffn_index), _resident((1, d))]
                 + [c[0] for c in cast_specs],
        out_specs=[pl.BlockSpec((tm, d), row)] + [c[1] for c in cast_specs],
        out_shape=[jax.ShapeDtypeStruct((n_rows, d), F32)] + [c[3] for c in cast_specs],
        compiler_params=_params(1),
        name="dense_layer",
    )(xs, *mix, mod, g2, wo, w1, w3, w2, final_g, *[c[2] for c in cast_specs])
    if not cast:
        return outs[0], None
    return outs[0], tuple(o.reshape(p.shape[1:]) for o, p in zip(outs[1:], cast[0]))


def _router_kernel(x_ref, att_ref, hf_ref, hb_ref, om_ref, gml_ref, mod_ref, g_ref, wo_ref, wr_ref, br_ref,
                   earlier_ref,
                   x1_ref, h_ref, route_ref, counts_ref, count_ref):
    mod = mod_ref[0]
    x1 = _mix_residual(x_ref, (att_ref, hf_ref, hb_ref, om_ref, gml_ref), mod, wo_ref)
    x1_ref[...] = x1
    hx = _adaln(x1, g_ref[...], mod[3:4], mod[4:5])
    h_ref[...] = _pack_bf16_pairs(hx)
    hx_hi = hx.astype(BF16)
    hx_lo = (hx - hx_hi.astype(F32)).astype(BF16)
    both = _dot_nt(wr_ref[...], hx_hi)
    logits = (both[:N_EXPERTS] + both[N_EXPERTS:] + _dot_nt(wr_ref[:N_EXPERTS, :], hx_lo)
              + br_ref[...])
    sub = lax.broadcasted_iota(jnp.int32, logits.shape, 0)
    top1 = jnp.max(logits, axis=0, keepdims=True)
    idx1 = jnp.min(jnp.where(logits == top1, sub, N_EXPERTS), axis=0, keepdims=True)
    rest = jnp.where(sub == idx1, -jnp.inf, logits)
    top2 = jnp.max(rest, axis=0, keepdims=True)
    idx2 = jnp.min(jnp.where(rest == top2, sub, N_EXPERTS), axis=0, keepdims=True)
    e2 = jnp.exp(top2 - top1)
    w_first = 1.0 / (1.0 + e2)
    @pl.when(pl.program_id(0) == 0)
    def _():
        count_ref[...] = jnp.zeros_like(count_ref)

    first, second = sub == idx1, sub == idx2
    hot = jnp.logical_or(first, second).astype(F32)
    before = _dot(hot.astype(BF16), earlier_ref[...]) + count_ref[:, :1]
    rank1 = jnp.sum(jnp.where(first, before, 0.0), axis=0, keepdims=True)
    rank2 = jnp.sum(jnp.where(second, before, 0.0), axis=0, keepdims=True)
    count_ref[...] += jnp.sum(hot, axis=1, keepdims=True)
    counts_ref[...] = count_ref[...]
    route = jnp.where(sub == 0, idx1.astype(F32), jnp.where(sub == 1, idx2.astype(F32), 0.0))
    route = jnp.where(sub == 2, w_first, jnp.where(sub == 3, e2 * w_first, route))
    route_ref[...] = jnp.where(sub == 4, rank1, jnp.where(sub == 5, rank2, route))


def _router(xs, mix, mod, g2, wo, layer, wr, br, seg, n_rows):
    d = xs.shape[1]
    tm = seg["tm"]
    row, mod_map = _row_maps(seg, tm)
    return pl.pallas_call(
        _router_kernel,
        grid=(n_rows // tm,),
        in_specs=[pl.BlockSpec((tm, d), row)] + _mix_specs(tm, row)
                 + [pl.BlockSpec((1, 6, d), mod_map),
                    _resident((1, d)), _layer_resident(wo, layer), _resident(wr.shape), _resident(br.shape),
                    _resident((tm, tm))],
        out_specs=[pl.BlockSpec((tm, d), row), pl.BlockSpec((tm, d // 2), row),
                   pl.BlockSpec((N_EXPERTS, tm), lambda i: (0, i)),
                   pl.BlockSpec((N_EXPERTS, LANES), lambda i: (0, 0))],
        out_shape=[jax.ShapeDtypeStruct((n_rows, d), F32), jax.ShapeDtypeStruct((n_rows, d // 2), jnp.int32),
                   jax.ShapeDtypeStruct((N_EXPERTS, n_rows), F32),
                   jax.ShapeDtypeStruct((N_EXPERTS, LANES), F32)],
        scratch_shapes=[pltpu.VMEM((N_EXPERTS, LANES), F32)],
        compiler_params=_params(1),
        name="mix_router",
    )(xs, *mix, mod, g2, wo, wr, br, jnp.triu(jnp.ones((tm, tm), BF16), 1))


def _pack_bf16_pairs(h):
    half = h.shape[1] // 2
    hi = lax.bitcast_convert_type(h[:, :half].astype(BF16).astype(F32), jnp.int32)
    lo = lax.bitcast_convert_type(h[:, half:].astype(BF16).astype(F32), jnp.int32)
    return (hi & jnp.int32(-65536)) | lax.shift_right_logical(lo, 16)


def _unpack_bf16_pairs(p):
    hi = lax.bitcast_convert_type(p & jnp.int32(-65536), F32)
    lo = lax.bitcast_convert_type(lax.shift_left(p, 16), F32)
    return hi, lo


def _route_plan(route, counts, tm):
    n_rows = route.shape[1]
    n_slots = TOP_K * n_rows
    idx1, idx2, rank1, rank2 = (route[r].astype(jnp.int32) for r in (0, 1, 4, 5))
    offs = jnp.concatenate([jnp.zeros((1,), jnp.int32), jnp.cumsum(counts[:, 0].astype(jnp.int32))])
    first_slot = lambda idx: sum(jnp.where(idx == e, offs[e], 0) for e in range(N_EXPERTS))
    slot_a = first_slot(idx1) + rank1
    slot_b = first_slot(idx2) + rank2
    n_tiles = n_slots // tm
    t_start = jnp.arange(n_tiles, dtype=jnp.int32) * tm
    e_first = jnp.sum(offs[None, 1:] <= t_start[:, None], axis=1).astype(jnp.int32)
    base_hi = jnp.minimum(t_start + tm, offs[e_first + 1])
    e_next = jnp.arange(1, N_EXPERTS, dtype=jnp.int32)
    start = offs[1:N_EXPERTS]
    x_tile = jnp.minimum(start // tm, n_tiles - 1)
    x_hi = jnp.where(start % tm != 0, jnp.minimum(offs[2:], (x_tile + 1) * tm), start)
    tiles = jnp.concatenate([t_start // tm, x_tile])
    experts = jnp.concatenate([e_first, e_next])
    lo = jnp.concatenate([t_start, start])
    hi = jnp.concatenate([base_hi, x_hi])
    order = jnp.argsort(tiles * (2 * N_EXPERTS) + experts)
    tiles, experts, lo, hi = tiles[order], experts[order], lo[order], hi[order]
    change = tiles[1:] != tiles[:-1]
    one = jnp.ones((1,), bool)
    first = jnp.concatenate([one, change]).astype(jnp.int32)
    last = jnp.concatenate([change, one]).astype(jnp.int32)
    return slot_a, slot_b, (tiles, experts, lo, hi, first, last)


def _scatter_rows(rows, idx_a, idx_b):
    n_rows, width = rows.shape
    workers = SC_CORES * SC_SUBCORES
    per_worker = n_rows // workers
    assert n_rows % (workers * SC_GATHER_ROWS) == 0
    mesh = plsc.VectorSubcoreMesh(core_axis_name="c", subcore_axis_name="s")

    @functools.partial(
        pl.kernel, mesh=mesh,
        out_type=jax.ShapeDtypeStruct((TOP_K * n_rows, width), rows.dtype),
        scratch_types=[pltpu.VMEM((TOP_K, SC_GATHER_ROWS), jnp.int32),
                       pltpu.VMEM((SC_GATHER_ROWS, width), rows.dtype),
                       pltpu.SemaphoreType.DMA],
        name="scatter_rows")
    def scatter(rows_hbm, idx_a_hbm, idx_b_hbm, out_hbm, idx_v, rows_v, sem):
        base = (lax.axis_index("s") * SC_CORES + lax.axis_index("c")) * per_worker

        @pl.loop(0, per_worker // SC_GATHER_ROWS)
        def _(i):
            off = pl.multiple_of(base + i * SC_GATHER_ROWS, SC_GATHER_ROWS)
            pltpu.sync_copy(idx_a_hbm.at[pl.ds(off, SC_GATHER_ROWS)], idx_v.at[0])
            pltpu.sync_copy(idx_b_hbm.at[pl.ds(off, SC_GATHER_ROWS)], idx_v.at[1])
            pltpu.sync_copy(rows_hbm.at[pl.ds(off, SC_GATHER_ROWS)], rows_v)
            pltpu.async_copy(rows_v, out_hbm.at[idx_v.at[0]], sem).wait()
            pltpu.async_copy(rows_v, out_hbm.at[idx_v.at[1]], sem).wait()

    return scatter(rows, idx_a, idx_b)


def _gather_rows(table, idx):
    n_idx = idx.shape[0]
    width = table.shape[1]
    workers = SC_CORES * SC_SUBCORES
    per_worker = n_idx // workers
    assert n_idx % (workers * SC_GATHER_ROWS) == 0
    mesh = plsc.VectorSubcoreMesh(core_axis_name="c", subcore_axis_name="s")

    @functools.partial(
        pl.kernel, mesh=mesh,
        out_type=jax.ShapeDtypeStruct((n_idx, width), table.dtype),
        scratch_types=[pltpu.VMEM((SC_GATHER_ROWS,), jnp.int32),
                       pltpu.VMEM((SC_GATHER_ROWS, width), table.dtype),
                       pltpu.SemaphoreType.DMA],
        name="gather_rows")
    def gather(table_hbm, idx_hbm, out_hbm, idx_v, rows_v, sem):
        base = (lax.axis_index("s") * SC_CORES + lax.axis_index("c")) * per_worker

        @pl.loop(0, per_worker // SC_GATHER_ROWS)
        def _(i):
            off = pl.multiple_of(base + i * SC_GATHER_ROWS, SC_GATHER_ROWS)
            pltpu.sync_copy(idx_hbm.at[pl.ds(off, SC_GATHER_ROWS)], idx_v)
            pltpu.async_copy(table_hbm.at[idx_v], rows_v, sem).wait()
            pltpu.sync_copy(rows_v, out_hbm.at[pl.ds(off, SC_GATHER_ROWS)])

    return gather(table, idx)


def _experts_kernel(tile_ref, exp_ref, lo_ref, hi_ref, first_ref, last_ref,
                    x_ref, w1_ref, w3_ref, w2_ref, o_ref, acc_ref):
    i = pl.program_id(0)
    tm, half = x_ref.shape
    f = w1_ref.shape[2]
    fc = FFN_CHUNK_COLS

    @pl.when(first_ref[i] == 1)
    def _():
        acc_ref[...] = jnp.zeros_like(acc_ref)

    lo, hi = lo_ref[i], hi_ref[i]

    @pl.when(hi > lo)
    def _():
        x_hi, x_lo = _unpack_bf16_pairs(x_ref[...])
        x_hi, x_lo = x_hi.astype(BF16), x_lo.astype(BF16)
        y = None
        for c in range(f // fc):
            cols = slice(c * fc, (c + 1) * fc)
            a = _dot(x_hi, w1_ref[0, :half, cols]) + _dot(x_lo, w1_ref[0, half:, cols])
            b = _dot(x_hi, w3_ref[0, :half, cols]) + _dot(x_lo, w3_ref[0, half:, cols])
            part = _dot((a * _sigmoid(a) * b).astype(BF16), w2_ref[0, cols, :])
            y = part if y is None else y + part
        rows = tile_ref[i] * tm + lax.broadcasted_iota(jnp.int32, (tm, 1), 0)
        keep = jnp.logical_and(rows >= lo, rows < hi)
        acc_ref[...] += jnp.where(keep, y, 0.0)

    @pl.when(last_ref[i] == 1)
    def _():
        o_ref[...] = _pack_bf16_pairs(acc_ref[...])


def _experts(xs_sorted, items, w1, w3, w2):
    n_slots, half = xs_sorted.shape
    n_exp, d, f = w1.shape
    tm = MOE_ROW_TILE
    assert f % FFN_CHUNK_COLS == 0
    tile_map = lambda i, tiles, *_: (tiles[i], 0)
    exp_map = lambda i, tiles, experts, *_: (experts[i], 0, 0)
    grid_spec = pltpu.PrefetchScalarGridSpec(
        num_scalar_prefetch=len(items),
        grid=(items[0].shape[0],),
        in_specs=[pl.BlockSpec((tm, half), tile_map),
                  pl.BlockSpec((1, d, f), exp_map), pl.BlockSpec((1, d, f), exp_map),
                  pl.BlockSpec((1, f, d), exp_map)],
        out_specs=pl.BlockSpec((tm, half), tile_map),
        scratch_shapes=[pltpu.VMEM((tm, d), F32)],
    )
    return pl.pallas_call(
        _experts_kernel,
        grid_spec=grid_spec,
        out_shape=jax.ShapeDtypeStruct((n_slots, half), jnp.int32),
        compiler_params=_params(1),
        name="experts",
    )(*items, xs_sorted, w1, w3, w2)


def _combine_kernel(x1_ref, ya_ref, yb_ref, route_ref, mod_ref, fg_ref, o_ref, *, final):
    half = ya_ref.shape[1]
    route = route_ref[...]
    padded = jnp.concatenate([route, jnp.zeros((LANES - route.shape[0], route.shape[1]), F32)], axis=0)
    route = jnp.transpose(padded)
    wa, wb = route[:, 2:3], route[:, 3:4]
    a_hi, a_lo = _unpack_bf16_pairs(ya_ref[...])
    b_hi, b_lo = _unpack_bf16_pairs(yb_ref[...])
    gate = mod_ref[0][5:6]
    out_hi = x1_ref[:, :half] + gate[:, :half] * (wa * a_hi + wb * b_hi)
    out_lo = x1_ref[:, half:] + gate[:, half:] * (wa * a_lo + wb * b_lo)
    if final:
        total = jnp.sum(out_hi * out_hi, axis=-1, keepdims=True) + jnp.sum(out_lo * out_lo, axis=-1, keepdims=True)
        scale = lax.rsqrt(total / (2 * half) + EPS)
        out_hi = out_hi * scale * fg_ref[:, :half]
        out_lo = out_lo * scale * fg_ref[:, half:]
    o_ref[:, :half] = out_hi
    o_ref[:, half:] = out_lo


def _combine(x1, y_pairs, route, mod, final_g, seg, final):
    n_rows, d = x1.shape
    tm = seg["tm"]
    row, mod_map = _row_maps(seg, tm)
    second = n_rows // tm
    return pl.pallas_call(
        functools.partial(_combine_kernel, final=final),
        grid=(n_rows // tm,),
        in_specs=[pl.BlockSpec((tm, d), row), pl.BlockSpec((tm, d // 2), row),
                  pl.BlockSpec((tm, d // 2), lambda i: (second + i, 0)),
                  pl.BlockSpec((N_EXPERTS, tm), lambda i: (0, i)), pl.BlockSpec((1, 6, d), mod_map),
                  pl.BlockSpec((1, d), lambda i: (0, 0))],
        out_specs=pl.BlockSpec((tm, d), row),
        out_shape=jax.ShapeDtypeStruct((n_rows, d), F32),
        compiler_params=_params(1),
        name="moe_combine",
    )(x1, y_pairs, y_pairs, route, mod, final_g)


def _moe_layer(xs, mix, mod, g2, wo, layer, wr, br, w1, w3, w2, final_g, seg, n_rows, final):
    x1, h_pairs, route, counts = _router(xs, mix, mod, g2, wo, layer, wr, br, seg, n_rows)
    slot_a, slot_b, items = _route_plan(route, counts, MOE_ROW_TILE)
    y_sorted = _experts(_scatter_rows(h_pairs, slot_a, slot_b), items, w1, w3, w2)
    y_pairs = _gather_rows(y_sorted, jnp.concatenate([slot_a, slot_b]))
    return _combine(x1, y_pairs, route, mod, final_g, seg, final)


def kernel(x, c, ctx, c_ctx, norm1_g, norm2_g, w_mod, b_mod, w_in, conv_w, conv_b, b_gates, attn_sink,
           g_att, g_ml, w_out, ffn_w1, ffn_w3, ffn_w2, w_router, b_router, exp_w1, exp_w3, exp_w2,
           final_g):
    b_, s_, d = x.shape
    lc = ctx.shape[1]
    depth = w_in.shape[0]
    n_lat, n_ctx = b_ * s_, b_ * lc
    tm = min(ROW_TILE, s_)
    assert s_ % tm == 0 and n_ctx % tm == 0
    assert (TOP_K * n_lat) % MOE_ROW_TILE == 0 and (TOP_K * n_ctx) % MOE_ROW_TILE == 0
    assert s_ % ML_CHUNK == 0 and lc % ML_CHUNK == 0 and n_lat % lc == 0 and b_ < MOD_ROWS
    seg = dict(B=b_, S=s_, Lc=lc, n_lat=n_lat, tm=tm)
    tm_in = INPROJ_TILE if s_ % INPROJ_TILE == 0 and n_ctx % INPROJ_TILE == 0 else tm
    seg_in = dict(seg, tm=tm_in)

    cond = jnp.zeros((MOD_ROWS, d), F32).at[:b_].set(c).at[b_].set(c_ctx)
    mods = _modulation(cond, w_mod, b_mod).reshape(depth, MOD_ROWS, 6, d)
    rope = _rope_tables(s_, tm_in)
    xs = (x.reshape(n_lat, d), ctx.reshape(n_ctx, d))
    final_row = final_g.reshape(1, d)

    w_in_b, w_out_b = w_in.astype(BF16), w_out.astype(BF16)
    ffn_b = (ffn_w1.astype(BF16), ffn_w3.astype(BF16), ffn_w2.astype(BF16))
    exp_b = None

    for layer in range(depth):
        last = layer == depth - 1
        w_gates = w_in[layer][:, MAIN_WIDTH:]
        w_gc = jnp.pad(w_gates, ((0, 0), (0, LANES - ML_GATES))).astype(BF16)
        b_gc = jnp.pad(b_gates[layer], (0, LANES - ML_GATES)).reshape(1, LANES)
        qa, ka, va, qm, km, vm, om, gc, gr, *stream = _input_projection(
            xs, mods[layer], norm1_g[layer].reshape(1, d), w_in_b, layer, w_gc,
            w_gates.T.astype(BF16), b_gc, b_gates[layer].reshape(ML_GATES, 1),
            conv_w[layer], conv_b[layer].reshape(1, -1), rope, seg_in)
        if stream:
            xs = stream[0]
        att = _attention(qa, ka, va, attn_sink[layer], g_att[layer].reshape(1, ATT_WIDTH), seg, not last)
        hf, hb = _mlstm(qm, km, vm, gc, gr, seg)
        mix = (att, hf, hb, om, g_ml[layer].reshape(1, ML_WIDTH))
        n_rows = n_lat if last else n_lat + n_ctx
        g2 = norm2_g[layer].reshape(1, d)
        i = layer // 2
        if layer % 2 == 0:
            cast = None if last else ((exp_w1, exp_w3, exp_w2), i)
            xs, exp_b = _dense_layer(xs, mix, mods[layer], g2, w_out_b, layer, *ffn_b, i, final_row, seg,
                                     n_rows, last, cast)
        else:
            wr = w_router[i].T
            wr_hi = wr.astype(BF16)
            wr = jnp.concatenate([wr_hi, (wr - wr_hi.astype(F32)).astype(BF16)], axis=0)
            br = b_router[i].reshape(N_EXPERTS, 1)
            xs = _moe_layer(xs, mix, mods[layer], g2, w_out_b, layer, wr, br, *exp_b, final_row, seg,
                            n_rows, last)
    return xs[:n_lat].reshape(b_, s_, d)
```

```python
import functools

import jax
import jax.numpy as jnp
import numpy as np
from jax import lax
from jax.experimental import pallas as pl
from jax.experimental.pallas import tpu as pltpu
from jax.experimental.pallas import tpu_sc as plsc

F32 = jnp.float32
BF16 = jnp.bfloat16

GRID_W = 64
ATT_HEADS = 8
ATT_KV_HEADS = 2
ATT_HEAD_DIM = 64
ATT_GROUP = ATT_HEADS // ATT_KV_HEADS
WINDOW = 128
ATT_BLOCK = 128
ROPE_THETA = 10000.0
ML_HEADS = 4
ML_QK_DIM = 64
ML_V_DIM = 128
ML_CONV = 5
GATE_CAP = 15.0
ATT_WIDTH = ATT_HEADS * ATT_HEAD_DIM
ATT_KV_WIDTH = ATT_KV_HEADS * ATT_HEAD_DIM
ML_QK_WIDTH = ML_HEADS * ML_QK_DIM
ML_WIDTH = ML_HEADS * ML_V_DIM
ML_GATES = 4 * ML_HEADS
MAIN_WIDTH = ATT_WIDTH + 2 * ATT_KV_WIDTH + 2 * ML_QK_WIDTH + 2 * ML_WIDTH
N_EXPERTS = 8
TOP_K = 2
EPS = 1e-6

LANES = 128
SUBLANES = 8
VMEM_LIMIT = 56 * 1024 * 1024
NEG = -1e30
SC_CORES = 2
SC_SUBCORES = 16
SC_GATHER_ROWS = 64
SC_IN_FLIGHT = 2

ROW_TILE = 512
INPROJ_TILE = 1024
MOE_ROW_TILE = 512
FFN_CHUNK_COLS = 256
ATT_STEP_BLOCKS = 8
ML_CHUNK = 128
ML_STEP_CHUNKS = 2
CONV_HALO = SUBLANES
MOD_ROWS = 16
MOD_COL_TILE = 1536


def _dot(a, b):
    return jnp.dot(a, b, preferred_element_type=F32)


def _dot_nt(a, b):
    return lax.dot_general(a, b, (((1,), (1,)), ((), ())), preferred_element_type=F32)


def _dot_tn(a, b):
    return lax.dot_general(a, b, (((0,), (0,)), ((), ())), preferred_element_type=F32)


def _dot_f32(a, b):
    return jnp.dot(a, b, preferred_element_type=F32, precision=lax.Precision.HIGHEST)


def _sigmoid(x):
    return 1.0 / (1.0 + jnp.exp(-x))


def _rms(x, g):
    return x * lax.rsqrt(jnp.mean(x * x, axis=-1, keepdims=True) + EPS) * g


def _adaln(x, g, shift, scale):
    return _rms(x, g) * (1.0 + scale) + shift


def _params(n_axes):
    return pltpu.CompilerParams(dimension_semantics=("arbitrary",) * n_axes,
                                vmem_limit_bytes=VMEM_LIMIT)


def _resident(shape):
    zeros = (0,) * len(shape)
    return pl.BlockSpec(shape, lambda *_: zeros, pipeline_mode=pl.Buffered(1))


def _layer_resident(stacked, layer, block=None):
    block = tuple(stacked.shape[1:]) if block is None else block
    index = (layer,) + (0,) * len(block)
    return pl.BlockSpec((None,) + block, lambda *_: index, pipeline_mode=pl.Buffered(1))


def _mod_kernel(c_ref, w_ref, b_ref, o_ref):
    c = c_ref[...]
    a = (c * _sigmoid(c)).astype(BF16)
    o_ref[0] = _dot(a, w_ref[0].astype(BF16)) + b_ref[0]


def _modulation(cond, w_mod, b_mod):
    depth, d, width = w_mod.shape
    tn = MOD_COL_TILE if width % MOD_COL_TILE == 0 else width
    return pl.pallas_call(
        _mod_kernel,
        grid=(depth, width // tn),
        in_specs=[pl.BlockSpec((MOD_ROWS, d), lambda l, j: (0, 0)),
                  pl.BlockSpec((1, d, tn), lambda l, j: (l, 0, j)),
                  pl.BlockSpec((1, 1, tn), lambda l, j: (l, 0, j))],
        out_specs=pl.BlockSpec((1, MOD_ROWS, tn), lambda l, j: (l, 0, j)),
        out_shape=jax.ShapeDtypeStruct((depth, MOD_ROWS, width), F32),
        compiler_params=_params(2),
        name="modulation",
    )(cond, w_mod, b_mod.reshape(depth, 1, width))


def _gate_act(u, is_forget):
    g = GATE_CAP * jnp.tanh(u / GATE_CAP)
    log_sig = jnp.minimum(g, 0.0) - jnp.log1p(jnp.exp(-jnp.abs(g)))
    return jnp.where(is_forget, log_sig, g)


def _conv_silu(xe, cw, cb, rows):
    n_ext = xe.shape[0]
    mid = ML_CONV // 2
    y = cb + cw[mid:mid + 1] * xe[CONV_HALO:CONV_HALO + rows]
    for tap in range(ML_CONV):
        if tap != mid:
            y = y + cw[tap:tap + 1] * pltpu.roll(xe, (mid - tap) % n_ext, 0)[CONV_HALO:CONV_HALO + rows]
    return y * _sigmoid(y)


def _inproj_kernel(*refs, n_lat_tiles, seq_lat, seq_ctx, split):
    n_x = 6 if split else 3
    x_refs, refs = refs[:n_x], refs[n_x:]
    (mod_ref, g_ref, w_ref, wgc_ref, wgr_ref, bgc_ref, bgr_ref, cw_ref, cb_ref, cos_ref, sa_ref, sb_ref,
     qa_ref, ka_ref, va_ref, qm_ref, km_ref, vm_ref, om_ref, gc_ref, gr_ref) = refs[:21]
    i = pl.program_id(0)
    if split:
        in_lat = i < n_lat_tiles
        x_tile, x_prev, x_next = (jnp.where(in_lat, x_refs[2 * k][...], x_refs[2 * k + 1][...]) for k in range(3))
        refs[21][...] = x_tile
    else:
        x_tile, x_prev, x_next = (r[...] for r in x_refs)
    mod = mod_ref[0]
    tm = x_tile.shape[0]
    seg = min(tm, seq_ctx)
    n_seg = tm // seg
    norm_gain = g_ref[...]
    normed = lambda rows: _adaln(rows, norm_gain, mod[0:1], mod[1:2]).astype(BF16)
    quarter = ATT_HEAD_DIM // 4

    def rope(u, rows):
        return (u * cos_ref[rows, :] + pltpu.roll(u, LANES - quarter, 1) * sa_ref[rows, :]
                + pltpu.roll(u, quarter, 1) * sb_ref[rows, :])

    c_kv = ATT_WIDTH
    c_qk = c_kv + 2 * ATT_KV_WIDTH
    c_vm = c_qk + 2 * ML_QK_WIDTH
    c_om = c_vm + ML_WIDTH
    w_qk = w_ref[:, c_qk:c_vm]
    seq_len = jnp.where(i < n_lat_tiles, seq_lat, seq_ctx)
    cw, cb = cw_ref[...], cb_ref[...]
    hx = [normed(x_tile[j * seg:(j + 1) * seg, :]) for j in range(n_seg)]
    qk = [_dot(h, w_qk) for h in hx]
    qk_prev = _dot(normed(x_prev), w_qk)
    qk_next = _dot(normed(x_next), w_qk)
    for j in range(n_seg):
        rows = slice(j * seg, (j + 1) * seg)
        first_row = i * tm + j * seg
        has_prev = (lax.rem(first_row, seq_len) != 0).astype(F32)
        has_next = (lax.rem(first_row + seg, seq_len) != 0).astype(F32)
        prev = qk_prev if j == 0 else qk[j - 1][seg - CONV_HALO:]
        nxt = qk_next if j == n_seg - 1 else qk[j + 1][:CONV_HALO]
        xe = jnp.concatenate([prev * has_prev, qk[j], nxt * has_next], axis=0)
        y = _conv_silu(xe, cw, cb, seg)
        qm_ref[rows, :] = (y[:, :ML_QK_WIDTH] * (ML_QK_DIM ** -0.5)).astype(BF16)
        km_ref[rows, :] = y[:, ML_QK_WIDTH:].astype(BF16)
        q = _dot(hx[j], w_ref[:, :c_kv])
        for c in range(ATT_WIDTH // LANES):
            sl = slice(c * LANES, (c + 1) * LANES)
            qa_ref[rows, sl] = (rope(q[:, sl], rows) * (ATT_HEAD_DIM ** -0.5)).astype(BF16)
        kv = _dot(hx[j], w_ref[:, c_kv:c_qk])
        ka_ref[rows, :] = rope(kv[:, :ATT_KV_WIDTH], rows).astype(BF16)
        va_ref[rows, :] = kv[:, ATT_KV_WIDTH:].astype(BF16)
        vm_ref[rows, :] = _dot(hx[j], w_ref[:, c_vm:c_om]).astype(BF16)
        om_ref[rows, :] = _dot(hx[j], w_ref[:, c_om:c_om + ML_WIDTH]).astype(BF16)
        gc = _dot(hx[j], wgc_ref[...]) + bgc_ref[...]
        lane = lax.broadcasted_iota(jnp.int32, gc.shape, 1)
        gc_ref[rows, :] = _gate_act(gc, (lane // ML_HEADS) % 2 == 1)
        gr = _dot_nt(wgr_ref[...], hx[j]) + bgr_ref[...]
        sub = lax.broadcasted_iota(jnp.int32, gr.shape, 0)
        gr_ref[:, rows] = _gate_act(gr, (sub // ML_HEADS) % 2 == 1)


def _input_projection(xs, mod, g1, w_in, layer, w_gc, w_gr, b_gc, b_gr, conv_w, conv_b, rope, seg):
    split = isinstance(xs, tuple)
    d = (xs[0] if split else xs).shape[1]
    tm = seg["tm"]
    nlat = seg["n_lat"] // tm
    n = sum(a.shape[0] for a in xs) if split else xs.shape[0]
    s_tiles = seg["S"] // tm
    halos_per_tile = tm // CONV_HALO
    row = lambda i: (i, 0)

    def x_specs(n_rows, first_tile):
        n_tiles = n_rows // tm
        t = lambda i: jnp.clip(i - first_tile, 0, n_tiles - 1)
        return [pl.BlockSpec((tm, d), lambda i: (t(i), 0)),
                pl.BlockSpec((CONV_HALO, d), lambda i: (jnp.maximum(t(i) * halos_per_tile - 1, 0), 0)),
                pl.BlockSpec((CONV_HALO, d),
                             lambda i: (jnp.minimum((t(i) + 1) * halos_per_tile, n_rows // CONV_HALO - 1), 0))]

    if split:
        lat, ctx = x_specs(xs[0].shape[0], 0), x_specs(xs[1].shape[0], nlat)
        x_in_specs = [s for pair in zip(lat, ctx) for s in pair]
        x_args = [xs[0], xs[1]] * 3
    else:
        x_in_specs, x_args = x_specs(n, 0), [xs] * 3
    mod_map = lambda i: (jnp.where(i < nlat, i // s_tiles, seg["B"]), 0, 0)
    rope_map = lambda i: (jnp.where(i < nlat, i % s_tiles, s_tiles), 0)
    widths = [(ATT_WIDTH, BF16), (ATT_KV_WIDTH, BF16), (ATT_KV_WIDTH, BF16), (ML_QK_WIDTH, BF16),
              (ML_QK_WIDTH, BF16), (ML_WIDTH, BF16), (ML_WIDTH, BF16), (LANES, F32)]
    out_shape = [jax.ShapeDtypeStruct((n, w), t) for w, t in widths]
    out_specs = [pl.BlockSpec((tm, w), row) for w, _ in widths]
    out_shape.append(jax.ShapeDtypeStruct((ML_GATES, n), F32))
    out_specs.append(pl.BlockSpec((ML_GATES, tm), lambda i: (0, i)))
    if split:
        out_shape.append(jax.ShapeDtypeStruct((n, d), F32))
        out_specs.append(pl.BlockSpec((tm, d), row))
    return pl.pallas_call(
        functools.partial(_inproj_kernel, n_lat_tiles=nlat, seq_lat=seg["S"], seq_ctx=seg["Lc"], split=split),
        grid=(n // tm,),
        in_specs=x_in_specs
                 + [pl.BlockSpec((1, 6, d), mod_map),
                  _resident((1, d)),
                  _layer_resident(w_in, layer, (d, MAIN_WIDTH)), _resident(w_gc.shape), _resident(w_gr.shape),
                  _resident(b_gc.shape), _resident(b_gr.shape),
                  _resident(conv_w.shape), _resident(conv_b.shape),
                  pl.BlockSpec((tm, LANES), rope_map),
                  pl.BlockSpec((tm, LANES), rope_map),
                  pl.BlockSpec((tm, LANES), rope_map)],
        out_specs=out_specs,
        out_shape=out_shape,
        compiler_params=_params(1),
        name="input_projection",
    )(*x_args, mod, g1, w_in, w_gc, w_gr, b_gc, b_gr, conv_w, conv_b, *rope)


def _rope_tables(s, tm):
    quarter = ATT_HEAD_DIM // 4
    t = jnp.arange(s)
    row = (t // GRID_W).astype(F32)
    col = (t % GRID_W).astype(F32)
    inv = ROPE_THETA ** (-jnp.arange(quarter, dtype=F32) / quarter)
    ang_r = row[:, None] * inv[None, :]
    ang_c = col[:, None] * inv[None, :]
    zero = jnp.zeros_like(ang_r)
    cos = jnp.concatenate([jnp.cos(ang_r)] * 2 + [jnp.cos(ang_c)] * 2, axis=1)
    sin_up = jnp.concatenate([-jnp.sin(ang_r), zero, -jnp.sin(ang_c), zero], axis=1)
    sin_dn = jnp.concatenate([zero, jnp.sin(ang_r), zero, jnp.sin(ang_c)], axis=1)
    reps = LANES // ATT_HEAD_DIM
    ident = [jnp.ones((tm, LANES), F32), jnp.zeros((tm, LANES), F32), jnp.zeros((tm, LANES), F32)]
    return tuple(jnp.concatenate([jnp.tile(a, (1, reps)), i], axis=0)
                 for a, i in zip((cos, sin_up, sin_dn), ident))


def _attn_kernel(sink_ref, *refs, window, blocks):
    blk = ATT_BLOCK
    if window:
        q_ref, kp_ref, kc_ref, kn_ref, vp_ref, vc_ref, vn_ref, kx_ref, vx_ref, g_ref, o_ref = refs
        j = pl.program_id(1)
        k_own, v_own = kc_ref[...], vc_ref[...]
        k_blocks = [kp_ref[...]] + [k_own[t * blk:(t + 1) * blk] for t in range(blocks)] + [kn_ref[...]]
        v_blocks = [vp_ref[...]] + [v_own[t * blk:(t + 1) * blk] for t in range(blocks)] + [vn_ref[...]]
    else:
        q_ref, kx_ref, vx_ref, g_ref, o_ref = refs
    for t in range(blocks):
        if window:
            has_prev = j > 0 if t == 0 else True
            has_next = j < pl.num_programs(1) - 1 if t == blocks - 1 else True
            win = (k_blocks[t:t + 3], v_blocks[t:t + 3], has_prev, has_next)
        else:
            win = None
        att = _attend_block(sink_ref, q_ref[t * blk:(t + 1) * blk, :], win, kx_ref[...], vx_ref[...])
        o_ref[t * blk:(t + 1) * blk, :] = _rms(att, g_ref[...]).astype(o_ref.dtype)


def _attend_block(sink_ref, q, win, k_ctx, v_ctx):
    blk = ATT_BLOCK
    if win is None:
        k_all, v_all, bias = k_ctx, v_ctx, None
    else:
        k_win, v_win, has_prev, has_next = win
        rows = lax.broadcasted_iota(jnp.int32, (blk, blk), 0)
        cols = lax.broadcasted_iota(jnp.int32, (blk, blk), 1)
        ok_p = jnp.logical_and(cols >= rows, has_prev)
        ok_n = jnp.logical_and(cols <= rows, has_next)
        bias = jnp.concatenate([jnp.where(ok_p, 0.0, NEG), jnp.zeros((blk, blk), F32),
                                jnp.where(ok_n, 0.0, NEG), jnp.zeros((blk, k_ctx.shape[0]), F32)], axis=1)
        k_all = jnp.concatenate(list(k_win) + [k_ctx], axis=0)
        v_all = jnp.concatenate(list(v_win) + [v_ctx], axis=0)
    dh = ATT_HEAD_DIM
    outs = []
    for h in range(ATT_KV_HEADS):
        k_h = k_all[:, h * dh:(h + 1) * dh]
        v_h = v_all[:, h * dh:(h + 1) * dh]
        q_h = jnp.concatenate([q[:, (h * ATT_GROUP + g) * dh:(h * ATT_GROUP + g + 1) * dh]
                               for g in range(ATT_GROUP)], axis=0)
        s_all = _dot_nt(q_h, k_h)
        p_parts, inv_parts = [], []
        for g in range(ATT_GROUP):
            sink = sink_ref[h * ATT_GROUP + g]
            s = s_all[g * blk:(g + 1) * blk]
            if bias is not None:
                s = s + bias
            m = jnp.maximum(jnp.max(s, axis=-1, keepdims=True), sink)
            p = jnp.exp(s - m)
            denom = jnp.sum(p, axis=-1, keepdims=True) + jnp.exp(sink - m)
            p_parts.append(p.astype(BF16))
            inv_parts.append(1.0 / denom)
        o = _dot(jnp.concatenate(p_parts, axis=0), v_h)
        for g in range(ATT_GROUP):
            outs.append(o[g * blk:(g + 1) * blk] * inv_parts[g])
    return jnp.concatenate(outs, axis=1)


def _attention(qa, ka, va, sink, g_att, seg, with_ctx):
    n = qa.shape[0]
    b_, s_, lc = seg["B"], seg["S"], seg["Lc"]
    blk = ATT_BLOCK
    blocks_per_seq = s_ // blk
    ctx_map = lambda b, j, *_: (b_ * s_ // lc + b, 0)
    ctx_specs = [pl.BlockSpec((lc, ATT_KV_WIDTH), ctx_map)] * 2
    g_spec = pl.BlockSpec((1, ATT_WIDTH), lambda b, j, *_: (0, 0))

    def call(window, blocks, steps, first_step, in_specs, args):
        grid_spec = pltpu.PrefetchScalarGridSpec(
            num_scalar_prefetch=1,
            grid=(b_, steps),
            in_specs=[pl.BlockSpec((blocks * blk, ATT_WIDTH),
                                   lambda b, j, *_: (first_step + b * steps + j, 0))] + in_specs,
            out_specs=pl.BlockSpec((blocks * blk, ATT_WIDTH), lambda b, j, *_: (b * steps + j, 0)))
        return pl.pallas_call(
            functools.partial(_attn_kernel, window=window, blocks=blocks),
            grid_spec=grid_spec,
            out_shape=jax.ShapeDtypeStruct((b_ * steps * blocks * blk, ATT_WIDTH), BF16),
            compiler_params=_params(2),
            name="window_attention" if window else "context_attention",
        )(sink, qa, *args)

    qb = min(ATT_STEP_BLOCKS, blocks_per_seq)
    assert blocks_per_seq % qb == 0
    nqs = blocks_per_seq // qb

    def edge_map(off):
        return lambda b, j, *_: (b * blocks_per_seq + jnp.clip(j * qb + off, 0, blocks_per_seq - 1), 0)

    kv_specs = [pl.BlockSpec((blk, ATT_KV_WIDTH), edge_map(-1)),
                pl.BlockSpec((qb * blk, ATT_KV_WIDTH), lambda b, j, *_: (b * nqs + j, 0)),
                pl.BlockSpec((blk, ATT_KV_WIDTH), edge_map(qb))]
    att_lat = call(True, qb, nqs, 0, kv_specs + kv_specs + ctx_specs + [g_spec],
                   (ka, ka, ka, va, va, va, ka, va, g_att))
    if not with_ctx:
        return att_lat, att_lat
    cb = min(ATT_STEP_BLOCKS, lc // blk)
    assert (lc // blk) % cb == 0 and (b_ * s_) % (cb * blk) == 0
    att_ctx = call(False, cb, lc // (cb * blk), b_ * s_ // (cb * blk), ctx_specs + [g_spec], (ka, va, g_att))
    return att_lat, att_ctx


def _mlstm_direction(q_ref, k_ref, v_ref, gc_ref, gr_ref, out_ref, state_ref, tok, *, reverse):
    chunk = ML_CHUNK
    rows = lax.broadcasted_iota(jnp.int32, (chunk, chunk), 0)
    cols = lax.broadcasted_iota(jnp.int32, (chunk, chunk), 1)
    lower = rows >= cols
    upper = rows <= cols
    seen = upper if reverse else lower
    gc = gc_ref[tok, :]
    gr = gr_ref[:, tok]
    b_col = _dot_f32(seen.astype(F32), gc)
    b_row = _dot_f32(gr, (lower if reverse else upper).astype(F32))
    b_end = jnp.sum(gc, axis=0, keepdims=True)
    base = 2 * ML_HEADS if reverse else 0
    pair_width = 2 * ML_QK_DIM
    lane = lax.broadcasted_iota(jnp.int32, (chunk, pair_width), 1)
    state_row = lax.broadcasted_iota(jnp.int32, (pair_width, 1), 0)
    ones = jnp.ones((chunk, ML_V_DIM), BF16)

    for pair in range(ML_HEADS // 2):
        q_pair = q_ref[tok, pair * pair_width:(pair + 1) * pair_width]
        k_pair = k_ref[tok, pair * pair_width:(pair + 1) * pair_width]
        state = state_ref[pair]
        state_bf = state.astype(BF16)
        update = None
        decays = []
        for sub in range(2):
            h = 2 * pair + sub
            i_idx = base + h
            f_idx = base + ML_HEADS + h
            own = (lane >= ML_QK_DIM) if sub else (lane < ML_QK_DIM)
            q_h = jnp.where(own, q_pair, jnp.zeros_like(q_pair))
            vx = jnp.concatenate([v_ref[tok, h * ML_V_DIM:(h + 1) * ML_V_DIM], ones], axis=1)
            bc = b_col[:, f_idx:f_idx + 1]
            d = bc - b_row[f_idx:f_idx + 1, :] + gr[i_idx:i_idx + 1, :]
            w = jnp.exp(jnp.where(seen, d, NEG))
            s = _dot_nt(q_h, k_pair) * w
            tot = _dot(s.astype(BF16), vx) + jnp.exp(bc) * _dot(q_h, state_bf)
            h_out = tot[:, :ML_V_DIM] / jnp.maximum(jnp.abs(tot[:, ML_V_DIM:]), 1.0)
            out_ref[tok, h * ML_V_DIM:(h + 1) * ML_V_DIM] = h_out.astype(out_ref.dtype)

            be = b_end[:, f_idx:f_idx + 1]
            kw = jnp.where(own, k_pair.astype(F32) * jnp.exp(be - bc + gc[:, i_idx:i_idx + 1]), 0.0)
            part = _dot_tn(kw.astype(BF16), vx)
            update = part if update is None else update + part
            decays.append(jnp.exp(be))
        decay = jnp.where(state_row < ML_QK_DIM, decays[0], decays[1])
        state_ref[pair] = decay * state + update


def _mlstm_kernel(qf_ref, kf_ref, vf_ref, gcf_ref, grf_ref, qb_ref, kb_ref, vb_ref, gcb_ref, grb_ref,
                  hf_ref, hb_ref, sf_ref, sb_ref):
    @pl.when(pl.program_id(1) == 0)
    def _():
        sf_ref[...] = jnp.zeros_like(sf_ref)
        sb_ref[...] = jnp.zeros_like(sb_ref)

    n_chunks = qf_ref.shape[0] // ML_CHUNK
    for t in range(n_chunks):
        fwd = slice(t * ML_CHUNK, (t + 1) * ML_CHUNK)
        bwd = slice((n_chunks - 1 - t) * ML_CHUNK, (n_chunks - t) * ML_CHUNK)
        _mlstm_direction(qf_ref, kf_ref, vf_ref, gcf_ref, grf_ref, hf_ref, sf_ref, fwd, reverse=False)
        _mlstm_direction(qb_ref, kb_ref, vb_ref, gcb_ref, grb_ref, hb_ref, sb_ref, bwd, reverse=True)


def _mlstm(qm, km, vm, gc, gr, seg):
    n = qm.shape[0]
    b_, s_, lc = seg["B"], seg["S"], seg["Lc"]
    chunk = ML_STEP_CHUNKS * ML_CHUNK
    assert s_ % chunk == 0 and lc % chunk == 0
    ncc, ncl = lc // chunk, s_ // chunk
    lat_chunks = b_ * ncl

    def blk(b, c, reverse):
        pos = jnp.where(c < ncc, c, c - ncc)
        if reverse:
            pos = jnp.where(c < ncc, ncc, ncl) - 1 - pos
        return jnp.where(c < ncc, lat_chunks + b * ncc + pos, b * ncl + pos)

    def specs(reverse):
        cur = lambda b, c: (blk(b, c, reverse), 0)
        return [pl.BlockSpec((chunk, ML_QK_WIDTH), cur), pl.BlockSpec((chunk, ML_QK_WIDTH), cur),
                pl.BlockSpec((chunk, ML_WIDTH), cur), pl.BlockSpec((chunk, LANES), cur),
                pl.BlockSpec((ML_GATES, chunk), lambda b, c: (0, blk(b, c, reverse)))]

    out = lambda reverse: pl.BlockSpec((chunk, ML_WIDTH), lambda b, c: (blk(b, c, reverse), 0))
    state = pltpu.VMEM((ML_HEADS // 2, 2 * ML_QK_DIM, 2 * ML_V_DIM), F32)
    return pl.pallas_call(
        _mlstm_kernel,
        grid=(b_, ncc + ncl),
        in_specs=specs(False) + specs(True),
        out_specs=[out(False), out(True)],
        out_shape=[jax.ShapeDtypeStruct((n, ML_WIDTH), BF16)] * 2,
        scratch_shapes=[state, state],
        compiler_params=_params(2),
        name="mlstm_scan",
    )(qm, km, vm, gc, gr, qm, km, vm, gc, gr)


def _swiglu(h, w1_ref, w3_ref, w2_ref):
    f = w1_ref.shape[1]
    fc = FFN_CHUNK_COLS
    y = None
    for c in range(f // fc):
        cols = slice(c * fc, (c + 1) * fc)
        a = _dot(h, w1_ref[:, cols])
        b = _dot(h, w3_ref[:, cols])
        part = _dot((a * _sigmoid(a) * b).astype(BF16), w2_ref[cols, :])
        y = part if y is None else y + part
    return y


def _mix_residual(x_ref, mix_refs, mod, wo_ref):
    att_lat_ref, att_ctx_ref, hf_ref, hb_ref, om_ref, gml_ref, n_lat_tiles = mix_refs
    att = jnp.where(pl.program_id(0) < n_lat_tiles, att_lat_ref[...], att_ctx_ref[...])
    mix = _dot(att, wo_ref[:ATT_WIDTH, :])
    for h in range(ML_HEADS):
        sl = slice(h * ML_V_DIM, (h + 1) * ML_V_DIM)
        tot = hf_ref[:, sl].astype(F32) + hb_ref[:, sl].astype(F32)
        ml = _rms(tot, gml_ref[:, sl]) * _sigmoid(om_ref[:, sl].astype(F32))
        mix = mix + _dot(ml.astype(BF16), wo_ref[ATT_WIDTH + h * ML_V_DIM:ATT_WIDTH + (h + 1) * ML_V_DIM, :])
    return x_ref[...] + mod[2:3] * mix


def _mix_specs(tm, row, mix, n_lat):
    lat_tiles = n_lat // tm
    ctx_tiles = mix[1].shape[0] // tm
    return [pl.BlockSpec((tm, ATT_WIDTH), lambda i, *_: (jnp.minimum(i, lat_tiles - 1), 0)),
            pl.BlockSpec((tm, ATT_WIDTH), lambda i, *_: (jnp.clip(i - lat_tiles, 0, ctx_tiles - 1), 0))] \
        + [pl.BlockSpec((tm, ML_WIDTH), row)] * 3 + [_resident((1, ML_WIDTH))]


def _dense_layer_kernel(x_ref, att_ref, attc_ref, hf_ref, hb_ref, om_ref, gml_ref, mod_ref, g_ref, wo_ref,
                        w1_ref, w3_ref, w2_ref, fg_ref, *rest, final, n_lat_tiles):
    n_cast = (len(rest) - 1) // 2
    o_ref = rest[n_cast]
    for src, dst in zip(rest[:n_cast], rest[n_cast + 1:]):
        dst[...] = src[...].astype(dst.dtype)
    mod = mod_ref[0]
    x1 = _mix_residual(x_ref, (att_ref, attc_ref, hf_ref, hb_ref, om_ref, gml_ref, n_lat_tiles), mod, wo_ref)
    hx = _adaln(x1, g_ref[...], mod[3:4], mod[4:5]).astype(BF16)
    out = x1 + mod[5:6] * _swiglu(hx, w1_ref, w3_ref, w2_ref)
    if final:
        out = _rms(out, fg_ref[...])
    o_ref[...] = out


def _row_maps(seg, tm):
    nlat = seg["n_lat"] // tm
    s_tiles = seg["S"] // tm
    return (lambda i, *_: (i, 0)), (lambda i, *_: (jnp.where(i < nlat, i // s_tiles, seg["B"]), 0, 0))


def _cast_slabs(stacked, index, n_steps):
    rows_per_layer = int(np.prod(stacked.shape[1:-1]))
    width = stacked.shape[-1]
    slab = next(r for r in range(2 * SUBLANES, rows_per_layer + 1, 2 * SUBLANES)
                if rows_per_layer % r == 0 and rows_per_layer // r <= n_steps)
    n_slabs = rows_per_layer // slab
    src = pl.BlockSpec((slab, width), lambda i: (index * n_slabs + jnp.minimum(i, n_slabs - 1), 0))
    dst = pl.BlockSpec((slab, width), lambda i: (jnp.minimum(i, n_slabs - 1), 0))
    flat = stacked.reshape(-1, width)
    return src, dst, flat, jax.ShapeDtypeStruct((rows_per_layer, width), BF16)


def _dense_layer(xs, mix, mod, g2, wo, layer, w1, w3, w2, ffn_index, final_g, seg, n_rows, final, cast=None):
    d = xs.shape[1]
    tm = seg["tm"]
    row, mod_map = _row_maps(seg, tm)
    n_steps = n_rows // tm
    cast_specs = [_cast_slabs(p, cast[1], n_steps) for p in cast[0]] if cast else []
    outs = pl.pallas_call(
        functools.partial(_dense_layer_kernel, final=final, n_lat_tiles=seg["n_lat"] // tm),
        grid=(n_steps,),
        in_specs=[pl.BlockSpec((tm, d), row)] + _mix_specs(tm, row, mix, seg["n_lat"])
                 + [pl.BlockSpec((1, 6, d), mod_map),
                    _resident((1, d)), _layer_resident(wo, layer), _layer_resident(w1, ffn_index),
                    _layer_resident(w3, ffn_index), _layer_resident(w2, ffn_index), _resident((1, d))]
                 + [c[0] for c in cast_specs],
        out_specs=[pl.BlockSpec((tm, d), row)] + [c[1] for c in cast_specs],
        out_shape=[jax.ShapeDtypeStruct((n_rows, d), F32)] + [c[3] for c in cast_specs],
        compiler_params=_params(1),
        name="dense_layer",
    )(xs, *mix, mod, g2, wo, w1, w3, w2, final_g, *[c[2] for c in cast_specs])
    if not cast:
        return outs[0], None
    return outs[0], tuple(o.reshape(p.shape[1:]) for o, p in zip(outs[1:], cast[0]))


def _router_kernel(x_ref, att_ref, attc_ref, hf_ref, hb_ref, om_ref, gml_ref, mod_ref, g_ref, wo_ref, wr_ref,
                   br_ref, earlier_ref,
                   x1_ref, h_ref, route_ref, counts_ref, count_ref, *, n_lat_tiles):
    mod = mod_ref[0]
    x1 = _mix_residual(x_ref, (att_ref, attc_ref, hf_ref, hb_ref, om_ref, gml_ref, n_lat_tiles), mod, wo_ref)
    x1_ref[...] = x1
    hx = _adaln(x1, g_ref[...], mod[3:4], mod[4:5])
    h_ref[...] = _pack_bf16_pairs(hx)
    hx_hi = hx.astype(BF16)
    hx_lo = (hx - hx_hi.astype(F32)).astype(BF16)
    both = _dot_nt(wr_ref[...], hx_hi)
    logits = (both[:N_EXPERTS] + both[N_EXPERTS:] + _dot_nt(wr_ref[:N_EXPERTS, :], hx_lo)
              + br_ref[...])
    sub = lax.broadcasted_iota(jnp.int32, logits.shape, 0)
    top1 = jnp.max(logits, axis=0, keepdims=True)
    idx1 = jnp.min(jnp.where(logits == top1, sub, N_EXPERTS), axis=0, keepdims=True)
    rest = jnp.where(sub == idx1, -jnp.inf, logits)
    top2 = jnp.max(rest, axis=0, keepdims=True)
    idx2 = jnp.min(jnp.where(rest == top2, sub, N_EXPERTS), axis=0, keepdims=True)
    e2 = jnp.exp(top2 - top1)
    w_first = 1.0 / (1.0 + e2)
    @pl.when(pl.program_id(0) == 0)
    def _():
        count_ref[...] = jnp.zeros_like(count_ref)

    first, second = sub == idx1, sub == idx2
    hot = jnp.logical_or(first, second).astype(F32)
    before = _dot(hot.astype(BF16), earlier_ref[...]) + count_ref[:, :1]
    rank1 = jnp.sum(jnp.where(first, before, 0.0), axis=0, keepdims=True)
    rank2 = jnp.sum(jnp.where(second, before, 0.0), axis=0, keepdims=True)
    count_ref[...] += jnp.sum(hot, axis=1, keepdims=True)
    counts_ref[...] = count_ref[...]
    route = jnp.where(sub == 0, idx1.astype(F32), jnp.where(sub == 1, idx2.astype(F32), 0.0))
    route = jnp.where(sub == 2, w_first, jnp.where(sub == 3, e2 * w_first, route))
    route_ref[...] = jnp.where(sub == 4, rank1, jnp.where(sub == 5, rank2, route))


def _router(xs, mix, mod, g2, wo, layer, wr, br, seg, n_rows):
    d = xs.shape[1]
    tm = seg["tm"]
    row, mod_map = _row_maps(seg, tm)
    return pl.pallas_call(
        functools.partial(_router_kernel, n_lat_tiles=seg["n_lat"] // tm),
        grid=(n_rows // tm,),
        in_specs=[pl.BlockSpec((tm, d), row)] + _mix_specs(tm, row, mix, seg["n_lat"])
                 + [pl.BlockSpec((1, 6, d), mod_map),
                    _resident((1, d)), _layer_resident(wo, layer), _resident(wr.shape), _resident(br.shape),
                    _resident((tm, tm))],
        out_specs=[pl.BlockSpec((tm, d), row), pl.BlockSpec((tm, d // 2), row),
                   pl.BlockSpec((N_EXPERTS, tm), lambda i: (0, i)),
                   pl.BlockSpec((N_EXPERTS, LANES), lambda i: (0, 0))],
        out_shape=[jax.ShapeDtypeStruct((n_rows, d), F32), jax.ShapeDtypeStruct((n_rows, d // 2), jnp.int32),
                   jax.ShapeDtypeStruct((N_EXPERTS, n_rows), F32),
                   jax.ShapeDtypeStruct((N_EXPERTS, LANES), F32)],
        scratch_shapes=[pltpu.VMEM((N_EXPERTS, LANES), F32)],
        compiler_params=_params(1),
        name="mix_router",
    )(xs, *mix, mod, g2, wo, wr, br, jnp.triu(jnp.ones((tm, tm), BF16), 1))


def _pack_bf16_pairs(h):
    half = h.shape[1] // 2
    hi = lax.bitcast_convert_type(h[:, :half].astype(BF16).astype(F32), jnp.int32)
    lo = lax.bitcast_convert_type(h[:, half:].astype(BF16).astype(F32), jnp.int32)
    return (hi & jnp.int32(-65536)) | lax.shift_right_logical(lo, 16)


def _unpack_bf16_pairs(p):
    hi = lax.bitcast_convert_type(p & jnp.int32(-65536), F32)
    lo = lax.bitcast_convert_type(lax.shift_left(p, 16), F32)
    return hi, lo


def _route_plan(route, counts, tm):
    n_rows = route.shape[1]
    n_slots = TOP_K * n_rows
    idx1, idx2, rank1, rank2 = (route[r].astype(jnp.int32) for r in (0, 1, 4, 5))
    offs = jnp.concatenate([jnp.zeros((1,), jnp.int32), jnp.cumsum(counts[:, 0].astype(jnp.int32))])
    first_slot = lambda idx: sum(jnp.where(idx == e, offs[e], 0) for e in range(N_EXPERTS))
    slot_a = first_slot(idx1) + rank1
    slot_b = first_slot(idx2) + rank2
    n_tiles = n_slots // tm
    t_start = jnp.arange(n_tiles, dtype=jnp.int32) * tm
    e_first = jnp.sum(offs[None, 1:] <= t_start[:, None], axis=1).astype(jnp.int32)
    base_hi = jnp.minimum(t_start + tm, offs[e_first + 1])
    e_next = jnp.arange(1, N_EXPERTS, dtype=jnp.int32)
    start = offs[1:N_EXPERTS]
    x_tile = jnp.minimum(start // tm, n_tiles - 1)
    x_hi = jnp.where(start % tm != 0, jnp.minimum(offs[2:], (x_tile + 1) * tm), start)
    tiles = jnp.concatenate([t_start // tm, x_tile])
    experts = jnp.concatenate([e_first, e_next])
    lo = jnp.concatenate([t_start, start])
    hi = jnp.concatenate([base_hi, x_hi])
    order = jnp.argsort(tiles * (2 * N_EXPERTS) + experts)
    tiles, experts, lo, hi = tiles[order], experts[order], lo[order], hi[order]
    change = tiles[1:] != tiles[:-1]
    one = jnp.ones((1,), bool)
    first = jnp.concatenate([one, change]).astype(jnp.int32)
    last = jnp.concatenate([change, one]).astype(jnp.int32)
    return slot_a, slot_b, (tiles, experts, lo, hi, first, last)


def _scatter_rows(rows, idx_a, idx_b):
    n_rows, width = rows.shape
    workers = SC_CORES * SC_SUBCORES
    per_worker = n_rows // workers
    assert n_rows % (workers * SC_GATHER_ROWS) == 0
    mesh = plsc.VectorSubcoreMesh(core_axis_name="c", subcore_axis_name="s")

    @functools.partial(
        pl.kernel, mesh=mesh,
        out_type=jax.ShapeDtypeStruct((TOP_K * n_rows, width), rows.dtype),
        scratch_types=[pltpu.VMEM((TOP_K, SC_GATHER_ROWS), jnp.int32),
                       pltpu.VMEM((SC_GATHER_ROWS, width), rows.dtype),
                       pltpu.SemaphoreType.DMA, pltpu.SemaphoreType.DMA],
        name="scatter_rows")
    def scatter(rows_hbm, idx_a_hbm, idx_b_hbm, out_hbm, idx_v, rows_v, sem_a, sem_b):
        base = (lax.axis_index("s") * SC_CORES + lax.axis_index("c")) * per_worker

        @pl.loop(0, per_worker // SC_GATHER_ROWS)
        def _(i):
            off = pl.multiple_of(base + i * SC_GATHER_ROWS, SC_GATHER_ROWS)
            pltpu.sync_copy(idx_a_hbm.at[pl.ds(off, SC_GATHER_ROWS)], idx_v.at[0])
            pltpu.sync_copy(idx_b_hbm.at[pl.ds(off, SC_GATHER_ROWS)], idx_v.at[1])
            pltpu.sync_copy(rows_hbm.at[pl.ds(off, SC_GATHER_ROWS)], rows_v)
            first = pltpu.async_copy(rows_v, out_hbm.at[idx_v.at[0]], sem_a)
            second = pltpu.async_copy(rows_v, out_hbm.at[idx_v.at[1]], sem_b)
            first.wait()
            second.wait()

    return scatter(rows, idx_a, idx_b)


def _gather_rows(table, idx):
    n_idx = idx.shape[0]
    width = table.shape[1]
    workers = SC_CORES * SC_SUBCORES
    per_worker = n_idx // workers
    assert n_idx % (workers * SC_GATHER_ROWS * SC_IN_FLIGHT) == 0
    mesh = plsc.VectorSubcoreMesh(core_axis_name="c", subcore_axis_name="s")

    @functools.partial(
        pl.kernel, mesh=mesh,
        out_type=jax.ShapeDtypeStruct((n_idx, width), table.dtype),
        scratch_types=[pltpu.VMEM((SC_IN_FLIGHT, SC_GATHER_ROWS), jnp.int32),
                       pltpu.VMEM((SC_IN_FLIGHT, SC_GATHER_ROWS, width), table.dtype)]
                      + [pltpu.SemaphoreType.DMA] * (2 * SC_IN_FLIGHT),
        name="gather_rows")
    def gather(table_hbm, idx_hbm, out_hbm, idx_v, rows_v, *sems):
        base = (lax.axis_index("s") * SC_CORES + lax.axis_index("c")) * per_worker

        @pl.loop(0, per_worker // (SC_GATHER_ROWS * SC_IN_FLIGHT))
        def _(i):
            offs = [pl.multiple_of(base + (i * SC_IN_FLIGHT + b) * SC_GATHER_ROWS, SC_GATHER_ROWS)
                    for b in range(SC_IN_FLIGHT)]
            for b, off in enumerate(offs):
                pltpu.sync_copy(idx_hbm.at[pl.ds(off, SC_GATHER_ROWS)], idx_v.at[b])
            reads = [pltpu.async_copy(table_hbm.at[idx_v.at[b]], rows_v.at[b], sems[b])
                     for b in range(SC_IN_FLIGHT)]
            writes = []
            for b, off in enumerate(offs):
                reads[b].wait()
                writes.append(pltpu.async_copy(rows_v.at[b], out_hbm.at[pl.ds(off, SC_GATHER_ROWS)],
                                               sems[SC_IN_FLIGHT + b]))
            for w in writes:
                w.wait()

    return gather(table, idx)


def _experts_kernel(tile_ref, exp_ref, lo_ref, hi_ref, first_ref, last_ref,
                    x_ref, w1_ref, w3_ref, w2_ref, o_ref, acc_ref):
    i = pl.program_id(0)
    tm, half = x_ref.shape
    f = w1_ref.shape[2]
    fc = FFN_CHUNK_COLS

    @pl.when(first_ref[i] == 1)
    def _():
        acc_ref[...] = jnp.zeros_like(acc_ref)

    lo, hi = lo_ref[i], hi_ref[i]

    @pl.when(hi > lo)
    def _():
        x_hi, x_lo = _unpack_bf16_pairs(x_ref[...])
        x_hi, x_lo = x_hi.astype(BF16), x_lo.astype(BF16)
        y = None
        for c in range(f // fc):
            cols = slice(c * fc, (c + 1) * fc)
            a = _dot(x_hi, w1_ref[0, :half, cols]) + _dot(x_lo, w1_ref[0, half:, cols])
            b = _dot(x_hi, w3_ref[0, :half, cols]) + _dot(x_lo, w3_ref[0, half:, cols])
            part = _dot((a * _sigmoid(a) * b).astype(BF16), w2_ref[0, cols, :])
            y = part if y is None else y + part
        rows = tile_ref[i] * tm + lax.broadcasted_iota(jnp.int32, (tm, 1), 0)
        keep = jnp.logical_and(rows >= lo, rows < hi)
        acc_ref[...] += jnp.where(keep, y, 0.0)

    @pl.when(last_ref[i] == 1)
    def _():
        o_ref[...] = _pack_bf16_pairs(acc_ref[...])


def _experts(xs_sorted, items, w1, w3, w2):
    n_slots, half = xs_sorted.shape
    n_exp, d, f = w1.shape
    tm = MOE_ROW_TILE
    assert f % FFN_CHUNK_COLS == 0
    tile_map = lambda i, tiles, *_: (tiles[i], 0)
    exp_map = lambda i, tiles, experts, *_: (experts[i], 0, 0)
    grid_spec = pltpu.PrefetchScalarGridSpec(
        num_scalar_prefetch=len(items),
        grid=(items[0].shape[0],),
        in_specs=[pl.BlockSpec((tm, half), tile_map),
                  pl.BlockSpec((1, d, f), exp_map), pl.BlockSpec((1, d, f), exp_map),
                  pl.BlockSpec((1, f, d), exp_map)],
        out_specs=pl.BlockSpec((tm, half), tile_map),
        scratch_shapes=[pltpu.VMEM((tm, d), F32)],
    )
    return pl.pallas_call(
        _experts_kernel,
        grid_spec=grid_spec,
        out_shape=jax.ShapeDtypeStruct((n_slots, half), jnp.int32),
        compiler_params=_params(1),
        name="experts",
    )(*items, xs_sorted, w1, w3, w2)


def _combine_kernel(x1_ref, ya_ref, yb_ref, route_ref, mod_ref, fg_ref, o_ref, *, final):
    half = ya_ref.shape[1]
    route = route_ref[...]
    padded = jnp.concatenate([route, jnp.zeros((LANES - route.shape[0], route.shape[1]), F32)], axis=0)
    route = jnp.transpose(padded)
    wa, wb = route[:, 2:3], route[:, 3:4]
    a_hi, a_lo = _unpack_bf16_pairs(ya_ref[...])
    b_hi, b_lo = _unpack_bf16_pairs(yb_ref[...])
    gate = mod_ref[0][5:6]
    out_hi = x1_ref[:, :half] + gate[:, :half] * (wa * a_hi + wb * b_hi)
    out_lo = x1_ref[:, half:] + gate[:, half:] * (wa * a_lo + wb * b_lo)
    if final:
        total = jnp.sum(out_hi * out_hi, axis=-1, keepdims=True) + jnp.sum(out_lo * out_lo, axis=-1, keepdims=True)
        scale = lax.rsqrt(total / (2 * half) + EPS)
        out_hi = out_hi * scale * fg_ref[:, :half]
        out_lo = out_lo * scale * fg_ref[:, half:]
    o_ref[:, :half] = out_hi
    o_ref[:, half:] = out_lo


def _combine(x1, y_pairs, route, mod, final_g, seg, final):
    n_rows, d = x1.shape
    tm = seg["tm"]
    row, mod_map = _row_maps(seg, tm)
    second = n_rows // tm
    return pl.pallas_call(
        functools.partial(_combine_kernel, final=final),
        grid=(n_rows // tm,),
        in_specs=[pl.BlockSpec((tm, d), row), pl.BlockSpec((tm, d // 2), row),
                  pl.BlockSpec((tm, d // 2), lambda i: (second + i, 0)),
                  pl.BlockSpec((N_EXPERTS, tm), lambda i: (0, i)), pl.BlockSpec((1, 6, d), mod_map),
                  pl.BlockSpec((1, d), lambda i: (0, 0))],
        out_specs=pl.BlockSpec((tm, d), row),
        out_shape=jax.ShapeDtypeStruct((n_rows, d), F32),
        compiler_params=_params(1),
        name="moe_combine",
    )(x1, y_pairs, y_pairs, route, mod, final_g)


def _moe_layer(xs, mix, mod, g2, wo, layer, wr, br, w1, w3, w2, final_g, seg, n_rows, final):
    x1, h_pairs, route, counts = _router(xs, mix, mod, g2, wo, layer, wr, br, seg, n_rows)
    slot_a, slot_b, items = _route_plan(route, counts, MOE_ROW_TILE)
    y_sorted = _experts(_scatter_rows(h_pairs, slot_a, slot_b), items, w1, w3, w2)
    y_pairs = _gather_rows(y_sorted, jnp.concatenate([slot_a, slot_b]))
    return _combine(x1, y_pairs, route, mod, final_g, seg, final)


def kernel(x, c, ctx, c_ctx, norm1_g, norm2_g, w_mod, b_mod, w_in, conv_w, conv_b, b_gates, attn_sink,
           g_att, g_ml, w_out, ffn_w1, ffn_w3, ffn_w2, w_router, b_router, exp_w1, exp_w3, exp_w2,
           final_g):
    b_, s_, d = x.shape
    lc = ctx.shape[1]
    depth = w_in.shape[0]
    n_lat, n_ctx = b_ * s_, b_ * lc
    tm = min(ROW_TILE, s_)
    assert s_ % tm == 0 and n_ctx % tm == 0
    assert (TOP_K * n_lat) % MOE_ROW_TILE == 0 and (TOP_K * n_ctx) % MOE_ROW_TILE == 0
    assert s_ % ML_CHUNK == 0 and lc % ML_CHUNK == 0 and n_lat % lc == 0 and b_ < MOD_ROWS
    seg = dict(B=b_, S=s_, Lc=lc, n_lat=n_lat, tm=tm)
    tm_in = INPROJ_TILE if s_ % INPROJ_TILE == 0 and n_ctx % INPROJ_TILE == 0 else tm
    seg_in = dict(seg, tm=tm_in)

    cond = jnp.zeros((MOD_ROWS, d), F32).at[:b_].set(c).at[b_].set(c_ctx)
    mods = _modulation(cond, w_mod, b_mod).reshape(depth, MOD_ROWS, 6, d)
    rope = _rope_tables(s_, tm_in)
    xs = (x.reshape(n_lat, d), ctx.reshape(n_ctx, d))
    final_row = final_g.reshape(1, d)

    w_in_b, w_out_b = w_in.astype(BF16), w_out.astype(BF16)
    ffn_b = (ffn_w1.astype(BF16), ffn_w3.astype(BF16), ffn_w2.astype(BF16))
    exp_b = None

    for layer in range(depth):
        last = layer == depth - 1
        w_gates = w_in[layer][:, MAIN_WIDTH:]
        w_gc = jnp.pad(w_gates, ((0, 0), (0, LANES - ML_GATES))).astype(BF16)
        b_gc = jnp.pad(b_gates[layer], (0, LANES - ML_GATES)).reshape(1, LANES)
        qa, ka, va, qm, km, vm, om, gc, gr, *stream = _input_projection(
            xs, mods[layer], norm1_g[layer].reshape(1, d), w_in_b, layer, w_gc,
            w_gates.T.astype(BF16), b_gc, b_gates[layer].reshape(ML_GATES, 1),
            conv_w[layer], conv_b[layer].reshape(1, -1), rope, seg_in)
        if stream:
            xs = stream[0]
        att = _attention(qa, ka, va, attn_sink[layer], g_att[layer].reshape(1, ATT_WIDTH), seg, not last)
        hf, hb = _mlstm(qm, km, vm, gc, gr, seg)
        mix = (*att, hf, hb, om, g_ml[layer].reshape(1, ML_WIDTH))
        n_rows = n_lat if last else n_lat + n_ctx
        g2 = norm2_g[layer].reshape(1, d)
        i = layer // 2
        if layer % 2 == 0:
            cast = None if last else ((exp_w1, exp_w3, exp_w2), i)
            xs, exp_b = _dense_layer(xs, mix, mods[layer], g2, w_out_b, layer, *ffn_b, i, final_row, seg,
                                     n_rows, last, cast)
        else:
            wr = w_router[i].T
            wr_hi = wr.astype(BF16)
            wr = jnp.concatenate([wr_hi, (wr - wr_hi.astype(F32)).astype(BF16)], axis=0)
            br = b_router[i].reshape(N_EXPERTS, 1)
            xs = _moe_layer(xs, mix, mods[layer], g2, w_out_b, layer, wr, br, *exp_b, final_row, seg,
                            n_rows, last)
    return xs[:n_lat].reshape(b_, s_, d)
```

```python
import functools

import jax
import jax.numpy as jnp
import numpy as np
from jax import lax
from jax.experimental import pallas as pl
from jax.experimental.pallas import tpu as pltpu
from jax.experimental.pallas import tpu_sc as plsc

F32 = jnp.float32
BF16 = jnp.bfloat16

GRID_W = 64
ATT_HEADS = 8
ATT_KV_HEADS = 2
ATT_HEAD_DIM = 64
ATT_GROUP = ATT_HEADS // ATT_KV_HEADS
WINDOW = 128
ATT_BLOCK = 128
ROPE_THETA = 10000.0
ML_HEADS = 4
ML_QK_DIM = 64
ML_V_DIM = 128
ML_CONV = 5
GATE_CAP = 15.0
ATT_WIDTH = ATT_HEADS * ATT_HEAD_DIM
ATT_KV_WIDTH = ATT_KV_HEADS * ATT_HEAD_DIM
ML_QK_WIDTH = ML_HEADS * ML_QK_DIM
ML_WIDTH = ML_HEADS * ML_V_DIM
ML_GATES = 4 * ML_HEADS
MAIN_WIDTH = ATT_WIDTH + 2 * ATT_KV_WIDTH + 2 * ML_QK_WIDTH + 2 * ML_WIDTH
N_EXPERTS = 8
TOP_K = 2
EPS = 1e-6

LANES = 128
SUBLANES = 8
VMEM_LIMIT = 56 * 1024 * 1024
NEG = -1e30
SC_CORES = 2
SC_SUBCORES = 16
SC_GATHER_ROWS = 64
SC_IN_FLIGHT = 2

ROW_TILE = 512
INPROJ_TILE = 1024
MOE_ROW_TILE = 512
FFN_CHUNK_COLS = 256
ATT_STEP_BLOCKS = 8
ML_CHUNK = 128
ML_STEP_CHUNKS = 2
CONV_HALO = SUBLANES
MOD_ROWS = 16
MOD_COL_TILE = 1536


def _dot(a, b):
    return jnp.dot(a, b, preferred_element_type=F32)


def _dot_nt(a, b):
    return lax.dot_general(a, b, (((1,), (1,)), ((), ())), preferred_element_type=F32)


def _dot_tn(a, b):
    return lax.dot_general(a, b, (((0,), (0,)), ((), ())), preferred_element_type=F32)


def _bf16_terms(x):
    hi = x.astype(BF16)
    rest = x - hi.astype(F32)
    mid = rest.astype(BF16)
    return hi, mid, (rest - mid.astype(F32)).astype(BF16)


def _dot_exact_rhs(mask, x):
    return sum(_dot(mask, t) for t in _bf16_terms(x))


def _dot_exact_lhs(x, mask):
    return sum(_dot(t, mask) for t in _bf16_terms(x))


def _sigmoid(x):
    return 1.0 / (1.0 + jnp.exp(-x))


def _rms(x, g):
    return x * lax.rsqrt(jnp.mean(x * x, axis=-1, keepdims=True) + EPS) * g


def _adaln(x, g, shift, scale):
    return _rms(x, g) * (1.0 + scale) + shift


def _params(n_axes):
    return pltpu.CompilerParams(dimension_semantics=("arbitrary",) * n_axes,
                                vmem_limit_bytes=VMEM_LIMIT)


def _resident(shape):
    zeros = (0,) * len(shape)
    return pl.BlockSpec(shape, lambda *_: zeros, pipeline_mode=pl.Buffered(1))


def _layer_resident(stacked, layer, block=None):
    block = tuple(stacked.shape[1:]) if block is None else block
    index = (layer,) + (0,) * len(block)
    return pl.BlockSpec((None,) + block, lambda *_: index, pipeline_mode=pl.Buffered(1))


def _mod_kernel(c_ref, w_ref, b_ref, o_ref):
    c = c_ref[...]
    a = (c * _sigmoid(c)).astype(BF16)
    o_ref[0] = _dot(a, w_ref[0].astype(BF16)) + b_ref[0]


def _modulation(cond, w_mod, b_mod):
    depth, d, width = w_mod.shape
    tn = MOD_COL_TILE if width % MOD_COL_TILE == 0 else width
    return pl.pallas_call(
        _mod_kernel,
        grid=(depth, width // tn),
        in_specs=[pl.BlockSpec((MOD_ROWS, d), lambda l, j: (0, 0)),
                  pl.BlockSpec((1, d, tn), lambda l, j: (l, 0, j)),
                  pl.BlockSpec((1, 1, tn), lambda l, j: (l, 0, j))],
        out_specs=pl.BlockSpec((1, MOD_ROWS, tn), lambda l, j: (l, 0, j)),
        out_shape=jax.ShapeDtypeStruct((depth, MOD_ROWS, width), F32),
        compiler_params=_params(2),
        name="modulation",
    )(cond, w_mod, b_mod.reshape(depth, 1, width))


def _gate_act(u, is_forget):
    g = GATE_CAP * jnp.tanh(u / GATE_CAP)
    log_sig = jnp.minimum(g, 0.0) - jnp.log1p(jnp.exp(-jnp.abs(g)))
    return jnp.where(is_forget, log_sig, g)


def _conv_silu(xe, cw, cb, rows):
    n_ext = xe.shape[0]
    mid = ML_CONV // 2
    y = cb + cw[mid:mid + 1] * xe[CONV_HALO:CONV_HALO + rows]
    for tap in range(ML_CONV):
        if tap != mid:
            y = y + cw[tap:tap + 1] * pltpu.roll(xe, (mid - tap) % n_ext, 0)[CONV_HALO:CONV_HALO + rows]
    return y * _sigmoid(y)


def _inproj_kernel(*refs, n_lat_tiles, seq_lat, seq_ctx, split):
    n_x = 6 if split else 3
    x_refs, refs = refs[:n_x], refs[n_x:]
    (mod_ref, g_ref, w_ref, wgc_ref, wgr_ref, bgc_ref, bgr_ref, cw_ref, cb_ref, cos_ref, sa_ref, sb_ref,
     qa_ref, ka_ref, va_ref, qm_ref, km_ref, vm_ref, om_ref, gc_ref, gr_ref) = refs[:21]
    i = pl.program_id(0)
    if split:
        in_lat = i < n_lat_tiles
        x_tile, x_prev, x_next = (jnp.where(in_lat, x_refs[2 * k][...], x_refs[2 * k + 1][...]) for k in range(3))
        refs[21][...] = x_tile
    else:
        x_tile, x_prev, x_next = (r[...] for r in x_refs)
    mod = mod_ref[0]
    tm = x_tile.shape[0]
    seg = min(tm, seq_ctx)
    n_seg = tm // seg
    norm_gain = g_ref[...]
    normed = lambda rows: _adaln(rows, norm_gain, mod[0:1], mod[1:2]).astype(BF16)
    quarter = ATT_HEAD_DIM // 4

    def rope(u, rows):
        return (u * cos_ref[rows, :] + pltpu.roll(u, LANES - quarter, 1) * sa_ref[rows, :]
                + pltpu.roll(u, quarter, 1) * sb_ref[rows, :])

    c_kv = ATT_WIDTH
    c_qk = c_kv + 2 * ATT_KV_WIDTH
    c_vm = c_qk + 2 * ML_QK_WIDTH
    c_om = c_vm + ML_WIDTH
    w_qk = w_ref[:, c_qk:c_vm]
    seq_len = jnp.where(i < n_lat_tiles, seq_lat, seq_ctx)
    cw, cb = cw_ref[...], cb_ref[...]
    hx = [normed(x_tile[j * seg:(j + 1) * seg, :]) for j in range(n_seg)]
    qk = [_dot(h, w_qk) for h in hx]
    qk_prev = _dot(normed(x_prev), w_qk)
    qk_next = _dot(normed(x_next), w_qk)
    for j in range(n_seg):
        rows = slice(j * seg, (j + 1) * seg)
        first_row = i * tm + j * seg
        has_prev = (lax.rem(first_row, seq_len) != 0).astype(F32)
        has_next = (lax.rem(first_row + seg, seq_len) != 0).astype(F32)
        prev = qk_prev if j == 0 else qk[j - 1][seg - CONV_HALO:]
        nxt = qk_next if j == n_seg - 1 else qk[j + 1][:CONV_HALO]
        xe = jnp.concatenate([prev * has_prev, qk[j], nxt * has_next], axis=0)
        y = _conv_silu(xe, cw, cb, seg)
        qm_ref[rows, :] = (y[:, :ML_QK_WIDTH] * (ML_QK_DIM ** -0.5)).astype(BF16)
        km_ref[rows, :] = y[:, ML_QK_WIDTH:].astype(BF16)
        q = _dot(hx[j], w_ref[:, :c_kv])
        for c in range(ATT_WIDTH // LANES):
            sl = slice(c * LANES, (c + 1) * LANES)
            qa_ref[rows, sl] = (rope(q[:, sl], rows) * (ATT_HEAD_DIM ** -0.5)).astype(BF16)
        kv = _dot(hx[j], w_ref[:, c_kv:c_qk])
        ka_ref[rows, :] = rope(kv[:, :ATT_KV_WIDTH], rows).astype(BF16)
        va_ref[rows, :] = kv[:, ATT_KV_WIDTH:].astype(BF16)
        vm_ref[rows, :] = _dot(hx[j], w_ref[:, c_vm:c_om]).astype(BF16)
        om_ref[rows, :] = _dot(hx[j], w_ref[:, c_om:c_om + ML_WIDTH]).astype(BF16)
        gc = _dot(hx[j], wgc_ref[...]) + bgc_ref[...]
        lane = lax.broadcasted_iota(jnp.int32, gc.shape, 1)
        gc_ref[rows, :] = _gate_act(gc, (lane // ML_HEADS) % 2 == 1)
        gr = _dot_nt(wgr_ref[...], hx[j]) + bgr_ref[...]
        sub = lax.broadcasted_iota(jnp.int32, gr.shape, 0)
        gr_ref[:, rows] = _gate_act(gr, (sub // ML_HEADS) % 2 == 1)


def _input_projection(xs, mod, g1, w_in, layer, w_gc, w_gr, b_gc, b_gr, conv_w, conv_b, rope, seg):
    split = isinstance(xs, tuple)
    d = (xs[0] if split else xs).shape[1]
    tm = seg["tm"]
    nlat = seg["n_lat"] // tm
    n = sum(a.shape[0] for a in xs) if split else xs.shape[0]
    s_tiles = seg["S"] // tm
    halos_per_tile = tm // CONV_HALO
    row = lambda i: (i, 0)

    def x_specs(n_rows, first_tile):
        n_tiles = n_rows // tm
        t = lambda i: jnp.clip(i - first_tile, 0, n_tiles - 1)
        return [pl.BlockSpec((tm, d), lambda i: (t(i), 0)),
                pl.BlockSpec((CONV_HALO, d), lambda i: (jnp.maximum(t(i) * halos_per_tile - 1, 0), 0)),
                pl.BlockSpec((CONV_HALO, d),
                             lambda i: (jnp.minimum((t(i) + 1) * halos_per_tile, n_rows // CONV_HALO - 1), 0))]

    if split:
        lat, ctx = x_specs(xs[0].shape[0], 0), x_specs(xs[1].shape[0], nlat)
        x_in_specs = [s for pair in zip(lat, ctx) for s in pair]
        x_args = [xs[0], xs[1]] * 3
    else:
        x_in_specs, x_args = x_specs(n, 0), [xs] * 3
    mod_map = lambda i: (jnp.where(i < nlat, i // s_tiles, seg["B"]), 0, 0)
    rope_map = lambda i: (jnp.where(i < nlat, i % s_tiles, s_tiles), 0)
    widths = [(ATT_WIDTH, BF16), (ATT_KV_WIDTH, BF16), (ATT_KV_WIDTH, BF16), (ML_QK_WIDTH, BF16),
              (ML_QK_WIDTH, BF16), (ML_WIDTH, BF16), (ML_WIDTH, BF16), (LANES, F32)]
    out_shape = [jax.ShapeDtypeStruct((n, w), t) for w, t in widths]
    out_specs = [pl.BlockSpec((tm, w), row) for w, _ in widths]
    out_shape.append(jax.ShapeDtypeStruct((ML_GATES, n), F32))
    out_specs.append(pl.BlockSpec((ML_GATES, tm), lambda i: (0, i)))
    if split:
        out_shape.append(jax.ShapeDtypeStruct((n, d), F32))
        out_specs.append(pl.BlockSpec((tm, d), row))
    return pl.pallas_call(
        functools.partial(_inproj_kernel, n_lat_tiles=nlat, seq_lat=seg["S"], seq_ctx=seg["Lc"], split=split),
        grid=(n // tm,),
        in_specs=x_in_specs
                 + [pl.BlockSpec((1, 6, d), mod_map),
                  _resident((1, d)),
                  _layer_resident(w_in, layer, (d, MAIN_WIDTH)), _resident(w_gc.shape), _resident(w_gr.shape),
                  _resident(b_gc.shape), _resident(b_gr.shape),
                  _resident(conv_w.shape), _resident(conv_b.shape),
                  pl.BlockSpec((tm, LANES), rope_map),
                  pl.BlockSpec((tm, LANES), rope_map),
                  pl.BlockSpec((tm, LANES), rope_map)],
        out_specs=out_specs,
        out_shape=out_shape,
        compiler_params=_params(1),
        name="input_projection",
    )(*x_args, mod, g1, w_in, w_gc, w_gr, b_gc, b_gr, conv_w, conv_b, *rope)


def _rope_tables(s, tm):
    quarter = ATT_HEAD_DIM // 4
    t = jnp.arange(s)
    row = (t // GRID_W).astype(F32)
    col = (t % GRID_W).astype(F32)
    inv = ROPE_THETA ** (-jnp.arange(quarter, dtype=F32) / quarter)
    ang_r = row[:, None] * inv[None, :]
    ang_c = col[:, None] * inv[None, :]
    zero = jnp.zeros_like(ang_r)
    cos = jnp.concatenate([jnp.cos(ang_r)] * 2 + [jnp.cos(ang_c)] * 2, axis=1)
    sin_up = jnp.concatenate([-jnp.sin(ang_r), zero, -jnp.sin(ang_c), zero], axis=1)
    sin_dn = jnp.concatenate([zero, jnp.sin(ang_r), zero, jnp.sin(ang_c)], axis=1)
    reps = LANES // ATT_HEAD_DIM
    ident = [jnp.ones((tm, LANES), F32), jnp.zeros((tm, LANES), F32), jnp.zeros((tm, LANES), F32)]
    return tuple(jnp.concatenate([jnp.tile(a, (1, reps)), i], axis=0)
                 for a, i in zip((cos, sin_up, sin_dn), ident))


def _attn_kernel(sink_ref, *refs, window, blocks):
    blk = ATT_BLOCK
    if window:
        q_ref, kp_ref, kc_ref, kn_ref, vp_ref, vc_ref, vn_ref, kx_ref, vx_ref, g_ref, o_ref = refs
        j = pl.program_id(1)
        k_own, v_own = kc_ref[...], vc_ref[...]
        k_blocks = [kp_ref[...]] + [k_own[t * blk:(t + 1) * blk] for t in range(blocks)] + [kn_ref[...]]
        v_blocks = [vp_ref[...]] + [v_own[t * blk:(t + 1) * blk] for t in range(blocks)] + [vn_ref[...]]
    else:
        q_ref, kx_ref, vx_ref, g_ref, o_ref = refs
    for t in range(blocks):
        if window:
            has_prev = j > 0 if t == 0 else True
            has_next = j < pl.num_programs(1) - 1 if t == blocks - 1 else True
            win = (k_blocks[t:t + 3], v_blocks[t:t + 3], has_prev, has_next)
        else:
            win = None
        att = _attend_block(sink_ref, q_ref[t * blk:(t + 1) * blk, :], win, kx_ref[...], vx_ref[...])
        o_ref[t * blk:(t + 1) * blk, :] = _rms(att, g_ref[...]).astype(o_ref.dtype)


def _attend_block(sink_ref, q, win, k_ctx, v_ctx):
    blk = ATT_BLOCK
    if win is None:
        k_all, v_all, bias = k_ctx, v_ctx, None
    else:
        k_win, v_win, has_prev, has_next = win
        rows = lax.broadcasted_iota(jnp.int32, (blk, blk), 0)
        cols = lax.broadcasted_iota(jnp.int32, (blk, blk), 1)
        ok_p = jnp.logical_and(cols >= rows, has_prev)
        ok_n = jnp.logical_and(cols <= rows, has_next)
        bias = jnp.concatenate([jnp.where(ok_p, 0.0, NEG), jnp.zeros((blk, blk), F32),
                                jnp.where(ok_n, 0.0, NEG), jnp.zeros((blk, k_ctx.shape[0]), F32)], axis=1)
        k_all = jnp.concatenate(list(k_win) + [k_ctx], axis=0)
        v_all = jnp.concatenate(list(v_win) + [v_ctx], axis=0)
    dh = ATT_HEAD_DIM
    outs = []
    for h in range(ATT_KV_HEADS):
        k_h = k_all[:, h * dh:(h + 1) * dh]
        v_h = v_all[:, h * dh:(h + 1) * dh]
        q_h = jnp.concatenate([q[:, (h * ATT_GROUP + g) * dh:(h * ATT_GROUP + g + 1) * dh]
                               for g in range(ATT_GROUP)], axis=0)
        s_all = _dot_nt(q_h, k_h)
        p_parts, inv_parts = [], []
        for g in range(ATT_GROUP):
            sink = sink_ref[h * ATT_GROUP + g]
            s = s_all[g * blk:(g + 1) * blk]
            if bias is not None:
                s = s + bias
            m = jnp.maximum(jnp.max(s, axis=-1, keepdims=True), sink)
            p = jnp.exp(s - m)
            denom = jnp.sum(p, axis=-1, keepdims=True) + jnp.exp(sink - m)
            p_parts.append(p.astype(BF16))
            inv_parts.append(1.0 / denom)
        o = _dot(jnp.concatenate(p_parts, axis=0), v_h)
        for g in range(ATT_GROUP):
            outs.append(o[g * blk:(g + 1) * blk] * inv_parts[g])
    return jnp.concatenate(outs, axis=1)


def _attention(qa, ka, va, sink, g_att, seg, with_ctx):
    n = qa.shape[0]
    b_, s_, lc = seg["B"], seg["S"], seg["Lc"]
    blk = ATT_BLOCK
    blocks_per_seq = s_ // blk
    ctx_map = lambda b, j, *_: (b_ * s_ // lc + b, 0)
    ctx_specs = [pl.BlockSpec((lc, ATT_KV_WIDTH), ctx_map)] * 2
    g_spec = pl.BlockSpec((1, ATT_WIDTH), lambda b, j, *_: (0, 0))

    def call(window, blocks, steps, first_step, in_specs, args):
        grid_spec = pltpu.PrefetchScalarGridSpec(
            num_scalar_prefetch=1,
            grid=(b_, steps),
            in_specs=[pl.BlockSpec((blocks * blk, ATT_WIDTH),
                                   lambda b, j, *_: (first_step + b * steps + j, 0))] + in_specs,
            out_specs=pl.BlockSpec((blocks * blk, ATT_WIDTH), lambda b, j, *_: (b * steps + j, 0)))
        return pl.pallas_call(
            functools.partial(_attn_kernel, window=window, blocks=blocks),
            grid_spec=grid_spec,
            out_shape=jax.ShapeDtypeStruct((b_ * steps * blocks * blk, ATT_WIDTH), BF16),
            compiler_params=_params(2),
            name="window_attention" if window else "context_attention",
        )(sink, qa, *args)

    qb = min(ATT_STEP_BLOCKS, blocks_per_seq)
    assert blocks_per_seq % qb == 0
    nqs = blocks_per_seq // qb

    def edge_map(off):
        return lambda b, j, *_: (b * blocks_per_seq + jnp.clip(j * qb + off, 0, blocks_per_seq - 1), 0)

    kv_specs = [pl.BlockSpec((blk, ATT_KV_WIDTH), edge_map(-1)),
                pl.BlockSpec((qb * blk, ATT_KV_WIDTH), lambda b, j, *_: (b * nqs + j, 0)),
                pl.BlockSpec((blk, ATT_KV_WIDTH), edge_map(qb))]
    att_lat = call(True, qb, nqs, 0, kv_specs + kv_specs + ctx_specs + [g_spec],
                   (ka, ka, ka, va, va, va, ka, va, g_att))
    if not with_ctx:
        return att_lat, att_lat
    cb = min(ATT_STEP_BLOCKS, lc // blk)
    assert (lc // blk) % cb == 0 and (b_ * s_) % (cb * blk) == 0
    att_ctx = call(False, cb, lc // (cb * blk), b_ * s_ // (cb * blk), ctx_specs + [g_spec], (ka, va, g_att))
    return att_lat, att_ctx


def _mlstm_direction(q_ref, k_ref, v_ref, gc_ref, gr_ref, out_ref, state_ref, tok, *, reverse):
    chunk = ML_CHUNK
    rows = lax.broadcasted_iota(jnp.int32, (chunk, chunk), 0)
    cols = lax.broadcasted_iota(jnp.int32, (chunk, chunk), 1)
    lower = rows >= cols
    upper = rows <= cols
    seen = upper if reverse else lower
    gc = gc_ref[tok, :]
    gr = gr_ref[:, tok]
    b_col = _dot_exact_rhs(seen.astype(BF16), gc)
    b_row = _dot_exact_lhs(gr, (lower if reverse else upper).astype(BF16))
    b_end = jnp.sum(gc, axis=0, keepdims=True)
    base = 2 * ML_HEADS if reverse else 0
    pair_width = 2 * ML_QK_DIM
    lane = lax.broadcasted_iota(jnp.int32, (chunk, pair_width), 1)
    state_row = lax.broadcasted_iota(jnp.int32, (pair_width, 1), 0)
    ones = jnp.ones((chunk, ML_V_DIM), BF16)

    for pair in range(ML_HEADS // 2):
        q_pair = q_ref[tok, pair * pair_width:(pair + 1) * pair_width]
        k_pair = k_ref[tok, pair * pair_width:(pair + 1) * pair_width]
        state = state_ref[pair]
        state_bf = state.astype(BF16)
        update = None
        decays = []
        for sub in range(2):
            h = 2 * pair + sub
            i_idx = base + h
            f_idx = base + ML_HEADS + h
            own = (lane >= ML_QK_DIM) if sub else (lane < ML_QK_DIM)
            q_h = jnp.where(own, q_pair, jnp.zeros_like(q_pair))
            vx = jnp.concatenate([v_ref[tok, h * ML_V_DIM:(h + 1) * ML_V_DIM], ones], axis=1)
            bc = b_col[:, f_idx:f_idx + 1]
            d = bc - (b_row[f_idx:f_idx + 1, :] - gr[i_idx:i_idx + 1, :])
            w = jnp.exp(jnp.where(seen, d, NEG))
            s = _dot_nt(q_h, k_pair) * w
            tot = _dot(s.astype(BF16), vx) + jnp.exp(bc) * _dot(q_h, state_bf)
            h_out = tot[:, :ML_V_DIM] / jnp.maximum(jnp.abs(tot[:, ML_V_DIM:]), 1.0)
            out_ref[tok, h * ML_V_DIM:(h + 1) * ML_V_DIM] = h_out.astype(out_ref.dtype)

            be = b_end[:, f_idx:f_idx + 1]
            kw = jnp.where(own, k_pair.astype(F32) * jnp.exp(be - bc + gc[:, i_idx:i_idx + 1]), 0.0)
            part = _dot_tn(kw.astype(BF16), vx)
            update = part if update is None else update + part
            decays.append(jnp.exp(be))
        decay = jnp.where(state_row < ML_QK_DIM, decays[0], decays[1])
        state_ref[pair] = decay * state + update


def _mlstm_kernel(qf_ref, kf_ref, vf_ref, gcf_ref, grf_ref, qb_ref, kb_ref, vb_ref, gcb_ref, grb_ref,
                  hf_ref, hb_ref, sf_ref, sb_ref):
    @pl.when(pl.program_id(1) == 0)
    def _():
        sf_ref[...] = jnp.zeros_like(sf_ref)
        sb_ref[...] = jnp.zeros_like(sb_ref)

    n_chunks = qf_ref.shape[0] // ML_CHUNK
    for t in range(n_chunks):
        fwd = slice(t * ML_CHUNK, (t + 1) * ML_CHUNK)
        bwd = slice((n_chunks - 1 - t) * ML_CHUNK, (n_chunks - t) * ML_CHUNK)
        _mlstm_direction(qf_ref, kf_ref, vf_ref, gcf_ref, grf_ref, hf_ref, sf_ref, fwd, reverse=False)
        _mlstm_direction(qb_ref, kb_ref, vb_ref, gcb_ref, grb_ref, hb_ref, sb_ref, bwd, reverse=True)


def _mlstm(qm, km, vm, gc, gr, seg):
    n = qm.shape[0]
    b_, s_, lc = seg["B"], seg["S"], seg["Lc"]
    chunk = ML_STEP_CHUNKS * ML_CHUNK
    assert s_ % chunk == 0 and lc % chunk == 0
    ncc, ncl = lc // chunk, s_ // chunk
    lat_chunks = b_ * ncl

    def blk(b, c, reverse):
        pos = jnp.where(c < ncc, c, c - ncc)
        if reverse:
            pos = jnp.where(c < ncc, ncc, ncl) - 1 - pos
        return jnp.where(c < ncc, lat_chunks + b * ncc + pos, b * ncl + pos)

    def specs(reverse):
        cur = lambda b, c: (blk(b, c, reverse), 0)
        return [pl.BlockSpec((chunk, ML_QK_WIDTH), cur), pl.BlockSpec((chunk, ML_QK_WIDTH), cur),
                pl.BlockSpec((chunk, ML_WIDTH), cur), pl.BlockSpec((chunk, LANES), cur),
                pl.BlockSpec((ML_GATES, chunk), lambda b, c: (0, blk(b, c, reverse)))]

    out = lambda reverse: pl.BlockSpec((chunk, ML_WIDTH), lambda b, c: (blk(b, c, reverse), 0))
    state = pltpu.VMEM((ML_HEADS // 2, 2 * ML_QK_DIM, 2 * ML_V_DIM), F32)
    return pl.pallas_call(
        _mlstm_kernel,
        grid=(b_, ncc + ncl),
        in_specs=specs(False) + specs(True),
        out_specs=[out(False), out(True)],
        out_shape=[jax.ShapeDtypeStruct((n, ML_WIDTH), BF16)] * 2,
        scratch_shapes=[state, state],
        compiler_params=_params(2),
        name="mlstm_scan",
    )(qm, km, vm, gc, gr, qm, km, vm, gc, gr)


def _swiglu(h, w1_ref, w3_ref, w2_ref):
    f = w1_ref.shape[1]
    fc = FFN_CHUNK_COLS
    y = None
    for c in range(f // fc):
        cols = slice(c * fc, (c + 1) * fc)
        a = _dot(h, w1_ref[:, cols])
        b = _dot(h, w3_ref[:, cols])
        part = _dot((a * _sigmoid(a) * b).astype(BF16), w2_ref[cols, :])
        y = part if y is None else y + part
    return y


def _mix_residual(x_ref, mix_refs, mod, wo_ref):
    att_lat_ref, att_ctx_ref, hf_ref, hb_ref, om_ref, gml_ref, n_lat_tiles = mix_refs
    parts = [jnp.where(pl.program_id(0) < n_lat_tiles, att_lat_ref[...], att_ctx_ref[...])]
    for h in range(ML_HEADS):
        sl = slice(h * ML_V_DIM, (h + 1) * ML_V_DIM)
        tot = hf_ref[:, sl].astype(F32) + hb_ref[:, sl].astype(F32)
        parts.append((_rms(tot, gml_ref[:, sl]) * _sigmoid(om_ref[:, sl].astype(F32))).astype(BF16))
    return x_ref[...] + mod[2:3] * _dot(jnp.concatenate(parts, axis=1), wo_ref[...])


def _mix_specs(tm, row, mix, n_lat):
    lat_tiles = n_lat // tm
    ctx_tiles = mix[1].shape[0] // tm
    return [pl.BlockSpec((tm, ATT_WIDTH), lambda i, *_: (jnp.minimum(i, lat_tiles - 1), 0)),
            pl.BlockSpec((tm, ATT_WIDTH), lambda i, *_: (jnp.clip(i - lat_tiles, 0, ctx_tiles - 1), 0))] \
        + [pl.BlockSpec((tm, ML_WIDTH), row)] * 3 + [_resident((1, ML_WIDTH))]


def _dense_layer_kernel(x_ref, att_ref, attc_ref, hf_ref, hb_ref, om_ref, gml_ref, mod_ref, g_ref, wo_ref,
                        w1_ref, w3_ref, w2_ref, fg_ref, *rest, final, n_lat_tiles):
    n_cast = (len(rest) - 1) // 2
    o_ref = rest[n_cast]
    for src, dst in zip(rest[:n_cast], rest[n_cast + 1:]):
        dst[...] = src[...].astype(dst.dtype)
    mod = mod_ref[0]
    x1 = _mix_residual(x_ref, (att_ref, attc_ref, hf_ref, hb_ref, om_ref, gml_ref, n_lat_tiles), mod, wo_ref)
    hx = _adaln(x1, g_ref[...], mod[3:4], mod[4:5]).astype(BF16)
    out = x1 + mod[5:6] * _swiglu(hx, w1_ref, w3_ref, w2_ref)
    if final:
        out = _rms(out, fg_ref[...])
    o_ref[...] = out


def _row_maps(seg, tm):
    nlat = seg["n_lat"] // tm
    s_tiles = seg["S"] // tm
    return (lambda i, *_: (i, 0)), (lambda i, *_: (jnp.where(i < nlat, i // s_tiles, seg["B"]), 0, 0))


def _cast_slabs(stacked, index, n_steps):
    rows_per_layer = int(np.prod(stacked.shape[1:-1]))
    width = stacked.shape[-1]
    slab = next(r for r in range(2 * SUBLANES, rows_per_layer + 1, 2 * SUBLANES)
                if rows_per_layer % r == 0 and rows_per_layer // r <= n_steps)
    n_slabs = rows_per_layer // slab
    src = pl.BlockSpec((slab, width), lambda i: (index * n_slabs + jnp.minimum(i, n_slabs - 1), 0))
    dst = pl.BlockSpec((slab, width), lambda i: (jnp.minimum(i, n_slabs - 1), 0))
    flat = stacked.reshape(-1, width)
    return src, dst, flat, jax.ShapeDtypeStruct((rows_per_layer, width), BF16)


def _dense_layer(xs, mix, mod, g2, wo, layer, w1, w3, w2, ffn_index, final_g, seg, n_rows, final, cast=None):
    d = xs.shape[1]
    tm = seg["tm"]
    row, mod_map = _row_maps(seg, tm)
    n_steps = n_rows // tm
    cast_specs = [_cast_slabs(p, cast[1], n_steps) for p in cast[0]] if cast else []
    outs = pl.pallas_call(
        functools.partial(_dense_layer_kernel, final=final, n_lat_tiles=seg["n_lat"] // tm),
        grid=(n_steps,),
        in_specs=[pl.BlockSpec((tm, d), row)] + _mix_specs(tm, row, mix, seg["n_lat"])
                 + [pl.BlockSpec((1, 6, d), mod_map),
                    _resident((1, d)), _layer_resident(wo, layer), _layer_resident(w1, ffn_index),
                    _layer_resident(w3, ffn_index), _layer_resident(w2, ffn_index), _resident((1, d))]
                 + [c[0] for c in cast_specs],
        out_specs=[pl.BlockSpec((tm, d), row)] + [c[1] for c in cast_specs],
        out_shape=[jax.ShapeDtypeStruct((n_rows, d), F32)] + [c[3] for c in cast_specs],
        compiler_params=_params(1),
        name="dense_layer",
    )(xs, *mix, mod, g2, wo, w1, w3, w2, final_g, *[c[2] for c in cast_specs])
    if not cast:
        return outs[0], None
    return outs[0], tuple(o.reshape(p.shape[1:]) for o, p in zip(outs[1:], cast[0]))


def _router_kernel(x_ref, att_ref, attc_ref, hf_ref, hb_ref, om_ref, gml_ref, mod_ref, g_ref, wo_ref, wr_ref,
                   br_ref, earlier_ref,
                   x1_ref, h_ref, route_ref, counts_ref, count_ref, *, n_lat_tiles):
    mod = mod_ref[0]
    x1 = _mix_residual(x_ref, (att_ref, attc_ref, hf_ref, hb_ref, om_ref, gml_ref, n_lat_tiles), mod, wo_ref)
    x1_ref[...] = x1
    hx = _adaln(x1, g_ref[...], mod[3:4], mod[4:5])
    h_ref[...] = _pack_bf16_pairs(hx)
    hx_hi = hx.astype(BF16)
    hx_lo = (hx - hx_hi.astype(F32)).astype(BF16)
    both = _dot_nt(wr_ref[...], hx_hi)
    logits = (both[:N_EXPERTS] + both[N_EXPERTS:] + _dot_nt(wr_ref[:N_EXPERTS, :], hx_lo)
              + br_ref[...])
    sub = lax.broadcasted_iota(jnp.int32, logits.shape, 0)
    top1 = jnp.max(logits, axis=0, keepdims=True)
    idx1 = jnp.min(jnp.where(logits == top1, sub, N_EXPERTS), axis=0, keepdims=True)
    rest = jnp.where(sub == idx1, -jnp.inf, logits)
    top2 = jnp.max(rest, axis=0, keepdims=True)
    idx2 = jnp.min(jnp.where(rest == top2, sub, N_EXPERTS), axis=0, keepdims=True)
    e2 = jnp.exp(top2 - top1)
    w_first = 1.0 / (1.0 + e2)
    @pl.when(pl.program_id(0) == 0)
    def _():
        count_ref[...] = jnp.zeros_like(count_ref)

    first, second = sub == idx1, sub == idx2
    hot = jnp.logical_or(first, second).astype(F32)
    before = _dot(hot.astype(BF16), earlier_ref[...]) + count_ref[:, :1]
    rank1 = jnp.sum(jnp.where(first, before, 0.0), axis=0, keepdims=True)
    rank2 = jnp.sum(jnp.where(second, before, 0.0), axis=0, keepdims=True)
    count_ref[...] += jnp.sum(hot, axis=1, keepdims=True)
    counts_ref[...] = count_ref[...]
    route = jnp.where(sub == 0, idx1.astype(F32), jnp.where(sub == 1, idx2.astype(F32), 0.0))
    route = jnp.where(sub == 2, w_first, jnp.where(sub == 3, e2 * w_first, route))
    route_ref[...] = jnp.where(sub == 4, rank1, jnp.where(sub == 5, rank2, route))


def _router(xs, mix, mod, g2, wo, layer, wr, br, seg, n_rows):
    d = xs.shape[1]
    tm = seg["tm"]
    row, mod_map = _row_maps(seg, tm)
    return pl.pallas_call(
        functools.partial(_router_kernel, n_lat_tiles=seg["n_lat"] // tm),
        grid=(n_rows // tm,),
        in_specs=[pl.BlockSpec((tm, d), row)] + _mix_specs(tm, row, mix, seg["n_lat"])
                 + [pl.BlockSpec((1, 6, d), mod_map),
                    _resident((1, d)), _layer_resident(wo, layer), _resident(wr.shape), _resident(br.shape),
                    _resident((tm, tm))],
        out_specs=[pl.BlockSpec((tm, d), row), pl.BlockSpec((tm, d // 2), row),
                   pl.BlockSpec((N_EXPERTS, tm), lambda i: (0, i)),
                   pl.BlockSpec((N_EXPERTS, LANES), lambda i: (0, 0))],
        out_shape=[jax.ShapeDtypeStruct((n_rows, d), F32), jax.ShapeDtypeStruct((n_rows, d // 2), jnp.int32),
                   jax.ShapeDtypeStruct((N_EXPERTS, n_rows), F32),
                   jax.ShapeDtypeStruct((N_EXPERTS, LANES), F32)],
        scratch_shapes=[pltpu.VMEM((N_EXPERTS, LANES), F32)],
        compiler_params=_params(1),
        name="mix_router",
    )(xs, *mix, mod, g2, wo, wr, br, jnp.triu(jnp.ones((tm, tm), BF16), 1))


def _pack_bf16_pairs(h):
    half = h.shape[1] // 2
    hi = lax.bitcast_convert_type(h[:, :half].astype(BF16).astype(F32), jnp.int32)
    lo = lax.bitcast_convert_type(h[:, half:].astype(BF16).astype(F32), jnp.int32)
    return (hi & jnp.int32(-65536)) | lax.shift_right_logical(lo, 16)


def _unpack_bf16_pairs(p):
    hi = lax.bitcast_convert_type(p & jnp.int32(-65536), F32)
    lo = lax.bitcast_convert_type(lax.shift_left(p, 16), F32)
    return hi, lo


def _route_plan(route, counts, tm):
    n_rows = route.shape[1]
    n_slots = TOP_K * n_rows
    idx1, idx2, rank1, rank2 = (route[r].astype(jnp.int32) for r in (0, 1, 4, 5))
    offs = jnp.concatenate([jnp.zeros((1,), jnp.int32), jnp.cumsum(counts[:, 0].astype(jnp.int32))])
    first_slot = lambda idx: sum(jnp.where(idx == e, offs[e], 0) for e in range(N_EXPERTS))
    slot_a = first_slot(idx1) + rank1
    slot_b = first_slot(idx2) + rank2
    n_tiles = n_slots // tm
    t_start = jnp.arange(n_tiles, dtype=jnp.int32) * tm
    e_first = jnp.sum(offs[None, 1:] <= t_start[:, None], axis=1).astype(jnp.int32)
    base_hi = jnp.minimum(t_start + tm, offs[e_first + 1])
    e_next = jnp.arange(1, N_EXPERTS, dtype=jnp.int32)
    start = offs[1:N_EXPERTS]
    x_tile = jnp.minimum(start // tm, n_tiles - 1)
    x_hi = jnp.where(start % tm != 0, jnp.minimum(offs[2:], (x_tile + 1) * tm), start)
    tiles = jnp.concatenate([t_start // tm, x_tile])
    experts = jnp.concatenate([e_first, e_next])
    lo = jnp.concatenate([t_start, start])
    hi = jnp.concatenate([base_hi, x_hi])
    order = jnp.argsort(tiles * (2 * N_EXPERTS) + experts)
    tiles, experts, lo, hi = tiles[order], experts[order], lo[order], hi[order]
    change = tiles[1:] != tiles[:-1]
    one = jnp.ones((1,), bool)
    first = jnp.concatenate([one, change]).astype(jnp.int32)
    last = jnp.concatenate([change, one]).astype(jnp.int32)
    return slot_a, slot_b, (tiles, experts, lo, hi, first, last)


def _scatter_rows(rows, idx_a, idx_b):
    n_rows, width = rows.shape
    workers = SC_CORES * SC_SUBCORES
    per_worker = n_rows // workers
    assert n_rows % (workers * SC_GATHER_ROWS) == 0
    mesh = plsc.VectorSubcoreMesh(core_axis_name="c", subcore_axis_name="s")

    @functools.partial(
        pl.kernel, mesh=mesh,
        out_type=jax.ShapeDtypeStruct((TOP_K * n_rows, width), rows.dtype),
        scratch_types=[pltpu.VMEM((TOP_K, SC_GATHER_ROWS), jnp.int32),
                       pltpu.VMEM((SC_GATHER_ROWS, width), rows.dtype),
                       pltpu.SemaphoreType.DMA, pltpu.SemaphoreType.DMA],
        name="scatter_rows")
    def scatter(rows_hbm, idx_a_hbm, idx_b_hbm, out_hbm, idx_v, rows_v, sem_a, sem_b):
        base = (lax.axis_index("s") * SC_CORES + lax.axis_index("c")) * per_worker

        @pl.loop(0, per_worker // SC_GATHER_ROWS)
        def _(i):
            off = pl.multiple_of(base + i * SC_GATHER_ROWS, SC_GATHER_ROWS)
            pltpu.sync_copy(idx_a_hbm.at[pl.ds(off, SC_GATHER_ROWS)], idx_v.at[0])
            pltpu.sync_copy(idx_b_hbm.at[pl.ds(off, SC_GATHER_ROWS)], idx_v.at[1])
            pltpu.sync_copy(rows_hbm.at[pl.ds(off, SC_GATHER_ROWS)], rows_v)
            first = pltpu.async_copy(rows_v, out_hbm.at[idx_v.at[0]], sem_a)
            second = pltpu.async_copy(rows_v, out_hbm.at[idx_v.at[1]], sem_b)
            first.wait()
            second.wait()

    return scatter(rows, idx_a, idx_b)


def _gather_rows(table, idx):
    n_idx = idx.shape[0]
    width = table.shape[1]
    workers = SC_CORES * SC_SUBCORES
    per_worker = n_idx // workers
    assert n_idx % (workers * SC_GATHER_ROWS * SC_IN_FLIGHT) == 0
    mesh = plsc.VectorSubcoreMesh(core_axis_name="c", subcore_axis_name="s")

    @functools.partial(
        pl.kernel, mesh=mesh,
        out_type=jax.ShapeDtypeStruct((n_idx, width), table.dtype),
        scratch_types=[pltpu.VMEM((SC_IN_FLIGHT, SC_GATHER_ROWS), jnp.int32),
                       pltpu.VMEM((SC_IN_FLIGHT, SC_GATHER_ROWS, width), table.dtype)]
                      + [pltpu.SemaphoreType.DMA] * (2 * SC_IN_FLIGHT),
        name="gather_rows")
    def gather(table_hbm, idx_hbm, out_hbm, idx_v, rows_v, *sems):
        base = (lax.axis_index("s") * SC_CORES + lax.axis_index("c")) * per_worker

        @pl.loop(0, per_worker // (SC_GATHER_ROWS * SC_IN_FLIGHT))
        def _(i):
            offs = [pl.multiple_of(base + (i * SC_IN_FLIGHT + b) * SC_GATHER_ROWS, SC_GATHER_ROWS)
                    for b in range(SC_IN_FLIGHT)]
            for b, off in enumerate(offs):
                pltpu.sync_copy(idx_hbm.at[pl.ds(off, SC_GATHER_ROWS)], idx_v.at[b])
            reads = [pltpu.async_copy(table_hbm.at[idx_v.at[b]], rows_v.at[b], sems[b])
                     for b in range(SC_IN_FLIGHT)]
            writes = []
            for b, off in enumerate(offs):
                reads[b].wait()
                writes.append(pltpu.async_copy(rows_v.at[b], out_hbm.at[pl.ds(off, SC_GATHER_ROWS)],
                                               sems[SC_IN_FLIGHT + b]))
            for w in writes:
                w.wait()

    return gather(table, idx)


def _experts_kernel(tile_ref, exp_ref, lo_ref, hi_ref, first_ref, last_ref,
                    x_ref, w1_ref, w3_ref, w2_ref, o_ref, acc_ref):
    i = pl.program_id(0)
    tm, half = x_ref.shape
    f = w1_ref.shape[2]
    fc = FFN_CHUNK_COLS

    @pl.when(first_ref[i] == 1)
    def _():
        acc_ref[...] = jnp.zeros_like(acc_ref)

    lo, hi = lo_ref[i], hi_ref[i]

    @pl.when(hi > lo)
    def _():
        x_hi, x_lo = _unpack_bf16_pairs(x_ref[...])
        x = jnp.concatenate([x_hi.astype(BF16), x_lo.astype(BF16)], axis=1)
        y = None
        for c in range(f // fc):
            cols = slice(c * fc, (c + 1) * fc)
            a = _dot(x, w1_ref[0, :, cols])
            b = _dot(x, w3_ref[0, :, cols])
            part = _dot((a * _sigmoid(a) * b).astype(BF16), w2_ref[0, cols, :])
            y = part if y is None else y + part
        rows = tile_ref[i] * tm + lax.broadcasted_iota(jnp.int32, (tm, 1), 0)
        keep = jnp.logical_and(rows >= lo, rows < hi)
        acc_ref[...] += jnp.where(keep, y, 0.0)

    @pl.when(last_ref[i] == 1)
    def _():
        o_ref[...] = _pack_bf16_pairs(acc_ref[...])


def _experts(xs_sorted, items, w1, w3, w2):
    n_slots, half = xs_sorted.shape
    n_exp, d, f = w1.shape
    tm = MOE_ROW_TILE
    assert f % FFN_CHUNK_COLS == 0
    tile_map = lambda i, tiles, *_: (tiles[i], 0)
    exp_map = lambda i, tiles, experts, *_: (experts[i], 0, 0)
    grid_spec = pltpu.PrefetchScalarGridSpec(
        num_scalar_prefetch=len(items),
        grid=(items[0].shape[0],),
        in_specs=[pl.BlockSpec((tm, half), tile_map),
                  pl.BlockSpec((1, d, f), exp_map), pl.BlockSpec((1, d, f), exp_map),
                  pl.BlockSpec((1, f, d), exp_map)],
        out_specs=pl.BlockSpec((tm, half), tile_map),
        scratch_shapes=[pltpu.VMEM((tm, d), F32)],
    )
    return pl.pallas_call(
        _experts_kernel,
        grid_spec=grid_spec,
        out_shape=jax.ShapeDtypeStruct((n_slots, half), jnp.int32),
        compiler_params=_params(1),
        name="experts",
    )(*items, xs_sorted, w1, w3, w2)


def _combine_kernel(x1_ref, ya_ref, yb_ref, route_ref, mod_ref, fg_ref, o_ref, *, final):
    half = ya_ref.shape[1]
    route = route_ref[...]
    padded = jnp.concatenate([route, jnp.zeros((LANES - route.shape[0], route.shape[1]), F32)], axis=0)
    route = jnp.transpose(padded)
    wa, wb = route[:, 2:3], route[:, 3:4]
    a_hi, a_lo = _unpack_bf16_pairs(ya_ref[...])
    b_hi, b_lo = _unpack_bf16_pairs(yb_ref[...])
    gate = mod_ref[0][5:6]
    out_hi = x1_ref[:, :half] + gate[:, :half] * (wa * a_hi + wb * b_hi)
    out_lo = x1_ref[:, half:] + gate[:, half:] * (wa * a_lo + wb * b_lo)
    if final:
        total = jnp.sum(out_hi * out_hi, axis=-1, keepdims=True) + jnp.sum(out_lo * out_lo, axis=-1, keepdims=True)
        scale = lax.rsqrt(total / (2 * half) + EPS)
        out_hi = out_hi * scale * fg_ref[:, :half]
        out_lo = out_lo * scale * fg_ref[:, half:]
    o_ref[:, :half] = out_hi
    o_ref[:, half:] = out_lo


def _combine(x1, y_pairs, route, mod, final_g, seg, final):
    n_rows, d = x1.shape
    tm = seg["tm"]
    row, mod_map = _row_maps(seg, tm)
    second = n_rows // tm
    return pl.pallas_call(
        functools.partial(_combine_kernel, final=final),
        grid=(n_rows // tm,),
        in_specs=[pl.BlockSpec((tm, d), row), pl.BlockSpec((tm, d // 2), row),
                  pl.BlockSpec((tm, d // 2), lambda i: (second + i, 0)),
                  pl.BlockSpec((N_EXPERTS, tm), lambda i: (0, i)), pl.BlockSpec((1, 6, d), mod_map),
                  pl.BlockSpec((1, d), lambda i: (0, 0))],
        out_specs=pl.BlockSpec((tm, d), row),
        out_shape=jax.ShapeDtypeStruct((n_rows, d), F32),
        compiler_params=_params(1),
        name="moe_combine",
    )(x1, y_pairs, y_pairs, route, mod, final_g)


def _moe_layer(xs, mix, mod, g2, wo, layer, wr, br, w1, w3, w2, final_g, seg, n_rows, final):
    x1, h_pairs, route, counts = _router(xs, mix, mod, g2, wo, layer, wr, br, seg, n_rows)
    slot_a, slot_b, items = _route_plan(route, counts, MOE_ROW_TILE)
    y_sorted = _experts(_scatter_rows(h_pairs, slot_a, slot_b), items, w1, w3, w2)
    y_pairs = _gather_rows(y_sorted, jnp.concatenate([slot_a, slot_b]))
    return _combine(x1, y_pairs, route, mod, final_g, seg, final)


def kernel(x, c, ctx, c_ctx, norm1_g, norm2_g, w_mod, b_mod, w_in, conv_w, conv_b, b_gates, attn_sink,
           g_att, g_ml, w_out, ffn_w1, ffn_w3, ffn_w2, w_router, b_router, exp_w1, exp_w3, exp_w2,
           final_g):
    b_, s_, d = x.shape
    lc = ctx.shape[1]
    depth = w_in.shape[0]
    n_lat, n_ctx = b_ * s_, b_ * lc
    tm = min(ROW_TILE, s_)
    assert s_ % tm == 0 and n_ctx % tm == 0
    assert (TOP_K * n_lat) % MOE_ROW_TILE == 0 and (TOP_K * n_ctx) % MOE_ROW_TILE == 0
    assert s_ % ML_CHUNK == 0 and lc % ML_CHUNK == 0 and n_lat % lc == 0 and b_ < MOD_ROWS
    seg = dict(B=b_, S=s_, Lc=lc, n_lat=n_lat, tm=tm)
    tm_in = INPROJ_TILE if s_ % INPROJ_TILE == 0 and n_ctx % INPROJ_TILE == 0 else tm
    seg_in = dict(seg, tm=tm_in)

    cond = jnp.zeros((MOD_ROWS, d), F32).at[:b_].set(c).at[b_].set(c_ctx)
    mods = _modulation(cond, w_mod, b_mod).reshape(depth, MOD_ROWS, 6, d)
    rope = _rope_tables(s_, tm_in)
    xs = (x.reshape(n_lat, d), ctx.reshape(n_ctx, d))
    final_row = final_g.reshape(1, d)

    w_in_b, w_out_b = w_in.astype(BF16), w_out.astype(BF16)
    ffn_b = (ffn_w1.astype(BF16), ffn_w3.astype(BF16), ffn_w2.astype(BF16))
    exp_b = None

    for layer in range(depth):
        last = layer == depth - 1
        w_gates = w_in[layer][:, MAIN_WIDTH:]
        w_gc = jnp.pad(w_gates, ((0, 0), (0, LANES - ML_GATES))).astype(BF16)
        b_gc = jnp.pad(b_gates[layer], (0, LANES - ML_GATES)).reshape(1, LANES)
        qa, ka, va, qm, km, vm, om, gc, gr, *stream = _input_projection(
            xs, mods[layer], norm1_g[layer].reshape(1, d), w_in_b, layer, w_gc,
            w_gates.T.astype(BF16), b_gc, b_gates[layer].reshape(ML_GATES, 1),
            conv_w[layer], conv_b[layer].reshape(1, -1), rope, seg_in)
        if stream:
            xs = stream[0]
        att = _attention(qa, ka, va, attn_sink[layer], g_att[layer].reshape(1, ATT_WIDTH), seg, not last)
        hf, hb = _mlstm(qm, km, vm, gc, gr, seg)
        mix = (*att, hf, hb, om, g_ml[layer].reshape(1, ML_WIDTH))
        n_rows = n_lat if last else n_lat + n_ctx
        g2 = norm2_g[layer].reshape(1, d)
        i = layer // 2
        if layer % 2 == 0:
            cast = None if last else ((exp_w1, exp_w3, exp_w2), i)
            xs, exp_b = _dense_layer(xs, mix, mods[layer], g2, w_out_b, layer, *ffn_b, i, final_row, seg,
                                     n_rows, last, cast)
        else:
            wr = w_router[i].T
            wr_hi = wr.astype(BF16)
            wr = jnp.concatenate([wr_hi, (wr - wr_hi.astype(F32)).astype(BF16)], axis=0)
            br = b_router[i].reshape(N_EXPERTS, 1)
            xs = _moe_layer(xs, mix, mods[layer], g2, w_out_b, layer, wr, br, *exp_b, final_row, seg,
                            n_rows, last)
    return xs[:n_lat].reshape(b_, s_, d)
```

```python
import functools

import jax
import jax.numpy as jnp
import numpy as np
from jax import lax
from jax.experimental import pallas as pl
from jax.experimental.pallas import tpu as pltpu
from jax.experimental.pallas import tpu_sc as plsc

F32 = jnp.float32
BF16 = jnp.bfloat16

GRID_W = 64
ATT_HEADS = 8
ATT_KV_HEADS = 2
ATT_HEAD_DIM = 64
ATT_GROUP = ATT_HEADS // ATT_KV_HEADS
WINDOW = 128
ATT_BLOCK = 128
ROPE_THETA = 10000.0
ML_HEADS = 4
ML_QK_DIM = 64
ML_V_DIM = 128
ML_CONV = 5
GATE_CAP = 15.0
ATT_WIDTH = ATT_HEADS * ATT_HEAD_DIM
ATT_KV_WIDTH = ATT_KV_HEADS * ATT_HEAD_DIM
ML_QK_WIDTH = ML_HEADS * ML_QK_DIM
ML_WIDTH = ML_HEADS * ML_V_DIM
ML_GATES = 4 * ML_HEADS
MAIN_WIDTH = ATT_WIDTH + 2 * ATT_KV_WIDTH + 2 * ML_QK_WIDTH + 2 * ML_WIDTH
N_EXPERTS = 8
TOP_K = 2
EPS = 1e-6

LANES = 128
SUBLANES = 8
VMEM_LIMIT = 56 * 1024 * 1024
NEG = -1e30
SC_CORES = 2
SC_SUBCORES = 16
SC_GATHER_ROWS = 64
SC_IN_FLIGHT = 2

ROW_TILE = 512
INPROJ_TILE = 1024
MOE_ROW_TILE = 512
FFN_CHUNK_COLS = 256
ATT_STEP_BLOCKS = 8
ML_CHUNK = 128
ML_STEP_CHUNKS = 4
CONV_HALO = SUBLANES
MOD_ROWS = 16
MOD_COL_TILE = 1536


def _dot(a, b):
    return jnp.dot(a, b, preferred_element_type=F32)


def _dot_nt(a, b):
    return lax.dot_general(a, b, (((1,), (1,)), ((), ())), preferred_element_type=F32)


def _dot_tn(a, b):
    return lax.dot_general(a, b, (((0,), (0,)), ((), ())), preferred_element_type=F32)


def _bf16_terms(x):
    hi = x.astype(BF16)
    rest = x - hi.astype(F32)
    mid = rest.astype(BF16)
    return hi, mid, (rest - mid.astype(F32)).astype(BF16)


def _dot_exact_rhs(mask, x):
    return sum(_dot(mask, t) for t in _bf16_terms(x))


def _dot_exact_lhs(x, mask):
    return sum(_dot(t, mask) for t in _bf16_terms(x))


def _sigmoid(x):
    return 1.0 / (1.0 + jnp.exp(-x))


def _rms(x, g):
    return x * lax.rsqrt(jnp.mean(x * x, axis=-1, keepdims=True) + EPS) * g


def _adaln(x, g, shift, scale):
    return _rms(x, g) * (1.0 + scale) + shift


def _params(n_axes):
    return pltpu.CompilerParams(dimension_semantics=("arbitrary",) * n_axes,
                                vmem_limit_bytes=VMEM_LIMIT)


def _resident(shape):
    zeros = (0,) * len(shape)
    return pl.BlockSpec(shape, lambda *_: zeros, pipeline_mode=pl.Buffered(1))


def _layer_resident(stacked, layer, block=None):
    block = tuple(stacked.shape[1:]) if block is None else block
    index = (layer,) + (0,) * len(block)
    return pl.BlockSpec((None,) + block, lambda *_: index, pipeline_mode=pl.Buffered(1))


def _mod_kernel(c_ref, w_ref, b_ref, o_ref):
    c = c_ref[...]
    a = (c * _sigmoid(c)).astype(BF16)
    o_ref[0] = _dot(a, w_ref[0].astype(BF16)) + b_ref[0]


def _modulation(cond, w_mod, b_mod):
    depth, d, width = w_mod.shape
    tn = MOD_COL_TILE if width % MOD_COL_TILE == 0 else width
    return pl.pallas_call(
        _mod_kernel,
        grid=(depth, width // tn),
        in_specs=[pl.BlockSpec((MOD_ROWS, d), lambda l, j: (0, 0)),
                  pl.BlockSpec((1, d, tn), lambda l, j: (l, 0, j)),
                  pl.BlockSpec((1, 1, tn), lambda l, j: (l, 0, j))],
        out_specs=pl.BlockSpec((1, MOD_ROWS, tn), lambda l, j: (l, 0, j)),
        out_shape=jax.ShapeDtypeStruct((depth, MOD_ROWS, width), F32),
        compiler_params=_params(2),
        name="modulation",
    )(cond, w_mod, b_mod.reshape(depth, 1, width))


def _gate_act(u, is_forget):
    g = GATE_CAP * jnp.tanh(u / GATE_CAP)
    log_sig = jnp.minimum(g, 0.0) - jnp.log1p(jnp.exp(-jnp.abs(g)))
    return jnp.where(is_forget, log_sig, g)


def _conv_silu(xe, cw, cb, rows):
    n_ext = xe.shape[0]
    mid = ML_CONV // 2
    y = cb + cw[mid:mid + 1] * xe[CONV_HALO:CONV_HALO + rows]
    for tap in range(ML_CONV):
        if tap != mid:
            y = y + cw[tap:tap + 1] * pltpu.roll(xe, (mid - tap) % n_ext, 0)[CONV_HALO:CONV_HALO + rows]
    return y * _sigmoid(y)


def _inproj_kernel(*refs, n_lat_tiles, seq_lat, seq_ctx, split):
    n_x = 6 if split else 3
    x_refs, refs = refs[:n_x], refs[n_x:]
    (mod_ref, g_ref, w_ref, wgc_ref, wgr_ref, bgc_ref, bgr_ref, cw_ref, cb_ref, cos_ref, sa_ref, sb_ref,
     qa_ref, ka_ref, va_ref, qm_ref, km_ref, vm_ref, om_ref, gc_ref, gr_ref) = refs[:21]
    i = pl.program_id(0)
    if split:
        in_lat = i < n_lat_tiles
        x_tile, x_prev, x_next = (jnp.where(in_lat, x_refs[2 * k][...], x_refs[2 * k + 1][...]) for k in range(3))
        refs[21][...] = x_tile
    else:
        x_tile, x_prev, x_next = (r[...] for r in x_refs)
    mod = mod_ref[0]
    tm = x_tile.shape[0]
    seg = min(tm, seq_ctx)
    n_seg = tm // seg
    norm_gain = g_ref[...]
    normed = lambda rows: _adaln(rows, norm_gain, mod[0:1], mod[1:2]).astype(BF16)
    quarter = ATT_HEAD_DIM // 4

    def rope(u, rows):
        return (u * cos_ref[rows, :] + pltpu.roll(u, LANES - quarter, 1) * sa_ref[rows, :]
                + pltpu.roll(u, quarter, 1) * sb_ref[rows, :])

    c_kv = ATT_WIDTH
    c_qk = c_kv + 2 * ATT_KV_WIDTH
    c_vm = c_qk + 2 * ML_QK_WIDTH
    c_om = c_vm + ML_WIDTH
    w_qk = w_ref[:, c_qk:c_vm]
    seq_len = jnp.where(i < n_lat_tiles, seq_lat, seq_ctx)
    cw, cb = cw_ref[...], cb_ref[...]
    hx = [normed(x_tile[j * seg:(j + 1) * seg, :]) for j in range(n_seg)]
    qk = [_dot(h, w_qk) for h in hx]
    qk_prev = _dot(normed(x_prev), w_qk)
    qk_next = _dot(normed(x_next), w_qk)
    for j in range(n_seg):
        rows = slice(j * seg, (j + 1) * seg)
        first_row = i * tm + j * seg
        has_prev = (lax.rem(first_row, seq_len) != 0).astype(F32)
        has_next = (lax.rem(first_row + seg, seq_len) != 0).astype(F32)
        prev = qk_prev if j == 0 else qk[j - 1][seg - CONV_HALO:]
        nxt = qk_next if j == n_seg - 1 else qk[j + 1][:CONV_HALO]
        xe = jnp.concatenate([prev * has_prev, qk[j], nxt * has_next], axis=0)
        y = _conv_silu(xe, cw, cb, seg)
        qm_ref[rows, :] = (y[:, :ML_QK_WIDTH] * (ML_QK_DIM ** -0.5)).astype(BF16)
        km_ref[rows, :] = y[:, ML_QK_WIDTH:].astype(BF16)
        q = _dot(hx[j], w_ref[:, :c_kv])
        for c in range(ATT_WIDTH // LANES):
            sl = slice(c * LANES, (c + 1) * LANES)
            qa_ref[rows, sl] = (rope(q[:, sl], rows) * (ATT_HEAD_DIM ** -0.5)).astype(BF16)
        kv = _dot(hx[j], w_ref[:, c_kv:c_qk])
        ka_ref[rows, :] = rope(kv[:, :ATT_KV_WIDTH], rows).astype(BF16)
        va_ref[rows, :] = kv[:, ATT_KV_WIDTH:].astype(BF16)
        vm_ref[rows, :] = _dot(hx[j], w_ref[:, c_vm:c_om]).astype(BF16)
        om_ref[rows, :] = _dot(hx[j], w_ref[:, c_om:c_om + ML_WIDTH]).astype(BF16)
        gc = _dot(hx[j], wgc_ref[...]) + bgc_ref[...]
        lane = lax.broadcasted_iota(jnp.int32, gc.shape, 1)
        gc_ref[rows, :] = _gate_act(gc, (lane // ML_HEADS) % 2 == 1)
        gr = _dot_nt(wgr_ref[...], hx[j]) + bgr_ref[...]
        sub = lax.broadcasted_iota(jnp.int32, gr.shape, 0)
        gr_ref[:, rows] = _gate_act(gr, (sub // ML_HEADS) % 2 == 1)


def _input_projection(xs, mod, g1, w_in, layer, w_gc, w_gr, b_gc, b_gr, conv_w, conv_b, rope, seg):
    split = isinstance(xs, tuple)
    d = (xs[0] if split else xs).shape[1]
    tm = seg["tm"]
    nlat = seg["n_lat"] // tm
    n = sum(a.shape[0] for a in xs) if split else xs.shape[0]
    s_tiles = seg["S"] // tm
    halos_per_tile = tm // CONV_HALO
    row = lambda i: (i, 0)

    def x_specs(n_rows, first_tile):
        n_tiles = n_rows // tm
        t = lambda i: jnp.clip(i - first_tile, 0, n_tiles - 1)
        return [pl.BlockSpec((tm, d), lambda i: (t(i), 0)),
                pl.BlockSpec((CONV_HALO, d), lambda i: (jnp.maximum(t(i) * halos_per_tile - 1, 0), 0)),
                pl.BlockSpec((CONV_HALO, d),
                             lambda i: (jnp.minimum((t(i) + 1) * halos_per_tile, n_rows // CONV_HALO - 1), 0))]

    if split:
        lat, ctx = x_specs(xs[0].shape[0], 0), x_specs(xs[1].shape[0], nlat)
        x_in_specs = [s for pair in zip(lat, ctx) for s in pair]
        x_args = [xs[0], xs[1]] * 3
    else:
        x_in_specs, x_args = x_specs(n, 0), [xs] * 3
    mod_map = lambda i: (jnp.where(i < nlat, i // s_tiles, seg["B"]), 0, 0)
    rope_map = lambda i: (jnp.where(i < nlat, i % s_tiles, s_tiles), 0)
    widths = [(ATT_WIDTH, BF16), (ATT_KV_WIDTH, BF16), (ATT_KV_WIDTH, BF16), (ML_QK_WIDTH, BF16),
              (ML_QK_WIDTH, BF16), (ML_WIDTH, BF16), (ML_WIDTH, BF16), (LANES, F32)]
    out_shape = [jax.ShapeDtypeStruct((n, w), t) for w, t in widths]
    out_specs = [pl.BlockSpec((tm, w), row) for w, _ in widths]
    out_shape.append(jax.ShapeDtypeStruct((ML_GATES, n), F32))
    out_specs.append(pl.BlockSpec((ML_GATES, tm), lambda i: (0, i)))
    if split:
        out_shape.append(jax.ShapeDtypeStruct((n, d), F32))
        out_specs.append(pl.BlockSpec((tm, d), row))
    return pl.pallas_call(
        functools.partial(_inproj_kernel, n_lat_tiles=nlat, seq_lat=seg["S"], seq_ctx=seg["Lc"], split=split),
        grid=(n // tm,),
        in_specs=x_in_specs
                 + [pl.BlockSpec((1, 6, d), mod_map),
                  _resident((1, d)),
                  _layer_resident(w_in, layer, (d, MAIN_WIDTH)), _resident(w_gc.shape), _resident(w_gr.shape),
                  _resident(b_gc.shape), _resident(b_gr.shape),
                  _resident(conv_w.shape), _resident(conv_b.shape),
                  pl.BlockSpec((tm, LANES), rope_map),
                  pl.BlockSpec((tm, LANES), rope_map),
                  pl.BlockSpec((tm, LANES), rope_map)],
        out_specs=out_specs,
        out_shape=out_shape,
        compiler_params=_params(1),
        name="input_projection",
    )(*x_args, mod, g1, w_in, w_gc, w_gr, b_gc, b_gr, conv_w, conv_b, *rope)


def _rope_tables(s, tm):
    quarter = ATT_HEAD_DIM // 4
    t = jnp.arange(s)
    row = (t // GRID_W).astype(F32)
    col = (t % GRID_W).astype(F32)
    inv = ROPE_THETA ** (-jnp.arange(quarter, dtype=F32) / quarter)
    ang_r = row[:, None] * inv[None, :]
    ang_c = col[:, None] * inv[None, :]
    zero = jnp.zeros_like(ang_r)
    cos = jnp.concatenate([jnp.cos(ang_r)] * 2 + [jnp.cos(ang_c)] * 2, axis=1)
    sin_up = jnp.concatenate([-jnp.sin(ang_r), zero, -jnp.sin(ang_c), zero], axis=1)
    sin_dn = jnp.concatenate([zero, jnp.sin(ang_r), zero, jnp.sin(ang_c)], axis=1)
    reps = LANES // ATT_HEAD_DIM
    ident = [jnp.ones((tm, LANES), F32), jnp.zeros((tm, LANES), F32), jnp.zeros((tm, LANES), F32)]
    return tuple(jnp.concatenate([jnp.tile(a, (1, reps)), i], axis=0)
                 for a, i in zip((cos, sin_up, sin_dn), ident))


def _attn_kernel(sink_ref, *refs, window, blocks):
    blk = ATT_BLOCK
    if window:
        q_ref, kp_ref, kc_ref, kn_ref, vp_ref, vc_ref, vn_ref, kx_ref, vx_ref, g_ref, o_ref = refs
        j = pl.program_id(1)
        k_own, v_own = kc_ref[...], vc_ref[...]
        k_blocks = [kp_ref[...]] + [k_own[t * blk:(t + 1) * blk] for t in range(blocks)] + [kn_ref[...]]
        v_blocks = [vp_ref[...]] + [v_own[t * blk:(t + 1) * blk] for t in range(blocks)] + [vn_ref[...]]
    else:
        q_ref, kx_ref, vx_ref, g_ref, o_ref = refs
    for t in range(blocks):
        if window:
            has_prev = j > 0 if t == 0 else True
            has_next = j < pl.num_programs(1) - 1 if t == blocks - 1 else True
            win = (k_blocks[t:t + 3], v_blocks[t:t + 3], has_prev, has_next)
        else:
            win = None
        att = _attend_block(sink_ref, q_ref[t * blk:(t + 1) * blk, :], win, kx_ref[...], vx_ref[...])
        o_ref[t * blk:(t + 1) * blk, :] = _rms(att, g_ref[...]).astype(o_ref.dtype)


def _attend_block(sink_ref, q, win, k_ctx, v_ctx):
    blk = ATT_BLOCK
    if win is None:
        k_all, v_all, bias = k_ctx, v_ctx, None
    else:
        k_win, v_win, has_prev, has_next = win
        rows = lax.broadcasted_iota(jnp.int32, (blk, blk), 0)
        cols = lax.broadcasted_iota(jnp.int32, (blk, blk), 1)
        ok_p = jnp.logical_and(cols >= rows, has_prev)
        ok_n = jnp.logical_and(cols <= rows, has_next)
        bias = jnp.concatenate([jnp.where(ok_p, 0.0, NEG), jnp.zeros((blk, blk), F32),
                                jnp.where(ok_n, 0.0, NEG), jnp.zeros((blk, k_ctx.shape[0]), F32)], axis=1)
        k_all = jnp.concatenate(list(k_win) + [k_ctx], axis=0)
        v_all = jnp.concatenate(list(v_win) + [v_ctx], axis=0)
    dh = ATT_HEAD_DIM
    outs = []
    for h in range(ATT_KV_HEADS):
        k_h = k_all[:, h * dh:(h + 1) * dh]
        v_h = v_all[:, h * dh:(h + 1) * dh]
        q_h = jnp.concatenate([q[:, (h * ATT_GROUP + g) * dh:(h * ATT_GROUP + g + 1) * dh]
                               for g in range(ATT_GROUP)], axis=0)
        s_all = _dot_nt(q_h, k_h)
        p_parts, inv_parts = [], []
        for g in range(ATT_GROUP):
            sink = sink_ref[h * ATT_GROUP + g]
            s = s_all[g * blk:(g + 1) * blk]
            if bias is not None:
                s = s + bias
            m = jnp.maximum(jnp.max(s, axis=-1, keepdims=True), sink)
            p = jnp.exp(s - m)
            denom = jnp.sum(p, axis=-1, keepdims=True) + jnp.exp(sink - m)
            p_parts.append(p.astype(BF16))
            inv_parts.append(1.0 / denom)
        o = _dot(jnp.concatenate(p_parts, axis=0), v_h)
        for g in range(ATT_GROUP):
            outs.append(o[g * blk:(g + 1) * blk] * inv_parts[g])
    return jnp.concatenate(outs, axis=1)


def _attention(qa, ka, va, sink, g_att, seg, with_ctx):
    n = qa.shape[0]
    b_, s_, lc = seg["B"], seg["S"], seg["Lc"]
    blk = ATT_BLOCK
    blocks_per_seq = s_ // blk
    ctx_map = lambda b, j, *_: (b_ * s_ // lc + b, 0)
    ctx_specs = [pl.BlockSpec((lc, ATT_KV_WIDTH), ctx_map)] * 2
    g_spec = pl.BlockSpec((1, ATT_WIDTH), lambda b, j, *_: (0, 0))

    def call(window, blocks, steps, first_step, in_specs, args):
        grid_spec = pltpu.PrefetchScalarGridSpec(
            num_scalar_prefetch=1,
            grid=(b_, steps),
            in_specs=[pl.BlockSpec((blocks * blk, ATT_WIDTH),
                                   lambda b, j, *_: (first_step + b * steps + j, 0))] + in_specs,
            out_specs=pl.BlockSpec((blocks * blk, ATT_WIDTH), lambda b, j, *_: (b * steps + j, 0)))
        return pl.pallas_call(
            functools.partial(_attn_kernel, window=window, blocks=blocks),
            grid_spec=grid_spec,
            out_shape=jax.ShapeDtypeStruct((b_ * steps * blocks * blk, ATT_WIDTH), BF16),
            compiler_params=_params(2),
            name="window_attention" if window else "context_attention",
        )(sink, qa, *args)

    qb = min(ATT_STEP_BLOCKS, blocks_per_seq)
    assert blocks_per_seq % qb == 0
    nqs = blocks_per_seq // qb

    def edge_map(off):
        return lambda b, j, *_: (b * blocks_per_seq + jnp.clip(j * qb + off, 0, blocks_per_seq - 1), 0)

    kv_specs = [pl.BlockSpec((blk, ATT_KV_WIDTH), edge_map(-1)),
                pl.BlockSpec((qb * blk, ATT_KV_WIDTH), lambda b, j, *_: (b * nqs + j, 0)),
                pl.BlockSpec((blk, ATT_KV_WIDTH), edge_map(qb))]
    att_lat = call(True, qb, nqs, 0, kv_specs + kv_specs + ctx_specs + [g_spec],
                   (ka, ka, ka, va, va, va, ka, va, g_att))
    if not with_ctx:
        return att_lat, att_lat
    cb = min(ATT_STEP_BLOCKS, lc // blk)
    assert (lc // blk) % cb == 0 and (b_ * s_) % (cb * blk) == 0
    att_ctx = call(False, cb, lc // (cb * blk), b_ * s_ // (cb * blk), ctx_specs + [g_spec], (ka, va, g_att))
    return att_lat, att_ctx


def _mlstm_direction(q_ref, k_ref, v_ref, gc_ref, gr_ref, out_ref, state_ref, tok, *, reverse):
    chunk = ML_CHUNK
    rows = lax.broadcasted_iota(jnp.int32, (chunk, chunk), 0)
    cols = lax.broadcasted_iota(jnp.int32, (chunk, chunk), 1)
    lower = rows >= cols
    upper = rows <= cols
    seen = upper if reverse else lower
    gc = gc_ref[tok, :]
    gr = gr_ref[:, tok]
    b_col = _dot_exact_rhs(seen.astype(BF16), gc)
    b_row = _dot_exact_lhs(gr, (lower if reverse else upper).astype(BF16))
    b_end = jnp.sum(gc, axis=0, keepdims=True)
    base = 2 * ML_HEADS if reverse else 0
    pair_width = 2 * ML_QK_DIM
    lane = lax.broadcasted_iota(jnp.int32, (chunk, pair_width), 1)
    state_row = lax.broadcasted_iota(jnp.int32, (pair_width, 1), 0)
    ones = jnp.ones((chunk, ML_V_DIM), BF16)

    for pair in range(ML_HEADS // 2):
        q_pair = q_ref[tok, pair * pair_width:(pair + 1) * pair_width]
        k_pair = k_ref[tok, pair * pair_width:(pair + 1) * pair_width]
        state = state_ref[pair]
        state_bf = state.astype(BF16)
        update = None
        decays = []
        for sub in range(2):
            h = 2 * pair + sub
            i_idx = base + h
            f_idx = base + ML_HEADS + h
            own = (lane >= ML_QK_DIM) if sub else (lane < ML_QK_DIM)
            q_h = jnp.where(own, q_pair, jnp.zeros_like(q_pair))
            vx = jnp.concatenate([v_ref[tok, h * ML_V_DIM:(h + 1) * ML_V_DIM], ones], axis=1)
            bc = b_col[:, f_idx:f_idx + 1]
            d = bc - (b_row[f_idx:f_idx + 1, :] - gr[i_idx:i_idx + 1, :])
            w = jnp.exp(jnp.where(seen, d, NEG))
            s = _dot_nt(q_h, k_pair) * w
            tot = _dot(s.astype(BF16), vx) + jnp.exp(bc) * _dot(q_h, state_bf)
            h_out = tot[:, :ML_V_DIM] / jnp.maximum(jnp.abs(tot[:, ML_V_DIM:]), 1.0)
            out_ref[tok, h * ML_V_DIM:(h + 1) * ML_V_DIM] = h_out.astype(out_ref.dtype)

            be = b_end[:, f_idx:f_idx + 1]
            kw = jnp.where(own, k_pair.astype(F32) * jnp.exp(be - bc + gc[:, i_idx:i_idx + 1]), 0.0)
            part = _dot_tn(kw.astype(BF16), vx)
            update = part if update is None else update + part
            decays.append(jnp.exp(be))
        decay = jnp.where(state_row < ML_QK_DIM, decays[0], decays[1])
        state_ref[pair] = decay * state + update


def _mlstm_both(fwd_refs, bwd_refs, hf_ref, hb_ref, sf_ref, sb_ref):
    n_chunks = hf_ref.shape[0] // ML_CHUNK
    for t in range(n_chunks):
        fwd = slice(t * ML_CHUNK, (t + 1) * ML_CHUNK)
        bwd = slice((n_chunks - 1 - t) * ML_CHUNK, (n_chunks - t) * ML_CHUNK)
        _mlstm_direction(*fwd_refs, hf_ref, sf_ref, fwd, reverse=False)
        _mlstm_direction(*bwd_refs, hb_ref, sb_ref, bwd, reverse=True)


def _mlstm_kernel(*refs):
    fwd_refs, bwd_refs, ctx_refs = refs[0:5], refs[5:10], refs[10:15]
    hf_ref, hb_ref, hf_ctx_ref, hb_ctx_ref, sf_ref, sb_ref = refs[15:]

    @pl.when(pl.program_id(1) == 0)
    def _():
        sf_ref[...] = jnp.zeros_like(sf_ref)
        sb_ref[...] = jnp.zeros_like(sb_ref)
        _mlstm_both(ctx_refs, ctx_refs, hf_ctx_ref, hb_ctx_ref, sf_ref, sb_ref)

    _mlstm_both(fwd_refs, bwd_refs, hf_ref, hb_ref, sf_ref, sb_ref)


def _mlstm(qm, km, vm, gc, gr, seg):
    b_, s_, lc = seg["B"], seg["S"], seg["Lc"]
    n_lat = b_ * s_
    step = min(ML_STEP_CHUNKS * ML_CHUNK, s_)
    assert s_ % step == 0 and lc % ML_CHUNK == 0 and n_lat % lc == 0
    n_steps = s_ // step
    arrays = (qm, km, vm, gc)
    widths = (ML_QK_WIDTH, ML_QK_WIDTH, ML_WIDTH, LANES)

    def specs(rows, block_of):
        return [pl.BlockSpec((rows, w), lambda b, c: (block_of(b, c), 0)) for w in widths] \
            + [pl.BlockSpec((ML_GATES, rows), lambda b, c: (0, block_of(b, c)))]

    fwd_block = lambda b, c: b * n_steps + c
    bwd_block = lambda b, c: b * n_steps + n_steps - 1 - c
    ctx_block = lambda b, c: n_lat // lc + b
    lat_out = lambda block_of: pl.BlockSpec((step, ML_WIDTH), lambda b, c: (block_of(b, c), 0))
    ctx_out = pl.BlockSpec((lc, ML_WIDTH), lambda b, c: (b, 0))
    state = pltpu.VMEM((ML_HEADS // 2, 2 * ML_QK_DIM, 2 * ML_V_DIM), F32)
    hf, hb, hf_ctx, hb_ctx = pl.pallas_call(
        _mlstm_kernel,
        grid=(b_, n_steps),
        in_specs=specs(step, fwd_block) + specs(step, bwd_block) + specs(lc, ctx_block),
        out_specs=[lat_out(fwd_block), lat_out(bwd_block), ctx_out, ctx_out],
        out_shape=[jax.ShapeDtypeStruct((n_lat, ML_WIDTH), BF16)] * 2
                  + [jax.ShapeDtypeStruct((b_ * lc, ML_WIDTH), BF16)] * 2,
        scratch_shapes=[state, state],
        compiler_params=_params(2),
        name="mlstm_scan",
    )(*arrays, gr, *arrays, gr, *arrays, gr)
    return hf, hf_ctx, hb, hb_ctx


def _swiglu(h, w1_ref, w3_ref, w2_ref):
    f = w1_ref.shape[1]
    fc = FFN_CHUNK_COLS
    y = None
    for c in range(f // fc):
        cols = slice(c * fc, (c + 1) * fc)
        a = _dot(h, w1_ref[:, cols])
        b = _dot(h, w3_ref[:, cols])
        part = _dot((a * _sigmoid(a) * b).astype(BF16), w2_ref[cols, :])
        y = part if y is None else y + part
    return y


N_MIX_REFS = 8


def _mix_residual(x_ref, mix_refs, n_lat_tiles, mod, wo_ref):
    att_lat, att_ctx, hf_lat, hf_ctx, hb_lat, hb_ctx, om_ref, gml_ref = mix_refs
    in_lat = pl.program_id(0) < n_lat_tiles
    pick = lambda lat, ctx, cols=slice(None): jnp.where(in_lat, lat[:, cols], ctx[:, cols])
    parts = [pick(att_lat, att_ctx)]
    for h in range(ML_HEADS):
        sl = slice(h * ML_V_DIM, (h + 1) * ML_V_DIM)
        tot = pick(hf_lat, hf_ctx, sl).astype(F32) + pick(hb_lat, hb_ctx, sl).astype(F32)
        parts.append((_rms(tot, gml_ref[:, sl]) * _sigmoid(om_ref[:, sl].astype(F32))).astype(BF16))
    return x_ref[...] + mod[2:3] * _dot(jnp.concatenate(parts, axis=1), wo_ref[...])


def _mix_specs(tm, row, mix, n_lat):
    lat_tiles = n_lat // tm
    specs = []
    for k in range(0, 6, 2):
        width = mix[k].shape[1]
        ctx_tiles = mix[k + 1].shape[0] // tm
        specs += [pl.BlockSpec((tm, width), lambda i, *_: (jnp.minimum(i, lat_tiles - 1), 0)),
                  pl.BlockSpec((tm, width), lambda i, *_, t=ctx_tiles: (jnp.clip(i - lat_tiles, 0, t - 1), 0))]
    return specs + [pl.BlockSpec((tm, ML_WIDTH), row), _resident((1, ML_WIDTH))]


def _dense_layer_kernel(x_ref, *refs, final, n_lat_tiles):
    mix_refs = refs[:N_MIX_REFS]
    mod_ref, g_ref, wo_ref, w1_ref, w3_ref, w2_ref, fg_ref = refs[N_MIX_REFS:N_MIX_REFS + 7]
    rest = refs[N_MIX_REFS + 7:]
    n_cast = (len(rest) - 1) // 2
    o_ref = rest[n_cast]
    for src, dst in zip(rest[:n_cast], rest[n_cast + 1:]):
        dst[...] = src[...].astype(dst.dtype)
    mod = mod_ref[0]
    x1 = _mix_residual(x_ref, mix_refs, n_lat_tiles, mod, wo_ref)
    hx = _adaln(x1, g_ref[...], mod[3:4], mod[4:5]).astype(BF16)
    out = x1 + mod[5:6] * _swiglu(hx, w1_ref, w3_ref, w2_ref)
    if final:
        out = _rms(out, fg_ref[...])
    o_ref[...] = out


def _row_maps(seg, tm):
    nlat = seg["n_lat"] // tm
    s_tiles = seg["S"] // tm
    return (lambda i, *_: (i, 0)), (lambda i, *_: (jnp.where(i < nlat, i // s_tiles, seg["B"]), 0, 0))


def _cast_slabs(stacked, index, n_steps):
    rows_per_layer = int(np.prod(stacked.shape[1:-1]))
    width = stacked.shape[-1]
    slab = next(r for r in range(2 * SUBLANES, rows_per_layer + 1, 2 * SUBLANES)
                if rows_per_layer % r == 0 and rows_per_layer // r <= n_steps)
    n_slabs = rows_per_layer // slab
    src = pl.BlockSpec((slab, width), lambda i: (index * n_slabs + jnp.minimum(i, n_slabs - 1), 0))
    dst = pl.BlockSpec((slab, width), lambda i: (jnp.minimum(i, n_slabs - 1), 0))
    flat = stacked.reshape(-1, width)
    return src, dst, flat, jax.ShapeDtypeStruct((rows_per_layer, width), BF16)


def _dense_layer(xs, mix, mod, g2, wo, layer, w1, w3, w2, ffn_index, final_g, seg, n_rows, final, cast=None):
    d = xs.shape[1]
    tm = seg["tm"]
    row, mod_map = _row_maps(seg, tm)
    n_steps = n_rows // tm
    cast_specs = [_cast_slabs(p, cast[1], n_steps) for p in cast[0]] if cast else []
    outs = pl.pallas_call(
        functools.partial(_dense_layer_kernel, final=final, n_lat_tiles=seg["n_lat"] // tm),
        grid=(n_steps,),
        in_specs=[pl.BlockSpec((tm, d), row)] + _mix_specs(tm, row, mix, seg["n_lat"])
                 + [pl.BlockSpec((1, 6, d), mod_map),
                    _resident((1, d)), _layer_resident(wo, layer), _layer_resident(w1, ffn_index),
                    _layer_resident(w3, ffn_index), _layer_resident(w2, ffn_index), _resident((1, d))]
                 + [c[0] for c in cast_specs],
        out_specs=[pl.BlockSpec((tm, d), row)] + [c[1] for c in cast_specs],
        out_shape=[jax.ShapeDtypeStruct((n_rows, d), F32)] + [c[3] for c in cast_specs],
        compiler_params=_params(1),
        name="dense_layer",
    )(xs, *mix, mod, g2, wo, w1, w3, w2, final_g, *[c[2] for c in cast_specs])
    if not cast:
        return outs[0], None
    return outs[0], tuple(o.reshape(p.shape[1:]) for o, p in zip(outs[1:], cast[0]))


def _router_kernel(x_ref, *refs, n_lat_tiles):
    mix_refs = refs[:N_MIX_REFS]
    (mod_ref, g_ref, wo_ref, wr_ref, br_ref, earlier_ref,
     x1_ref, h_ref, route_ref, counts_ref, count_ref) = refs[N_MIX_REFS:]
    mod = mod_ref[0]
    x1 = _mix_residual(x_ref, mix_refs, n_lat_tiles, mod, wo_ref)
    x1_ref[...] = x1
    hx = _adaln(x1, g_ref[...], mod[3:4], mod[4:5])
    h_ref[...] = _pack_bf16_pairs(hx)
    hx_hi = hx.astype(BF16)
    hx_lo = (hx - hx_hi.astype(F32)).astype(BF16)
    both = _dot_nt(wr_ref[...], hx_hi)
    logits = (both[:N_EXPERTS] + both[N_EXPERTS:] + _dot_nt(wr_ref[:N_EXPERTS, :], hx_lo)
              + br_ref[...])
    sub = lax.broadcasted_iota(jnp.int32, logits.shape, 0)
    top1 = jnp.max(logits, axis=0, keepdims=True)
    idx1 = jnp.min(jnp.where(logits == top1, sub, N_EXPERTS), axis=0, keepdims=True)
    rest = jnp.where(sub == idx1, -jnp.inf, logits)
    top2 = jnp.max(rest, axis=0, keepdims=True)
    idx2 = jnp.min(jnp.where(rest == top2, sub, N_EXPERTS), axis=0, keepdims=True)
    e2 = jnp.exp(top2 - top1)
    w_first = 1.0 / (1.0 + e2)
    @pl.when(pl.program_id(0) == 0)
    def _():
        count_ref[...] = jnp.zeros_like(count_ref)

    first, second = sub == idx1, sub == idx2
    hot = jnp.logical_or(first, second).astype(F32)
    before = _dot(hot.astype(BF16), earlier_ref[...]) + count_ref[:, :1]
    rank1 = jnp.sum(jnp.where(first, before, 0.0), axis=0, keepdims=True)
    rank2 = jnp.sum(jnp.where(second, before, 0.0), axis=0, keepdims=True)
    count_ref[...] += jnp.sum(hot, axis=1, keepdims=True)
    counts_ref[...] = count_ref[...]
    route = jnp.where(sub == 0, idx1.astype(F32), jnp.where(sub == 1, idx2.astype(F32), 0.0))
    route = jnp.where(sub == 2, w_first, jnp.where(sub == 3, e2 * w_first, route))
    route_ref[...] = jnp.where(sub == 4, rank1, jnp.where(sub == 5, rank2, route))


def _router(xs, mix, mod, g2, wo, layer, wr, br, seg, n_rows):
    d = xs.shape[1]
    tm = seg["tm"]
    row, mod_map = _row_maps(seg, tm)
    return pl.pallas_call(
        functools.partial(_router_kernel, n_lat_tiles=seg["n_lat"] // tm),
        grid=(n_rows // tm,),
        in_specs=[pl.BlockSpec((tm, d), row)] + _mix_specs(tm, row, mix, seg["n_lat"])
                 + [pl.BlockSpec((1, 6, d), mod_map),
                    _resident((1, d)), _layer_resident(wo, layer), _resident(wr.shape), _resident(br.shape),
                    _resident((tm, tm))],
        out_specs=[pl.BlockSpec((tm, d), row), pl.BlockSpec((tm, d // 2), row),
                   pl.BlockSpec((N_EXPERTS, tm), lambda i: (0, i)),
                   pl.BlockSpec((N_EXPERTS, LANES), lambda i: (0, 0))],
        out_shape=[jax.ShapeDtypeStruct((n_rows, d), F32), jax.ShapeDtypeStruct((n_rows, d // 2), jnp.int32),
                   jax.ShapeDtypeStruct((N_EXPERTS, n_rows), F32),
                   jax.ShapeDtypeStruct((N_EXPERTS, LANES), F32)],
        scratch_shapes=[pltpu.VMEM((N_EXPERTS, LANES), F32)],
        compiler_params=_params(1),
        name="mix_router",
    )(xs, *mix, mod, g2, wo, wr, br, jnp.triu(jnp.ones((tm, tm), BF16), 1))


BF16_BITS = 16
HIGH_HALF_WORD = -(1 << BF16_BITS)


def _pack_bf16_pairs(h):
    half = h.shape[1] // 2
    hi = lax.bitcast_convert_type(h[:, :half].astype(BF16).astype(F32), jnp.int32)
    lo = lax.bitcast_convert_type(h[:, half:].astype(BF16).astype(F32), jnp.int32)
    return (hi & jnp.int32(HIGH_HALF_WORD)) | lax.shift_right_logical(lo, BF16_BITS)


def _unpack_bf16_pairs(p):
    hi = lax.bitcast_convert_type(p & jnp.int32(HIGH_HALF_WORD), F32)
    lo = lax.bitcast_convert_type(lax.shift_left(p, BF16_BITS), F32)
    return hi, lo


def _route_plan(route, counts, tm):
    n_rows = route.shape[1]
    n_slots = TOP_K * n_rows
    idx1, idx2, rank1, rank2 = (route[r].astype(jnp.int32) for r in (0, 1, 4, 5))
    offs = jnp.concatenate([jnp.zeros((1,), jnp.int32), jnp.cumsum(counts[:, 0].astype(jnp.int32))])
    first_slot = lambda idx: sum(jnp.where(idx == e, offs[e], 0) for e in range(N_EXPERTS))
    slot_a = first_slot(idx1) + rank1
    slot_b = first_slot(idx2) + rank2
    n_tiles = n_slots // tm
    t_start = jnp.arange(n_tiles, dtype=jnp.int32) * tm
    e_first = jnp.sum(offs[None, 1:] <= t_start[:, None], axis=1).astype(jnp.int32)
    base_hi = jnp.minimum(t_start + tm, offs[e_first + 1])
    e_next = jnp.arange(1, N_EXPERTS, dtype=jnp.int32)
    start = offs[1:N_EXPERTS]
    x_tile = jnp.minimum(start // tm, n_tiles - 1)
    x_hi = jnp.where(start % tm != 0, jnp.minimum(offs[2:], (x_tile + 1) * tm), start)
    tiles = jnp.concatenate([t_start // tm, x_tile])
    experts = jnp.concatenate([e_first, e_next])
    lo = jnp.concatenate([t_start, start])
    hi = jnp.concatenate([base_hi, x_hi])
    order = jnp.argsort(tiles * (2 * N_EXPERTS) + experts)
    tiles, experts, lo, hi = tiles[order], experts[order], lo[order], hi[order]
    change = tiles[1:] != tiles[:-1]
    one = jnp.ones((1,), bool)
    first = jnp.concatenate([one, change]).astype(jnp.int32)
    last = jnp.concatenate([change, one]).astype(jnp.int32)
    return slot_a, slot_b, (tiles, experts, lo, hi, first, last)


def _scatter_rows(rows, idx_a, idx_b):
    n_rows, width = rows.shape
    workers = SC_CORES * SC_SUBCORES
    per_worker = n_rows // workers
    assert n_rows % (workers * SC_GATHER_ROWS) == 0
    mesh = plsc.VectorSubcoreMesh(core_axis_name="c", subcore_axis_name="s")

    @functools.partial(
        pl.kernel, mesh=mesh,
        out_type=jax.ShapeDtypeStruct((TOP_K * n_rows, width), rows.dtype),
        scratch_types=[pltpu.VMEM((TOP_K, SC_GATHER_ROWS), jnp.int32),
                       pltpu.VMEM((SC_GATHER_ROWS, width), rows.dtype),
                       pltpu.SemaphoreType.DMA, pltpu.SemaphoreType.DMA],
        name="scatter_rows")
    def scatter(rows_hbm, idx_a_hbm, idx_b_hbm, out_hbm, idx_v, rows_v, sem_a, sem_b):
        base = (lax.axis_index("s") * SC_CORES + lax.axis_index("c")) * per_worker

        @pl.loop(0, per_worker // SC_GATHER_ROWS)
        def _(i):
            off = pl.multiple_of(base + i * SC_GATHER_ROWS, SC_GATHER_ROWS)
            pltpu.sync_copy(idx_a_hbm.at[pl.ds(off, SC_GATHER_ROWS)], idx_v.at[0])
            pltpu.sync_copy(idx_b_hbm.at[pl.ds(off, SC_GATHER_ROWS)], idx_v.at[1])
            pltpu.sync_copy(rows_hbm.at[pl.ds(off, SC_GATHER_ROWS)], rows_v)
            first = pltpu.async_copy(rows_v, out_hbm.at[idx_v.at[0]], sem_a)
            second = pltpu.async_copy(rows_v, out_hbm.at[idx_v.at[1]], sem_b)
            first.wait()
            second.wait()

    return scatter(rows, idx_a, idx_b)


def _gather_rows(table, idx):
    n_idx = idx.shape[0]
    width = table.shape[1]
    workers = SC_CORES * SC_SUBCORES
    per_worker = n_idx // workers
    assert n_idx % (workers * SC_GATHER_ROWS * SC_IN_FLIGHT) == 0
    mesh = plsc.VectorSubcoreMesh(core_axis_name="c", subcore_axis_name="s")

    @functools.partial(
        pl.kernel, mesh=mesh,
        out_type=jax.ShapeDtypeStruct((n_idx, width), table.dtype),
        scratch_types=[pltpu.VMEM((SC_IN_FLIGHT, SC_GATHER_ROWS), jnp.int32),
                       pltpu.VMEM((SC_IN_FLIGHT, SC_GATHER_ROWS, width), table.dtype)]
                      + [pltpu.SemaphoreType.DMA] * (2 * SC_IN_FLIGHT),
        name="gather_rows")
    def gather(table_hbm, idx_hbm, out_hbm, idx_v, rows_v, *sems):
        base = (lax.axis_index("s") * SC_CORES + lax.axis_index("c")) * per_worker

        @pl.loop(0, per_worker // (SC_GATHER_ROWS * SC_IN_FLIGHT))
        def _(i):
            offs = [pl.multiple_of(base + (i * SC_IN_FLIGHT + b) * SC_GATHER_ROWS, SC_GATHER_ROWS)
                    for b in range(SC_IN_FLIGHT)]
            for b, off in enumerate(offs):
                pltpu.sync_copy(idx_hbm.at[pl.ds(off, SC_GATHER_ROWS)], idx_v.at[b])
            reads = [pltpu.async_copy(table_hbm.at[idx_v.at[b]], rows_v.at[b], sems[b])
                     for b in range(SC_IN_FLIGHT)]
            writes = []
            for b, off in enumerate(offs):
                reads[b].wait()
                writes.append(pltpu.async_copy(rows_v.at[b], out_hbm.at[pl.ds(off, SC_GATHER_ROWS)],
                                               sems[SC_IN_FLIGHT + b]))
            for w in writes:
                w.wait()

    return gather(table, idx)


def _experts_kernel(tile_ref, exp_ref, lo_ref, hi_ref, first_ref, last_ref,
                    x_ref, w1_ref, w3_ref, w2_ref, o_ref, acc_ref):
    i = pl.program_id(0)
    tm, half = x_ref.shape
    f = w1_ref.shape[2]
    fc = FFN_CHUNK_COLS

    @pl.when(first_ref[i] == 1)
    def _():
        acc_ref[...] = jnp.zeros_like(acc_ref)

    lo, hi = lo_ref[i], hi_ref[i]

    @pl.when(hi > lo)
    def _():
        x_hi, x_lo = _unpack_bf16_pairs(x_ref[...])
        x = jnp.concatenate([x_hi.astype(BF16), x_lo.astype(BF16)], axis=1)
        y = None
        for c in range(f // fc):
            cols = slice(c * fc, (c + 1) * fc)
            a = _dot(x, w1_ref[0, :, cols])
            b = _dot(x, w3_ref[0, :, cols])
            part = _dot((a * _sigmoid(a) * b).astype(BF16), w2_ref[0, cols, :])
            y = part if y is None else y + part
        rows = tile_ref[i] * tm + lax.broadcasted_iota(jnp.int32, (tm, 1), 0)
        keep = jnp.logical_and(rows >= lo, rows < hi)
        acc_ref[...] += jnp.where(keep, y, 0.0)

    @pl.when(last_ref[i] == 1)
    def _():
        o_ref[...] = _pack_bf16_pairs(acc_ref[...])


def _experts(xs_sorted, items, w1, w3, w2):
    n_slots, half = xs_sorted.shape
    n_exp, d, f = w1.shape
    tm = MOE_ROW_TILE
    assert f % FFN_CHUNK_COLS == 0
    tile_map = lambda i, tiles, *_: (tiles[i], 0)
    exp_map = lambda i, tiles, experts, *_: (experts[i], 0, 0)
    grid_spec = pltpu.PrefetchScalarGridSpec(
        num_scalar_prefetch=len(items),
        grid=(items[0].shape[0],),
        in_specs=[pl.BlockSpec((tm, half), tile_map),
                  pl.BlockSpec((1, d, f), exp_map), pl.BlockSpec((1, d, f), exp_map),
                  pl.BlockSpec((1, f, d), exp_map)],
        out_specs=pl.BlockSpec((tm, half), tile_map),
        scratch_shapes=[pltpu.VMEM((tm, d), F32)],
    )
    return pl.pallas_call(
        _experts_kernel,
        grid_spec=grid_spec,
        out_shape=jax.ShapeDtypeStruct((n_slots, half), jnp.int32),
        compiler_params=_params(1),
        name="experts",
    )(*items, xs_sorted, w1, w3, w2)


def _combine_kernel(x1_ref, ya_ref, yb_ref, route_ref, mod_ref, fg_ref, o_ref, *, final):
    half = ya_ref.shape[1]
    route = route_ref[...]
    padded = jnp.concatenate([route, jnp.zeros((LANES - route.shape[0], route.shape[1]), F32)], axis=0)
    route = jnp.transpose(padded)
    wa, wb = route[:, 2:3], route[:, 3:4]
    a_hi, a_lo = _unpack_bf16_pairs(ya_ref[...])
    b_hi, b_lo = _unpack_bf16_pairs(yb_ref[...])
    gate = mod_ref[0][5:6]
    out_hi = x1_ref[:, :half] + gate[:, :half] * (wa * a_hi + wb * b_hi)
    out_lo = x1_ref[:, half:] + gate[:, half:] * (wa * a_lo + wb * b_lo)
    if final:
        total = jnp.sum(out_hi * out_hi, axis=-1, keepdims=True) + jnp.sum(out_lo * out_lo, axis=-1, keepdims=True)
        scale = lax.rsqrt(total / (2 * half) + EPS)
        out_hi = out_hi * scale * fg_ref[:, :half]
        out_lo = out_lo * scale * fg_ref[:, half:]
    o_ref[:, :half] = out_hi
    o_ref[:, half:] = out_lo


def _combine(x1, y_pairs, route, mod, final_g, seg, final):
    n_rows, d = x1.shape
    tm = seg["tm"]
    row, mod_map = _row_maps(seg, tm)
    second = n_rows // tm
    return pl.pallas_call(
        functools.partial(_combine_kernel, final=final),
        grid=(n_rows // tm,),
        in_specs=[pl.BlockSpec((tm, d), row), pl.BlockSpec((tm, d // 2), row),
                  pl.BlockSpec((tm, d // 2), lambda i: (second + i, 0)),
                  pl.BlockSpec((N_EXPERTS, tm), lambda i: (0, i)), pl.BlockSpec((1, 6, d), mod_map),
                  pl.BlockSpec((1, d), lambda i: (0, 0))],
        out_specs=pl.BlockSpec((tm, d), row),
        out_shape=jax.ShapeDtypeStruct((n_rows, d), F32),
        compiler_params=_params(1),
        name="moe_combine",
    )(x1, y_pairs, y_pairs, route, mod, final_g)


def _moe_layer(xs, mix, mod, g2, wo, layer, wr, br, w1, w3, w2, final_g, seg, n_rows, final):
    x1, h_pairs, route, counts = _router(xs, mix, mod, g2, wo, layer, wr, br, seg, n_rows)
    slot_a, slot_b, items = _route_plan(route, counts, MOE_ROW_TILE)
    y_sorted = _experts(_scatter_rows(h_pairs, slot_a, slot_b), items, w1, w3, w2)
    y_pairs = _gather_rows(y_sorted, jnp.concatenate([slot_a, slot_b]))
    return _combine(x1, y_pairs, route, mod, final_g, seg, final)


def kernel(x, c, ctx, c_ctx, norm1_g, norm2_g, w_mod, b_mod, w_in, conv_w, conv_b, b_gates, attn_sink,
           g_att, g_ml, w_out, ffn_w1, ffn_w3, ffn_w2, w_router, b_router, exp_w1, exp_w3, exp_w2,
           final_g):
    b_, s_, d = x.shape
    lc = ctx.shape[1]
    depth = w_in.shape[0]
    n_lat, n_ctx = b_ * s_, b_ * lc
    tm = min(ROW_TILE, s_)
    assert s_ % tm == 0 and n_ctx % tm == 0
    assert (TOP_K * n_lat) % MOE_ROW_TILE == 0 and (TOP_K * n_ctx) % MOE_ROW_TILE == 0
    assert s_ % ML_CHUNK == 0 and lc % ML_CHUNK == 0 and n_lat % lc == 0 and b_ < MOD_ROWS
    seg = dict(B=b_, S=s_, Lc=lc, n_lat=n_lat, tm=tm)
    tm_in = INPROJ_TILE if s_ % INPROJ_TILE == 0 and n_ctx % INPROJ_TILE == 0 else tm
    seg_in = dict(seg, tm=tm_in)

    cond = jnp.zeros((MOD_ROWS, d), F32).at[:b_].set(c).at[b_].set(c_ctx)
    mods = _modulation(cond, w_mod, b_mod).reshape(depth, MOD_ROWS, 6, d)
    rope = _rope_tables(s_, tm_in)
    xs = (x.reshape(n_lat, d), ctx.reshape(n_ctx, d))
    final_row = final_g.reshape(1, d)

    w_in_b, w_out_b = w_in.astype(BF16), w_out.astype(BF16)
    ffn_b = (ffn_w1.astype(BF16), ffn_w3.astype(BF16), ffn_w2.astype(BF16))
    exp_b = None

    for layer in range(depth):
        last = layer == depth - 1
        w_gates = w_in[layer][:, MAIN_WIDTH:]
        w_gc = jnp.pad(w_gates, ((0, 0), (0, LANES - ML_GATES))).astype(BF16)
        b_gc = jnp.pad(b_gates[layer], (0, LANES - ML_GATES)).reshape(1, LANES)
        qa, ka, va, qm, km, vm, om, gc, gr, *stream = _input_projection(
            xs, mods[layer], norm1_g[layer].reshape(1, d), w_in_b, layer, w_gc,
            w_gates.T.astype(BF16), b_gc, b_gates[layer].reshape(ML_GATES, 1),
            conv_w[layer], conv_b[layer].reshape(1, -1), rope, seg_in)
        if stream:
            xs = stream[0]
        att = _attention(qa, ka, va, attn_sink[layer], g_att[layer].reshape(1, ATT_WIDTH), seg, not last)
        mix = (*att, *_mlstm(qm, km, vm, gc, gr, seg), om, g_ml[layer].reshape(1, ML_WIDTH))
        n_rows = n_lat if last else n_lat + n_ctx
        g2 = norm2_g[layer].reshape(1, d)
        i = layer // 2
        if layer % 2 == 0:
            cast = None if last else ((exp_w1, exp_w3, exp_w2), i)
            xs, exp_b = _dense_layer(xs, mix, mods[layer], g2, w_out_b, layer, *ffn_b, i, final_row, seg,
                                     n_rows, last, cast)
        else:
            wr = w_router[i].T
            wr_hi = wr.astype(BF16)
            wr = jnp.concatenate([wr_hi, (wr - wr_hi.astype(F32)).astype(BF16)], axis=0)
            br = b_router[i].reshape(N_EXPERTS, 1)
            xs = _moe_layer(xs, mix, mods[layer], g2, w_out_b, layer, wr, br, *exp_b, final_row, seg,
                            n_rows, last)
    return xs[:n_lat].reshape(b_, s_, d)
```

```python
import functools

import jax
import jax.numpy as jnp
import numpy as np
from jax import lax
from jax.experimental import pallas as pl
from jax.experimental.pallas import tpu as pltpu
from jax.experimental.pallas import tpu_sc as plsc

F32 = jnp.float32
BF16 = jnp.bfloat16

GRID_W = 64
ATT_HEADS = 8
ATT_KV_HEADS = 2
ATT_HEAD_DIM = 64
ATT_GROUP = ATT_HEADS // ATT_KV_HEADS
WINDOW = 128
ATT_BLOCK = 128
ROPE_THETA = 10000.0
ML_HEADS = 4
ML_QK_DIM = 64
ML_V_DIM = 128
ML_CONV = 5
GATE_CAP = 15.0
ATT_WIDTH = ATT_HEADS * ATT_HEAD_DIM
ATT_KV_WIDTH = ATT_KV_HEADS * ATT_HEAD_DIM
ML_QK_WIDTH = ML_HEADS * ML_QK_DIM
ML_WIDTH = ML_HEADS * ML_V_DIM
ML_GATES = 4 * ML_HEADS
MAIN_WIDTH = ATT_WIDTH + 2 * ATT_KV_WIDTH + 2 * ML_QK_WIDTH + 2 * ML_WIDTH
N_EXPERTS = 8
TOP_K = 2
EPS = 1e-6

LANES = 128
SUBLANES = 8
VMEM_LIMIT = 56 * 1024 * 1024
NEG = -1e30
SC_CORES = 2
SC_SUBCORES = 16
SC_GATHER_ROWS = 64
SC_IN_FLIGHT = 2

ROW_TILE = 512
INPROJ_TILE = 1024
MOE_ROW_TILE = 512
FFN_CHUNK_COLS = 256
ATT_STEP_BLOCKS = 16
ML_CHUNK = 128
ML_STEP_CHUNKS = 8
CONV_HALO = SUBLANES
MOD_ROWS = 16
MOD_COL_TILE = 1536


def _dot(a, b):
    return jnp.dot(a, b, preferred_element_type=F32)


def _dot_nt(a, b):
    return lax.dot_general(a, b, (((1,), (1,)), ((), ())), preferred_element_type=F32)


def _dot_tn(a, b):
    return lax.dot_general(a, b, (((0,), (0,)), ((), ())), preferred_element_type=F32)


def _bf16_terms(x):
    hi = x.astype(BF16)
    rest = x - hi.astype(F32)
    mid = rest.astype(BF16)
    return hi, mid, (rest - mid.astype(F32)).astype(BF16)


def _dot_exact_rhs(mask, x):
    return sum(_dot(mask, t) for t in _bf16_terms(x))


def _dot_exact_lhs(x, mask):
    return sum(_dot(t, mask) for t in _bf16_terms(x))


def _sigmoid(x):
    return 1.0 / (1.0 + jnp.exp(-x))


def _rms(x, g):
    return x * lax.rsqrt(jnp.mean(x * x, axis=-1, keepdims=True) + EPS) * g


def _adaln(x, g, shift, scale):
    return _rms(x, g * (1.0 + scale)) + shift


def _params(n_axes):
    return pltpu.CompilerParams(dimension_semantics=("arbitrary",) * n_axes,
                                vmem_limit_bytes=VMEM_LIMIT)


def _resident(shape):
    zeros = (0,) * len(shape)
    return pl.BlockSpec(shape, lambda *_: zeros, pipeline_mode=pl.Buffered(1))


def _layer_resident(stacked, layer, block=None):
    block = tuple(stacked.shape[1:]) if block is None else block
    index = (layer,) + (0,) * len(block)
    return pl.BlockSpec((None,) + block, lambda *_: index, pipeline_mode=pl.Buffered(1))


def _mod_kernel(c_ref, w_ref, b_ref, o_ref):
    c = c_ref[...]
    a = (c * _sigmoid(c)).astype(BF16)
    o_ref[0] = _dot(a, w_ref[0].astype(BF16)) + b_ref[0]


def _modulation(cond, w_mod, b_mod):
    depth, d, width = w_mod.shape
    tn = MOD_COL_TILE if width % MOD_COL_TILE == 0 else width
    return pl.pallas_call(
        _mod_kernel,
        grid=(depth, width // tn),
        in_specs=[pl.BlockSpec((MOD_ROWS, d), lambda l, j: (0, 0)),
                  pl.BlockSpec((1, d, tn), lambda l, j: (l, 0, j)),
                  pl.BlockSpec((1, 1, tn), lambda l, j: (l, 0, j))],
        out_specs=pl.BlockSpec((1, MOD_ROWS, tn), lambda l, j: (l, 0, j)),
        out_shape=jax.ShapeDtypeStruct((depth, MOD_ROWS, width), F32),
        compiler_params=_params(2),
        name="modulation",
    )(cond, w_mod, b_mod.reshape(depth, 1, width))


def _gate_act(u, is_forget):
    g = GATE_CAP * jnp.tanh(u / GATE_CAP)
    log_sig = jnp.minimum(g, 0.0) - jnp.log1p(jnp.exp(-jnp.abs(g)))
    return jnp.where(is_forget, log_sig, g)


def _conv_silu(xe, cw, cb, rows):
    n_ext = xe.shape[0]
    mid = ML_CONV // 2
    y = cb + cw[mid:mid + 1] * xe[CONV_HALO:CONV_HALO + rows]
    for tap in range(ML_CONV):
        if tap != mid:
            y = y + cw[tap:tap + 1] * pltpu.roll(xe, (mid - tap) % n_ext, 0)[CONV_HALO:CONV_HALO + rows]
    return y * _sigmoid(y)


def _inproj_kernel(*refs, n_lat_tiles, seq_lat, seq_ctx, split):
    n_x = 6 if split else 3
    x_refs, refs = refs[:n_x], refs[n_x:]
    (mod_ref, g_ref, w_ref, wgc_ref, wgr_ref, bgc_ref, bgr_ref, cw_ref, cb_ref, cos_ref, sa_ref, sb_ref,
     qa_ref, ka_ref, va_ref, qm_ref, km_ref, vm_ref, om_ref, gc_ref, gr_ref) = refs[:21]
    i = pl.program_id(0)
    if split:
        in_lat = i < n_lat_tiles
        x_tile, x_prev, x_next = (jnp.where(in_lat, x_refs[2 * k][...], x_refs[2 * k + 1][...]) for k in range(3))
        refs[21][...] = x_tile
    else:
        x_tile, x_prev, x_next = (r[...] for r in x_refs)
    mod = mod_ref[0]
    tm = x_tile.shape[0]
    seg = min(tm, seq_ctx)
    n_seg = tm // seg
    norm_gain = g_ref[...]
    normed = lambda rows: _adaln(rows, norm_gain, mod[0:1], mod[1:2]).astype(BF16)
    quarter = ATT_HEAD_DIM // 4

    def rope(u, rows):
        return (u * cos_ref[rows, :] + pltpu.roll(u, LANES - quarter, 1) * sa_ref[rows, :]
                + pltpu.roll(u, quarter, 1) * sb_ref[rows, :])

    c_kv = ATT_WIDTH
    c_qk = c_kv + 2 * ATT_KV_WIDTH
    c_vm = c_qk + 2 * ML_QK_WIDTH
    c_om = c_vm + ML_WIDTH
    w_qk = w_ref[:, c_qk:c_vm]
    seq_len = jnp.where(i < n_lat_tiles, seq_lat, seq_ctx)
    cw, cb = cw_ref[...], cb_ref[...]
    hx = [normed(x_tile[j * seg:(j + 1) * seg, :]) for j in range(n_seg)]
    qk = [_dot(h, w_qk) for h in hx]
    qk_prev = _dot(normed(x_prev), w_qk)
    qk_next = _dot(normed(x_next), w_qk)
    for j in range(n_seg):
        rows = slice(j * seg, (j + 1) * seg)
        first_row = i * tm + j * seg
        has_prev = (lax.rem(first_row, seq_len) != 0).astype(F32)
        has_next = (lax.rem(first_row + seg, seq_len) != 0).astype(F32)
        prev = qk_prev if j == 0 else qk[j - 1][seg - CONV_HALO:]
        nxt = qk_next if j == n_seg - 1 else qk[j + 1][:CONV_HALO]
        xe = jnp.concatenate([prev * has_prev, qk[j], nxt * has_next], axis=0)
        y = _conv_silu(xe, cw, cb, seg)
        qm_ref[rows, :] = (y[:, :ML_QK_WIDTH] * (ML_QK_DIM ** -0.5)).astype(BF16)
        km_ref[rows, :] = y[:, ML_QK_WIDTH:].astype(BF16)
        q = _dot(hx[j], w_ref[:, :c_kv])
        for c in range(ATT_WIDTH // LANES):
            sl = slice(c * LANES, (c + 1) * LANES)
            qa_ref[rows, sl] = (rope(q[:, sl], rows) * (ATT_HEAD_DIM ** -0.5)).astype(BF16)
        kv = _dot(hx[j], w_ref[:, c_kv:c_qk])
        ka_ref[rows, :] = rope(kv[:, :ATT_KV_WIDTH], rows).astype(BF16)
        va_ref[rows, :] = kv[:, ATT_KV_WIDTH:].astype(BF16)
        vm_ref[rows, :] = _dot(hx[j], w_ref[:, c_vm:c_om]).astype(BF16)
        om_ref[rows, :] = _dot(hx[j], w_ref[:, c_om:c_om + ML_WIDTH]).astype(BF16)
        gc = _dot(hx[j], wgc_ref[...]) + bgc_ref[...]
        lane = lax.broadcasted_iota(jnp.int32, gc.shape, 1)
        gc_ref[rows, :] = _gate_act(gc, (lane // ML_HEADS) % 2 == 1)
        gr = _dot_nt(wgr_ref[...], hx[j]) + bgr_ref[...]
        sub = lax.broadcasted_iota(jnp.int32, gr.shape, 0)
        gr_ref[:, rows] = _gate_act(gr, (sub // ML_HEADS) % 2 == 1)


def _input_projection(xs, mod, g1, w_in, layer, w_gc, w_gr, b_gc, b_gr, conv_w, conv_b, rope, seg):
    split = isinstance(xs, tuple)
    d = (xs[0] if split else xs).shape[1]
    tm = seg["tm"]
    nlat = seg["n_lat"] // tm
    n = sum(a.shape[0] for a in xs) if split else xs.shape[0]
    s_tiles = seg["S"] // tm
    halos_per_tile = tm // CONV_HALO
    row = lambda i: (i, 0)

    def x_specs(n_rows, first_tile):
        n_tiles = n_rows // tm
        t = lambda i: jnp.clip(i - first_tile, 0, n_tiles - 1)
        return [pl.BlockSpec((tm, d), lambda i: (t(i), 0)),
                pl.BlockSpec((CONV_HALO, d), lambda i: (jnp.maximum(t(i) * halos_per_tile - 1, 0), 0)),
                pl.BlockSpec((CONV_HALO, d),
                             lambda i: (jnp.minimum((t(i) + 1) * halos_per_tile, n_rows // CONV_HALO - 1), 0))]

    if split:
        lat, ctx = x_specs(xs[0].shape[0], 0), x_specs(xs[1].shape[0], nlat)
        x_in_specs = [s for pair in zip(lat, ctx) for s in pair]
        x_args = [xs[0], xs[1]] * 3
    else:
        x_in_specs, x_args = x_specs(n, 0), [xs] * 3
    mod_map = lambda i: (jnp.where(i < nlat, i // s_tiles, seg["B"]), 0, 0)
    rope_map = lambda i: (jnp.where(i < nlat, i % s_tiles, s_tiles), 0)
    widths = [(ATT_WIDTH, BF16), (ATT_KV_WIDTH, BF16), (ATT_KV_WIDTH, BF16), (ML_QK_WIDTH, BF16),
              (ML_QK_WIDTH, BF16), (ML_WIDTH, BF16), (ML_WIDTH, BF16), (LANES, F32)]
    out_shape = [jax.ShapeDtypeStruct((n, w), t) for w, t in widths]
    out_specs = [pl.BlockSpec((tm, w), row) for w, _ in widths]
    out_shape.append(jax.ShapeDtypeStruct((ML_GATES, n), F32))
    out_specs.append(pl.BlockSpec((ML_GATES, tm), lambda i: (0, i)))
    if split:
        out_shape.append(jax.ShapeDtypeStruct((n, d), F32))
        out_specs.append(pl.BlockSpec((tm, d), row))
    return pl.pallas_call(
        functools.partial(_inproj_kernel, n_lat_tiles=nlat, seq_lat=seg["S"], seq_ctx=seg["Lc"], split=split),
        grid=(n // tm,),
        in_specs=x_in_specs
                 + [pl.BlockSpec((1, 6, d), mod_map),
                  _resident((1, d)),
                  _layer_resident(w_in, layer, (d, MAIN_WIDTH)), _resident(w_gc.shape), _resident(w_gr.shape),
                  _resident(b_gc.shape), _resident(b_gr.shape),
                  _resident(conv_w.shape), _resident(conv_b.shape),
                  pl.BlockSpec((tm, LANES), rope_map),
                  pl.BlockSpec((tm, LANES), rope_map),
                  pl.BlockSpec((tm, LANES), rope_map)],
        out_specs=out_specs,
        out_shape=out_shape,
        compiler_params=_params(1),
        name="input_projection",
    )(*x_args, mod, g1, w_in, w_gc, w_gr, b_gc, b_gr, conv_w, conv_b, *rope)


def _rope_tables(s, tm):
    quarter = ATT_HEAD_DIM // 4
    t = jnp.arange(s)
    row = (t // GRID_W).astype(F32)
    col = (t % GRID_W).astype(F32)
    inv = ROPE_THETA ** (-jnp.arange(quarter, dtype=F32) / quarter)
    ang_r = row[:, None] * inv[None, :]
    ang_c = col[:, None] * inv[None, :]
    zero = jnp.zeros_like(ang_r)
    cos = jnp.concatenate([jnp.cos(ang_r)] * 2 + [jnp.cos(ang_c)] * 2, axis=1)
    sin_up = jnp.concatenate([-jnp.sin(ang_r), zero, -jnp.sin(ang_c), zero], axis=1)
    sin_dn = jnp.concatenate([zero, jnp.sin(ang_r), zero, jnp.sin(ang_c)], axis=1)
    reps = LANES // ATT_HEAD_DIM
    ident = [jnp.ones((tm, LANES), F32), jnp.zeros((tm, LANES), F32), jnp.zeros((tm, LANES), F32)]
    return tuple(jnp.concatenate([jnp.tile(a, (1, reps)), i], axis=0)
                 for a, i in zip((cos, sin_up, sin_dn), ident))


def _attn_kernel(sink_ref, *refs, window, blocks):
    blk = ATT_BLOCK
    if window:
        q_ref, kp_ref, kc_ref, kn_ref, vp_ref, vc_ref, vn_ref, kx_ref, vx_ref, g_ref, o_ref = refs
        j = pl.program_id(1)
        k_own, v_own = kc_ref[...], vc_ref[...]
        k_blocks = [kp_ref[...]] + [k_own[t * blk:(t + 1) * blk] for t in range(blocks)] + [kn_ref[...]]
        v_blocks = [vp_ref[...]] + [v_own[t * blk:(t + 1) * blk] for t in range(blocks)] + [vn_ref[...]]
    else:
        q_ref, kx_ref, vx_ref, g_ref, o_ref = refs
    for t in range(blocks):
        if window:
            has_prev = j > 0 if t == 0 else True
            has_next = j < pl.num_programs(1) - 1 if t == blocks - 1 else True
            win = (k_blocks[t:t + 3], v_blocks[t:t + 3], has_prev, has_next)
        else:
            win = None
        att = _attend_block(sink_ref, q_ref[t * blk:(t + 1) * blk, :], win, kx_ref[...], vx_ref[...])
        o_ref[t * blk:(t + 1) * blk, :] = _rms(att, g_ref[...]).astype(o_ref.dtype)


def _attend_block(sink_ref, q, win, k_ctx, v_ctx):
    blk = ATT_BLOCK
    if win is None:
        k_all, v_all, bias = k_ctx, v_ctx, None
    else:
        k_win, v_win, has_prev, has_next = win
        rows = lax.broadcasted_iota(jnp.int32, (blk, blk), 0)
        cols = lax.broadcasted_iota(jnp.int32, (blk, blk), 1)
        ok_p = jnp.logical_and(cols >= rows, has_prev)
        ok_n = jnp.logical_and(cols <= rows, has_next)
        bias = jnp.concatenate([jnp.where(ok_p, 0.0, NEG), jnp.zeros((blk, blk), F32),
                                jnp.where(ok_n, 0.0, NEG), jnp.zeros((blk, k_ctx.shape[0]), F32)], axis=1)
        k_all = jnp.concatenate(list(k_win) + [k_ctx], axis=0)
        v_all = jnp.concatenate(list(v_win) + [v_ctx], axis=0)
    dh = ATT_HEAD_DIM
    outs = []
    for h in range(ATT_KV_HEADS):
        k_h = k_all[:, h * dh:(h + 1) * dh]
        v_h = v_all[:, h * dh:(h + 1) * dh]
        q_h = jnp.concatenate([q[:, (h * ATT_GROUP + g) * dh:(h * ATT_GROUP + g + 1) * dh]
                               for g in range(ATT_GROUP)], axis=0)
        s_all = _dot_nt(q_h, k_h)
        p_parts, inv_parts = [], []
        for g in range(ATT_GROUP):
            sink = sink_ref[h * ATT_GROUP + g]
            s = s_all[g * blk:(g + 1) * blk]
            if bias is not None:
                s = s + bias
            m = jnp.maximum(jnp.max(s, axis=-1, keepdims=True), sink)
            p = jnp.exp(s - m)
            denom = jnp.sum(p, axis=-1, keepdims=True) + jnp.exp(sink - m)
            p_parts.append(p.astype(BF16))
            inv_parts.append(1.0 / denom)
        o = _dot(jnp.concatenate(p_parts, axis=0), v_h)
        for g in range(ATT_GROUP):
            outs.append(o[g * blk:(g + 1) * blk] * inv_parts[g])
    return jnp.concatenate(outs, axis=1)


def _attention(qa, ka, va, sink, g_att, seg, with_ctx):
    n = qa.shape[0]
    b_, s_, lc = seg["B"], seg["S"], seg["Lc"]
    blk = ATT_BLOCK
    blocks_per_seq = s_ // blk
    ctx_map = lambda b, j, *_: (b_ * s_ // lc + b, 0)
    ctx_specs = [pl.BlockSpec((lc, ATT_KV_WIDTH), ctx_map)] * 2
    g_spec = pl.BlockSpec((1, ATT_WIDTH), lambda b, j, *_: (0, 0))

    def call(window, blocks, steps, first_step, in_specs, args):
        grid_spec = pltpu.PrefetchScalarGridSpec(
            num_scalar_prefetch=1,
            grid=(b_, steps),
            in_specs=[pl.BlockSpec((blocks * blk, ATT_WIDTH),
                                   lambda b, j, *_: (first_step + b * steps + j, 0))] + in_specs,
            out_specs=pl.BlockSpec((blocks * blk, ATT_WIDTH), lambda b, j, *_: (b * steps + j, 0)))
        return pl.pallas_call(
            functools.partial(_attn_kernel, window=window, blocks=blocks),
            grid_spec=grid_spec,
            out_shape=jax.ShapeDtypeStruct((b_ * steps * blocks * blk, ATT_WIDTH), BF16),
            compiler_params=_params(2),
            name="window_attention" if window else "context_attention",
        )(sink, qa, *args)

    qb = min(ATT_STEP_BLOCKS, blocks_per_seq)
    assert blocks_per_seq % qb == 0
    nqs = blocks_per_seq // qb

    def edge_map(off):
        return lambda b, j, *_: (b * blocks_per_seq + jnp.clip(j * qb + off, 0, blocks_per_seq - 1), 0)

    kv_specs = [pl.BlockSpec((blk, ATT_KV_WIDTH), edge_map(-1)),
                pl.BlockSpec((qb * blk, ATT_KV_WIDTH), lambda b, j, *_: (b * nqs + j, 0)),
                pl.BlockSpec((blk, ATT_KV_WIDTH), edge_map(qb))]
    att_lat = call(True, qb, nqs, 0, kv_specs + kv_specs + ctx_specs + [g_spec],
                   (ka, ka, ka, va, va, va, ka, va, g_att))
    if not with_ctx:
        return att_lat, att_lat
    cb = min(ATT_STEP_BLOCKS, lc // blk)
    assert (lc // blk) % cb == 0 and (b_ * s_) % (cb * blk) == 0
    att_ctx = call(False, cb, lc // (cb * blk), b_ * s_ // (cb * blk), ctx_specs + [g_spec], (ka, va, g_att))
    return att_lat, att_ctx


def _mlstm_direction(q_ref, k_ref, v_ref, gc_ref, gr_ref, out_ref, state_ref, tok, *, reverse):
    chunk = ML_CHUNK
    rows = lax.broadcasted_iota(jnp.int32, (chunk, chunk), 0)
    cols = lax.broadcasted_iota(jnp.int32, (chunk, chunk), 1)
    lower = rows >= cols
    upper = rows <= cols
    seen = upper if reverse else lower
    gc = gc_ref[tok, :]
    gr = gr_ref[:, tok]
    b_col = _dot_exact_rhs(seen.astype(BF16), gc)
    b_row = _dot_exact_lhs(gr, (lower if reverse else upper).astype(BF16))
    b_end = jnp.sum(gc, axis=0, keepdims=True)
    base = 2 * ML_HEADS if reverse else 0
    pair_width = 2 * ML_QK_DIM
    lane = lax.broadcasted_iota(jnp.int32, (chunk, pair_width), 1)
    state_row = lax.broadcasted_iota(jnp.int32, (pair_width, 1), 0)
    ones = jnp.ones((chunk, ML_V_DIM), BF16)

    for pair in range(ML_HEADS // 2):
        q_pair = q_ref[tok, pair * pair_width:(pair + 1) * pair_width]
        k_pair = k_ref[tok, pair * pair_width:(pair + 1) * pair_width]
        state = state_ref[pair]
        state_bf = state.astype(BF16)
        update = None
        decays = []
        for sub in range(2):
            h = 2 * pair + sub
            i_idx = base + h
            f_idx = base + ML_HEADS + h
            own = (lane >= ML_QK_DIM) if sub else (lane < ML_QK_DIM)
            q_h = jnp.where(own, q_pair, jnp.zeros_like(q_pair))
            vx = jnp.concatenate([v_ref[tok, h * ML_V_DIM:(h + 1) * ML_V_DIM], ones], axis=1)
            bc = b_col[:, f_idx:f_idx + 1]
            d = bc - (b_row[f_idx:f_idx + 1, :] - gr[i_idx:i_idx + 1, :])
            w = jnp.exp(jnp.where(seen, d, NEG))
            s = _dot_nt(q_h, k_pair) * w
            tot = _dot(s.astype(BF16), vx) + jnp.exp(bc) * _dot(q_h, state_bf)
            h_out = tot[:, :ML_V_DIM] / jnp.maximum(jnp.abs(tot[:, ML_V_DIM:]), 1.0)
            out_ref[tok, h * ML_V_DIM:(h + 1) * ML_V_DIM] = h_out.astype(out_ref.dtype)

            be = b_end[:, f_idx:f_idx + 1]
            kw = jnp.where(own, k_pair.astype(F32) * jnp.exp(be - bc + gc[:, i_idx:i_idx + 1]), 0.0)
            part = _dot_tn(kw.astype(BF16), vx)
            update = part if update is None else update + part
            decays.append(jnp.exp(be))
        decay = jnp.where(state_row < ML_QK_DIM, decays[0], decays[1])
        state_ref[pair] = decay * state + update


def _mlstm_both(fwd_refs, bwd_refs, hf_ref, hb_ref, sf_ref, sb_ref):
    n_chunks = hf_ref.shape[0] // ML_CHUNK
    for t in range(n_chunks):
        fwd = slice(t * ML_CHUNK, (t + 1) * ML_CHUNK)
        bwd = slice((n_chunks - 1 - t) * ML_CHUNK, (n_chunks - t) * ML_CHUNK)
        _mlstm_direction(*fwd_refs, hf_ref, sf_ref, fwd, reverse=False)
        _mlstm_direction(*bwd_refs, hb_ref, sb_ref, bwd, reverse=True)


def _mlstm_kernel(*refs):
    fwd_refs, bwd_refs, ctx_refs = refs[0:5], refs[5:10], refs[10:15]
    hf_ref, hb_ref, hf_ctx_ref, hb_ctx_ref, sf_ref, sb_ref = refs[15:]

    @pl.when(pl.program_id(1) == 0)
    def _():
        sf_ref[...] = jnp.zeros_like(sf_ref)
        sb_ref[...] = jnp.zeros_like(sb_ref)
        _mlstm_both(ctx_refs, ctx_refs, hf_ctx_ref, hb_ctx_ref, sf_ref, sb_ref)

    _mlstm_both(fwd_refs, bwd_refs, hf_ref, hb_ref, sf_ref, sb_ref)


def _mlstm(qm, km, vm, gc, gr, seg):
    b_, s_, lc = seg["B"], seg["S"], seg["Lc"]
    n_lat = b_ * s_
    step = min(ML_STEP_CHUNKS * ML_CHUNK, s_)
    assert s_ % step == 0 and lc % ML_CHUNK == 0 and n_lat % lc == 0
    n_steps = s_ // step
    arrays = (qm, km, vm, gc)
    widths = (ML_QK_WIDTH, ML_QK_WIDTH, ML_WIDTH, LANES)

    def specs(rows, block_of):
        return [pl.BlockSpec((rows, w), lambda b, c: (block_of(b, c), 0)) for w in widths] \
            + [pl.BlockSpec((ML_GATES, rows), lambda b, c: (0, block_of(b, c)))]

    fwd_block = lambda b, c: b * n_steps + c
    bwd_block = lambda b, c: b * n_steps + n_steps - 1 - c
    ctx_block = lambda b, c: n_lat // lc + b
    lat_out = lambda block_of: pl.BlockSpec((step, ML_WIDTH), lambda b, c: (block_of(b, c), 0))
    ctx_out = pl.BlockSpec((lc, ML_WIDTH), lambda b, c: (b, 0))
    state = pltpu.VMEM((ML_HEADS // 2, 2 * ML_QK_DIM, 2 * ML_V_DIM), F32)
    hf, hb, hf_ctx, hb_ctx = pl.pallas_call(
        _mlstm_kernel,
        grid=(b_, n_steps),
        in_specs=specs(step, fwd_block) + specs(step, bwd_block) + specs(lc, ctx_block),
        out_specs=[lat_out(fwd_block), lat_out(bwd_block), ctx_out, ctx_out],
        out_shape=[jax.ShapeDtypeStruct((n_lat, ML_WIDTH), BF16)] * 2
                  + [jax.ShapeDtypeStruct((b_ * lc, ML_WIDTH), BF16)] * 2,
        scratch_shapes=[state, state],
        compiler_params=_params(2),
        name="mlstm_scan",
    )(*arrays, gr, *arrays, gr, *arrays, gr)
    return hf, hf_ctx, hb, hb_ctx


def _swiglu(h, w1_ref, w3_ref, w2_ref):
    f = w1_ref.shape[1]
    fc = FFN_CHUNK_COLS
    y = None
    for c in range(f // fc):
        cols = slice(c * fc, (c + 1) * fc)
        a = _dot(h, w1_ref[:, cols])
        b = _dot(h, w3_ref[:, cols])
        part = _dot((a * _sigmoid(a) * b).astype(BF16), w2_ref[cols, :])
        y = part if y is None else y + part
    return y


N_MIX_REFS = 8


def _mix_residual(x_ref, mix_refs, n_lat_tiles, mod, wo_ref):
    att_lat, att_ctx, hf_lat, hf_ctx, hb_lat, hb_ctx, om_ref, gml_ref = mix_refs
    in_lat = pl.program_id(0) < n_lat_tiles
    pick = lambda lat, ctx, cols=slice(None): jnp.where(in_lat, lat[:, cols], ctx[:, cols])
    parts = [pick(att_lat, att_ctx)]
    for h in range(ML_HEADS):
        sl = slice(h * ML_V_DIM, (h + 1) * ML_V_DIM)
        tot = pick(hf_lat, hf_ctx, sl).astype(F32) + pick(hb_lat, hb_ctx, sl).astype(F32)
        parts.append((_rms(tot, gml_ref[:, sl]) * _sigmoid(om_ref[:, sl].astype(F32))).astype(BF16))
    return x_ref[...] + mod[2:3] * _dot(jnp.concatenate(parts, axis=1), wo_ref[...])


def _mix_specs(tm, row, mix, n_lat):
    lat_tiles = n_lat // tm
    specs = []
    for k in range(0, 6, 2):
        width = mix[k].shape[1]
        ctx_tiles = mix[k + 1].shape[0] // tm
        specs += [pl.BlockSpec((tm, width), lambda i, *_: (jnp.minimum(i, lat_tiles - 1), 0)),
                  pl.BlockSpec((tm, width), lambda i, *_, t=ctx_tiles: (jnp.clip(i - lat_tiles, 0, t - 1), 0))]
    return specs + [pl.BlockSpec((tm, ML_WIDTH), row), _resident((1, ML_WIDTH))]


def _dense_layer_kernel(x_ref, *refs, final, n_lat_tiles):
    mix_refs = refs[:N_MIX_REFS]
    mod_ref, g_ref, wo_ref, w1_ref, w3_ref, w2_ref, fg_ref = refs[N_MIX_REFS:N_MIX_REFS + 7]
    rest = refs[N_MIX_REFS + 7:]
    n_cast = (len(rest) - 1) // 2
    o_ref = rest[n_cast]
    for src, dst in zip(rest[:n_cast], rest[n_cast + 1:]):
        dst[...] = src[...].astype(dst.dtype)
    mod = mod_ref[0]
    x1 = _mix_residual(x_ref, mix_refs, n_lat_tiles, mod, wo_ref)
    hx = _adaln(x1, g_ref[...], mod[3:4], mod[4:5]).astype(BF16)
    out = x1 + mod[5:6] * _swiglu(hx, w1_ref, w3_ref, w2_ref)
    if final:
        out = _rms(out, fg_ref[...])
    o_ref[...] = out


def _row_maps(seg, tm):
    nlat = seg["n_lat"] // tm
    s_tiles = seg["S"] // tm
    return (lambda i, *_: (i, 0)), (lambda i, *_: (jnp.where(i < nlat, i // s_tiles, seg["B"]), 0, 0))


def _cast_slabs(stacked, index, n_steps):
    rows_per_layer = int(np.prod(stacked.shape[1:-1]))
    width = stacked.shape[-1]
    slab = next(r for r in range(2 * SUBLANES, rows_per_layer + 1, 2 * SUBLANES)
                if rows_per_layer % r == 0 and rows_per_layer // r <= n_steps)
    n_slabs = rows_per_layer // slab
    src = pl.BlockSpec((slab, width), lambda i: (index * n_slabs + jnp.minimum(i, n_slabs - 1), 0))
    dst = pl.BlockSpec((slab, width), lambda i: (jnp.minimum(i, n_slabs - 1), 0))
    flat = stacked.reshape(-1, width)
    return src, dst, flat, jax.ShapeDtypeStruct((rows_per_layer, width), BF16)


def _dense_layer(xs, mix, mod, g2, wo, layer, w1, w3, w2, ffn_index, final_g, seg, n_rows, final, cast=None):
    d = xs.shape[1]
    tm = seg["tm"]
    row, mod_map = _row_maps(seg, tm)
    n_steps = n_rows // tm
    cast_specs = [_cast_slabs(p, cast[1], n_steps) for p in cast[0]] if cast else []
    outs = pl.pallas_call(
        functools.partial(_dense_layer_kernel, final=final, n_lat_tiles=seg["n_lat"] // tm),
        grid=(n_steps,),
        in_specs=[pl.BlockSpec((tm, d), row)] + _mix_specs(tm, row, mix, seg["n_lat"])
                 + [pl.BlockSpec((1, 6, d), mod_map),
                    _resident((1, d)), _layer_resident(wo, layer), _layer_resident(w1, ffn_index),
                    _layer_resident(w3, ffn_index), _layer_resident(w2, ffn_index), _resident((1, d))]
                 + [c[0] for c in cast_specs],
        out_specs=[pl.BlockSpec((tm, d), row)] + [c[1] for c in cast_specs],
        out_shape=[jax.ShapeDtypeStruct((n_rows, d), F32)] + [c[3] for c in cast_specs],
        compiler_params=_params(1),
        name="dense_layer",
    )(xs, *mix, mod, g2, wo, w1, w3, w2, final_g, *[c[2] for c in cast_specs])
    if not cast:
        return outs[0], None
    return outs[0], tuple(o.reshape(p.shape[1:]) for o, p in zip(outs[1:], cast[0]))


def _router_kernel(x_ref, *refs, n_lat_tiles):
    mix_refs = refs[:N_MIX_REFS]
    (mod_ref, g_ref, wo_ref, wr_ref, br_ref, earlier_ref,
     x1_ref, h_ref, route_ref, counts_ref, count_ref) = refs[N_MIX_REFS:]
    mod = mod_ref[0]
    x1 = _mix_residual(x_ref, mix_refs, n_lat_tiles, mod, wo_ref)
    x1_ref[...] = x1
    hx = _adaln(x1, g_ref[...], mod[3:4], mod[4:5])
    h_ref[...] = _pack_bf16_pairs(hx)
    hx_hi = hx.astype(BF16)
    hx_lo = (hx - hx_hi.astype(F32)).astype(BF16)
    both = _dot_nt(wr_ref[...], hx_hi)
    logits = (both[:N_EXPERTS] + both[N_EXPERTS:] + _dot_nt(wr_ref[:N_EXPERTS, :], hx_lo)
              + br_ref[...])
    sub = lax.broadcasted_iota(jnp.int32, logits.shape, 0)
    top1 = jnp.max(logits, axis=0, keepdims=True)
    idx1 = jnp.min(jnp.where(logits == top1, sub, N_EXPERTS), axis=0, keepdims=True)
    rest = jnp.where(sub == idx1, -jnp.inf, logits)
    top2 = jnp.max(rest, axis=0, keepdims=True)
    idx2 = jnp.min(jnp.where(rest == top2, sub, N_EXPERTS), axis=0, keepdims=True)
    e2 = jnp.exp(top2 - top1)
    w_first = 1.0 / (1.0 + e2)
    @pl.when(pl.program_id(0) == 0)
    def _():
        count_ref[...] = jnp.zeros_like(count_ref)

    first, second = sub == idx1, sub == idx2
    hot = jnp.logical_or(first, second).astype(F32)
    before = _dot(hot.astype(BF16), earlier_ref[...]) + count_ref[:, :1]
    rank1 = jnp.sum(jnp.where(first, before, 0.0), axis=0, keepdims=True)
    rank2 = jnp.sum(jnp.where(second, before, 0.0), axis=0, keepdims=True)
    count_ref[...] += jnp.sum(hot, axis=1, keepdims=True)
    counts_ref[...] = count_ref[...]
    route = jnp.where(sub == 0, idx1.astype(F32), jnp.where(sub == 1, idx2.astype(F32), 0.0))
    route = jnp.where(sub == 2, w_first, jnp.where(sub == 3, e2 * w_first, route))
    route_ref[...] = jnp.where(sub == 4, rank1, jnp.where(sub == 5, rank2, route))


def _router(xs, mix, mod, g2, wo, layer, wr, br, seg, n_rows):
    d = xs.shape[1]
    tm = seg["tm"]
    row, mod_map = _row_maps(seg, tm)
    return pl.pallas_call(
        functools.partial(_router_kernel, n_lat_tiles=seg["n_lat"] // tm),
        grid=(n_rows // tm,),
        in_specs=[pl.BlockSpec((tm, d), row)] + _mix_specs(tm, row, mix, seg["n_lat"])
                 + [pl.BlockSpec((1, 6, d), mod_map),
                    _resident((1, d)), _layer_resident(wo, layer), _resident(wr.shape), _resident(br.shape),
                    _resident((tm, tm))],
        out_specs=[pl.BlockSpec((tm, d), row), pl.BlockSpec((tm, d // 2), row),
                   pl.BlockSpec((N_EXPERTS, tm), lambda i: (0, i)),
                   pl.BlockSpec((N_EXPERTS, LANES), lambda i: (0, 0))],
        out_shape=[jax.ShapeDtypeStruct((n_rows, d), F32), jax.ShapeDtypeStruct((n_rows, d // 2), jnp.int32),
                   jax.ShapeDtypeStruct((N_EXPERTS, n_rows), F32),
                   jax.ShapeDtypeStruct((N_EXPERTS, LANES), F32)],
        scratch_shapes=[pltpu.VMEM((N_EXPERTS, LANES), F32)],
        compiler_params=_params(1),
        name="mix_router",
    )(xs, *mix, mod, g2, wo, wr, br, jnp.triu(jnp.ones((tm, tm), BF16), 1))


BF16_BITS = 16
HIGH_HALF_WORD = -(1 << BF16_BITS)


def _pack_bf16_pairs(h):
    half = h.shape[1] // 2
    hi = lax.bitcast_convert_type(h[:, :half].astype(BF16).astype(F32), jnp.int32)
    lo = lax.bitcast_convert_type(h[:, half:].astype(BF16).astype(F32), jnp.int32)
    return (hi & jnp.int32(HIGH_HALF_WORD)) | lax.shift_right_logical(lo, BF16_BITS)


def _unpack_bf16_pairs(p):
    hi = lax.bitcast_convert_type(p & jnp.int32(HIGH_HALF_WORD), F32)
    lo = lax.bitcast_convert_type(lax.shift_left(p, BF16_BITS), F32)
    return hi, lo


def _route_plan(route, counts, tm):
    n_rows = route.shape[1]
    n_slots = TOP_K * n_rows
    idx1, idx2, rank1, rank2 = (route[r].astype(jnp.int32) for r in (0, 1, 4, 5))
    offs = jnp.concatenate([jnp.zeros((1,), jnp.int32), jnp.cumsum(counts[:, 0].astype(jnp.int32))])
    first_slot = lambda idx: sum(jnp.where(idx == e, offs[e], 0) for e in range(N_EXPERTS))
    slot_a = first_slot(idx1) + rank1
    slot_b = first_slot(idx2) + rank2
    n_tiles = n_slots // tm
    t_start = jnp.arange(n_tiles, dtype=jnp.int32) * tm
    e_first = jnp.sum(offs[None, 1:] <= t_start[:, None], axis=1).astype(jnp.int32)
    base_hi = jnp.minimum(t_start + tm, offs[e_first + 1])
    e_next = jnp.arange(1, N_EXPERTS, dtype=jnp.int32)
    start = offs[1:N_EXPERTS]
    x_tile = jnp.minimum(start // tm, n_tiles - 1)
    x_hi = jnp.where(start % tm != 0, jnp.minimum(offs[2:], (x_tile + 1) * tm), start)
    tiles = jnp.concatenate([t_start // tm, x_tile])
    experts = jnp.concatenate([e_first, e_next])
    lo = jnp.concatenate([t_start, start])
    hi = jnp.concatenate([base_hi, x_hi])
    order = jnp.argsort(tiles * (2 * N_EXPERTS) + experts)
    tiles, experts, lo, hi = tiles[order], experts[order], lo[order], hi[order]
    change = tiles[1:] != tiles[:-1]
    one = jnp.ones((1,), bool)
    first = jnp.concatenate([one, change]).astype(jnp.int32)
    last = jnp.concatenate([change, one]).astype(jnp.int32)
    return slot_a, slot_b, (tiles, experts, lo, hi, first, last)


def _scatter_rows(rows, idx_a, idx_b):
    n_rows, width = rows.shape
    workers = SC_CORES * SC_SUBCORES
    per_worker = n_rows // workers
    assert n_rows % (workers * SC_GATHER_ROWS) == 0
    mesh = plsc.VectorSubcoreMesh(core_axis_name="c", subcore_axis_name="s")

    @functools.partial(
        pl.kernel, mesh=mesh,
        out_type=jax.ShapeDtypeStruct((TOP_K * n_rows, width), rows.dtype),
        scratch_types=[pltpu.VMEM((TOP_K, SC_GATHER_ROWS), jnp.int32),
                       pltpu.VMEM((SC_GATHER_ROWS, width), rows.dtype),
                       pltpu.SemaphoreType.DMA, pltpu.SemaphoreType.DMA],
        name="scatter_rows")
    def scatter(rows_hbm, idx_a_hbm, idx_b_hbm, out_hbm, idx_v, rows_v, sem_a, sem_b):
        base = (lax.axis_index("s") * SC_CORES + lax.axis_index("c")) * per_worker

        @pl.loop(0, per_worker // SC_GATHER_ROWS)
        def _(i):
            off = pl.multiple_of(base + i * SC_GATHER_ROWS, SC_GATHER_ROWS)
            pltpu.sync_copy(idx_a_hbm.at[pl.ds(off, SC_GATHER_ROWS)], idx_v.at[0])
            pltpu.sync_copy(idx_b_hbm.at[pl.ds(off, SC_GATHER_ROWS)], idx_v.at[1])
            pltpu.sync_copy(rows_hbm.at[pl.ds(off, SC_GATHER_ROWS)], rows_v)
            first = pltpu.async_copy(rows_v, out_hbm.at[idx_v.at[0]], sem_a)
            second = pltpu.async_copy(rows_v, out_hbm.at[idx_v.at[1]], sem_b)
            first.wait()
            second.wait()

    return scatter(rows, idx_a, idx_b)


def _gather_rows(table, idx):
    n_idx = idx.shape[0]
    width = table.shape[1]
    workers = SC_CORES * SC_SUBCORES
    per_worker = n_idx // workers
    assert n_idx % (workers * SC_GATHER_ROWS * SC_IN_FLIGHT) == 0
    mesh = plsc.VectorSubcoreMesh(core_axis_name="c", subcore_axis_name="s")

    @functools.partial(
        pl.kernel, mesh=mesh,
        out_type=jax.ShapeDtypeStruct((n_idx, width), table.dtype),
        scratch_types=[pltpu.VMEM((SC_IN_FLIGHT, SC_GATHER_ROWS), jnp.int32),
                       pltpu.VMEM((SC_IN_FLIGHT, SC_GATHER_ROWS, width), table.dtype)]
                      + [pltpu.SemaphoreType.DMA] * (2 * SC_IN_FLIGHT),
        name="gather_rows")
    def gather(table_hbm, idx_hbm, out_hbm, idx_v, rows_v, *sems):
        base = (lax.axis_index("s") * SC_CORES + lax.axis_index("c")) * per_worker

        @pl.loop(0, per_worker // (SC_GATHER_ROWS * SC_IN_FLIGHT))
        def _(i):
            offs = [pl.multiple_of(base + (i * SC_IN_FLIGHT + b) * SC_GATHER_ROWS, SC_GATHER_ROWS)
                    for b in range(SC_IN_FLIGHT)]
            for b, off in enumerate(offs):
                pltpu.sync_copy(idx_hbm.at[pl.ds(off, SC_GATHER_ROWS)], idx_v.at[b])
            reads = [pltpu.async_copy(table_hbm.at[idx_v.at[b]], rows_v.at[b], sems[b])
                     for b in range(SC_IN_FLIGHT)]
            writes = []
            for b, off in enumerate(offs):
                reads[b].wait()
                writes.append(pltpu.async_copy(rows_v.at[b], out_hbm.at[pl.ds(off, SC_GATHER_ROWS)],
                                               sems[SC_IN_FLIGHT + b]))
            for w in writes:
                w.wait()

    return gather(table, idx)


def _experts_kernel(tile_ref, exp_ref, lo_ref, hi_ref, first_ref, last_ref,
                    x_ref, w1_ref, w3_ref, w2_ref, o_ref, acc_ref):
    i = pl.program_id(0)
    tm, half = x_ref.shape
    f = w1_ref.shape[2]
    fc = FFN_CHUNK_COLS

    @pl.when(first_ref[i] == 1)
    def _():
        acc_ref[...] = jnp.zeros_like(acc_ref)

    lo, hi = lo_ref[i], hi_ref[i]

    @pl.when(hi > lo)
    def _():
        x_hi, x_lo = _unpack_bf16_pairs(x_ref[...])
        x = jnp.concatenate([x_hi.astype(BF16), x_lo.astype(BF16)], axis=1)
        y = None
        for c in range(f // fc):
            cols = slice(c * fc, (c + 1) * fc)
            a = _dot(x, w1_ref[0, :, cols])
            b = _dot(x, w3_ref[0, :, cols])
            part = _dot((a * _sigmoid(a) * b).astype(BF16), w2_ref[0, cols, :])
            y = part if y is None else y + part
        rows = tile_ref[i] * tm + lax.broadcasted_iota(jnp.int32, (tm, 1), 0)
        keep = jnp.logical_and(rows >= lo, rows < hi)
        acc_ref[...] += jnp.where(keep, y, 0.0)

    @pl.when(last_ref[i] == 1)
    def _():
        o_ref[...] = _pack_bf16_pairs(acc_ref[...])


def _experts(xs_sorted, items, w1, w3, w2):
    n_slots, half = xs_sorted.shape
    n_exp, d, f = w1.shape
    tm = MOE_ROW_TILE
    assert f % FFN_CHUNK_COLS == 0
    tile_map = lambda i, tiles, *_: (tiles[i], 0)
    exp_map = lambda i, tiles, experts, *_: (experts[i], 0, 0)
    grid_spec = pltpu.PrefetchScalarGridSpec(
        num_scalar_prefetch=len(items),
        grid=(items[0].shape[0],),
        in_specs=[pl.BlockSpec((tm, half), tile_map),
                  pl.BlockSpec((1, d, f), exp_map), pl.BlockSpec((1, d, f), exp_map),
                  pl.BlockSpec((1, f, d), exp_map)],
        out_specs=pl.BlockSpec((tm, half), tile_map),
        scratch_shapes=[pltpu.VMEM((tm, d), F32)],
    )
    return pl.pallas_call(
        _experts_kernel,
        grid_spec=grid_spec,
        out_shape=jax.ShapeDtypeStruct((n_slots, half), jnp.int32),
        compiler_params=_params(1),
        name="experts",
    )(*items, xs_sorted, w1, w3, w2)


def _combine_kernel(x1_ref, ya_ref, yb_ref, route_ref, mod_ref, fg_ref, o_ref, *, final):
    half = ya_ref.shape[1]
    route = route_ref[...]
    padded = jnp.concatenate([route, jnp.zeros((LANES - route.shape[0], route.shape[1]), F32)], axis=0)
    route = jnp.transpose(padded)
    wa, wb = route[:, 2:3], route[:, 3:4]
    a_hi, a_lo = _unpack_bf16_pairs(ya_ref[...])
    b_hi, b_lo = _unpack_bf16_pairs(yb_ref[...])
    gate = mod_ref[0][5:6]
    out_hi = x1_ref[:, :half] + gate[:, :half] * (wa * a_hi + wb * b_hi)
    out_lo = x1_ref[:, half:] + gate[:, half:] * (wa * a_lo + wb * b_lo)
    if final:
        total = jnp.sum(out_hi * out_hi, axis=-1, keepdims=True) + jnp.sum(out_lo * out_lo, axis=-1, keepdims=True)
        scale = lax.rsqrt(total / (2 * half) + EPS)
        out_hi = out_hi * scale * fg_ref[:, :half]
        out_lo = out_lo * scale * fg_ref[:, half:]
    o_ref[:, :half] = out_hi
    o_ref[:, half:] = out_lo


def _combine(x1, y_pairs, route, mod, final_g, seg, final):
    n_rows, d = x1.shape
    tm = seg["tm"]
    row, mod_map = _row_maps(seg, tm)
    second = n_rows // tm
    return pl.pallas_call(
        functools.partial(_combine_kernel, final=final),
        grid=(n_rows // tm,),
        in_specs=[pl.BlockSpec((tm, d), row), pl.BlockSpec((tm, d // 2), row),
                  pl.BlockSpec((tm, d // 2), lambda i: (second + i, 0)),
                  pl.BlockSpec((N_EXPERTS, tm), lambda i: (0, i)), pl.BlockSpec((1, 6, d), mod_map),
                  pl.BlockSpec((1, d), lambda i: (0, 0))],
        out_specs=pl.BlockSpec((tm, d), row),
        out_shape=jax.ShapeDtypeStruct((n_rows, d), F32),
        compiler_params=_params(1),
        name="moe_combine",
    )(x1, y_pairs, y_pairs, route, mod, final_g)


def _moe_layer(xs, mix, mod, g2, wo, layer, wr, br, w1, w3, w2, final_g, seg, n_rows, final):
    x1, h_pairs, route, counts = _router(xs, mix, mod, g2, wo, layer, wr, br, seg, n_rows)
    slot_a, slot_b, items = _route_plan(route, counts, MOE_ROW_TILE)
    y_sorted = _experts(_scatter_rows(h_pairs, slot_a, slot_b), items, w1, w3, w2)
    y_pairs = _gather_rows(y_sorted, jnp.concatenate([slot_a, slot_b]))
    return _combine(x1, y_pairs, route, mod, final_g, seg, final)


def kernel(x, c, ctx, c_ctx, norm1_g, norm2_g, w_mod, b_mod, w_in, conv_w, conv_b, b_gates, attn_sink,
           g_att, g_ml, w_out, ffn_w1, ffn_w3, ffn_w2, w_router, b_router, exp_w1, exp_w3, exp_w2,
           final_g):
    b_, s_, d = x.shape
    lc = ctx.shape[1]
    depth = w_in.shape[0]
    n_lat, n_ctx = b_ * s_, b_ * lc
    tm = min(ROW_TILE, s_)
    assert s_ % tm == 0 and n_ctx % tm == 0
    assert (TOP_K * n_lat) % MOE_ROW_TILE == 0 and (TOP_K * n_ctx) % MOE_ROW_TILE == 0
    assert s_ % ML_CHUNK == 0 and lc % ML_CHUNK == 0 and n_lat % lc == 0 and b_ < MOD_ROWS
    seg = dict(B=b_, S=s_, Lc=lc, n_lat=n_lat, tm=tm)
    tm_in = INPROJ_TILE if s_ % INPROJ_TILE == 0 and n_ctx % INPROJ_TILE == 0 else tm
    seg_in = dict(seg, tm=tm_in)

    cond = jnp.zeros((MOD_ROWS, d), F32).at[:b_].set(c).at[b_].set(c_ctx)
    mods = _modulation(cond, w_mod, b_mod).reshape(depth, MOD_ROWS, 6, d)
    rope = _rope_tables(s_, tm_in)
    xs = (x.reshape(n_lat, d), ctx.reshape(n_ctx, d))
    final_row = final_g.reshape(1, d)

    w_in_b, w_out_b = w_in.astype(BF16), w_out.astype(BF16)
    ffn_b = (ffn_w1.astype(BF16), ffn_w3.astype(BF16), ffn_w2.astype(BF16))
    exp_b = None

    for layer in range(depth):
        last = layer == depth - 1
        w_gates = w_in[layer][:, MAIN_WIDTH:]
        w_gc = jnp.pad(w_gates, ((0, 0), (0, LANES - ML_GATES))).astype(BF16)
        b_gc = jnp.pad(b_gates[layer], (0, LANES - ML_GATES)).reshape(1, LANES)
        qa, ka, va, qm, km, vm, om, gc, gr, *stream = _input_projection(
            xs, mods[layer], norm1_g[layer].reshape(1, d), w_in_b, layer, w_gc,
            w_gates.T.astype(BF16), b_gc, b_gates[layer].reshape(ML_GATES, 1),
            conv_w[layer], conv_b[layer].reshape(1, -1), rope, seg_in)
        if stream:
            xs = stream[0]
        att = _attention(qa, ka, va, attn_sink[layer], g_att[layer].reshape(1, ATT_WIDTH), seg, not last)
        mix = (*att, *_mlstm(qm, km, vm, gc, gr, seg), om, g_ml[layer].reshape(1, ML_WIDTH))
        n_rows = n_lat if last else n_lat + n_ctx
        g2 = norm2_g[layer].reshape(1, d)
        i = layer // 2
        if layer % 2 == 0:
            cast = None if last else ((exp_w1, exp_w3, exp_w2), i)
            xs, exp_b = _dense_layer(xs, mix, mods[layer], g2, w_out_b, layer, *ffn_b, i, final_row, seg,
                                     n_rows, last, cast)
        else:
            wr = w_router[i].T
            wr_hi = wr.astype(BF16)
            wr = jnp.concatenate([wr_hi, (wr - wr_hi.astype(F32)).astype(BF16)], axis=0)
            br = b_router[i].reshape(N_EXPERTS, 1)
            xs = _moe_layer(xs, mix, mods[layer], g2, w_out_b, layer, wr, br, *exp_b, final_row, seg,
                            n_rows, last)
    return xs[:n_lat].reshape(b_, s_, d)
```

```python
import functools

import jax
import jax.numpy as jnp
import numpy as np
from jax import lax
from jax.experimental import pallas as pl
from jax.experimental.pallas import tpu as pltpu
from jax.experimental.pallas import tpu_sc as plsc

F32 = jnp.float32
BF16 = jnp.bfloat16

GRID_W = 64
ATT_HEADS = 8
ATT_KV_HEADS = 2
ATT_HEAD_DIM = 64
ATT_GROUP = ATT_HEADS // ATT_KV_HEADS
WINDOW = 128
ATT_BLOCK = 128
ROPE_THETA = 10000.0
ML_HEADS = 4
ML_QK_DIM = 64
ML_V_DIM = 128
ML_CONV = 5
GATE_CAP = 15.0
ATT_WIDTH = ATT_HEADS * ATT_HEAD_DIM
ATT_KV_WIDTH = ATT_KV_HEADS * ATT_HEAD_DIM
ML_QK_WIDTH = ML_HEADS * ML_QK_DIM
ML_WIDTH = ML_HEADS * ML_V_DIM
ML_GATES = 4 * ML_HEADS
MAIN_WIDTH = ATT_WIDTH + 2 * ATT_KV_WIDTH + 2 * ML_QK_WIDTH + 2 * ML_WIDTH
N_EXPERTS = 8
TOP_K = 2
EPS = 1e-6

LANES = 128
SUBLANES = 8
VMEM_LIMIT = 56 * 1024 * 1024
NEG = -1e30
SC_CORES = 2
SC_SUBCORES = 16
SC_GATHER_ROWS = 64
SC_IN_FLIGHT = 2

ROW_TILE = 512
INPROJ_TILE = 1024
MOE_ROW_TILE = 512
MOE_EDGE_TILE = 1024
FFN_CHUNK_COLS = 256
ATT_STEP_BLOCKS = 16
ML_CHUNK = 128
ML_STEP_CHUNKS = 8
CONV_HALO = SUBLANES
MOD_ROWS = 16
MOD_COL_TILE = 1536


def _dot(a, b):
    return jnp.dot(a, b, preferred_element_type=F32)


def _dot_nt(a, b):
    return lax.dot_general(a, b, (((1,), (1,)), ((), ())), preferred_element_type=F32)


def _dot_tn(a, b):
    return lax.dot_general(a, b, (((0,), (0,)), ((), ())), preferred_element_type=F32)


def _bf16_terms(x):
    hi = x.astype(BF16)
    rest = x - hi.astype(F32)
    mid = rest.astype(BF16)
    return hi, mid, (rest - mid.astype(F32)).astype(BF16)


def _dot_exact_rhs(mask, x):
    return sum(_dot(mask, t) for t in _bf16_terms(x))


def _dot_exact_lhs(x, mask):
    return sum(_dot(t, mask) for t in _bf16_terms(x))


def _sigmoid(x):
    return 1.0 / (1.0 + jnp.exp(-x))


def _rms(x, g):
    return x * lax.rsqrt(jnp.mean(x * x, axis=-1, keepdims=True) + EPS) * g


def _adaln(x, g, shift, scale):
    return _rms(x, g * (1.0 + scale)) + shift


def _params(n_axes):
    return pltpu.CompilerParams(dimension_semantics=("arbitrary",) * n_axes,
                                vmem_limit_bytes=VMEM_LIMIT)


def _resident(shape):
    zeros = (0,) * len(shape)
    return pl.BlockSpec(shape, lambda *_: zeros, pipeline_mode=pl.Buffered(1))


def _layer_resident(stacked, layer, block=None):
    block = tuple(stacked.shape[1:]) if block is None else block
    index = (layer,) + (0,) * len(block)
    return pl.BlockSpec((None,) + block, lambda *_: index, pipeline_mode=pl.Buffered(1))


def _mod_kernel(c_ref, w_ref, b_ref, o_ref):
    c = c_ref[...]
    a = (c * _sigmoid(c)).astype(BF16)
    o_ref[0] = _dot(a, w_ref[0].astype(BF16)) + b_ref[0]


def _modulation(cond, w_mod, b_mod):
    depth, d, width = w_mod.shape
    tn = MOD_COL_TILE if width % MOD_COL_TILE == 0 else width
    return pl.pallas_call(
        _mod_kernel,
        grid=(depth, width // tn),
        in_specs=[pl.BlockSpec((MOD_ROWS, d), lambda l, j: (0, 0)),
                  pl.BlockSpec((1, d, tn), lambda l, j: (l, 0, j)),
                  pl.BlockSpec((1, 1, tn), lambda l, j: (l, 0, j))],
        out_specs=pl.BlockSpec((1, MOD_ROWS, tn), lambda l, j: (l, 0, j)),
        out_shape=jax.ShapeDtypeStruct((depth, MOD_ROWS, width), F32),
        compiler_params=_params(2),
        name="modulation",
    )(cond, w_mod, b_mod.reshape(depth, 1, width))


def _gate_act(u, is_forget):
    g = GATE_CAP * jnp.tanh(u / GATE_CAP)
    log_sig = jnp.minimum(g, 0.0) - jnp.log1p(jnp.exp(-jnp.abs(g)))
    return jnp.where(is_forget, log_sig, g)


def _conv_silu(xe, cw, cb, rows):
    n_ext = xe.shape[0]
    mid = ML_CONV // 2
    y = cb + cw[mid:mid + 1] * xe[CONV_HALO:CONV_HALO + rows]
    for tap in range(ML_CONV):
        if tap != mid:
            y = y + cw[tap:tap + 1] * pltpu.roll(xe, (mid - tap) % n_ext, 0)[CONV_HALO:CONV_HALO + rows]
    return y * _sigmoid(y)


def _inproj_kernel(*refs, n_lat_tiles, seq_lat, seq_ctx, split):
    n_x = 6 if split else 3
    x_refs, refs = refs[:n_x], refs[n_x:]
    (mod_ref, g_ref, w_ref, wgc_ref, wgr_ref, bgc_ref, bgr_ref, cw_ref, cb_ref, cos_ref, sa_ref, sb_ref,
     qa_ref, ka_ref, va_ref, qm_ref, km_ref, vm_ref, om_ref, gc_ref, gr_ref) = refs[:21]
    i = pl.program_id(0)
    if split:
        in_lat = i < n_lat_tiles
        x_tile, x_prev, x_next = (jnp.where(in_lat, x_refs[2 * k][...], x_refs[2 * k + 1][...]) for k in range(3))
        refs[21][...] = x_tile
    else:
        x_tile, x_prev, x_next = (r[...] for r in x_refs)
    mod = mod_ref[0]
    tm = x_tile.shape[0]
    seg = min(tm, seq_ctx)
    n_seg = tm // seg
    norm_gain = g_ref[...]
    normed = lambda rows: _adaln(rows, norm_gain, mod[0:1], mod[1:2]).astype(BF16)
    quarter = ATT_HEAD_DIM // 4

    def rope(u, rows):
        return (u * cos_ref[rows, :] + pltpu.roll(u, LANES - quarter, 1) * sa_ref[rows, :]
                + pltpu.roll(u, quarter, 1) * sb_ref[rows, :])

    c_kv = ATT_WIDTH
    c_qk = c_kv + 2 * ATT_KV_WIDTH
    c_vm = c_qk + 2 * ML_QK_WIDTH
    c_om = c_vm + ML_WIDTH
    w_qk = w_ref[:, c_qk:c_vm]
    seq_len = jnp.where(i < n_lat_tiles, seq_lat, seq_ctx)
    cw, cb = cw_ref[...], cb_ref[...]
    hx = [normed(x_tile[j * seg:(j + 1) * seg, :]) for j in range(n_seg)]
    qk = [_dot(h, w_qk) for h in hx]
    qk_prev = _dot(normed(x_prev), w_qk)
    qk_next = _dot(normed(x_next), w_qk)
    for j in range(n_seg):
        rows = slice(j * seg, (j + 1) * seg)
        first_row = i * tm + j * seg
        has_prev = (lax.rem(first_row, seq_len) != 0).astype(F32)
        has_next = (lax.rem(first_row + seg, seq_len) != 0).astype(F32)
        prev = qk_prev if j == 0 else qk[j - 1][seg - CONV_HALO:]
        nxt = qk_next if j == n_seg - 1 else qk[j + 1][:CONV_HALO]
        xe = jnp.concatenate([prev * has_prev, qk[j], nxt * has_next], axis=0)
        y = _conv_silu(xe, cw, cb, seg)
        qm_ref[rows, :] = (y[:, :ML_QK_WIDTH] * (ML_QK_DIM ** -0.5)).astype(BF16)
        km_ref[rows, :] = y[:, ML_QK_WIDTH:].astype(BF16)
        q = _dot(hx[j], w_ref[:, :c_kv])
        for c in range(ATT_WIDTH // LANES):
            sl = slice(c * LANES, (c + 1) * LANES)
            qa_ref[rows, sl] = (rope(q[:, sl], rows) * (ATT_HEAD_DIM ** -0.5)).astype(BF16)
        kv = _dot(hx[j], w_ref[:, c_kv:c_qk])
        ka_ref[rows, :] = rope(kv[:, :ATT_KV_WIDTH], rows).astype(BF16)
        va_ref[rows, :] = kv[:, ATT_KV_WIDTH:].astype(BF16)
        vm_ref[rows, :] = _dot(hx[j], w_ref[:, c_vm:c_om]).astype(BF16)
        om_ref[rows, :] = _dot(hx[j], w_ref[:, c_om:c_om + ML_WIDTH]).astype(BF16)
        gc = _dot(hx[j], wgc_ref[...]) + bgc_ref[...]
        lane = lax.broadcasted_iota(jnp.int32, gc.shape, 1)
        gc_ref[rows, :] = _gate_act(gc, (lane // ML_HEADS) % 2 == 1)
        gr = _dot_nt(wgr_ref[...], hx[j]) + bgr_ref[...]
        sub = lax.broadcasted_iota(jnp.int32, gr.shape, 0)
        gr_ref[:, rows] = _gate_act(gr, (sub // ML_HEADS) % 2 == 1)


def _input_projection(xs, mod, g1, w_in, layer, w_gc, w_gr, b_gc, b_gr, conv_w, conv_b, rope, seg):
    split = isinstance(xs, tuple)
    d = (xs[0] if split else xs).shape[1]
    tm = seg["tm"]
    nlat = seg["n_lat"] // tm
    n = sum(a.shape[0] for a in xs) if split else xs.shape[0]
    s_tiles = seg["S"] // tm
    halos_per_tile = tm // CONV_HALO
    row = lambda i: (i, 0)

    def x_specs(n_rows, first_tile):
        n_tiles = n_rows // tm
        t = lambda i: jnp.clip(i - first_tile, 0, n_tiles - 1)
        return [pl.BlockSpec((tm, d), lambda i: (t(i), 0)),
                pl.BlockSpec((CONV_HALO, d), lambda i: (jnp.maximum(t(i) * halos_per_tile - 1, 0), 0)),
                pl.BlockSpec((CONV_HALO, d),
                             lambda i: (jnp.minimum((t(i) + 1) * halos_per_tile, n_rows // CONV_HALO - 1), 0))]

    if split:
        lat, ctx = x_specs(xs[0].shape[0], 0), x_specs(xs[1].shape[0], nlat)
        x_in_specs = [s for pair in zip(lat, ctx) for s in pair]
        x_args = [xs[0], xs[1]] * 3
    else:
        x_in_specs, x_args = x_specs(n, 0), [xs] * 3
    mod_map = lambda i: (jnp.where(i < nlat, i // s_tiles, seg["B"]), 0, 0)
    rope_map = lambda i: (jnp.where(i < nlat, i % s_tiles, s_tiles), 0)
    widths = [(ATT_WIDTH, BF16), (ATT_KV_WIDTH, BF16), (ATT_KV_WIDTH, BF16), (ML_QK_WIDTH, BF16),
              (ML_QK_WIDTH, BF16), (ML_WIDTH, BF16), (ML_WIDTH, BF16), (LANES, F32)]
    out_shape = [jax.ShapeDtypeStruct((n, w), t) for w, t in widths]
    out_specs = [pl.BlockSpec((tm, w), row) for w, _ in widths]
    out_shape.append(jax.ShapeDtypeStruct((ML_GATES, n), F32))
    out_specs.append(pl.BlockSpec((ML_GATES, tm), lambda i: (0, i)))
    if split:
        out_shape.append(jax.ShapeDtypeStruct((n, d), F32))
        out_specs.append(pl.BlockSpec((tm, d), row))
    return pl.pallas_call(
        functools.partial(_inproj_kernel, n_lat_tiles=nlat, seq_lat=seg["S"], seq_ctx=seg["Lc"], split=split),
        grid=(n // tm,),
        in_specs=x_in_specs
                 + [pl.BlockSpec((1, 6, d), mod_map),
                  _resident((1, d)),
                  _layer_resident(w_in, layer, (d, MAIN_WIDTH)), _resident(w_gc.shape), _resident(w_gr.shape),
                  _resident(b_gc.shape), _resident(b_gr.shape),
                  _resident(conv_w.shape), _resident(conv_b.shape),
                  pl.BlockSpec((tm, LANES), rope_map),
                  pl.BlockSpec((tm, LANES), rope_map),
                  pl.BlockSpec((tm, LANES), rope_map)],
        out_specs=out_specs,
        out_shape=out_shape,
        compiler_params=_params(1),
        name="input_projection",
    )(*x_args, mod, g1, w_in, w_gc, w_gr, b_gc, b_gr, conv_w, conv_b, *rope)


def _rope_tables(s, tm):
    quarter = ATT_HEAD_DIM // 4
    t = jnp.arange(s)
    row = (t // GRID_W).astype(F32)
    col = (t % GRID_W).astype(F32)
    inv = ROPE_THETA ** (-jnp.arange(quarter, dtype=F32) / quarter)
    ang_r = row[:, None] * inv[None, :]
    ang_c = col[:, None] * inv[None, :]
    zero = jnp.zeros_like(ang_r)
    cos = jnp.concatenate([jnp.cos(ang_r)] * 2 + [jnp.cos(ang_c)] * 2, axis=1)
    sin_up = jnp.concatenate([-jnp.sin(ang_r), zero, -jnp.sin(ang_c), zero], axis=1)
    sin_dn = jnp.concatenate([zero, jnp.sin(ang_r), zero, jnp.sin(ang_c)], axis=1)
    reps = LANES // ATT_HEAD_DIM
    ident = [jnp.ones((tm, LANES), F32), jnp.zeros((tm, LANES), F32), jnp.zeros((tm, LANES), F32)]
    return tuple(jnp.concatenate([jnp.tile(a, (1, reps)), i], axis=0)
                 for a, i in zip((cos, sin_up, sin_dn), ident))


def _attn_kernel(sink_ref, *refs, window, blocks):
    blk = ATT_BLOCK
    if window:
        q_ref, kp_ref, kc_ref, kn_ref, vp_ref, vc_ref, vn_ref, kx_ref, vx_ref, g_ref, o_ref = refs
        j = pl.program_id(1)
        k_own, v_own = kc_ref[...], vc_ref[...]
        k_blocks = [kp_ref[...]] + [k_own[t * blk:(t + 1) * blk] for t in range(blocks)] + [kn_ref[...]]
        v_blocks = [vp_ref[...]] + [v_own[t * blk:(t + 1) * blk] for t in range(blocks)] + [vn_ref[...]]
    else:
        q_ref, kx_ref, vx_ref, g_ref, o_ref = refs
    for t in range(blocks):
        if window:
            has_prev = j > 0 if t == 0 else True
            has_next = j < pl.num_programs(1) - 1 if t == blocks - 1 else True
            win = (k_blocks[t:t + 3], v_blocks[t:t + 3], has_prev, has_next)
        else:
            win = None
        att = _attend_block(sink_ref, q_ref[t * blk:(t + 1) * blk, :], win, kx_ref[...], vx_ref[...])
        o_ref[t * blk:(t + 1) * blk, :] = _rms(att, g_ref[...]).astype(o_ref.dtype)


def _attend_block(sink_ref, q, win, k_ctx, v_ctx):
    blk = ATT_BLOCK
    if win is None:
        k_all, v_all, bias = k_ctx, v_ctx, None
    else:
        k_win, v_win, has_prev, has_next = win
        rows = lax.broadcasted_iota(jnp.int32, (blk, blk), 0)
        cols = lax.broadcasted_iota(jnp.int32, (blk, blk), 1)
        ok_p = jnp.logical_and(cols >= rows, has_prev)
        ok_n = jnp.logical_and(cols <= rows, has_next)
        bias = jnp.concatenate([jnp.where(ok_p, 0.0, NEG), jnp.zeros((blk, blk), F32),
                                jnp.where(ok_n, 0.0, NEG), jnp.zeros((blk, k_ctx.shape[0]), F32)], axis=1)
        k_all = jnp.concatenate(list(k_win) + [k_ctx], axis=0)
        v_all = jnp.concatenate(list(v_win) + [v_ctx], axis=0)
    dh = ATT_HEAD_DIM
    outs = []
    for h in range(ATT_KV_HEADS):
        k_h = k_all[:, h * dh:(h + 1) * dh]
        v_h = v_all[:, h * dh:(h + 1) * dh]
        q_h = jnp.concatenate([q[:, (h * ATT_GROUP + g) * dh:(h * ATT_GROUP + g + 1) * dh]
                               for g in range(ATT_GROUP)], axis=0)
        s_all = _dot_nt(q_h, k_h)
        p_parts, inv_parts = [], []
        for g in range(ATT_GROUP):
            sink = sink_ref[h * ATT_GROUP + g]
            s = s_all[g * blk:(g + 1) * blk]
            if bias is not None:
                s = s + bias
            m = jnp.maximum(jnp.max(s, axis=-1, keepdims=True), sink)
            p = jnp.exp(s - m)
            denom = jnp.sum(p, axis=-1, keepdims=True) + jnp.exp(sink - m)
            p_parts.append(p.astype(BF16))
            inv_parts.append(1.0 / denom)
        o = _dot(jnp.concatenate(p_parts, axis=0), v_h)
        for g in range(ATT_GROUP):
            outs.append(o[g * blk:(g + 1) * blk] * inv_parts[g])
    return jnp.concatenate(outs, axis=1)


def _attention(qa, ka, va, sink, g_att, seg, with_ctx):
    n = qa.shape[0]
    b_, s_, lc = seg["B"], seg["S"], seg["Lc"]
    blk = ATT_BLOCK
    blocks_per_seq = s_ // blk
    ctx_map = lambda b, j, *_: (b_ * s_ // lc + b, 0)
    ctx_specs = [pl.BlockSpec((lc, ATT_KV_WIDTH), ctx_map)] * 2
    g_spec = pl.BlockSpec((1, ATT_WIDTH), lambda b, j, *_: (0, 0))

    def call(window, blocks, steps, first_step, in_specs, args):
        grid_spec = pltpu.PrefetchScalarGridSpec(
            num_scalar_prefetch=1,
            grid=(b_, steps),
            in_specs=[pl.BlockSpec((blocks * blk, ATT_WIDTH),
                                   lambda b, j, *_: (first_step + b * steps + j, 0))] + in_specs,
            out_specs=pl.BlockSpec((blocks * blk, ATT_WIDTH), lambda b, j, *_: (b * steps + j, 0)))
        return pl.pallas_call(
            functools.partial(_attn_kernel, window=window, blocks=blocks),
            grid_spec=grid_spec,
            out_shape=jax.ShapeDtypeStruct((b_ * steps * blocks * blk, ATT_WIDTH), BF16),
            compiler_params=_params(2),
            name="window_attention" if window else "context_attention",
        )(sink, qa, *args)

    qb = min(ATT_STEP_BLOCKS, blocks_per_seq)
    assert blocks_per_seq % qb == 0
    nqs = blocks_per_seq // qb

    def edge_map(off):
        return lambda b, j, *_: (b * blocks_per_seq + jnp.clip(j * qb + off, 0, blocks_per_seq - 1), 0)

    kv_specs = [pl.BlockSpec((blk, ATT_KV_WIDTH), edge_map(-1)),
                pl.BlockSpec((qb * blk, ATT_KV_WIDTH), lambda b, j, *_: (b * nqs + j, 0)),
                pl.BlockSpec((blk, ATT_KV_WIDTH), edge_map(qb))]
    att_lat = call(True, qb, nqs, 0, kv_specs + kv_specs + ctx_specs + [g_spec],
                   (ka, ka, ka, va, va, va, ka, va, g_att))
    if not with_ctx:
        return att_lat, att_lat
    cb = min(ATT_STEP_BLOCKS, lc // blk)
    assert (lc // blk) % cb == 0 and (b_ * s_) % (cb * blk) == 0
    att_ctx = call(False, cb, lc // (cb * blk), b_ * s_ // (cb * blk), ctx_specs + [g_spec], (ka, va, g_att))
    return att_lat, att_ctx


def _mlstm_direction(q_ref, k_ref, v_ref, gc_ref, gr_ref, out_ref, state_ref, tok, *, reverse):
    chunk = ML_CHUNK
    rows = lax.broadcasted_iota(jnp.int32, (chunk, chunk), 0)
    cols = lax.broadcasted_iota(jnp.int32, (chunk, chunk), 1)
    lower = rows >= cols
    upper = rows <= cols
    seen = upper if reverse else lower
    gc = gc_ref[tok, :]
    gr = gr_ref[:, tok]
    b_col = _dot_exact_rhs(seen.astype(BF16), gc)
    b_row = _dot_exact_lhs(gr, (lower if reverse else upper).astype(BF16))
    b_end = jnp.sum(gc, axis=0, keepdims=True)
    base = 2 * ML_HEADS if reverse else 0
    pair_width = 2 * ML_QK_DIM
    lane = lax.broadcasted_iota(jnp.int32, (chunk, pair_width), 1)
    state_row = lax.broadcasted_iota(jnp.int32, (pair_width, 1), 0)
    ones = jnp.ones((chunk, ML_V_DIM), BF16)

    for pair in range(ML_HEADS // 2):
        q_pair = q_ref[tok, pair * pair_width:(pair + 1) * pair_width]
        k_pair = k_ref[tok, pair * pair_width:(pair + 1) * pair_width]
        state = state_ref[pair]
        state_bf = state.astype(BF16)
        update = None
        decays = []
        for sub in range(2):
            h = 2 * pair + sub
            i_idx = base + h
            f_idx = base + ML_HEADS + h
            own = (lane >= ML_QK_DIM) if sub else (lane < ML_QK_DIM)
            q_h = jnp.where(own, q_pair, jnp.zeros_like(q_pair))
            vx = jnp.concatenate([v_ref[tok, h * ML_V_DIM:(h + 1) * ML_V_DIM], ones], axis=1)
            bc = b_col[:, f_idx:f_idx + 1]
            d = bc - (b_row[f_idx:f_idx + 1, :] - gr[i_idx:i_idx + 1, :])
            w = jnp.exp(jnp.where(seen, d, NEG))
            s = _dot_nt(q_h, k_pair) * w
            tot = _dot(s.astype(BF16), vx) + jnp.exp(bc) * _dot(q_h, state_bf)
            h_out = tot[:, :ML_V_DIM] / jnp.maximum(jnp.abs(tot[:, ML_V_DIM:]), 1.0)
            out_ref[tok, h * ML_V_DIM:(h + 1) * ML_V_DIM] = h_out.astype(out_ref.dtype)

            be = b_end[:, f_idx:f_idx + 1]
            kw = jnp.where(own, k_pair.astype(F32) * jnp.exp(be - bc + gc[:, i_idx:i_idx + 1]), 0.0)
            part = _dot_tn(kw.astype(BF16), vx)
            update = part if update is None else update + part
            decays.append(jnp.exp(be))
        decay = jnp.where(state_row < ML_QK_DIM, decays[0], decays[1])
        state_ref[pair] = decay * state + update


def _mlstm_both(fwd_refs, bwd_refs, hf_ref, hb_ref, sf_ref, sb_ref):
    n_chunks = hf_ref.shape[0] // ML_CHUNK
    for t in range(n_chunks):
        fwd = slice(t * ML_CHUNK, (t + 1) * ML_CHUNK)
        bwd = slice((n_chunks - 1 - t) * ML_CHUNK, (n_chunks - t) * ML_CHUNK)
        _mlstm_direction(*fwd_refs, hf_ref, sf_ref, fwd, reverse=False)
        _mlstm_direction(*bwd_refs, hb_ref, sb_ref, bwd, reverse=True)


def _mlstm_kernel(*refs):
    fwd_refs, bwd_refs, ctx_refs = refs[0:5], refs[5:10], refs[10:15]
    hf_ref, hb_ref, hf_ctx_ref, hb_ctx_ref, sf_ref, sb_ref = refs[15:]

    @pl.when(pl.program_id(1) == 0)
    def _():
        sf_ref[...] = jnp.zeros_like(sf_ref)
        sb_ref[...] = jnp.zeros_like(sb_ref)
        _mlstm_both(ctx_refs, ctx_refs, hf_ctx_ref, hb_ctx_ref, sf_ref, sb_ref)

    _mlstm_both(fwd_refs, bwd_refs, hf_ref, hb_ref, sf_ref, sb_ref)


def _mlstm(qm, km, vm, gc, gr, seg):
    b_, s_, lc = seg["B"], seg["S"], seg["Lc"]
    n_lat = b_ * s_
    step = min(ML_STEP_CHUNKS * ML_CHUNK, s_)
    assert s_ % step == 0 and lc % ML_CHUNK == 0 and n_lat % lc == 0
    n_steps = s_ // step
    arrays = (qm, km, vm, gc)
    widths = (ML_QK_WIDTH, ML_QK_WIDTH, ML_WIDTH, LANES)

    def specs(rows, block_of):
        return [pl.BlockSpec((rows, w), lambda b, c: (block_of(b, c), 0)) for w in widths] \
            + [pl.BlockSpec((ML_GATES, rows), lambda b, c: (0, block_of(b, c)))]

    fwd_block = lambda b, c: b * n_steps + c
    bwd_block = lambda b, c: b * n_steps + n_steps - 1 - c
    ctx_block = lambda b, c: n_lat // lc + b
    lat_out = lambda block_of: pl.BlockSpec((step, ML_WIDTH), lambda b, c: (block_of(b, c), 0))
    ctx_out = pl.BlockSpec((lc, ML_WIDTH), lambda b, c: (b, 0))
    state = pltpu.VMEM((ML_HEADS // 2, 2 * ML_QK_DIM, 2 * ML_V_DIM), F32)
    hf, hb, hf_ctx, hb_ctx = pl.pallas_call(
        _mlstm_kernel,
        grid=(b_, n_steps),
        in_specs=specs(step, fwd_block) + specs(step, bwd_block) + specs(lc, ctx_block),
        out_specs=[lat_out(fwd_block), lat_out(bwd_block), ctx_out, ctx_out],
        out_shape=[jax.ShapeDtypeStruct((n_lat, ML_WIDTH), BF16)] * 2
                  + [jax.ShapeDtypeStruct((b_ * lc, ML_WIDTH), BF16)] * 2,
        scratch_shapes=[state, state],
        compiler_params=_params(2),
        name="mlstm_scan",
    )(*arrays, gr, *arrays, gr, *arrays, gr)
    return hf, hf_ctx, hb, hb_ctx


def _swiglu(h, w1_ref, w3_ref, w2_ref):
    f = w1_ref.shape[1]
    fc = FFN_CHUNK_COLS
    y = None
    for c in range(f // fc):
        cols = slice(c * fc, (c + 1) * fc)
        a = _dot(h, w1_ref[:, cols])
        b = _dot(h, w3_ref[:, cols])
        part = _dot((a * _sigmoid(a) * b).astype(BF16), w2_ref[cols, :])
        y = part if y is None else y + part
    return y


N_MIX_REFS = 8


def _mix_residual(x_ref, mix_refs, n_lat_tiles, mod, wo_ref):
    att_lat, att_ctx, hf_lat, hf_ctx, hb_lat, hb_ctx, om_ref, gml_ref = mix_refs
    in_lat = pl.program_id(0) < n_lat_tiles
    pick = lambda lat, ctx, cols=slice(None): jnp.where(in_lat, lat[:, cols], ctx[:, cols])
    parts = [pick(att_lat, att_ctx)]
    for h in range(ML_HEADS):
        sl = slice(h * ML_V_DIM, (h + 1) * ML_V_DIM)
        tot = pick(hf_lat, hf_ctx, sl).astype(F32) + pick(hb_lat, hb_ctx, sl).astype(F32)
        parts.append((_rms(tot, gml_ref[:, sl]) * _sigmoid(om_ref[:, sl].astype(F32))).astype(BF16))
    return x_ref[...] + mod[2:3] * _dot(jnp.concatenate(parts, axis=1), wo_ref[...])


def _mix_specs(tm, row, mix, n_lat):
    lat_tiles = n_lat // tm
    specs = []
    for k in range(0, 6, 2):
        width = mix[k].shape[1]
        ctx_tiles = mix[k + 1].shape[0] // tm
        specs += [pl.BlockSpec((tm, width), lambda i, *_: (jnp.minimum(i, lat_tiles - 1), 0)),
                  pl.BlockSpec((tm, width), lambda i, *_, t=ctx_tiles: (jnp.clip(i - lat_tiles, 0, t - 1), 0))]
    return specs + [pl.BlockSpec((tm, ML_WIDTH), row), _resident((1, ML_WIDTH))]


def _dense_layer_kernel(x_ref, *refs, final, n_lat_tiles):
    mix_refs = refs[:N_MIX_REFS]
    mod_ref, g_ref, wo_ref, w1_ref, w3_ref, w2_ref, fg_ref = refs[N_MIX_REFS:N_MIX_REFS + 7]
    rest = refs[N_MIX_REFS + 7:]
    n_cast = (len(rest) - 1) // 2
    o_ref = rest[n_cast]
    for src, dst in zip(rest[:n_cast], rest[n_cast + 1:]):
        dst[...] = src[...].astype(dst.dtype)
    mod = mod_ref[0]
    x1 = _mix_residual(x_ref, mix_refs, n_lat_tiles, mod, wo_ref)
    hx = _adaln(x1, g_ref[...], mod[3:4], mod[4:5]).astype(BF16)
    out = x1 + mod[5:6] * _swiglu(hx, w1_ref, w3_ref, w2_ref)
    if final:
        out = _rms(out, fg_ref[...])
    o_ref[...] = out


def _row_maps(seg, tm):
    nlat = seg["n_lat"] // tm
    s_tiles = seg["S"] // tm
    return (lambda i, *_: (i, 0)), (lambda i, *_: (jnp.where(i < nlat, i // s_tiles, seg["B"]), 0, 0))


def _cast_slabs(stacked, index, n_steps):
    rows_per_layer = int(np.prod(stacked.shape[1:-1]))
    width = stacked.shape[-1]
    slab = next(r for r in range(2 * SUBLANES, rows_per_layer + 1, 2 * SUBLANES)
                if rows_per_layer % r == 0 and rows_per_layer // r <= n_steps)
    n_slabs = rows_per_layer // slab
    src = pl.BlockSpec((slab, width), lambda i: (index * n_slabs + jnp.minimum(i, n_slabs - 1), 0))
    dst = pl.BlockSpec((slab, width), lambda i: (jnp.minimum(i, n_slabs - 1), 0))
    flat = stacked.reshape(-1, width)
    return src, dst, flat, jax.ShapeDtypeStruct((rows_per_layer, width), BF16)


def _dense_layer(xs, mix, mod, g2, wo, layer, w1, w3, w2, ffn_index, final_g, seg, n_rows, final, cast=None):
    d = xs.shape[1]
    tm = seg["tm"]
    row, mod_map = _row_maps(seg, tm)
    n_steps = n_rows // tm
    cast_specs = [_cast_slabs(p, cast[1], n_steps) for p in cast[0]] if cast else []
    outs = pl.pallas_call(
        functools.partial(_dense_layer_kernel, final=final, n_lat_tiles=seg["n_lat"] // tm),
        grid=(n_steps,),
        in_specs=[pl.BlockSpec((tm, d), row)] + _mix_specs(tm, row, mix, seg["n_lat"])
                 + [pl.BlockSpec((1, 6, d), mod_map),
                    _resident((1, d)), _layer_resident(wo, layer), _layer_resident(w1, ffn_index),
                    _layer_resident(w3, ffn_index), _layer_resident(w2, ffn_index), _resident((1, d))]
                 + [c[0] for c in cast_specs],
        out_specs=[pl.BlockSpec((tm, d), row)] + [c[1] for c in cast_specs],
        out_shape=[jax.ShapeDtypeStruct((n_rows, d), F32)] + [c[3] for c in cast_specs],
        compiler_params=_params(1),
        name="dense_layer",
    )(xs, *mix, mod, g2, wo, w1, w3, w2, final_g, *[c[2] for c in cast_specs])
    if not cast:
        return outs[0], None
    return outs[0], tuple(o.reshape(p.shape[1:]) for o, p in zip(outs[1:], cast[0]))


def _router_kernel(x_ref, *refs, n_lat_tiles):
    mix_refs = refs[:N_MIX_REFS]
    (mod_ref, g_ref, wo_ref, wr_ref, br_ref, earlier_ref,
     x1_ref, h_ref, route_ref, counts_ref, count_ref) = refs[N_MIX_REFS:]
    mod = mod_ref[0]
    x1 = _mix_residual(x_ref, mix_refs, n_lat_tiles, mod, wo_ref)
    x1_ref[...] = x1
    hx = _adaln(x1, g_ref[...], mod[3:4], mod[4:5])
    h_ref[...] = _pack_bf16_pairs(hx)
    hx_hi = hx.astype(BF16)
    hx_lo = (hx - hx_hi.astype(F32)).astype(BF16)
    both = _dot_nt(wr_ref[...], hx_hi)
    logits = (both[:N_EXPERTS] + both[N_EXPERTS:] + _dot_nt(wr_ref[:N_EXPERTS, :], hx_lo)
              + br_ref[...])
    sub = lax.broadcasted_iota(jnp.int32, logits.shape, 0)
    top1 = jnp.max(logits, axis=0, keepdims=True)
    idx1 = jnp.min(jnp.where(logits == top1, sub, N_EXPERTS), axis=0, keepdims=True)
    rest = jnp.where(sub == idx1, -jnp.inf, logits)
    top2 = jnp.max(rest, axis=0, keepdims=True)
    idx2 = jnp.min(jnp.where(rest == top2, sub, N_EXPERTS), axis=0, keepdims=True)
    e2 = jnp.exp(top2 - top1)
    w_first = 1.0 / (1.0 + e2)
    @pl.when(pl.program_id(0) == 0)
    def _():
        count_ref[...] = jnp.zeros_like(count_ref)

    first, second = sub == idx1, sub == idx2
    hot = jnp.logical_or(first, second).astype(F32)
    before = _dot(hot.astype(BF16), earlier_ref[...]) + count_ref[:, :1]
    rank1 = jnp.sum(jnp.where(first, before, 0.0), axis=0, keepdims=True)
    rank2 = jnp.sum(jnp.where(second, before, 0.0), axis=0, keepdims=True)
    count_ref[...] += jnp.sum(hot, axis=1, keepdims=True)
    counts_ref[...] = count_ref[...]
    route = jnp.where(sub == 0, idx1.astype(F32), jnp.where(sub == 1, idx2.astype(F32), 0.0))
    route = jnp.where(sub == 2, w_first, jnp.where(sub == 3, e2 * w_first, route))
    route_ref[...] = jnp.where(sub == 4, rank1, jnp.where(sub == 5, rank2, route))


def _router(xs, mix, mod, g2, wo, layer, wr, br, seg, n_rows):
    d = xs.shape[1]
    tm = seg["tm"]
    row, mod_map = _row_maps(seg, tm)
    return pl.pallas_call(
        functools.partial(_router_kernel, n_lat_tiles=seg["n_lat"] // tm),
        grid=(n_rows // tm,),
        in_specs=[pl.BlockSpec((tm, d), row)] + _mix_specs(tm, row, mix, seg["n_lat"])
                 + [pl.BlockSpec((1, 6, d), mod_map),
                    _resident((1, d)), _layer_resident(wo, layer), _resident(wr.shape), _resident(br.shape),
                    _resident((tm, tm))],
        out_specs=[pl.BlockSpec((tm, d), row), pl.BlockSpec((tm, d // 2), row),
                   pl.BlockSpec((N_EXPERTS, tm), lambda i: (0, i)),
                   pl.BlockSpec((N_EXPERTS, LANES), lambda i: (0, 0))],
        out_shape=[jax.ShapeDtypeStruct((n_rows, d), F32), jax.ShapeDtypeStruct((n_rows, d // 2), jnp.int32),
                   jax.ShapeDtypeStruct((N_EXPERTS, n_rows), F32),
                   jax.ShapeDtypeStruct((N_EXPERTS, LANES), F32)],
        scratch_shapes=[pltpu.VMEM((N_EXPERTS, LANES), F32)],
        compiler_params=_params(1),
        name="mix_router",
    )(xs, *mix, mod, g2, wo, wr, br, jnp.triu(jnp.ones((tm, tm), BF16), 1))


BF16_BITS = 16
HIGH_HALF_WORD = -(1 << BF16_BITS)


def _pack_bf16_pairs(h):
    half = h.shape[1] // 2
    hi = lax.bitcast_convert_type(h[:, :half].astype(BF16).astype(F32), jnp.int32)
    lo = lax.bitcast_convert_type(h[:, half:].astype(BF16).astype(F32), jnp.int32)
    return (hi & jnp.int32(HIGH_HALF_WORD)) | lax.shift_right_logical(lo, BF16_BITS)


def _unpack_bf16_pairs(p):
    hi = lax.bitcast_convert_type(p & jnp.int32(HIGH_HALF_WORD), F32)
    lo = lax.bitcast_convert_type(lax.shift_left(p, BF16_BITS), F32)
    return hi, lo


def _route_plan(route, counts, tm):
    n_rows = route.shape[1]
    n_slots = TOP_K * n_rows
    idx1, idx2, rank1, rank2 = (route[r].astype(jnp.int32) for r in (0, 1, 4, 5))
    offs = jnp.concatenate([jnp.zeros((1,), jnp.int32), jnp.cumsum(counts[:, 0].astype(jnp.int32))])
    first_slot = lambda idx: sum(jnp.where(idx == e, offs[e], 0) for e in range(N_EXPERTS))
    slot_a = first_slot(idx1) + rank1
    slot_b = first_slot(idx2) + rank2
    n_tiles = n_slots // tm
    t_start = jnp.arange(n_tiles, dtype=jnp.int32) * tm
    e_first = jnp.sum(offs[None, 1:] <= t_start[:, None], axis=1).astype(jnp.int32)
    base_hi = jnp.minimum(t_start + tm, offs[e_first + 1])
    e_next = jnp.arange(1, N_EXPERTS, dtype=jnp.int32)
    start = offs[1:N_EXPERTS]
    x_tile = jnp.minimum(start // tm, n_tiles - 1)
    x_hi = jnp.where(start % tm != 0, jnp.minimum(offs[2:], (x_tile + 1) * tm), start)
    tiles = jnp.concatenate([t_start // tm, x_tile])
    experts = jnp.concatenate([e_first, e_next])
    lo = jnp.concatenate([t_start, start])
    hi = jnp.concatenate([base_hi, x_hi])
    order = jnp.argsort(tiles * (2 * N_EXPERTS) + experts)
    tiles, experts, lo, hi = tiles[order], experts[order], lo[order], hi[order]
    change = tiles[1:] != tiles[:-1]
    one = jnp.ones((1,), bool)
    first = jnp.concatenate([one, change]).astype(jnp.int32)
    last = jnp.concatenate([change, one]).astype(jnp.int32)
    return slot_a, slot_b, (tiles, experts, lo, hi, first, last)


def _scatter_rows(rows, idx_a, idx_b):
    n_rows, width = rows.shape
    workers = SC_CORES * SC_SUBCORES
    per_worker = n_rows // workers
    assert n_rows % (workers * SC_GATHER_ROWS) == 0
    mesh = plsc.VectorSubcoreMesh(core_axis_name="c", subcore_axis_name="s")

    @functools.partial(
        pl.kernel, mesh=mesh,
        out_type=jax.ShapeDtypeStruct((TOP_K * n_rows, width), rows.dtype),
        scratch_types=[pltpu.VMEM((TOP_K, SC_GATHER_ROWS), jnp.int32),
                       pltpu.VMEM((SC_GATHER_ROWS, width), rows.dtype),
                       pltpu.SemaphoreType.DMA, pltpu.SemaphoreType.DMA],
        name="scatter_rows")
    def scatter(rows_hbm, idx_a_hbm, idx_b_hbm, out_hbm, idx_v, rows_v, sem_a, sem_b):
        base = (lax.axis_index("s") * SC_CORES + lax.axis_index("c")) * per_worker

        @pl.loop(0, per_worker // SC_GATHER_ROWS)
        def _(i):
            off = pl.multiple_of(base + i * SC_GATHER_ROWS, SC_GATHER_ROWS)
            pltpu.sync_copy(idx_a_hbm.at[pl.ds(off, SC_GATHER_ROWS)], idx_v.at[0])
            pltpu.sync_copy(idx_b_hbm.at[pl.ds(off, SC_GATHER_ROWS)], idx_v.at[1])
            pltpu.sync_copy(rows_hbm.at[pl.ds(off, SC_GATHER_ROWS)], rows_v)
            first = pltpu.async_copy(rows_v, out_hbm.at[idx_v.at[0]], sem_a)
            second = pltpu.async_copy(rows_v, out_hbm.at[idx_v.at[1]], sem_b)
            first.wait()
            second.wait()

    return scatter(rows, idx_a, idx_b)


def _gather_rows(table, idx):
    n_idx = idx.shape[0]
    width = table.shape[1]
    workers = SC_CORES * SC_SUBCORES
    per_worker = n_idx // workers
    assert n_idx % (workers * SC_GATHER_ROWS * SC_IN_FLIGHT) == 0
    mesh = plsc.VectorSubcoreMesh(core_axis_name="c", subcore_axis_name="s")

    @functools.partial(
        pl.kernel, mesh=mesh,
        out_type=jax.ShapeDtypeStruct((n_idx, width), table.dtype),
        scratch_types=[pltpu.VMEM((SC_IN_FLIGHT, SC_GATHER_ROWS), jnp.int32),
                       pltpu.VMEM((SC_IN_FLIGHT, SC_GATHER_ROWS, width), table.dtype)]
                      + [pltpu.SemaphoreType.DMA] * (2 * SC_IN_FLIGHT),
        name="gather_rows")
    def gather(table_hbm, idx_hbm, out_hbm, idx_v, rows_v, *sems):
        base = (lax.axis_index("s") * SC_CORES + lax.axis_index("c")) * per_worker

        @pl.loop(0, per_worker // (SC_GATHER_ROWS * SC_IN_FLIGHT))
        def _(i):
            offs = [pl.multiple_of(base + (i * SC_IN_FLIGHT + b) * SC_GATHER_ROWS, SC_GATHER_ROWS)
                    for b in range(SC_IN_FLIGHT)]
            for b, off in enumerate(offs):
                pltpu.sync_copy(idx_hbm.at[pl.ds(off, SC_GATHER_ROWS)], idx_v.at[b])
            reads = [pltpu.async_copy(table_hbm.at[idx_v.at[b]], rows_v.at[b], sems[b])
                     for b in range(SC_IN_FLIGHT)]
            writes = []
            for b, off in enumerate(offs):
                reads[b].wait()
                writes.append(pltpu.async_copy(rows_v.at[b], out_hbm.at[pl.ds(off, SC_GATHER_ROWS)],
                                               sems[SC_IN_FLIGHT + b]))
            for w in writes:
                w.wait()

    return gather(table, idx)


def _experts_kernel(tile_ref, exp_ref, lo_ref, hi_ref, first_ref, last_ref,
                    x_ref, w1_ref, w3_ref, w2_ref, o_ref, acc_ref):
    i = pl.program_id(0)
    tm, half = x_ref.shape
    f = w1_ref.shape[2]
    fc = FFN_CHUNK_COLS

    @pl.when(first_ref[i] == 1)
    def _():
        acc_ref[...] = jnp.zeros_like(acc_ref)

    lo, hi = lo_ref[i], hi_ref[i]

    @pl.when(hi > lo)
    def _():
        x_hi, x_lo = _unpack_bf16_pairs(x_ref[...])
        x = jnp.concatenate([x_hi.astype(BF16), x_lo.astype(BF16)], axis=1)
        y = None
        for c in range(f // fc):
            cols = slice(c * fc, (c + 1) * fc)
            a = _dot(x, w1_ref[0, :, cols])
            b = _dot(x, w3_ref[0, :, cols])
            part = _dot((a * _sigmoid(a) * b).astype(BF16), w2_ref[0, cols, :])
            y = part if y is None else y + part
        rows = tile_ref[i] * tm + lax.broadcasted_iota(jnp.int32, (tm, 1), 0)
        keep = jnp.logical_and(rows >= lo, rows < hi)
        acc_ref[...] += jnp.where(keep, y, 0.0)

    @pl.when(last_ref[i] == 1)
    def _():
        o_ref[...] = _pack_bf16_pairs(acc_ref[...])


def _experts(xs_sorted, items, w1, w3, w2):
    n_slots, half = xs_sorted.shape
    n_exp, d, f = w1.shape
    tm = MOE_ROW_TILE
    assert f % FFN_CHUNK_COLS == 0
    tile_map = lambda i, tiles, *_: (tiles[i], 0)
    exp_map = lambda i, tiles, experts, *_: (experts[i], 0, 0)
    grid_spec = pltpu.PrefetchScalarGridSpec(
        num_scalar_prefetch=len(items),
        grid=(items[0].shape[0],),
        in_specs=[pl.BlockSpec((tm, half), tile_map),
                  pl.BlockSpec((1, d, f), exp_map), pl.BlockSpec((1, d, f), exp_map),
                  pl.BlockSpec((1, f, d), exp_map)],
        out_specs=pl.BlockSpec((tm, half), tile_map),
        scratch_shapes=[pltpu.VMEM((tm, d), F32)],
    )
    return pl.pallas_call(
        _experts_kernel,
        grid_spec=grid_spec,
        out_shape=jax.ShapeDtypeStruct((n_slots, half), jnp.int32),
        compiler_params=_params(1),
        name="experts",
    )(*items, xs_sorted, w1, w3, w2)


def _combine_kernel(x1_ref, ya_ref, yb_ref, route_ref, mod_ref, fg_ref, o_ref, *, final):
    half = ya_ref.shape[1]
    route = route_ref[...]
    padded = jnp.concatenate([route, jnp.zeros((LANES - route.shape[0], route.shape[1]), F32)], axis=0)
    route = jnp.transpose(padded)
    wa, wb = route[:, 2:3], route[:, 3:4]
    a_hi, a_lo = _unpack_bf16_pairs(ya_ref[...])
    b_hi, b_lo = _unpack_bf16_pairs(yb_ref[...])
    gate = mod_ref[0][5:6]
    out_hi = x1_ref[:, :half] + gate[:, :half] * (wa * a_hi + wb * b_hi)
    out_lo = x1_ref[:, half:] + gate[:, half:] * (wa * a_lo + wb * b_lo)
    if final:
        total = jnp.sum(out_hi * out_hi, axis=-1, keepdims=True) + jnp.sum(out_lo * out_lo, axis=-1, keepdims=True)
        scale = lax.rsqrt(total / (2 * half) + EPS)
        out_hi = out_hi * scale * fg_ref[:, :half]
        out_lo = out_lo * scale * fg_ref[:, half:]
    o_ref[:, :half] = out_hi
    o_ref[:, half:] = out_lo


def _combine(x1, y_pairs, route, mod, final_g, seg, final):
    n_rows, d = x1.shape
    tm = seg["tm"]
    row, mod_map = _row_maps(seg, tm)
    second = n_rows // tm
    return pl.pallas_call(
        functools.partial(_combine_kernel, final=final),
        grid=(n_rows // tm,),
        in_specs=[pl.BlockSpec((tm, d), row), pl.BlockSpec((tm, d // 2), row),
                  pl.BlockSpec((tm, d // 2), lambda i: (second + i, 0)),
                  pl.BlockSpec((N_EXPERTS, tm), lambda i: (0, i)), pl.BlockSpec((1, 6, d), mod_map),
                  pl.BlockSpec((1, d), lambda i: (0, 0))],
        out_specs=pl.BlockSpec((tm, d), row),
        out_shape=jax.ShapeDtypeStruct((n_rows, d), F32),
        compiler_params=_params(1),
        name="moe_combine",
    )(x1, y_pairs, y_pairs, route, mod, final_g)


def _moe_layer(xs, mix, mod, g2, wo, layer, wr, br, w1, w3, w2, final_g, seg, n_rows, final):
    if seg["S"] % MOE_EDGE_TILE == 0 and (n_rows - seg["n_lat"]) % MOE_EDGE_TILE == 0:
        seg = dict(seg, tm=MOE_EDGE_TILE)
    x1, h_pairs, route, counts = _router(xs, mix, mod, g2, wo, layer, wr, br, seg, n_rows)
    slot_a, slot_b, items = _route_plan(route, counts, MOE_ROW_TILE)
    y_sorted = _experts(_scatter_rows(h_pairs, slot_a, slot_b), items, w1, w3, w2)
    y_pairs = _gather_rows(y_sorted, jnp.concatenate([slot_a, slot_b]))
    return _combine(x1, y_pairs, route, mod, final_g, seg, final)


def kernel(x, c, ctx, c_ctx, norm1_g, norm2_g, w_mod, b_mod, w_in, conv_w, conv_b, b_gates, attn_sink,
           g_att, g_ml, w_out, ffn_w1, ffn_w3, ffn_w2, w_router, b_router, exp_w1, exp_w3, exp_w2,
           final_g):
    b_, s_, d = x.shape
    lc = ctx.shape[1]
    depth = w_in.shape[0]
    n_lat, n_ctx = b_ * s_, b_ * lc
    tm = min(ROW_TILE, s_)
    assert s_ % tm == 0 and n_ctx % tm == 0
    assert (TOP_K * n_lat) % MOE_ROW_TILE == 0 and (TOP_K * n_ctx) % MOE_ROW_TILE == 0
    assert s_ % ML_CHUNK == 0 and lc % ML_CHUNK == 0 and n_lat % lc == 0 and b_ < MOD_ROWS
    seg = dict(B=b_, S=s_, Lc=lc, n_lat=n_lat, tm=tm)
    tm_in = INPROJ_TILE if s_ % INPROJ_TILE == 0 and n_ctx % INPROJ_TILE == 0 else tm
    seg_in = dict(seg, tm=tm_in)

    cond = jnp.zeros((MOD_ROWS, d), F32).at[:b_].set(c).at[b_].set(c_ctx)
    mods = _modulation(cond, w_mod, b_mod).reshape(depth, MOD_ROWS, 6, d)
    rope = _rope_tables(s_, tm_in)
    xs = (x.reshape(n_lat, d), ctx.reshape(n_ctx, d))
    final_row = final_g.reshape(1, d)

    w_in_b, w_out_b = w_in.astype(BF16), w_out.astype(BF16)
    ffn_b = (ffn_w1.astype(BF16), ffn_w3.astype(BF16), ffn_w2.astype(BF16))
    exp_b = None

    for layer in range(depth):
        last = layer == depth - 1
        w_gates = w_in[layer][:, MAIN_WIDTH:]
        w_gc = jnp.pad(w_gates, ((0, 0), (0, LANES - ML_GATES))).astype(BF16)
        b_gc = jnp.pad(b_gates[layer], (0, LANES - ML_GATES)).reshape(1, LANES)
        qa, ka, va, qm, km, vm, om, gc, gr, *stream = _input_projection(
            xs, mods[layer], norm1_g[layer].reshape(1, d), w_in_b, layer, w_gc,
            w_gates.T.astype(BF16), b_gc, b_gates[layer].reshape(ML_GATES, 1),
            conv_w[layer], conv_b[layer].reshape(1, -1), rope, seg_in)
        if stream:
            xs = stream[0]
        att = _attention(qa, ka, va, attn_sink[layer], g_att[layer].reshape(1, ATT_WIDTH), seg, not last)
        mix = (*att, *_mlstm(qm, km, vm, gc, gr, seg), om, g_ml[layer].reshape(1, ML_WIDTH))
        n_rows = n_lat if last else n_lat + n_ctx
        g2 = norm2_g[layer].reshape(1, d)
        i = layer // 2
        if layer % 2 == 0:
            cast = None if last else ((exp_w1, exp_w3, exp_w2), i)
            xs, exp_b = _dense_layer(xs, mix, mods[layer], g2, w_out_b, layer, *ffn_b, i, final_row, seg,
                                     n_rows, last, cast)
        else:
            wr = w_router[i].T
            wr_hi = wr.astype(BF16)
            wr = jnp.concatenate([wr_hi, (wr - wr_hi.astype(F32)).astype(BF16)], axis=0)
            br = b_router[i].reshape(N_EXPERTS, 1)
            xs = _moe_layer(xs, mix, mods[layer], g2, w_out_b, layer, wr, br, *exp_b, final_row, seg,
                            n_rows, last)
    return xs[:n_lat].reshape(b_, s_, d)
```
